```python
import jax, jax.numpy as jnp
from jax import lax
import numpy as np

D_MODEL = 1024
BATCH = 2
SEQ = 16384
DEPTH = 2

N_A = (DEPTH + 1) // 2
N_B = DEPTH // 2

HEAD_DIM = 64
A_GROUPS = 3
A_HEADS_PER_GROUP = 5
A_HEADS = A_GROUPS * A_HEADS_PER_GROUP
A_WIDTH = A_HEADS * HEAD_DIM
DILATED_PATTERNS = ((128, 1), (512, 4), (2048, 16))
QUERY_BLOCK = 128
ROT_DIM = HEAD_DIM // 4
ROPE_THETA = 500000.0
NEG_INF = -1e30

CHUNK = 128
GMLP_HALF = 2 * D_MODEL
GMLP_GROUPS = 8

N_EXPERT_GROUPS = 4
EXPERTS_PER_GROUP = 4
N_EXPERTS = N_EXPERT_GROUPS * EXPERTS_PER_GROUP
TOP_K_INNER = 2
D_EXPERT = 256

EPS = 1e-6

kernel_name = "hybrid_dilated_attn_gmlp_hmoe_encoder"


def rms_norm(x, g):
    xf = x.astype(jnp.float32)
    y = xf * lax.rsqrt(jnp.mean(xf * xf, axis=-1, keepdims=True) + EPS)
    return (y * g.astype(jnp.float32)).astype(x.dtype)


def layer_norm(x, g):
    xf = x.astype(jnp.float32)
    mu = jnp.mean(xf, axis=-1, keepdims=True)
    xc = xf - mu
    y = xc * lax.rsqrt(jnp.mean(xc * xc, axis=-1, keepdims=True) + EPS)
    return (y * g.astype(jnp.float32)).astype(x.dtype)


def modulate(h, shift, scale):
    return (h * (1.0 + scale[:, None, :]) + shift[:, None, :]).astype(h.dtype)


def partial_rope(t, pos):
    inv_freq = ROPE_THETA ** (-jnp.arange(0, ROT_DIM, 2, dtype=jnp.float32) / ROT_DIM)
    ang = pos.astype(jnp.float32)[:, None] * inv_freq[None, :]
    cos = jnp.concatenate([jnp.cos(ang), jnp.cos(ang)], -1)[None, :, None, :]
    sin = jnp.concatenate([jnp.sin(ang), jnp.sin(ang)], -1)[None, :, None, :]
    rot, rest = t[..., :ROT_DIM], t[..., ROT_DIM:]
    r1, r2 = rot[..., :ROT_DIM // 2], rot[..., ROT_DIM // 2:]
    rot_half = jnp.concatenate([-r2, r1], -1)
    rot = rot * cos + rot_half * sin
    return jnp.concatenate([rot.astype(t.dtype), rest], -1)


def banded_dilated_attention(q, k, v, dilation, radius):
    B, S, H, Dh = q.shape
    L = S // dilation
    Lp = -(-L // QUERY_BLOCK) * QUERY_BLOCK
    n_blk = Lp // QUERY_BLOCK

    def to_sub(t):
        return t.reshape(B, L, dilation, H, Dh).transpose(0, 2, 1, 3, 4)

    qs = jnp.pad(to_sub(q), ((0, 0), (0, 0), (0, Lp - L), (0, 0), (0, 0)))
    qs = qs.reshape(B, dilation, n_blk, QUERY_BLOCK, H, Dh)

    def key_blocks(t):
        tp = jnp.pad(to_sub(t), ((0, 0), (0, 0), (radius, Lp + QUERY_BLOCK - L - radius), (0, 0), (0, 0)))
        tp = tp.reshape(B, dilation, n_blk + 1, QUERY_BLOCK, H, Dh)
        return jnp.concatenate([tp[:, :, :-1], tp[:, :, 1:]], axis=3)

    kb, vb = key_blocks(k), key_blocks(v)
    s = jnp.einsum('bdnqhc,bdnkhc->bdnqhk', qs, kb).astype(jnp.float32) * (Dh ** -0.5)

    t_idx = jnp.arange(QUERY_BLOCK)[:, None]
    k_idx = jnp.arange(2 * QUERY_BLOCK)[None, :]
    rel = k_idx - t_idx
    band = (rel >= 0) & (rel <= 2 * radius)
    key_pos = jnp.arange(n_blk)[:, None] * QUERY_BLOCK + jnp.arange(2 * QUERY_BLOCK)[None, :] - radius
    kvalid = (key_pos >= 0) & (key_pos < L)
    mask = band[None, :, :] & kvalid[:, None, :]
    s = jnp.where(mask[None, None, :, :, None, :], s, NEG_INF)

    m = jnp.max(s, axis=-1, keepdims=True)
    p = jnp.exp(s - m)
    denom = jnp.sum(p, axis=-1)
    o = jnp.einsum('bdnqhk,bdnkhc->bdnqhc', p, vb.astype(jnp.float32)) / denom[..., None]
    lse = m[..., 0] + jnp.log(denom)

    o = o.reshape(B, dilation, Lp, H, Dh)[:, :, :L].transpose(0, 2, 1, 3, 4).reshape(B, S, H, Dh)
    lse = lse.reshape(B, dilation, Lp, H)[:, :, :L].transpose(0, 2, 1, 3).reshape(B, S, H)
    return o, lse


def mixer_dilated_attention(h, w_qkv, w_o):
    B, S, _ = h.shape
    qkv = (h @ w_qkv).reshape(B, S, 3, A_HEADS, HEAD_DIM)
    pos = jnp.arange(S)
    q = partial_rope(qkv[:, :, 0], pos)
    k = partial_rope(qkv[:, :, 1], pos)
    v = qkv[:, :, 2]
    outs, lses = [], []
    for g, (window, dil) in enumerate(DILATED_PATTERNS):
        sl = slice(g * A_HEADS_PER_GROUP, (g + 1) * A_HEADS_PER_GROUP)
        o, l = banded_dilated_attention(q[:, :, sl], k[:, :, sl], v[:, :, sl], dil, window // (2 * dil))
        outs.append(o)
        lses.append(l)
    o = jnp.stack(outs, axis=2)
    lse = jnp.stack(lses, axis=2)
    alpha = jax.nn.softmax(lse, axis=2)
    o = (o * alpha[..., None]).astype(h.dtype).reshape(B, S, A_WIDTH)
    return o @ w_o


def mixer_spatial_gating(h, w_in, v_gain, w_s, b_s, w_o):
    B, S, _ = h.shape
    z = jax.nn.gelu(h @ w_in, approximate=False)
    u, v = z[..., :GMLP_HALF], z[..., GMLP_HALF:]
    v = layer_norm(v, v_gain)
    vc = v.reshape(B, S // CHUNK, CHUNK, GMLP_GROUPS, GMLP_HALF // GMLP_GROUPS)
    vs = jnp.einsum('gpq,bnqgc->bnpgc', w_s, vc) + jnp.transpose(b_s)[None, None, :, :, None]
    return (u * vs.reshape(B, S, GMLP_HALF).astype(u.dtype)) @ w_o


def hierarchical_moe(h, w_group, b_group, w_expert, b_expert, w_gate, w_up, w_down):
    B, S, D = h.shape
    T = B * S
    ht = h.reshape(T, D)
    g_logits = (ht @ w_group + b_group).astype(jnp.float32)
    g_prob = jax.nn.softmax(g_logits, axis=-1)
    g_val, g_idx = lax.top_k(g_prob, 1)
    e_all = (jnp.einsum('td,gde->tge', ht, w_expert) + b_expert).astype(jnp.float32)
    e_logits = jnp.take_along_axis(e_all, g_idx[:, :, None], axis=1)[:, 0]
    top_v, top_i = lax.top_k(e_logits, TOP_K_INNER)
    top_p = jax.nn.softmax(top_v, axis=-1)
    w_inner = jnp.sum(jax.nn.one_hot(top_i, EXPERTS_PER_GROUP, dtype=jnp.float32) * top_p[..., None], axis=1)
    gate = jax.nn.one_hot(g_idx[:, 0], N_EXPERT_GROUPS, dtype=jnp.float32)[:, :, None] * (g_val[:, :, None] * w_inner[:, None, :])
    gate = gate.reshape(T, N_EXPERTS)
    hg = jnp.einsum('td,edf->tef', ht, w_gate)
    hu = jnp.einsum('td,edf->tef', ht, w_up)
    act = (jax.nn.silu(hg) * hu * gate[..., None]).astype(h.dtype)
    y = jnp.einsum('tef,efd->td', act, w_down)
    return y.reshape(B, S, D)


def setup_inputs(seed: int = 0) -> dict:
    key = jax.random.key(seed)
    ks = jax.random.split(key, 24)
    f32 = jnp.float32

    def nrm(k, shape, fan_in, s=1.0):
        return jax.random.normal(k, shape, f32) * (s * fan_in ** -0.5)

    def gain(k, shape):
        return 1.0 + 0.02 * jax.random.normal(k, shape, f32)

    D = D_MODEL
    return {
        "x": jax.random.normal(ks[0], (BATCH, SEQ, D), f32),
        "c": jax.random.normal(ks[1], (BATCH, D), f32),
        "norm_mix": gain(ks[2], (DEPTH, D)),
        "norm_ffn": gain(ks[3], (DEPTH, D)),
        "w_ada": nrm(ks[4], (DEPTH, D, 6 * D), D, 0.5),
        "b_ada": 0.01 * jax.random.normal(ks[5], (DEPTH, 6 * D), f32),
        "a_w_qkv": nrm(ks[6], (N_A, D, 3 * A_WIDTH), D),
        "a_w_o": nrm(ks[7], (N_A, A_WIDTH, D), A_WIDTH),
        "b_w_in": nrm(ks[8], (N_B, D, 2 * GMLP_HALF), D),
        "b_v_gain": gain(ks[9], (N_B, GMLP_HALF)),
        "b_w_s": nrm(ks[10], (N_B, GMLP_GROUPS, CHUNK, CHUNK), CHUNK),
        "b_b_s": 1.0 + 0.01 * jax.random.normal(ks[11], (N_B, GMLP_GROUPS, CHUNK), f32),
        "b_w_o": nrm(ks[12], (N_B, GMLP_HALF, D), GMLP_HALF),
        "r_w_group": nrm(ks[13], (DEPTH, D, N_EXPERT_GROUPS), D),
        "r_b_group": 0.01 * jax.random.normal(ks[14], (DEPTH, N_EXPERT_GROUPS), f32),
        "r_w_expert": nrm(ks[15], (DEPTH, N_EXPERT_GROUPS, D, EXPERTS_PER_GROUP), D),
        "r_b_expert": 0.01 * jax.random.normal(ks[16], (DEPTH, N_EXPERT_GROUPS, EXPERTS_PER_GROUP), f32),
        "e_w_gate": nrm(ks[17], (DEPTH, N_EXPERTS, D, D_EXPERT), D),
        "e_w_up": nrm(ks[18], (DEPTH, N_EXPERTS, D, D_EXPERT), D),
        "e_w_down": nrm(ks[19], (DEPTH, N_EXPERTS, D_EXPERT, D), D_EXPERT),
        "final_norm": gain(ks[20], (D,)),
    }


def reference(x, c, norm_mix, norm_ffn, w_ada, b_ada, a_w_qkv, a_w_o, b_w_in, b_v_gain, b_w_s, b_b_s, b_w_o,
              r_w_group, r_b_group, r_w_expert, r_b_expert, e_w_gate, e_w_up, e_w_down, final_norm):
    c_act = jax.nn.silu(c)
    for l in range(DEPTH):
        mod = c_act @ w_ada[l] + b_ada[l]
        sh_m, sc_m, gt_m, sh_f, sc_f, gt_f = jnp.split(mod, 6, axis=-1)
        h = modulate(rms_norm(x, norm_mix[l]), sh_m, sc_m)
        j = l // 2
        if l % 2 == 0:
            y = mixer_dilated_attention(h, a_w_qkv[j], a_w_o[j])
        else:
            y = mixer_spatial_gating(h, b_w_in[j], b_v_gain[j], b_w_s[j], b_b_s[j], b_w_o[j])
        x = x + (gt_m[:, None, :] * y).astype(x.dtype)
        h = modulate(rms_norm(x, norm_ffn[l]), sh_f, sc_f)
        y = hierarchical_moe(h, r_w_group[l], r_b_group[l], r_w_expert[l], r_b_expert[l],
                             e_w_gate[l], e_w_up[l], e_w_down[l])
        x = x + (gt_f[:, None, :] * y).astype(x.dtype)
    return rms_norm(x, final_norm)
```

```python
import functools

import jax
import jax.numpy as jnp
import numpy as np
from jax import lax
from jax.experimental import pallas as pl
from jax.experimental.pallas import tpu as pltpu

D_MODEL = 1024
DEPTH = 2
HEAD_DIM = 64
HEADS_PER_GROUP = 5
N_ATTN_GROUPS = 3
DILATIONS = (1, 4, 16)
BAND_RADIUS = 64
ROT_DIM = HEAD_DIM // 4
ROPE_THETA = 500000.0
NEG_INF = -1e30
CHUNK = 128
GMLP_HALF = 2 * D_MODEL
GMLP_GROUPS = 8
N_EXPERT_GROUPS = 4
EXPERTS_PER_GROUP = 4
N_EXPERTS = 16
D_EXPERT = 256
EPS = 1e-6

LANES = 128
GROUP_WIDTH = 384
ROUTER_ROWS = 32

F32 = jnp.float32
BF16 = jnp.bfloat16
VMEM_LIMIT = 56 * 1024 * 1024


def _cparams(sem):
    return pltpu.CompilerParams(dimension_semantics=sem, vmem_limit_bytes=VMEM_LIMIT)


def _const_spec(shape, index_map):
    return pl.BlockSpec(shape, index_map, pipeline_mode=pl.Buffered(1))


def _silu(x):
    return x * (1.0 / (1.0 + jnp.exp(-x)))


def _gelu(x):
    return 0.5 * x * (1.0 + lax.erf(x * (2.0 ** -0.5)))


def _norm_modulate(x, gain, shift, scale):
    ms = jnp.mean(x * x, axis=-1, keepdims=True)
    return (x * lax.rsqrt(ms + EPS) * gain) * (1.0 + scale) + shift


def _ada_kernel(c_ref, w_ref, b_ref, o_ref):
    ca = _silu(c_ref[...])
    o_ref[0] = jnp.dot(ca, w_ref[0], preferred_element_type=F32,
                       precision=lax.Precision.HIGHEST) + b_ref[0]


def _ada_mod(c, w_ada, b_ada):
    b = c.shape[0]
    tn = 1536
    c8 = jnp.pad(c, ((0, 8 - b), (0, 0)))
    out = pl.pallas_call(
        _ada_kernel,
        grid=(DEPTH, 6 * D_MODEL // tn),
        in_specs=[pl.BlockSpec((8, D_MODEL), lambda l, j: (0, 0)),
                  pl.BlockSpec((1, D_MODEL, tn), lambda l, j: (l, 0, j)),
                  pl.BlockSpec((1, 1, tn), lambda l, j: (l, 0, j))],
        out_specs=pl.BlockSpec((1, 8, tn), lambda l, j: (l, 0, j)),
        out_shape=jax.ShapeDtypeStruct((DEPTH, 8, 6 * D_MODEL), F32),
        compiler_params=_cparams(("arbitrary", "arbitrary")),
        name="ada_mod",
    )(c8, w_ada, b_ada.reshape(DEPTH, 1, 6 * D_MODEL))
    return out[:, :b].reshape(DEPTH, b, 6, 1, D_MODEL)


def _ffn_prep(x1, gain, shift, scale, wr_ref, br_ref):
    tm = x1.shape[0]
    hf = _norm_modulate(x1, gain, shift, scale).astype(BF16)
    lt = lax.dot_general(wr_ref[...], hf, (((1,), (1,)), ((), ())),
                         preferred_element_type=F32) + br_ref[...]
    gl = [lt[i:i + 1, :] for i in range(N_EXPERT_GROUPS)]
    gmax = jnp.maximum(jnp.maximum(gl[0], gl[1]), jnp.maximum(gl[2], gl[3]))
    ge = [jnp.exp(g - gmax) for g in gl]
    gsum = ge[0] + ge[1] + ge[2] + ge[3]
    gp = [e / gsum for e in ge]
    g_val = jnp.maximum(jnp.maximum(gp[0], gp[1]), jnp.maximum(gp[2], gp[3]))
    g_idx = jnp.where(gp[0] == g_val, 0, jnp.where(gp[1] == g_val, 1, jnp.where(gp[2] == g_val, 2, 3)))
    el = []
    for j in range(EXPERTS_PER_GROUP):
        rows = [lt[4 + 4 * g + j:5 + 4 * g + j, :] for g in range(N_EXPERT_GROUPS)]
        el.append(jnp.where(g_idx == 0, rows[0], jnp.where(g_idx == 1, rows[1],
                  jnp.where(g_idx == 2, rows[2], rows[3]))))
    v1 = jnp.maximum(jnp.maximum(el[0], el[1]), jnp.maximum(el[2], el[3]))
    i1 = jnp.where(el[0] == v1, 0, jnp.where(el[1] == v1, 1, jnp.where(el[2] == v1, 2, 3)))
    el2 = [jnp.where(i1 == j, -jnp.inf, el[j]) for j in range(EXPERTS_PER_GROUP)]
    v2 = jnp.maximum(jnp.maximum(el2[0], el2[1]), jnp.maximum(el2[2], el2[3]))
    i2 = jnp.where((el2[0] == v2) & (i1 != 0), 0,
                   jnp.where((el2[1] == v2) & (i1 != 1), 1,
                             jnp.where((el2[2] == v2) & (i1 != 2), 2, 3)))
    e21 = jnp.exp(v2 - v1)
    den = 1.0 + e21
    w1 = g_val * (1.0 / den)
    w2 = g_val * (e21 / den)
    e1 = g_idx * EXPERTS_PER_GROUP + i1
    e2 = g_idx * EXPERTS_PER_GROUP + i2
    row = lax.broadcasted_iota(jnp.int32, (LANES, tm), 0)
    gate_t = jnp.where(row == e1, w1, 0.0) + jnp.where(row == e2, w2, 0.0)
    return hf, gate_t.T


def _rope(t, cos, sgn, first_half):
    out = []
    for c in range(GROUP_WIDTH // LANES):
        xc = t[:, c * LANES:(c + 1) * LANES]
        other = jnp.where(first_half, pltpu.roll(xc, LANES - ROT_DIM // 2, 1),
                          pltpu.roll(xc, ROT_DIM // 2, 1))
        out.append(xc * cos + other * sgn)
    return jnp.concatenate(out, axis=1)


def _qkv_kernel(x_ref, gain_ref, sh_ref, sc_ref, w_ref, cos_ref, sgn_ref, *rest, tm):
    out_refs, hn_scr = rest[:9], rest[9]
    hn = _norm_modulate(x_ref[0], gain_ref[...], sh_ref[0, 0], sc_ref[0, 0])
    n_slab = D_MODEL // LANES
    lane = lax.broadcasted_iota(jnp.int32, (tm, LANES), 1) % HEAD_DIM
    first_half = lane < ROT_DIM // 2
    for g, d in enumerate(DILATIONS):
        rows = tm // d
        if d == 1:
            hp, cos, sgn = hn, cos_ref[...], sgn_ref[...]
        else:
            if g == 1:
                for s in range(n_slab):
                    hn_scr[s] = hn[:, s * LANES:(s + 1) * LANES]
            hp = jnp.concatenate(
                [jnp.concatenate([hn_scr[s, pl.ds(r, rows, stride=d), :] for s in range(n_slab)], axis=1)
                 for r in range(d)], axis=0)
            cos = jnp.concatenate([cos_ref[pl.ds(r, rows, stride=d), :] for r in range(d)], axis=0)
            sgn = jnp.concatenate([sgn_ref[pl.ds(r, rows, stride=d), :] for r in range(d)], axis=0)
        res = jnp.dot(hp.astype(BF16), w_ref[g], preferred_element_type=F32)
        q = _rope(res[:, :GROUP_WIDTH], cos, sgn, first_half)
        k = _rope(res[:, GROUP_WIDTH:2 * GROUP_WIDTH], cos, sgn, first_half)
        v = res[:, 2 * GROUP_WIDTH:]
        for t, o_ref in zip((q, k, v), out_refs[3 * g:3 * g + 3]):
            o_ref[0] = t.astype(BF16).reshape(d, rows, GROUP_WIDTH)


def _qkv(x, gain, shift, scale, w_groups, cos_t, sgn_t, tm=512):
    b, s, _ = x.shape
    out_shape, out_specs = [], []
    for d in DILATIONS:
        for _ in range(3):
            out_shape.append(jax.ShapeDtypeStruct((b, d, s // d, GROUP_WIDTH), BF16))
            out_specs.append(pl.BlockSpec((1, d, tm // d, GROUP_WIDTH), lambda bi, i: (bi, 0, i, 0)))
    return pl.pallas_call(
        functools.partial(_qkv_kernel, tm=tm),
        grid=(b, s // tm),
        in_specs=[pl.BlockSpec((1, tm, D_MODEL), lambda bi, i: (bi, i, 0)),
                  _const_spec((1, D_MODEL), lambda bi, i: (0, 0)),
                  pl.BlockSpec((1, 1, 1, D_MODEL), lambda bi, i: (bi, 0, 0, 0)),
                  pl.BlockSpec((1, 1, 1, D_MODEL), lambda bi, i: (bi, 0, 0, 0)),
                  _const_spec((N_ATTN_GROUPS, D_MODEL, 3 * GROUP_WIDTH), lambda bi, i: (0, 0, 0)),
                  pl.BlockSpec((tm, LANES), lambda bi, i: (i, 0)),
                  pl.BlockSpec((tm, LANES), lambda bi, i: (i, 0))],
        out_specs=out_specs,
        out_shape=out_shape,
        scratch_shapes=[pltpu.VMEM((D_MODEL // LANES, tm, LANES), F32)],
        compiler_params=_cparams(("arbitrary", "arbitrary")),
        name="qkv_rope",
    )(x, gain, shift, scale, w_groups, cos_t, sgn_t)


def _attn_kernel(q_ref, kp_ref, kc_ref, kn_ref, vp_ref, vc_ref, vn_ref, o_ref, lse_ref,
                 kbuf, vbuf, *, seq, tq):
    i = pl.program_id(2)
    r = BAND_RADIUS
    kbuf[0:r] = kp_ref[0, 0]
    kbuf[r:r + tq] = kc_ref[0, 0]
    kbuf[r + tq:] = kn_ref[0, 0]
    vbuf[0:r] = vp_ref[0, 0]
    vbuf[r:r + tq] = vc_ref[0, 0]
    vbuf[r + tq:] = vn_ref[0, 0]
    qb = 2 * r
    lane = lax.broadcasted_iota(jnp.int32, (qb, LANES), 1)
    low = lane < HEAD_DIM
    tidx = lax.broadcasted_iota(jnp.int32, (qb, 2 * qb), 0)
    kidx = lax.broadcasted_iota(jnp.int32, (qb, 2 * qb), 1)
    rel = kidx - tidx
    band = (rel >= 0) & (rel <= 2 * r)
    for j in range(tq // qb):
        kpos = i * tq + (j * qb - r) + kidx
        mask = band & (kpos >= 0) & (kpos < seq)
        q = q_ref[0, 0, j * qb:(j + 1) * qb, :]
        k = kbuf[j * qb:(j + 2) * qb, :]
        v = vbuf[j * qb:(j + 2) * qb, :]
        lse_tile = jnp.zeros((qb, LANES), F32)
        o_chunks = []
        for c in range(GROUP_WIDTH // LANES):
            qc = q[:, c * LANES:(c + 1) * LANES]
            kc = k[:, c * LANES:(c + 1) * LANES]
            vc = v[:, c * LANES:(c + 1) * LANES]
            outs = []
            for hh in range(2):
                head = 2 * c + hh
                if head >= HEADS_PER_GROUP:
                    outs.append(jnp.zeros((qb, LANES), F32))
                    continue
                qm = jnp.where(low if hh == 0 else jnp.logical_not(low), qc, jnp.zeros_like(qc))
                s = lax.dot_general(qm, kc, (((1,), (1,)), ((), ())), preferred_element_type=F32)
                s = jnp.where(mask, s, NEG_INF)
                m = jnp.max(s, axis=-1, keepdims=True)
                p = jnp.exp(s - m)
                den = jnp.sum(p, axis=-1, keepdims=True)
                o = jnp.dot(p.astype(BF16), vc, preferred_element_type=F32) / den
                lse_tile = jnp.where(lane == head, m + jnp.log(den), lse_tile)
                outs.append(o)
            o_chunks.append(jnp.where(low, outs[0], outs[1]))
        o_ref[0, 0, j * qb:(j + 1) * qb, :] = jnp.concatenate(o_chunks, axis=1).astype(BF16)
        lse_ref[0, 0, j * qb:(j + 1) * qb, :] = lse_tile


def _attention(q, k, v, tq=512):
    b, d, seq, w = q.shape
    r = BAND_RADIUS
    nb = seq // r
    cur = pl.BlockSpec((1, 1, tq, w), lambda bi, ri, i: (bi, ri, i, 0))
    prev = pl.BlockSpec((1, 1, r, w), lambda bi, ri, i: (bi, ri, jnp.maximum(i * (tq // r) - 1, 0), 0))
    nxt = pl.BlockSpec((1, 1, r, w), lambda bi, ri, i: (bi, ri, jnp.minimum((i + 1) * (tq // r), nb - 1), 0))
    return pl.pallas_call(
        functools.partial(_attn_kernel, seq=seq, tq=tq),
        grid=(b, d, seq // tq),
        in_specs=[cur, prev, cur, nxt, prev, cur, nxt],
        out_specs=[pl.BlockSpec((1, 1, tq, w), lambda bi, ri, i: (bi, ri, i, 0)),
                   pl.BlockSpec((1, 1, tq, LANES), lambda bi, ri, i: (bi, ri, i, 0))],
        out_shape=[jax.ShapeDtypeStruct((b, d, seq, w), BF16),
                   jax.ShapeDtypeStruct((b, d, seq, LANES), F32)],
        scratch_shapes=[pltpu.VMEM((tq + 2 * r, w), BF16), pltpu.VMEM((tq + 2 * r, w), BF16)],
        compiler_params=_cparams(("arbitrary", "arbitrary", "arbitrary")),
        name=f"band_attn_d{d}",
    )(q, k, k, k, v, v, v)


def _attn_out_kernel(o0, o1, o2, l0, l1, l2, x_ref, wo_ref, exp_ref, gt_ref, gain_ref, sh_ref, sc_ref,
                     wr_ref, br_ref, x1_ref, hf_ref, gate_ref, oscr, lscr, *, tm):
    n_slab = GROUP_WIDTH // LANES
    for g, (d, o_ref, l_ref) in enumerate(zip(DILATIONS, (o0, o1, o2), (l0, l1, l2))):
        rows = tm // d
        for r in range(d):
            dst = slice(None) if d == 1 else pl.ds(r, rows, stride=d)
            blk = o_ref[0, r].astype(F32)
            for c in range(n_slab):
                oscr[g * n_slab + c, dst, :] = blk[:, c * LANES:(c + 1) * LANES]
            lscr[g, dst, :] = l_ref[0, r]
    lse = [lscr[g] for g in range(N_ATTN_GROUPS)]
    mx = jnp.maximum(jnp.maximum(lse[0], lse[1]), lse[2])
    ex = [jnp.exp(l - mx) for l in lse]
    tot = ex[0] + ex[1] + ex[2]
    y = jnp.zeros((tm, D_MODEL), F32)
    for g in range(N_ATTN_GROUPS):
        alpha = ex[g] / tot
        a_wide = jnp.dot(alpha, exp_ref[...], preferred_element_type=F32,
                         precision=lax.Precision.HIGHEST)
        og = jnp.concatenate([oscr[g * n_slab + c] for c in range(n_slab)], axis=1)
        y = y + jnp.dot((og * a_wide).astype(BF16), wo_ref[g], preferred_element_type=F32)
    x1 = x_ref[0] + gt_ref[0, 0] * y
    x1_ref[0] = x1
    hf, gate = _ffn_prep(x1, gain_ref[...], sh_ref[0, 0], sc_ref[0, 0], wr_ref, br_ref)
    hf_ref[0] = hf
    gate_ref[0] = gate


def _mod_spec():
    return pl.BlockSpec((1, 1, 1, D_MODEL), lambda bi, i: (bi, 0, 0, 0))


def _attn_out(o_list, l_list, x, wo_groups, expand, gt, gain, shift, scale, wr_t, br, tm=512):
    b, s, _ = x.shape
    o_specs = [pl.BlockSpec((1, d, tm // d, GROUP_WIDTH), lambda bi, i: (bi, 0, i, 0)) for d in DILATIONS]
    l_specs = [pl.BlockSpec((1, d, tm // d, LANES), lambda bi, i: (bi, 0, i, 0)) for d in DILATIONS]
    tok = lambda w: pl.BlockSpec((1, tm, w), lambda bi, i: (bi, i, 0))
    return pl.pallas_call(
        functools.partial(_attn_out_kernel, tm=tm),
        grid=(b, s // tm),
        in_specs=o_specs + l_specs + [
            tok(D_MODEL),
            _const_spec((N_ATTN_GROUPS, GROUP_WIDTH, D_MODEL), lambda bi, i: (0, 0, 0)),
            _const_spec((LANES, GROUP_WIDTH), lambda bi, i: (0, 0)),
            _mod_spec(),
            _const_spec((1, D_MODEL), lambda bi, i: (0, 0)),
            _mod_spec(), _mod_spec(),
            _const_spec((ROUTER_ROWS, D_MODEL), lambda bi, i: (0, 0)),
            _const_spec((ROUTER_ROWS, 1), lambda bi, i: (0, 0))],
        out_specs=[tok(D_MODEL), tok(D_MODEL), tok(LANES)],
        out_shape=[jax.ShapeDtypeStruct((b, s, D_MODEL), F32),
                   jax.ShapeDtypeStruct((b, s, D_MODEL), BF16),
                   jax.ShapeDtypeStruct((b, s, LANES), F32)],
        scratch_shapes=[pltpu.VMEM((N_ATTN_GROUPS * GROUP_WIDTH // LANES, tm, LANES), F32),
                        pltpu.VMEM((N_ATTN_GROUPS, tm, LANES), F32)],
        compiler_params=_cparams(("arbitrary", "arbitrary")),
        name="attn_merge_proj",
    )(*o_list, *l_list, x, wo_groups, expand, gt, gain, shift, scale, wr_t, br)


def _gmlp_kernel(x_ref, gain_ref, sh_ref, sc_ref, win_ref, vg_ref, ws_ref, bs_ref, wo_ref, gt_ref,
                 fgain_ref, fsh_ref, fsc_ref, wr_ref, br_ref, x1_ref, hf_ref, gate_ref, gated_scr, *, tm):
    x = x_ref[0]
    hn = _norm_modulate(x, gain_ref[...], sh_ref[0, 0], sc_ref[0, 0]).astype(BF16)
    u = _gelu(jnp.dot(hn, win_ref[:, :GMLP_HALF], preferred_element_type=F32))
    v = _gelu(jnp.dot(hn, win_ref[:, GMLP_HALF:], preferred_element_type=F32))
    mu = jnp.mean(v, axis=-1, keepdims=True)
    vc = v - mu
    vn = (vc * lax.rsqrt(jnp.mean(vc * vc, axis=-1, keepdims=True) + EPS) * vg_ref[...]).astype(BF16)
    gw = GMLP_HALF // GMLP_GROUPS
    for c in range(tm // CHUNK):
        rs = slice(c * CHUNK, (c + 1) * CHUNK)
        for g in range(GMLP_GROUPS):
            cs = slice(g * gw, (g + 1) * gw)
            vs = jnp.dot(ws_ref[g], vn[rs, cs], preferred_element_type=F32) + bs_ref[:, g:g + 1]
            gated_scr[rs, cs] = (u[rs, cs] * vs).astype(BF16)
    y = jnp.dot(gated_scr[...], wo_ref[...], preferred_element_type=F32)
    x1 = x + gt_ref[0, 0] * y
    x1_ref[0] = x1
    hf, gate = _ffn_prep(x1, fgain_ref[...], fsh_ref[0, 0], fsc_ref[0, 0], wr_ref, br_ref)
    hf_ref[0] = hf
    gate_ref[0] = gate


def _gmlp(x, gain, shift, scale, w_in, v_gain, w_s, b_s_t, w_o, gt, fgain, fshift, fscale, wr_t, br, tm=256):
    b, s, _ = x.shape
    tok = lambda w: pl.BlockSpec((1, tm, w), lambda bi, i: (bi, i, 0))
    c2 = lambda shape: _const_spec(shape, lambda bi, i: (0,) * len(shape))
    return pl.pallas_call(
        functools.partial(_gmlp_kernel, tm=tm),
        grid=(b, s // tm),
        in_specs=[tok(D_MODEL), c2((1, D_MODEL)), _mod_spec(), _mod_spec(),
                  c2((D_MODEL, 2 * GMLP_HALF)), c2((1, GMLP_HALF)),
                  c2((GMLP_GROUPS, CHUNK, CHUNK)), c2((CHUNK, GMLP_GROUPS)),
                  c2((GMLP_HALF, D_MODEL)), _mod_spec(),
                  c2((1, D_MODEL)), _mod_spec(), _mod_spec(),
                  c2((ROUTER_ROWS, D_MODEL)), c2((ROUTER_ROWS, 1))],
        out_specs=[tok(D_MODEL), tok(D_MODEL), tok(LANES)],
        out_shape=[jax.ShapeDtypeStruct((b, s, D_MODEL), F32),
                   jax.ShapeDtypeStruct((b, s, D_MODEL), BF16),
                   jax.ShapeDtypeStruct((b, s, LANES), F32)],
        scratch_shapes=[pltpu.VMEM((tm, GMLP_HALF), BF16)],
        compiler_params=_cparams(("arbitrary", "arbitrary")),
        name="gmlp",
    )(x, gain, shift, scale, w_in, v_gain, w_s, b_s_t, w_o, gt, fgain, fshift, fscale, wr_t, br)


def _moe_kernel(h_ref, gate_ref, wg_ref, wu_ref, wd_ref, x_ref, gt_ref, fin_ref, o_ref, acc, *, final_norm):
    e = pl.program_id(1)

    @pl.when(e == 0)
    def _():
        acc[...] = jnp.zeros_like(acc)

    h = h_ref[...]
    lane = lax.broadcasted_iota(jnp.int32, gate_ref.shape, 1)
    gcol = jnp.sum(jnp.where(lane == e, gate_ref[...], 0.0), axis=1, keepdims=True)
    hg = jnp.dot(h, wg_ref[0], preferred_element_type=F32)
    hu = jnp.dot(h, wu_ref[0], preferred_element_type=F32)
    act = (_silu(hg) * hu * gcol).astype(BF16)
    acc[...] += jnp.dot(act, wd_ref[0], preferred_element_type=F32)

    @pl.when(e == N_EXPERTS - 1)
    def _():
        x2 = x_ref[...] + gt_ref[0, 0] * acc[...]
        if final_norm:
            ms = jnp.mean(x2 * x2, axis=-1, keepdims=True)
            x2 = x2 * lax.rsqrt(ms + EPS) * fin_ref[...]
        o_ref[...] = x2


def _moe(h, gate, wg, wu, wd, x, gt, fin_gain, seq, final_norm, tm=1024):
    t = h.shape[0]
    tiles_per_batch = seq // tm
    return pl.pallas_call(
        functools.partial(_moe_kernel, final_norm=final_norm),
        grid=(t // tm, N_EXPERTS),
        in_specs=[pl.BlockSpec((tm, D_MODEL), lambda i, e: (i, 0)),
                  pl.BlockSpec((tm, LANES), lambda i, e: (i, 0)),
                  pl.BlockSpec((1, D_MODEL, D_EXPERT), lambda i, e: (e, 0, 0)),
                  pl.BlockSpec((1, D_MODEL, D_EXPERT), lambda i, e: (e, 0, 0)),
                  pl.BlockSpec((1, D_EXPERT, D_MODEL), lambda i, e: (e, 0, 0)),
                  pl.BlockSpec((tm, D_MODEL), lambda i, e: (i, 0)),
                  pl.BlockSpec((1, 1, 1, D_MODEL), lambda i, e: (i // tiles_per_batch, 0, 0, 0)),
                  pl.BlockSpec((1, D_MODEL), lambda i, e: (0, 0))],
        out_specs=pl.BlockSpec((tm, D_MODEL), lambda i, e: (i, 0)),
        out_shape=jax.ShapeDtypeStruct((t, D_MODEL), F32),
        scratch_shapes=[pltpu.VMEM((tm, D_MODEL), F32)],
        compiler_params=_cparams(("arbitrary", "arbitrary")),
        name="moe_dense",
    )(h, gate, wg, wu, wd, x, gt, fin_gain)


def _rope_tables(seq):
    inv_freq = ROPE_THETA ** (-jnp.arange(0, ROT_DIM, 2, dtype=F32) / ROT_DIM)
    ang = jnp.arange(seq).astype(F32)[:, None] * inv_freq[None, :]
    cos, sin = jnp.cos(ang), jnp.sin(ang)
    ones = jnp.ones((seq, HEAD_DIM - ROT_DIM), F32)
    cos_h = jnp.concatenate([cos, cos, ones], axis=1)
    sgn_h = jnp.concatenate([-sin, sin, 0.0 * ones], axis=1)
    reps = LANES // HEAD_DIM
    return jnp.tile(cos_h, (1, reps)), jnp.tile(sgn_h, (1, reps))


def _pad_heads(w, axis):
    pad = [(0, 0)] * w.ndim
    pad[axis] = (0, GROUP_WIDTH - HEADS_PER_GROUP * HEAD_DIM)
    return jnp.pad(w, pad)


def _router_weights(w_group, b_group, w_expert, b_expert):
    we = jnp.transpose(w_expert, (1, 0, 2)).reshape(D_MODEL, N_EXPERTS)
    w = jnp.concatenate([w_group, we], axis=1)
    w = jnp.pad(w, ((0, 0), (0, ROUTER_ROWS - w.shape[1])))
    bias = jnp.pad(jnp.concatenate([b_group, b_expert.reshape(-1)]), (0, ROUTER_ROWS - 20))
    return w.T.astype(BF16), bias.reshape(ROUTER_ROWS, 1)


def kernel(x, c, norm_mix, norm_ffn, w_ada, b_ada, a_w_qkv, a_w_o, b_w_in, b_v_gain, b_w_s, b_b_s, b_w_o,
           r_w_group, r_b_group, r_w_expert, r_b_expert, e_w_gate, e_w_up, e_w_down, final_norm):
    b, s, _ = x.shape
    mod = _ada_mod(c, w_ada, b_ada)
    sh_m, sc_m, gt_m, sh_f, sc_f, gt_f = [mod[:, :, i:i + 1] for i in range(6)]
    row = lambda v: v.reshape(1, -1)
    gw = HEADS_PER_GROUP * HEAD_DIM

    wq, wk, wv = [a_w_qkv[0][:, i * 3 * gw:(i + 1) * 3 * gw] for i in range(3)]
    wq = wq * (HEAD_DIM ** -0.5)
    w_groups = jnp.stack([
        jnp.concatenate([_pad_heads(w[:, g * gw:(g + 1) * gw], 1) for w in (wq, wk, wv)], axis=1)
        for g in range(N_ATTN_GROUPS)]).astype(BF16)
    wo_groups = jnp.stack([_pad_heads(a_w_o[0][g * gw:(g + 1) * gw], 0)
                           for g in range(N_ATTN_GROUPS)]).astype(BF16)
    expand = (jnp.arange(LANES)[:, None] == jnp.arange(GROUP_WIDTH)[None, :] // HEAD_DIM).astype(F32)
    cos_t, sgn_t = _rope_tables(s)
    qkv = _qkv(x, row(norm_mix[0]), sh_m[0], sc_m[0], w_groups, cos_t, sgn_t)
    o_list, l_list = [], []
    for g in range(N_ATTN_GROUPS):
        o, l = _attention(*qkv[3 * g:3 * g + 3])
        o_list.append(o)
        l_list.append(l)
    wr_t, br = _router_weights(r_w_group[0], r_b_group[0], r_w_expert[0], r_b_expert[0])
    x1, hf, gate = _attn_out(o_list, l_list, x, wo_groups, expand, gt_m[0], row(norm_ffn[0]),
                             sh_f[0], sc_f[0], wr_t, br)
    t = b * s
    x2 = _moe(hf.reshape(t, D_MODEL), gate.reshape(t, LANES), e_w_gate[0].astype(BF16),
              e_w_up[0].astype(BF16), e_w_down[0].astype(BF16), x1.reshape(t, D_MODEL), gt_f[0],
              row(final_norm), s, final_norm=False).reshape(b, s, D_MODEL)

    wr_t, br = _router_weights(r_w_group[1], r_b_group[1], r_w_expert[1], r_b_expert[1])
    x3, hf, gate = _gmlp(x2, row(norm_mix[1]), sh_m[1], sc_m[1], b_w_in[0].astype(BF16), row(b_v_gain[0]),
                         b_w_s[0].astype(BF16), b_b_s[0].T, b_w_o[0].astype(BF16), gt_m[1],
                         row(norm_ffn[1]), sh_f[1], sc_f[1], wr_t, br)
    out = _moe(hf.reshape(t, D_MODEL), gate.reshape(t, LANES), e_w_gate[1].astype(BF16),
               e_w_up[1].astype(BF16), e_w_down[1].astype(BF16), x3.reshape(t, D_MODEL), gt_f[1],
               row(final_norm), s, final_norm=True)
    return out.reshape(b, s, D_MODEL)
```

```python
import functools

import jax
import jax.numpy as jnp
import numpy as np
from jax import lax
from jax.experimental import pallas as pl
from jax.experimental.pallas import tpu as pltpu

D_MODEL = 1024
DEPTH = 2
HEAD_DIM = 64
HEADS_PER_GROUP = 5
N_ATTN_GROUPS = 3
DILATIONS = (1, 4, 16)
BAND_RADIUS = 64
ROT_DIM = HEAD_DIM // 4
ROPE_THETA = 500000.0
NEG_INF = -1e30
CHUNK = 128
GMLP_HALF = 2 * D_MODEL
GMLP_GROUPS = 8
N_EXPERT_GROUPS = 4
EXPERTS_PER_GROUP = 4
N_EXPERTS = 16
D_EXPERT = 256
EPS = 1e-6

LANES = 128
GROUP_WIDTH = 384
ROUTER_ROWS = 32
PAIRS_PER_GROUP = 6
N_CLASSES = N_EXPERT_GROUPS * PAIRS_PER_GROUP
CLASS_ROWS = 32
HALF_WORDS = D_MODEL // 2
SUBLANES = 8
TOKEN_ROWS = SUBLANES
PACKED_SLABS = HALF_WORDS // LANES
MOE_TILE = 256

F32 = jnp.float32
BF16 = jnp.bfloat16
VMEM_LIMIT = 56 * 1024 * 1024


def _cparams(sem):
    return pltpu.CompilerParams(dimension_semantics=sem, vmem_limit_bytes=VMEM_LIMIT)


def _const_spec(shape, index_map):
    return pl.BlockSpec(shape, index_map, pipeline_mode=pl.Buffered(1))


def _silu(x):
    return x * (1.0 / (1.0 + jnp.exp(-x)))


def _gelu(x):
    return 0.5 * x * (1.0 + lax.erf(x * (2.0 ** -0.5)))


def _norm_modulate(x, gain, shift, scale):
    ms = jnp.mean(x * x, axis=-1, keepdims=True)
    return (x * lax.rsqrt(ms + EPS) * gain) * (1.0 + scale) + shift


def _ada_kernel(c_ref, w_ref, b_ref, o_ref):
    ca = _silu(c_ref[...])
    o_ref[0] = jnp.dot(ca, w_ref[0], preferred_element_type=F32,
                       precision=lax.Precision.HIGHEST) + b_ref[0]


def _ada_mod(c, w_ada, b_ada):
    b = c.shape[0]
    tn = 1536
    c8 = jnp.pad(c, ((0, 8 - b), (0, 0)))
    out = pl.pallas_call(
        _ada_kernel,
        grid=(DEPTH, 6 * D_MODEL // tn),
        in_specs=[pl.BlockSpec((8, D_MODEL), lambda l, j: (0, 0)),
                  pl.BlockSpec((1, D_MODEL, tn), lambda l, j: (l, 0, j)),
                  pl.BlockSpec((1, 1, tn), lambda l, j: (l, 0, j))],
        out_specs=pl.BlockSpec((1, 8, tn), lambda l, j: (l, 0, j)),
        out_shape=jax.ShapeDtypeStruct((DEPTH, 8, 6 * D_MODEL), F32),
        compiler_params=_cparams(("arbitrary", "arbitrary")),
        name="ada_mod",
    )(c8, w_ada, b_ada.reshape(DEPTH, 1, 6 * D_MODEL))
    return out[:, :b].reshape(DEPTH, b, 6, 1, D_MODEL)


def _pack_bf16_pairs(x):
    n = x.shape[1] // 2
    lo = lax.bitcast_convert_type(x[:, :n].astype(BF16).astype(F32), jnp.uint32)
    hi = lax.bitcast_convert_type(x[:, n:].astype(BF16).astype(F32), jnp.uint32)
    return (lo >> 16) | (hi & jnp.uint32(0xFFFF0000))


def _unpack_bf16_pairs(w):
    lo = lax.bitcast_convert_type(w << 16, F32)
    hi = lax.bitcast_convert_type(w & jnp.uint32(0xFFFF0000), F32)
    return jnp.concatenate([lo, hi], axis=1)


def _ffn_prep(x1, gain, shift, scale, wr_ref, br_ref, carry_ref, first_step):
    tm = x1.shape[0]
    hf32 = _norm_modulate(x1, gain, shift, scale)
    hf = hf32.astype(BF16)
    lt = lax.dot_general(wr_ref[...], hf, (((1,), (1,)), ((), ())),
                         preferred_element_type=F32) + br_ref[...]
    gl = [lt[i:i + 1, :] for i in range(N_EXPERT_GROUPS)]
    gmax = jnp.maximum(jnp.maximum(gl[0], gl[1]), jnp.maximum(gl[2], gl[3]))
    ge = [jnp.exp(g - gmax) for g in gl]
    gsum = ge[0] + ge[1] + ge[2] + ge[3]
    gp = [e / gsum for e in ge]
    g_val = jnp.maximum(jnp.maximum(gp[0], gp[1]), jnp.maximum(gp[2], gp[3]))
    g_idx = jnp.where(gp[0] == g_val, 0, jnp.where(gp[1] == g_val, 1, jnp.where(gp[2] == g_val, 2, 3)))
    el = []
    for j in range(EXPERTS_PER_GROUP):
        rows = [lt[4 + 4 * g + j:5 + 4 * g + j, :] for g in range(N_EXPERT_GROUPS)]
        el.append(jnp.where(g_idx == 0, rows[0], jnp.where(g_idx == 1, rows[1],
                  jnp.where(g_idx == 2, rows[2], rows[3]))))
    v1 = jnp.maximum(jnp.maximum(el[0], el[1]), jnp.maximum(el[2], el[3]))
    i1 = jnp.where(el[0] == v1, 0, jnp.where(el[1] == v1, 1, jnp.where(el[2] == v1, 2, 3)))
    el2 = [jnp.where(i1 == j, -jnp.inf, el[j]) for j in range(EXPERTS_PER_GROUP)]
    v2 = jnp.maximum(jnp.maximum(el2[0], el2[1]), jnp.maximum(el2[2], el2[3]))
    i2 = jnp.where((el2[0] == v2) & (i1 != 0), 0,
                   jnp.where((el2[1] == v2) & (i1 != 1), 1,
                             jnp.where((el2[2] == v2) & (i1 != 2), 2, 3)))
    e21 = jnp.exp(v2 - v1)
    den = 1.0 + e21
    w1 = g_val * (1.0 / den)
    w2 = g_val * (e21 / den)
    swap = i2 < i1
    a = jnp.where(swap, i2, i1)
    bb = jnp.where(swap, i1, i2)
    w_a = jnp.where(swap, w2, w1)
    w_b = jnp.where(swap, w1, w2)
    cls = g_idx * PAIRS_PER_GROUP + ((a * (7 - a)) >> 1) + (bb - a - 1)
    row = lax.broadcasted_iota(jnp.int32, (LANES, tm), 0)
    extras_t = jnp.where(row == 0, w_a, jnp.where(row == 1, w_b, 0.0))
    extras = lax.bitcast_convert_type(extras_t.T, jnp.uint32)
    packed = _pack_bf16_pairs(hf32)

    @pl.when(first_step)
    def _():
        carry_ref[...] = jnp.zeros_like(carry_ref)

    crow = lax.broadcasted_iota(jnp.int32, (CLASS_ROWS, tm), 0)
    onehot = (crow == cls).astype(F32)
    si = lax.broadcasted_iota(jnp.int32, (tm, tm), 0)
    ti = lax.broadcasted_iota(jnp.int32, (tm, tm), 1)
    before = (si < ti).astype(BF16)
    prefix = jnp.dot(onehot.astype(BF16), before, preferred_element_type=F32)
    carry = carry_ref[:, 0:1]
    rank = jnp.sum(onehot * (prefix + carry), axis=0, keepdims=True).astype(jnp.int32)
    carry_ref[...] = carry_ref[...] + jnp.sum(onehot, axis=1, keepdims=True)
    return packed, extras, cls, rank


def _rope(t, cos, sgn, first_half):
    out = []
    for c in range(GROUP_WIDTH // LANES):
        xc = t[:, c * LANES:(c + 1) * LANES]
        other = jnp.where(first_half, pltpu.roll(xc, LANES - ROT_DIM // 2, 1),
                          pltpu.roll(xc, ROT_DIM // 2, 1))
        out.append(xc * cos + other * sgn)
    return jnp.concatenate(out, axis=1)


def _qkv_kernel(x_ref, gain_ref, sh_ref, sc_ref, w_ref, cos_ref, sgn_ref, *rest, tm):
    out_refs, hn_scr = rest[:9], rest[9]
    hn = _norm_modulate(x_ref[0], gain_ref[...], sh_ref[0, 0], sc_ref[0, 0])
    n_slab = D_MODEL // LANES
    lane = lax.broadcasted_iota(jnp.int32, (tm, LANES), 1) % HEAD_DIM
    first_half = lane < ROT_DIM // 2
    for g, d in enumerate(DILATIONS):
        rows = tm // d
        if d == 1:
            hp, cos, sgn = hn, cos_ref[...], sgn_ref[...]
        else:
            if g == 1:
                for s in range(n_slab):
                    hn_scr[s] = hn[:, s * LANES:(s + 1) * LANES]
            hp = jnp.concatenate(
                [jnp.concatenate([hn_scr[s, pl.ds(r, rows, stride=d), :] for s in range(n_slab)], axis=1)
                 for r in range(d)], axis=0)
            cos = jnp.concatenate([cos_ref[pl.ds(r, rows, stride=d), :] for r in range(d)], axis=0)
            sgn = jnp.concatenate([sgn_ref[pl.ds(r, rows, stride=d), :] for r in range(d)], axis=0)
        res = jnp.dot(hp.astype(BF16), w_ref[g], preferred_element_type=F32)
        q = _rope(res[:, :GROUP_WIDTH], cos, sgn, first_half)
        k = _rope(res[:, GROUP_WIDTH:2 * GROUP_WIDTH], cos, sgn, first_half)
        v = res[:, 2 * GROUP_WIDTH:]
        for t, o_ref in zip((q, k, v), out_refs[3 * g:3 * g + 3]):
            o_ref[0] = t.astype(BF16).reshape(d, rows, GROUP_WIDTH)


def _qkv(x, gain, shift, scale, w_groups, cos_t, sgn_t, tm=512):
    b, s, _ = x.shape
    out_shape, out_specs = [], []
    for d in DILATIONS:
        for _ in range(3):
            out_shape.append(jax.ShapeDtypeStruct((b, d, s // d, GROUP_WIDTH), BF16))
            out_specs.append(pl.BlockSpec((1, d, tm // d, GROUP_WIDTH), lambda bi, i: (bi, 0, i, 0)))
    return pl.pallas_call(
        functools.partial(_qkv_kernel, tm=tm),
        grid=(b, s // tm),
        in_specs=[pl.BlockSpec((1, tm, D_MODEL), lambda bi, i: (bi, i, 0)),
                  _const_spec((1, D_MODEL), lambda bi, i: (0, 0)),
                  pl.BlockSpec((1, 1, 1, D_MODEL), lambda bi, i: (bi, 0, 0, 0)),
                  pl.BlockSpec((1, 1, 1, D_MODEL), lambda bi, i: (bi, 0, 0, 0)),
                  _const_spec((N_ATTN_GROUPS, D_MODEL, 3 * GROUP_WIDTH), lambda bi, i: (0, 0, 0)),
                  pl.BlockSpec((tm, LANES), lambda bi, i: (i, 0)),
                  pl.BlockSpec((tm, LANES), lambda bi, i: (i, 0))],
        out_specs=out_specs,
        out_shape=out_shape,
        scratch_shapes=[pltpu.VMEM((D_MODEL // LANES, tm, LANES), F32)],
        compiler_params=_cparams(("arbitrary", "arbitrary")),
        name="qkv_rope",
    )(x, gain, shift, scale, w_groups, cos_t, sgn_t)


def _attn_kernel(q_ref, kp_ref, kc_ref, kn_ref, vp_ref, vc_ref, vn_ref, o_ref, lse_ref,
                 kbuf, vbuf, *, seq, tq):
    i = pl.program_id(2)
    r = BAND_RADIUS
    kbuf[0:r] = kp_ref[0, 0]
    kbuf[r:r + tq] = kc_ref[0, 0]
    kbuf[r + tq:] = kn_ref[0, 0]
    vbuf[0:r] = vp_ref[0, 0]
    vbuf[r:r + tq] = vc_ref[0, 0]
    vbuf[r + tq:] = vn_ref[0, 0]
    qb = 2 * r
    lane = lax.broadcasted_iota(jnp.int32, (qb, LANES), 1)
    low = lane < HEAD_DIM
    tidx = lax.broadcasted_iota(jnp.int32, (qb, 2 * qb), 0)
    kidx = lax.broadcasted_iota(jnp.int32, (qb, 2 * qb), 1)
    rel = kidx - tidx
    band = (rel >= 0) & (rel <= 2 * r)
    for j in range(tq // qb):
        kpos = i * tq + (j * qb - r) + kidx
        mask = band & (kpos >= 0) & (kpos < seq)
        q = q_ref[0, 0, j * qb:(j + 1) * qb, :]
        k = kbuf[j * qb:(j + 2) * qb, :]
        v = vbuf[j * qb:(j + 2) * qb, :]
        lse_tile = jnp.zeros((qb, LANES), F32)
        o_chunks = []
        for c in range(GROUP_WIDTH // LANES):
            qc = q[:, c * LANES:(c + 1) * LANES]
            kc = k[:, c * LANES:(c + 1) * LANES]
            vc = v[:, c * LANES:(c + 1) * LANES]
            outs = []
            for hh in range(2):
                head = 2 * c + hh
                if head >= HEADS_PER_GROUP:
                    outs.append(jnp.zeros((qb, LANES), F32))
                    continue
                qm = jnp.where(low if hh == 0 else jnp.logical_not(low), qc, jnp.zeros_like(qc))
                s = lax.dot_general(qm, kc, (((1,), (1,)), ((), ())), preferred_element_type=F32)
                s = jnp.where(mask, s, NEG_INF)
                m = jnp.max(s, axis=-1, keepdims=True)
                p = jnp.exp(s - m)
                den = jnp.sum(p, axis=-1, keepdims=True)
                o = jnp.dot(p.astype(BF16), vc, preferred_element_type=F32) / den
                lse_tile = jnp.where(lane == head, m + jnp.log(den), lse_tile)
                outs.append(o)
            o_chunks.append(jnp.where(low, outs[0], outs[1]))
        o_ref[0, 0, j * qb:(j + 1) * qb, :] = jnp.concatenate(o_chunks, axis=1).astype(BF16)
        lse_ref[0, 0, j * qb:(j + 1) * qb, :] = lse_tile


def _attention(q, k, v, tq=512):
    b, d, seq, w = q.shape
    r = BAND_RADIUS
    nb = seq // r
    cur = pl.BlockSpec((1, 1, tq, w), lambda bi, ri, i: (bi, ri, i, 0))
    prev = pl.BlockSpec((1, 1, r, w), lambda bi, ri, i: (bi, ri, jnp.maximum(i * (tq // r) - 1, 0), 0))
    nxt = pl.BlockSpec((1, 1, r, w), lambda bi, ri, i: (bi, ri, jnp.minimum((i + 1) * (tq // r), nb - 1), 0))
    return pl.pallas_call(
        functools.partial(_attn_kernel, seq=seq, tq=tq),
        grid=(b, d, seq // tq),
        in_specs=[cur, prev, cur, nxt, prev, cur, nxt],
        out_specs=[pl.BlockSpec((1, 1, tq, w), lambda bi, ri, i: (bi, ri, i, 0)),
                   pl.BlockSpec((1, 1, tq, LANES), lambda bi, ri, i: (bi, ri, i, 0))],
        out_shape=[jax.ShapeDtypeStruct((b, d, seq, w), BF16),
                   jax.ShapeDtypeStruct((b, d, seq, LANES), F32)],
        scratch_shapes=[pltpu.VMEM((tq + 2 * r, w), BF16), pltpu.VMEM((tq + 2 * r, w), BF16)],
        compiler_params=_cparams(("arbitrary", "arbitrary", "arbitrary")),
        name=f"band_attn_d{d}",
    )(q, k, k, k, v, v, v)


def _attn_out_kernel(o0, o1, o2, l0, l1, l2, x_ref, wo_ref, exp_ref, gt_ref, gain_ref, sh_ref, sc_ref,
                     wr_ref, br_ref, x1_ref, rows_ref, route_ref, cnt_ref, oscr, lscr, carry_ref, *, tm):
    n_slab = GROUP_WIDTH // LANES
    for g, (d, o_ref, l_ref) in enumerate(zip(DILATIONS, (o0, o1, o2), (l0, l1, l2))):
        rows = tm // d
        for r in range(d):
            dst = slice(None) if d == 1 else pl.ds(r, rows, stride=d)
            blk = o_ref[0, r].astype(F32)
            for c in range(n_slab):
                oscr[g * n_slab + c, dst, :] = blk[:, c * LANES:(c + 1) * LANES]
            lscr[g, dst, :] = l_ref[0, r]
    lse = [lscr[g] for g in range(N_ATTN_GROUPS)]
    mx = jnp.maximum(jnp.maximum(lse[0], lse[1]), lse[2])
    ex = [jnp.exp(l - mx) for l in lse]
    tot = ex[0] + ex[1] + ex[2]
    y = jnp.zeros((tm, D_MODEL), F32)
    for g in range(N_ATTN_GROUPS):
        alpha = ex[g] / tot
        a_wide = jnp.dot(alpha, exp_ref[...], preferred_element_type=F32,
                         precision=lax.Precision.HIGHEST)
        og = jnp.concatenate([oscr[g * n_slab + c] for c in range(n_slab)], axis=1)
        y = y + jnp.dot((og * a_wide).astype(BF16), wo_ref[g], preferred_element_type=F32)
    x1 = x_ref[0] + gt_ref[0, 0] * y
    x1_ref[0] = x1
    _route_outputs(x1, gain_ref, sh_ref, sc_ref, wr_ref, br_ref, carry_ref, rows_ref, route_ref, cnt_ref)


def _route_outputs(x1, gain_ref, sh_ref, sc_ref, wr_ref, br_ref, carry_ref, rows_ref, route_ref, cnt_ref):
    first = (pl.program_id(0) == 0) & (pl.program_id(1) == 0)
    packed, extras, cls, rank = _ffn_prep(x1, gain_ref[...], sh_ref[0, 0], sc_ref[0, 0], wr_ref, br_ref,
                                          carry_ref, first)
    tm = x1.shape[0]
    for s in range(PACKED_SLABS):
        rows_ref[pl.ds(s, tm, stride=TOKEN_ROWS), :] = packed[:, s * LANES:(s + 1) * LANES]
    rows_ref[pl.ds(PACKED_SLABS, tm, stride=TOKEN_ROWS), :] = extras
    for s in range(PACKED_SLABS + 1, TOKEN_ROWS):
        rows_ref[pl.ds(s, tm, stride=TOKEN_ROWS), :] = jnp.zeros((tm, LANES), jnp.uint32)
    r8 = lax.broadcasted_iota(jnp.int32, (SUBLANES, tm), 0)
    route_ref[0, 0] = jnp.where(r8 == 0, cls, jnp.where(r8 == 1, rank, 0))
    cnt_ref[...] = carry_ref[...]


def _route_out_specs(b, s, tm):
    specs = [pl.BlockSpec((1, tm, D_MODEL), lambda bi, i: (bi, i, 0)),
             pl.BlockSpec((tm * TOKEN_ROWS, LANES), lambda bi, i: (bi * (s // tm) + i, 0)),
             pl.BlockSpec((1, 1, SUBLANES, tm), lambda bi, i: (bi, i, 0, 0)),
             pl.BlockSpec((CLASS_ROWS, LANES), lambda bi, i: (0, 0))]
    shapes = [jax.ShapeDtypeStruct((b, s, D_MODEL), F32),
              jax.ShapeDtypeStruct((b * s * TOKEN_ROWS, LANES), jnp.uint32),
              jax.ShapeDtypeStruct((b, s // tm, SUBLANES, tm), jnp.int32),
              jax.ShapeDtypeStruct((CLASS_ROWS, LANES), F32)]
    return specs, shapes


def _mod_spec():
    return pl.BlockSpec((1, 1, 1, D_MODEL), lambda bi, i: (bi, 0, 0, 0))


def _attn_out(o_list, l_list, x, wo_groups, expand, gt, gain, shift, scale, wr_t, br, tm=512):
    b, s, _ = x.shape
    o_specs = [pl.BlockSpec((1, d, tm // d, GROUP_WIDTH), lambda bi, i: (bi, 0, i, 0)) for d in DILATIONS]
    l_specs = [pl.BlockSpec((1, d, tm // d, LANES), lambda bi, i: (bi, 0, i, 0)) for d in DILATIONS]
    out_specs, out_shape = _route_out_specs(b, s, tm)
    return pl.pallas_call(
        functools.partial(_attn_out_kernel, tm=tm),
        grid=(b, s // tm),
        in_specs=o_specs + l_specs + [
            pl.BlockSpec((1, tm, D_MODEL), lambda bi, i: (bi, i, 0)),
            _const_spec((N_ATTN_GROUPS, GROUP_WIDTH, D_MODEL), lambda bi, i: (0, 0, 0)),
            _const_spec((LANES, GROUP_WIDTH), lambda bi, i: (0, 0)),
            _mod_spec(),
            _const_spec((1, D_MODEL), lambda bi, i: (0, 0)),
            _mod_spec(), _mod_spec(),
            _const_spec((ROUTER_ROWS, D_MODEL), lambda bi, i: (0, 0)),
            _const_spec((ROUTER_ROWS, 1), lambda bi, i: (0, 0))],
        out_specs=out_specs,
        out_shape=out_shape,
        scratch_shapes=[pltpu.VMEM((N_ATTN_GROUPS * GROUP_WIDTH // LANES, tm, LANES), F32),
                        pltpu.VMEM((N_ATTN_GROUPS, tm, LANES), F32),
                        pltpu.VMEM((CLASS_ROWS, LANES), F32)],
        compiler_params=_cparams(("arbitrary", "arbitrary")),
        name="attn_merge_proj",
    )(*o_list, *l_list, x, wo_groups, expand, gt, gain, shift, scale, wr_t, br)


def _gmlp_kernel(x_ref, gain_ref, sh_ref, sc_ref, win_ref, vg_ref, ws_ref, bs_ref, wo_ref, gt_ref,
                 fgain_ref, fsh_ref, fsc_ref, wr_ref, br_ref, x1_ref, rows_ref, route_ref, cnt_ref,
                 gated_scr, carry_ref, *, tm):
    x = x_ref[0]
    hn = _norm_modulate(x, gain_ref[...], sh_ref[0, 0], sc_ref[0, 0]).astype(BF16)
    u = _gelu(jnp.dot(hn, win_ref[:, :GMLP_HALF], preferred_element_type=F32))
    v = _gelu(jnp.dot(hn, win_ref[:, GMLP_HALF:], preferred_element_type=F32))
    mu = jnp.mean(v, axis=-1, keepdims=True)
    vc = v - mu
    vn = (vc * lax.rsqrt(jnp.mean(vc * vc, axis=-1, keepdims=True) + EPS) * vg_ref[...]).astype(BF16)
    gw = GMLP_HALF // GMLP_GROUPS
    for c in range(tm // CHUNK):
        rs = slice(c * CHUNK, (c + 1) * CHUNK)
        for g in range(GMLP_GROUPS):
            cs = slice(g * gw, (g + 1) * gw)
            vs = jnp.dot(ws_ref[g], vn[rs, cs], preferred_element_type=F32) + bs_ref[:, g:g + 1]
            gated_scr[rs, cs] = (u[rs, cs] * vs).astype(BF16)
    y = jnp.dot(gated_scr[...], wo_ref[...], preferred_element_type=F32)
    x1 = x + gt_ref[0, 0] * y
    x1_ref[0] = x1
    _route_outputs(x1, fgain_ref, fsh_ref, fsc_ref, wr_ref, br_ref, carry_ref, rows_ref, route_ref, cnt_ref)


def _gmlp(x, gain, shift, scale, w_in, v_gain, w_s, b_s_t, w_o, gt, fgain, fshift, fscale, wr_t, br, tm=256):
    b, s, _ = x.shape
    tok = lambda w: pl.BlockSpec((1, tm, w), lambda bi, i: (bi, i, 0))
    c2 = lambda shape: _const_spec(shape, lambda bi, i: (0,) * len(shape))
    out_specs, out_shape = _route_out_specs(b, s, tm)
    return pl.pallas_call(
        functools.partial(_gmlp_kernel, tm=tm),
        grid=(b, s // tm),
        in_specs=[tok(D_MODEL), c2((1, D_MODEL)), _mod_spec(), _mod_spec(),
                  c2((D_MODEL, 2 * GMLP_HALF)), c2((1, GMLP_HALF)),
                  c2((GMLP_GROUPS, CHUNK, CHUNK)), c2((CHUNK, GMLP_GROUPS)),
                  c2((GMLP_HALF, D_MODEL)), _mod_spec(),
                  c2((1, D_MODEL)), _mod_spec(), _mod_spec(),
                  c2((ROUTER_ROWS, D_MODEL)), c2((ROUTER_ROWS, 1))],
        out_specs=out_specs,
        out_shape=out_shape,
        scratch_shapes=[pltpu.VMEM((tm, GMLP_HALF), BF16), pltpu.VMEM((CLASS_ROWS, LANES), F32)],
        compiler_params=_cparams(("arbitrary", "arbitrary")),
        name="gmlp",
    )(x, gain, shift, scale, w_in, v_gain, w_s, b_s_t, w_o, gt, fgain, fshift, fscale, wr_t, br)


def _token_copy(src, src_tok, dst, dst_tok, sem):
    s0 = pl.multiple_of(src_tok * TOKEN_ROWS, TOKEN_ROWS)
    d0 = pl.multiple_of(dst_tok * TOKEN_ROWS, TOKEN_ROWS)
    return pltpu.make_async_copy(src.at[pl.ds(s0, TOKEN_ROWS), :], dst.at[pl.ds(d0, TOKEN_ROWS), :], sem)


def _slabs(ref, n_tok, n_slab):
    return jnp.concatenate([ref[pl.ds(s, n_tok, stride=TOKEN_ROWS), :] for s in range(n_slab)], axis=1)


def _dispatch_kernel(ztile_ref, pos_ref, rows_ref, out_hbm, zbuf, sem, *, tm):
    zrows = MOE_TILE * TOKEN_ROWS

    def zero_copy(tok):
        z0 = pl.multiple_of(tok * TOKEN_ROWS, zrows)
        return pltpu.make_async_copy(zbuf, out_hbm.at[pl.ds(z0, zrows), :], sem)

    @pl.when(pl.program_id(0) == 0)
    def _():
        zbuf[...] = jnp.zeros_like(zbuf)
        for c in range(N_CLASSES):
            zero_copy(ztile_ref[c]).start()
        for c in range(N_CLASSES):
            zero_copy(ztile_ref[c]).wait()

        def clear_idle(j, carry):
            cp = zero_copy(j * MOE_TILE)
            cp.start()
            cp.wait()
            return carry

        lax.fori_loop(ztile_ref[N_CLASSES], out_hbm.shape[0] // zrows, clear_idle, 0)

    def issue(r, carry):
        _token_copy(rows_ref, r, out_hbm, pos_ref[0, 0, r], sem).start()
        return carry

    lax.fori_loop(0, tm, issue, 0, unroll=8)
    pltpu.make_async_copy(rows_ref, out_hbm.at[pl.ds(0, tm * TOKEN_ROWS), :], sem).wait()


def _dispatch(rows, pos, ztile, n_tok, tm=512):
    t = pos.shape[0]
    return pl.pallas_call(
        functools.partial(_dispatch_kernel, tm=tm),
        grid_spec=pltpu.PrefetchScalarGridSpec(
            num_scalar_prefetch=1,
            grid=(t // tm,),
            in_specs=[pl.BlockSpec((1, 1, tm), lambda i, z: (i, 0, 0), memory_space=pltpu.SMEM),
                      pl.BlockSpec((tm * TOKEN_ROWS, LANES), lambda i, z: (i, 0))],
            out_specs=pl.BlockSpec(memory_space=pl.ANY),
            scratch_shapes=[pltpu.VMEM((MOE_TILE * TOKEN_ROWS, LANES), jnp.uint32), pltpu.SemaphoreType.DMA]),
        out_shape=jax.ShapeDtypeStruct((n_tok * TOKEN_ROWS, LANES), jnp.uint32),
        compiler_params=_cparams(("arbitrary",)),
        name="moe_dispatch",
    )(ztile, pos.reshape(t // tm, 1, tm), rows)


def _expert_kernel(ea_ref, eb_ref, blk_ref, nused_ref, x_ref, wga, wgb, wua, wub, wda, wdb, y_ref):
    @pl.when(pl.program_id(0) < nused_ref[0])
    def _():
        h = _unpack_bf16_pairs(_slabs(x_ref, MOE_TILE, PACKED_SLABS)).astype(BF16)
        extras = x_ref[pl.ds(PACKED_SLABS, MOE_TILE, stride=TOKEN_ROWS), :]
        gate_a = lax.bitcast_convert_type(extras[:, 0:1], F32)
        gate_b = lax.bitcast_convert_type(extras[:, 1:2], F32)
        y = jnp.zeros((MOE_TILE, D_MODEL), F32)
        for gate, wg, wu, wd in ((gate_a, wga, wua, wda), (gate_b, wgb, wub, wdb)):
            hg = jnp.dot(h, wg[0], preferred_element_type=F32)
            hu = jnp.dot(h, wu[0], preferred_element_type=F32)
            act = (_silu(hg) * hu * gate).astype(BF16)
            y = y + jnp.dot(act, wd[0], preferred_element_type=F32)
        for s in range(D_MODEL // LANES):
            y_ref[pl.ds(s, MOE_TILE, stride=TOKEN_ROWS), :] = y[:, s * LANES:(s + 1) * LANES]

    @pl.when(pl.program_id(0) >= nused_ref[0])
    def _():
        y_ref[...] = jnp.zeros_like(y_ref)


def _experts(xs, tile_ea, tile_eb, tile_blk, n_used, wg, wu, wd):
    n_tiles = xs.shape[0] // (MOE_TILE * TOKEN_ROWS)
    up = lambda sel: pl.BlockSpec((1, D_MODEL, D_EXPERT), lambda j, ea, eb, blk, nu: ((ea, eb)[sel][j], 0, 0))
    down = lambda sel: pl.BlockSpec((1, D_EXPERT, D_MODEL), lambda j, ea, eb, blk, nu: ((ea, eb)[sel][j], 0, 0))
    shape = (MOE_TILE * TOKEN_ROWS, LANES)
    return pl.pallas_call(
        _expert_kernel,
        grid_spec=pltpu.PrefetchScalarGridSpec(
            num_scalar_prefetch=4,
            grid=(n_tiles,),
            in_specs=[pl.BlockSpec(shape, lambda j, ea, eb, blk, nu: (blk[j], 0)),
                      up(0), up(1), up(0), up(1), down(0), down(1)],
            out_specs=pl.BlockSpec(shape, lambda j, ea, eb, blk, nu: (j, 0))),
        out_shape=jax.ShapeDtypeStruct(xs.shape, F32),
        compiler_params=_cparams(("arbitrary",)),
        name="moe_experts",
    )(tile_ea, tile_eb, tile_blk, n_used, xs, wg, wg, wu, wu, wd, wd)


def _combine_kernel(pos_ref, x_ref, ys_hbm, gt_ref, fin_ref, o_ref, ybuf, sem, *, tm, final_norm):
    def issue(r, carry):
        _token_copy(ys_hbm, pos_ref[0, 0, r], ybuf, r, sem).start()
        return carry

    lax.fori_loop(0, tm, issue, 0, unroll=8)
    pltpu.make_async_copy(ys_hbm.at[pl.ds(0, tm * TOKEN_ROWS), :], ybuf, sem).wait()
    x2 = x_ref[...] + gt_ref[0, 0] * _slabs(ybuf, tm, D_MODEL // LANES)
    if final_norm:
        ms = jnp.mean(x2 * x2, axis=-1, keepdims=True)
        x2 = x2 * lax.rsqrt(ms + EPS) * fin_ref[...]
    o_ref[...] = x2


def _combine(ys, pos, x, gt, fin_gain, seq, final_norm, tm=512):
    t = x.shape[0]
    tiles_per_batch = seq // tm
    return pl.pallas_call(
        functools.partial(_combine_kernel, tm=tm, final_norm=final_norm),
        grid=(t // tm,),
        in_specs=[pl.BlockSpec((1, 1, tm), lambda i: (i, 0, 0), memory_space=pltpu.SMEM),
                  pl.BlockSpec((tm, D_MODEL), lambda i: (i, 0)),
                  pl.BlockSpec(memory_space=pl.ANY),
                  pl.BlockSpec((1, 1, 1, D_MODEL), lambda i: (i // tiles_per_batch, 0, 0, 0)),
                  pl.BlockSpec((1, D_MODEL), lambda i: (0, 0))],
        out_specs=pl.BlockSpec((tm, D_MODEL), lambda i: (i, 0)),
        out_shape=jax.ShapeDtypeStruct((t, D_MODEL), F32),
        scratch_shapes=[pltpu.VMEM((tm * TOKEN_ROWS, LANES), F32), pltpu.SemaphoreType.DMA],
        compiler_params=_cparams(("arbitrary",)),
        name="moe_combine",
    )(pos.reshape(t // tm, 1, tm), x, ys, gt, fin_gain)


_PAIR_A = np.array([0, 0, 0, 1, 1, 2], np.int32)
_PAIR_B = np.array([1, 2, 3, 2, 3, 3], np.int32)


def _moe(rows, route, counts, wg, wu, wd, x, gt, fin_gain, seq, final_norm):
    t = x.shape[0]
    cls = route[:, 0, :].reshape(t)
    rank = route[:, 1, :].reshape(t)
    cnt = counts[:N_CLASSES, 0].astype(jnp.int32)
    tiles = jnp.maximum((cnt + MOE_TILE - 1) // MOE_TILE, 1)
    tile_end = jnp.cumsum(tiles)
    row_off = (tile_end - tiles) * MOE_TILE
    pos = row_off[cls] + rank
    n_tiles = t // MOE_TILE + N_CLASSES
    n_used = tile_end[-1]
    ztile = jnp.concatenate([(tile_end - 1) * MOE_TILE, n_used.reshape(1)])
    j = jnp.minimum(jnp.arange(n_tiles, dtype=jnp.int32), n_used - 1)
    tile_cls = jnp.searchsorted(tile_end, j, side="right").astype(jnp.int32)
    group = tile_cls // PAIRS_PER_GROUP
    tile_ea = group * EXPERTS_PER_GROUP + jnp.asarray(_PAIR_A)[tile_cls % PAIRS_PER_GROUP]
    tile_eb = group * EXPERTS_PER_GROUP + jnp.asarray(_PAIR_B)[tile_cls % PAIRS_PER_GROUP]
    xs = _dispatch(rows, pos, ztile, n_tiles * MOE_TILE)
    ys = _experts(xs, tile_ea, tile_eb, j, n_used.reshape(1), wg, wu, wd)
    return _combine(ys, pos, x, gt, fin_gain, seq, final_norm)


def _rope_tables(seq):
    inv_freq = ROPE_THETA ** (-jnp.arange(0, ROT_DIM, 2, dtype=F32) / ROT_DIM)
    ang = jnp.arange(seq).astype(F32)[:, None] * inv_freq[None, :]
    cos, sin = jnp.cos(ang), jnp.sin(ang)
    ones = jnp.ones((seq, HEAD_DIM - ROT_DIM), F32)
    cos_h = jnp.concatenate([cos, cos, ones], axis=1)
    sgn_h = jnp.concatenate([-sin, sin, 0.0 * ones], axis=1)
    reps = LANES // HEAD_DIM
    return jnp.tile(cos_h, (1, reps)), jnp.tile(sgn_h, (1, reps))


def _pad_heads(w, axis):
    pad = [(0, 0)] * w.ndim
    pad[axis] = (0, GROUP_WIDTH - HEADS_PER_GROUP * HEAD_DIM)
    return jnp.pad(w, pad)


def _router_weights(w_group, b_group, w_expert, b_expert):
    we = jnp.transpose(w_expert, (1, 0, 2)).reshape(D_MODEL, N_EXPERTS)
    w = jnp.concatenate([w_group, we], axis=1)
    w = jnp.pad(w, ((0, 0), (0, ROUTER_ROWS - w.shape[1])))
    bias = jnp.pad(jnp.concatenate([b_group, b_expert.reshape(-1)]), (0, ROUTER_ROWS - 20))
    return w.T.astype(BF16), bias.reshape(ROUTER_ROWS, 1)


def kernel(x, c, norm_mix, norm_ffn, w_ada, b_ada, a_w_qkv, a_w_o, b_w_in, b_v_gain, b_w_s, b_b_s, b_w_o,
           r_w_group, r_b_group, r_w_expert, r_b_expert, e_w_gate, e_w_up, e_w_down, final_norm):
    b, s, _ = x.shape
    mod = _ada_mod(c, w_ada, b_ada)
    sh_m, sc_m, gt_m, sh_f, sc_f, gt_f = [mod[:, :, i:i + 1] for i in range(6)]
    row = lambda v: v.reshape(1, -1)
    gw = HEADS_PER_GROUP * HEAD_DIM

    wq, wk, wv = [a_w_qkv[0][:, i * 3 * gw:(i + 1) * 3 * gw] for i in range(3)]
    wq = wq * (HEAD_DIM ** -0.5)
    w_groups = jnp.stack([
        jnp.concatenate([_pad_heads(w[:, g * gw:(g + 1) * gw], 1) for w in (wq, wk, wv)], axis=1)
        for g in range(N_ATTN_GROUPS)]).astype(BF16)
    wo_groups = jnp.stack([_pad_heads(a_w_o[0][g * gw:(g + 1) * gw], 0)
                           for g in range(N_ATTN_GROUPS)]).astype(BF16)
    expand = (jnp.arange(LANES)[:, None] == jnp.arange(GROUP_WIDTH)[None, :] // HEAD_DIM).astype(F32)
    cos_t, sgn_t = _rope_tables(s)
    qkv = _qkv(x, row(norm_mix[0]), sh_m[0], sc_m[0], w_groups, cos_t, sgn_t)
    o_list, l_list = [], []
    for g in range(N_ATTN_GROUPS):
        o, l = _attention(*qkv[3 * g:3 * g + 3])
        o_list.append(o)
        l_list.append(l)
    wr_t, br = _router_weights(r_w_group[0], r_b_group[0], r_w_expert[0], r_b_expert[0])
    x1, rows, route, counts = _attn_out(o_list, l_list, x, wo_groups, expand, gt_m[0], row(norm_ffn[0]),
                                        sh_f[0], sc_f[0], wr_t, br)
    t = b * s
    flat = lambda r: r.reshape(-1, r.shape[-2], r.shape[-1])
    x2 = _moe(rows, flat(route), counts, e_w_gate[0].astype(BF16),
              e_w_up[0].astype(BF16), e_w_down[0].astype(BF16), x1.reshape(t, D_MODEL), gt_f[0],
              row(final_norm), s, final_norm=False).reshape(b, s, D_MODEL)

    wr_t, br = _router_weights(r_w_group[1], r_b_group[1], r_w_expert[1], r_b_expert[1])
    x3, rows, route, counts = _gmlp(x2, row(norm_mix[1]), sh_m[1], sc_m[1], b_w_in[0].astype(BF16),
                                    row(b_v_gain[0]), b_w_s[0].astype(BF16), b_b_s[0].T, b_w_o[0].astype(BF16),
                                    gt_m[1], row(norm_ffn[1]), sh_f[1], sc_f[1], wr_t, br)
    out = _moe(rows, flat(route), counts, e_w_gate[1].astype(BF16),
               e_w_up[1].astype(BF16), e_w_down[1].astype(BF16), x3.reshape(t, D_MODEL), gt_f[1],
               row(final_norm), s, final_norm=True)
    return out.reshape(b, s, D_MODEL)
```

```python
import functools

import jax
import jax.numpy as jnp
import numpy as np
from jax import lax
from jax.experimental import pallas as pl
from jax.experimental.pallas import tpu as pltpu

D_MODEL = 1024
DEPTH = 2
HEAD_DIM = 64
HEADS_PER_GROUP = 5
N_ATTN_GROUPS = 3
DILATIONS = (1, 4, 16)
BAND_RADIUS = 64
ROT_DIM = HEAD_DIM // 4
ROPE_THETA = 500000.0
NEG_INF = -1e30
CHUNK = 128
GMLP_HALF = 2 * D_MODEL
GMLP_GROUPS = 8
N_EXPERT_GROUPS = 4
EXPERTS_PER_GROUP = 4
N_EXPERTS = 16
D_EXPERT = 256
EPS = 1e-6

LANES = 128
GROUP_WIDTH = 384
ROUTER_ROWS = 32
PAIRS_PER_GROUP = 6
N_CLASSES = N_EXPERT_GROUPS * PAIRS_PER_GROUP
CLASS_ROWS = 32
HALF_WORDS = D_MODEL // 2
SUBLANES = 8
TOKEN_ROWS = SUBLANES
PACKED_SLABS = HALF_WORDS // LANES
MOE_TILE = 256

F32 = jnp.float32
BF16 = jnp.bfloat16
VMEM_LIMIT = 56 * 1024 * 1024


def _cparams(sem):
    return pltpu.CompilerParams(dimension_semantics=sem, vmem_limit_bytes=VMEM_LIMIT)


def _const_spec(shape, index_map):
    return pl.BlockSpec(shape, index_map, pipeline_mode=pl.Buffered(1))


def _silu(x):
    return x * (1.0 / (1.0 + jnp.exp(-x)))


def _gelu(x):
    return 0.5 * x * (1.0 + lax.erf(x * (2.0 ** -0.5)))


def _norm_modulate(x, gain, shift, scale):
    ms = jnp.mean(x * x, axis=-1, keepdims=True)
    return (x * lax.rsqrt(ms + EPS) * gain) * (1.0 + scale) + shift


def _ada_kernel(c_ref, w_ref, b_ref, o_ref):
    ca = _silu(c_ref[...])
    o_ref[0] = jnp.dot(ca, w_ref[0], preferred_element_type=F32,
                       precision=lax.Precision.HIGHEST) + b_ref[0]


def _ada_mod(c, w_ada, b_ada):
    b = c.shape[0]
    tn = 1536
    c8 = jnp.pad(c, ((0, 8 - b), (0, 0)))
    out = pl.pallas_call(
        _ada_kernel,
        grid=(DEPTH, 6 * D_MODEL // tn),
        in_specs=[pl.BlockSpec((8, D_MODEL), lambda l, j: (0, 0)),
                  pl.BlockSpec((1, D_MODEL, tn), lambda l, j: (l, 0, j)),
                  pl.BlockSpec((1, 1, tn), lambda l, j: (l, 0, j))],
        out_specs=pl.BlockSpec((1, 8, tn), lambda l, j: (l, 0, j)),
        out_shape=jax.ShapeDtypeStruct((DEPTH, 8, 6 * D_MODEL), F32),
        compiler_params=_cparams(("arbitrary", "arbitrary")),
        name="ada_mod",
    )(c8, w_ada, b_ada.reshape(DEPTH, 1, 6 * D_MODEL))
    return out[:, :b].reshape(DEPTH, b, 6, 1, D_MODEL)


def _pack_bf16_pairs(x):
    n = x.shape[1] // 2
    lo = lax.bitcast_convert_type(x[:, :n].astype(BF16).astype(F32), jnp.uint32)
    hi = lax.bitcast_convert_type(x[:, n:].astype(BF16).astype(F32), jnp.uint32)
    return (lo >> 16) | (hi & jnp.uint32(0xFFFF0000))


def _unpack_bf16_pairs(w):
    lo = lax.bitcast_convert_type(w << 16, F32)
    hi = lax.bitcast_convert_type(w & jnp.uint32(0xFFFF0000), F32)
    return jnp.concatenate([lo, hi], axis=1)


def _ffn_prep(x1, gain, shift, scale, wr_ref, br_ref, carry_ref, first_step):
    tm = x1.shape[0]
    hf32 = _norm_modulate(x1, gain, shift, scale)
    hf = hf32.astype(BF16)
    lt = lax.dot_general(wr_ref[...], hf, (((1,), (1,)), ((), ())),
                         preferred_element_type=F32) + br_ref[...]
    gl = [lt[i:i + 1, :] for i in range(N_EXPERT_GROUPS)]
    gmax = jnp.maximum(jnp.maximum(gl[0], gl[1]), jnp.maximum(gl[2], gl[3]))
    ge = [jnp.exp(g - gmax) for g in gl]
    gsum = ge[0] + ge[1] + ge[2] + ge[3]
    gp = [e / gsum for e in ge]
    g_val = jnp.maximum(jnp.maximum(gp[0], gp[1]), jnp.maximum(gp[2], gp[3]))
    g_idx = jnp.where(gp[0] == g_val, 0, jnp.where(gp[1] == g_val, 1, jnp.where(gp[2] == g_val, 2, 3)))
    el = []
    for j in range(EXPERTS_PER_GROUP):
        rows = [lt[4 + 4 * g + j:5 + 4 * g + j, :] for g in range(N_EXPERT_GROUPS)]
        el.append(jnp.where(g_idx == 0, rows[0], jnp.where(g_idx == 1, rows[1],
                  jnp.where(g_idx == 2, rows[2], rows[3]))))
    v1 = jnp.maximum(jnp.maximum(el[0], el[1]), jnp.maximum(el[2], el[3]))
    i1 = jnp.where(el[0] == v1, 0, jnp.where(el[1] == v1, 1, jnp.where(el[2] == v1, 2, 3)))
    el2 = [jnp.where(i1 == j, -jnp.inf, el[j]) for j in range(EXPERTS_PER_GROUP)]
    v2 = jnp.maximum(jnp.maximum(el2[0], el2[1]), jnp.maximum(el2[2], el2[3]))
    i2 = jnp.where((el2[0] == v2) & (i1 != 0), 0,
                   jnp.where((el2[1] == v2) & (i1 != 1), 1,
                             jnp.where((el2[2] == v2) & (i1 != 2), 2, 3)))
    e21 = jnp.exp(v2 - v1)
    den = 1.0 + e21
    w1 = g_val * (1.0 / den)
    w2 = g_val * (e21 / den)
    swap = i2 < i1
    a = jnp.where(swap, i2, i1)
    bb = jnp.where(swap, i1, i2)
    w_a = jnp.where(swap, w2, w1)
    w_b = jnp.where(swap, w1, w2)
    cls = g_idx * PAIRS_PER_GROUP + ((a * (7 - a)) >> 1) + (bb - a - 1)
    row = lax.broadcasted_iota(jnp.int32, (LANES, tm), 0)
    extras_t = jnp.where(row == 0, w_a, jnp.where(row == 1, w_b, 0.0))
    extras = lax.bitcast_convert_type(extras_t.T, jnp.uint32)
    packed = _pack_bf16_pairs(hf32)

    @pl.when(first_step)
    def _():
        carry_ref[...] = jnp.zeros_like(carry_ref)

    crow = lax.broadcasted_iota(jnp.int32, (CLASS_ROWS, tm), 0)
    onehot = (crow == cls).astype(F32)
    si = lax.broadcasted_iota(jnp.int32, (tm, tm), 0)
    ti = lax.broadcasted_iota(jnp.int32, (tm, tm), 1)
    before = (si < ti).astype(BF16)
    prefix = jnp.dot(onehot.astype(BF16), before, preferred_element_type=F32)
    carry = carry_ref[:, 0:1]
    rank = jnp.sum(onehot * (prefix + carry), axis=0, keepdims=True).astype(jnp.int32)
    carry_ref[...] = carry_ref[...] + jnp.sum(onehot, axis=1, keepdims=True)
    return packed, extras, cls, rank


def _rope(t, cos, sgn, first_half):
    out = []
    for c in range(GROUP_WIDTH // LANES):
        xc = t[:, c * LANES:(c + 1) * LANES]
        other = jnp.where(first_half, pltpu.roll(xc, LANES - ROT_DIM // 2, 1),
                          pltpu.roll(xc, ROT_DIM // 2, 1))
        out.append(xc * cos + other * sgn)
    return jnp.concatenate(out, axis=1)


def _qkv_kernel(x_ref, gain_ref, sh_ref, sc_ref, w_ref, cos_ref, sgn_ref, *rest, tm):
    out_refs, hn_scr = rest[:9], rest[9]
    hn = _norm_modulate(x_ref[0], gain_ref[...], sh_ref[0, 0], sc_ref[0, 0])
    n_slab = D_MODEL // LANES
    lane = lax.broadcasted_iota(jnp.int32, (tm, LANES), 1) % HEAD_DIM
    first_half = lane < ROT_DIM // 2
    for g, d in enumerate(DILATIONS):
        rows = tm // d
        if d == 1:
            hp, cos, sgn = hn, cos_ref[...], sgn_ref[...]
        else:
            if g == 1:
                for s in range(n_slab):
                    hn_scr[s] = hn[:, s * LANES:(s + 1) * LANES]
            hp = jnp.concatenate(
                [jnp.concatenate([hn_scr[s, pl.ds(r, rows, stride=d), :] for s in range(n_slab)], axis=1)
                 for r in range(d)], axis=0)
            cos = jnp.concatenate([cos_ref[pl.ds(r, rows, stride=d), :] for r in range(d)], axis=0)
            sgn = jnp.concatenate([sgn_ref[pl.ds(r, rows, stride=d), :] for r in range(d)], axis=0)
        res = jnp.dot(hp.astype(BF16), w_ref[g], preferred_element_type=F32)
        q = _rope(res[:, :GROUP_WIDTH], cos, sgn, first_half)
        k = _rope(res[:, GROUP_WIDTH:2 * GROUP_WIDTH], cos, sgn, first_half)
        v = res[:, 2 * GROUP_WIDTH:]
        for t, o_ref in zip((q, k, v), out_refs[3 * g:3 * g + 3]):
            o_ref[0] = t.astype(BF16).reshape(d, rows, GROUP_WIDTH)


def _qkv(x, gain, shift, scale, w_groups, cos_t, sgn_t, tm=512):
    b, s, _ = x.shape
    out_shape, out_specs = [], []
    for d in DILATIONS:
        for _ in range(3):
            out_shape.append(jax.ShapeDtypeStruct((b, d, s // d, GROUP_WIDTH), BF16))
            out_specs.append(pl.BlockSpec((1, d, tm // d, GROUP_WIDTH), lambda bi, i: (bi, 0, i, 0)))
    return pl.pallas_call(
        functools.partial(_qkv_kernel, tm=tm),
        grid=(b, s // tm),
        in_specs=[pl.BlockSpec((1, tm, D_MODEL), lambda bi, i: (bi, i, 0)),
                  _const_spec((1, D_MODEL), lambda bi, i: (0, 0)),
                  pl.BlockSpec((1, 1, 1, D_MODEL), lambda bi, i: (bi, 0, 0, 0)),
                  pl.BlockSpec((1, 1, 1, D_MODEL), lambda bi, i: (bi, 0, 0, 0)),
                  _const_spec((N_ATTN_GROUPS, D_MODEL, 3 * GROUP_WIDTH), lambda bi, i: (0, 0, 0)),
                  pl.BlockSpec((tm, LANES), lambda bi, i: (i, 0)),
                  pl.BlockSpec((tm, LANES), lambda bi, i: (i, 0))],
        out_specs=out_specs,
        out_shape=out_shape,
        scratch_shapes=[pltpu.VMEM((D_MODEL // LANES, tm, LANES), F32)],
        compiler_params=_cparams(("arbitrary", "arbitrary")),
        name="qkv_rope",
    )(x, gain, shift, scale, w_groups, cos_t, sgn_t)


def _attn_kernel(q_ref, kp_ref, kc_ref, kn_ref, vp_ref, vc_ref, vn_ref, o_ref, lse_ref,
                 kbuf, vbuf, *, seq, tq):
    i = pl.program_id(2)
    r = BAND_RADIUS
    kbuf[0:r] = kp_ref[0, 0]
    kbuf[r:r + tq] = kc_ref[0, 0]
    kbuf[r + tq:] = kn_ref[0, 0]
    vbuf[0:r] = vp_ref[0, 0]
    vbuf[r:r + tq] = vc_ref[0, 0]
    vbuf[r + tq:] = vn_ref[0, 0]
    qb = 2 * r
    lane = lax.broadcasted_iota(jnp.int32, (qb, LANES), 1)
    low = lane < HEAD_DIM
    tidx = lax.broadcasted_iota(jnp.int32, (qb, 2 * qb), 0)
    kidx = lax.broadcasted_iota(jnp.int32, (qb, 2 * qb), 1)
    rel = kidx - tidx
    band = (rel >= 0) & (rel <= 2 * r)
    for j in range(tq // qb):
        kpos = i * tq + (j * qb - r) + kidx
        mask = band & (kpos >= 0) & (kpos < seq)
        q = q_ref[0, 0, j * qb:(j + 1) * qb, :]
        k = kbuf[j * qb:(j + 2) * qb, :]
        v = vbuf[j * qb:(j + 2) * qb, :]
        lse_tile = jnp.zeros((qb, LANES), F32)
        o_chunks = []
        for c in range(GROUP_WIDTH // LANES):
            qc = q[:, c * LANES:(c + 1) * LANES]
            kc = k[:, c * LANES:(c + 1) * LANES]
            vc = v[:, c * LANES:(c + 1) * LANES]
            outs = []
            for hh in range(2):
                head = 2 * c + hh
                if head >= HEADS_PER_GROUP:
                    outs.append(jnp.zeros((qb, LANES), F32))
                    continue
                qm = jnp.where(low if hh == 0 else jnp.logical_not(low), qc, jnp.zeros_like(qc))
                s = lax.dot_general(qm, kc, (((1,), (1,)), ((), ())), preferred_element_type=F32)
                s = jnp.where(mask, s, NEG_INF)
                m = jnp.max(s, axis=-1, keepdims=True)
                p = jnp.exp(s - m)
                den = jnp.sum(p, axis=-1, keepdims=True)
                o = jnp.dot(p.astype(BF16), vc, preferred_element_type=F32) / den
                lse_tile = jnp.where(lane == head, m + jnp.log(den), lse_tile)
                outs.append(o)
            o_chunks.append(jnp.where(low, outs[0], outs[1]))
        o_ref[0, 0, j * qb:(j + 1) * qb, :] = jnp.concatenate(o_chunks, axis=1).astype(BF16)
        lse_ref[0, 0, j * qb:(j + 1) * qb, :] = lse_tile


def _attention(q, k, v, tq=512):
    b, d, seq, w = q.shape
    r = BAND_RADIUS
    nb = seq // r
    cur = pl.BlockSpec((1, 1, tq, w), lambda bi, ri, i: (bi, ri, i, 0))
    prev = pl.BlockSpec((1, 1, r, w), lambda bi, ri, i: (bi, ri, jnp.maximum(i * (tq // r) - 1, 0), 0))
    nxt = pl.BlockSpec((1, 1, r, w), lambda bi, ri, i: (bi, ri, jnp.minimum((i + 1) * (tq // r), nb - 1), 0))
    return pl.pallas_call(
        functools.partial(_attn_kernel, seq=seq, tq=tq),
        grid=(b, d, seq // tq),
        in_specs=[cur, prev, cur, nxt, prev, cur, nxt],
        out_specs=[pl.BlockSpec((1, 1, tq, w), lambda bi, ri, i: (bi, ri, i, 0)),
                   pl.BlockSpec((1, 1, tq, LANES), lambda bi, ri, i: (bi, ri, i, 0))],
        out_shape=[jax.ShapeDtypeStruct((b, d, seq, w), BF16),
                   jax.ShapeDtypeStruct((b, d, seq, LANES), F32)],
        scratch_shapes=[pltpu.VMEM((tq + 2 * r, w), BF16), pltpu.VMEM((tq + 2 * r, w), BF16)],
        compiler_params=_cparams(("arbitrary", "arbitrary", "arbitrary")),
        name=f"band_attn_d{d}",
    )(q, k, k, k, v, v, v)


def _attn_out_kernel(o0, o1, o2, l0, l1, l2, x_ref, wo_ref, exp_ref, gt_ref, gain_ref, sh_ref, sc_ref,
                     wr_ref, br_ref, x1_ref, rows_ref, route_ref, cnt_ref, oscr, lscr, carry_ref, *, tm):
    n_slab = GROUP_WIDTH // LANES
    for g, (d, o_ref, l_ref) in enumerate(zip(DILATIONS, (o0, o1, o2), (l0, l1, l2))):
        rows = tm // d
        for r in range(d):
            dst = slice(None) if d == 1 else pl.ds(r, rows, stride=d)
            blk = o_ref[0, r].astype(F32)
            for c in range(n_slab):
                oscr[g * n_slab + c, dst, :] = blk[:, c * LANES:(c + 1) * LANES]
            lscr[g, dst, :] = l_ref[0, r]
    lse = [lscr[g] for g in range(N_ATTN_GROUPS)]
    mx = jnp.maximum(jnp.maximum(lse[0], lse[1]), lse[2])
    ex = [jnp.exp(l - mx) for l in lse]
    tot = ex[0] + ex[1] + ex[2]
    y = jnp.zeros((tm, D_MODEL), F32)
    for g in range(N_ATTN_GROUPS):
        alpha = ex[g] / tot
        a_wide = jnp.dot(alpha, exp_ref[...], preferred_element_type=F32,
                         precision=lax.Precision.HIGHEST)
        og = jnp.concatenate([oscr[g * n_slab + c] for c in range(n_slab)], axis=1)
        y = y + jnp.dot((og * a_wide).astype(BF16), wo_ref[g], preferred_element_type=F32)
    x1 = x_ref[0] + gt_ref[0, 0] * y
    x1_ref[0] = x1
    _route_outputs(x1, gain_ref, sh_ref, sc_ref, wr_ref, br_ref, carry_ref, rows_ref, route_ref, cnt_ref)


def _route_outputs(x1, gain_ref, sh_ref, sc_ref, wr_ref, br_ref, carry_ref, rows_ref, route_ref, cnt_ref):
    first = (pl.program_id(0) == 0) & (pl.program_id(1) == 0)
    packed, extras, cls, rank = _ffn_prep(x1, gain_ref[...], sh_ref[0, 0], sc_ref[0, 0], wr_ref, br_ref,
                                          carry_ref, first)
    tm = x1.shape[0]
    for s in range(PACKED_SLABS):
        rows_ref[pl.ds(s, tm, stride=TOKEN_ROWS), :] = packed[:, s * LANES:(s + 1) * LANES]
    rows_ref[pl.ds(PACKED_SLABS, tm, stride=TOKEN_ROWS), :] = extras
    for s in range(PACKED_SLABS + 1, TOKEN_ROWS):
        rows_ref[pl.ds(s, tm, stride=TOKEN_ROWS), :] = jnp.zeros((tm, LANES), jnp.uint32)
    r8 = lax.broadcasted_iota(jnp.int32, (SUBLANES, tm), 0)
    route_ref[0, 0] = jnp.where(r8 == 0, cls, jnp.where(r8 == 1, rank, 0))
    cnt_ref[...] = carry_ref[...]


def _route_out_specs(b, s, tm):
    specs = [pl.BlockSpec((1, tm, D_MODEL), lambda bi, i: (bi, i, 0)),
             pl.BlockSpec((tm * TOKEN_ROWS, LANES), lambda bi, i: (bi * (s // tm) + i, 0)),
             pl.BlockSpec((1, 1, SUBLANES, tm), lambda bi, i: (bi, i, 0, 0)),
             pl.BlockSpec((CLASS_ROWS, LANES), lambda bi, i: (0, 0))]
    shapes = [jax.ShapeDtypeStruct((b, s, D_MODEL), F32),
              jax.ShapeDtypeStruct((b * s * TOKEN_ROWS, LANES), jnp.uint32),
              jax.ShapeDtypeStruct((b, s // tm, SUBLANES, tm), jnp.int32),
              jax.ShapeDtypeStruct((CLASS_ROWS, LANES), F32)]
    return specs, shapes


def _mod_spec():
    return pl.BlockSpec((1, 1, 1, D_MODEL), lambda bi, i: (bi, 0, 0, 0))


def _attn_out(o_list, l_list, x, wo_groups, expand, gt, gain, shift, scale, wr_t, br, tm=512):
    b, s, _ = x.shape
    o_specs = [pl.BlockSpec((1, d, tm // d, GROUP_WIDTH), lambda bi, i: (bi, 0, i, 0)) for d in DILATIONS]
    l_specs = [pl.BlockSpec((1, d, tm // d, LANES), lambda bi, i: (bi, 0, i, 0)) for d in DILATIONS]
    out_specs, out_shape = _route_out_specs(b, s, tm)
    return pl.pallas_call(
        functools.partial(_attn_out_kernel, tm=tm),
        grid=(b, s // tm),
        in_specs=o_specs + l_specs + [
            pl.BlockSpec((1, tm, D_MODEL), lambda bi, i: (bi, i, 0)),
            _const_spec((N_ATTN_GROUPS, GROUP_WIDTH, D_MODEL), lambda bi, i: (0, 0, 0)),
            _const_spec((LANES, GROUP_WIDTH), lambda bi, i: (0, 0)),
            _mod_spec(),
            _const_spec((1, D_MODEL), lambda bi, i: (0, 0)),
            _mod_spec(), _mod_spec(),
            _const_spec((ROUTER_ROWS, D_MODEL), lambda bi, i: (0, 0)),
            _const_spec((ROUTER_ROWS, 1), lambda bi, i: (0, 0))],
        out_specs=out_specs,
        out_shape=out_shape,
        scratch_shapes=[pltpu.VMEM((N_ATTN_GROUPS * GROUP_WIDTH // LANES, tm, LANES), F32),
                        pltpu.VMEM((N_ATTN_GROUPS, tm, LANES), F32),
                        pltpu.VMEM((CLASS_ROWS, LANES), F32)],
        compiler_params=_cparams(("arbitrary", "arbitrary")),
        name="attn_merge_proj",
    )(*o_list, *l_list, x, wo_groups, expand, gt, gain, shift, scale, wr_t, br)


def _gather_moe_rows(pos_ref, pos_next_ref, ys_hbm, ybuf, sems, step, n_steps, tm):
    rows = tm * TOKEN_ROWS

    def request(p_ref, slot):
        def issue(r, carry):
            _token_copy(ys_hbm, p_ref[0, 0, r], ybuf, slot * tm + r, sems.at[slot]).start()
            return carry

        lax.fori_loop(0, tm, issue, 0, unroll=8)

    @pl.when(step == 0)
    def _():
        request(pos_ref, 0)

    @pl.when(step + 1 < n_steps)
    def _():
        request(pos_next_ref, (step + 1) % 2)

    slot = step % 2
    base = pl.multiple_of(slot * rows, rows)
    pltpu.make_async_copy(ys_hbm.at[pl.ds(0, rows), :], ybuf.at[pl.ds(base, rows), :], sems.at[slot]).wait()
    return jnp.concatenate([ybuf[pl.ds(base + s, tm, stride=TOKEN_ROWS), :] for s in range(D_MODEL // LANES)],
                           axis=1)


def _gather_specs(n_steps, tm, index_of):
    return [pl.BlockSpec((1, 1, tm), lambda *g: (index_of(*g), 0, 0), memory_space=pltpu.SMEM),
            pl.BlockSpec((1, 1, tm), lambda *g: (jnp.minimum(index_of(*g) + 1, n_steps - 1), 0, 0),
                         memory_space=pltpu.SMEM),
            pl.BlockSpec(memory_space=pl.ANY)]


def _gather_scratch(tm):
    return [pltpu.VMEM((2 * tm * TOKEN_ROWS, LANES), F32), pltpu.SemaphoreType.DMA((2,))]


def _gmlp_kernel(pos_ref, pos_next_ref, ys_hbm, gtf_ref, x_ref, gain_ref, sh_ref, sc_ref, win_ref, vg_ref, ws_ref,
                 bs_ref, wo_ref, gt_ref, fgain_ref, fsh_ref, fsc_ref, wr_ref, br_ref, x1_ref, rows_ref, route_ref,
                 cnt_ref, gated_scr, carry_ref, ybuf, sems, *, tm):
    step = pl.program_id(0) * pl.num_programs(1) + pl.program_id(1)
    n_steps = pl.num_programs(0) * pl.num_programs(1)
    y_moe = _gather_moe_rows(pos_ref, pos_next_ref, ys_hbm, ybuf, sems, step, n_steps, tm)
    x = x_ref[0] + gtf_ref[0, 0] * y_moe
    hn =_norm_modulate(x, gain_ref[...], sh_ref[0, 0], sc_ref[0, 0]).astype(BF16)
    u = _gelu(jnp.dot(hn, win_ref[:, :GMLP_HALF], preferred_element_type=F32))
    v = _gelu(jnp.dot(hn, win_ref[:, GMLP_HALF:], preferred_element_type=F32))
    mu = jnp.mean(v, axis=-1, keepdims=True)
    vc = v - mu
    vn = (vc * lax.rsqrt(jnp.mean(vc * vc, axis=-1, keepdims=True) + EPS) * vg_ref[...]).astype(BF16)
    gw = GMLP_HALF // GMLP_GROUPS
    for c in range(tm // CHUNK):
        rs = slice(c * CHUNK, (c + 1) * CHUNK)
        for g in range(GMLP_GROUPS):
            cs = slice(g * gw, (g + 1) * gw)
            vs = jnp.dot(ws_ref[g], vn[rs, cs], preferred_element_type=F32) + bs_ref[:, g:g + 1]
            gated_scr[rs, cs] = (u[rs, cs] * vs).astype(BF16)
    y = jnp.dot(gated_scr[...], wo_ref[...], preferred_element_type=F32)
    x1 = x + gt_ref[0, 0] * y
    x1_ref[0] = x1
    _route_outputs(x1, fgain_ref, fsh_ref, fsc_ref, wr_ref, br_ref, carry_ref, rows_ref, route_ref, cnt_ref)


def _gmlp(ys, pos, gt_prev, x, gain, shift, scale, w_in, v_gain, w_s, b_s_t, w_o, gt, fgain, fshift, fscale,
          wr_t, br, tm=256):
    b, s, _ = x.shape
    tok = lambda w: pl.BlockSpec((1, tm, w), lambda bi, i: (bi, i, 0))
    c2 = lambda shape: _const_spec(shape, lambda bi, i: (0,) * len(shape))
    out_specs, out_shape = _route_out_specs(b, s, tm)
    n_steps = b * s // tm
    return pl.pallas_call(
        functools.partial(_gmlp_kernel, tm=tm),
        grid=(b, s // tm),
        in_specs=_gather_specs(n_steps, tm, lambda bi, i: bi * (s // tm) + i) + [
                  _mod_spec(),
                  tok(D_MODEL), c2((1, D_MODEL)), _mod_spec(), _mod_spec(),
                  c2((D_MODEL, 2 * GMLP_HALF)), c2((1, GMLP_HALF)),
                  c2((GMLP_GROUPS, CHUNK, CHUNK)), c2((CHUNK, GMLP_GROUPS)),
                  c2((GMLP_HALF, D_MODEL)), _mod_spec(),
                  c2((1, D_MODEL)), _mod_spec(), _mod_spec(),
                  c2((ROUTER_ROWS, D_MODEL)), c2((ROUTER_ROWS, 1))],
        out_specs=out_specs,
        out_shape=out_shape,
        scratch_shapes=[pltpu.VMEM((tm, GMLP_HALF), BF16), pltpu.VMEM((CLASS_ROWS, LANES), F32)]
        + _gather_scratch(tm),
        compiler_params=_cparams(("arbitrary", "arbitrary")),
        name="gmlp",
    )(pos.reshape(n_steps, 1, tm), pos.reshape(n_steps, 1, tm), ys, gt_prev,
      x, gain, shift, scale, w_in, v_gain, w_s, b_s_t, w_o, gt, fgain, fshift, fscale, wr_t, br)


def _token_copy(src, src_tok, dst, dst_tok, sem):
    s0 = pl.multiple_of(src_tok * TOKEN_ROWS, TOKEN_ROWS)
    d0 = pl.multiple_of(dst_tok * TOKEN_ROWS, TOKEN_ROWS)
    return pltpu.make_async_copy(src.at[pl.ds(s0, TOKEN_ROWS), :], dst.at[pl.ds(d0, TOKEN_ROWS), :], sem)


def _slabs(ref, n_tok, n_slab):
    return jnp.concatenate([ref[pl.ds(s, n_tok, stride=TOKEN_ROWS), :] for s in range(n_slab)], axis=1)


def _dispatch_kernel(ztile_ref, pos_ref, rows_ref, out_hbm, zbuf, sem, *, tm):
    zrows = MOE_TILE * TOKEN_ROWS

    def zero_copy(tok):
        z0 = pl.multiple_of(tok * TOKEN_ROWS, zrows)
        return pltpu.make_async_copy(zbuf, out_hbm.at[pl.ds(z0, zrows), :], sem)

    @pl.when(pl.program_id(0) == 0)
    def _():
        zbuf[...] = jnp.zeros_like(zbuf)
        for c in range(N_CLASSES):
            zero_copy(ztile_ref[c]).start()
        for c in range(N_CLASSES):
            zero_copy(ztile_ref[c]).wait()

        def clear_idle(j, carry):
            cp = zero_copy(j * MOE_TILE)
            cp.start()
            cp.wait()
            return carry

        lax.fori_loop(ztile_ref[N_CLASSES], out_hbm.shape[0] // zrows, clear_idle, 0)

    def issue(r, carry):
        _token_copy(rows_ref, r, out_hbm, pos_ref[0, 0, r], sem).start()
        return carry

    lax.fori_loop(0, tm, issue, 0, unroll=8)
    pltpu.make_async_copy(rows_ref, out_hbm.at[pl.ds(0, tm * TOKEN_ROWS), :], sem).wait()


def _dispatch(rows, pos, ztile, n_tok, tm=512):
    t = pos.shape[0]
    return pl.pallas_call(
        functools.partial(_dispatch_kernel, tm=tm),
        grid_spec=pltpu.PrefetchScalarGridSpec(
            num_scalar_prefetch=1,
            grid=(t // tm,),
            in_specs=[pl.BlockSpec((1, 1, tm), lambda i, z: (i, 0, 0), memory_space=pltpu.SMEM),
                      pl.BlockSpec((tm * TOKEN_ROWS, LANES), lambda i, z: (i, 0))],
            out_specs=pl.BlockSpec(memory_space=pl.ANY),
            scratch_shapes=[pltpu.VMEM((MOE_TILE * TOKEN_ROWS, LANES), jnp.uint32), pltpu.SemaphoreType.DMA]),
        out_shape=jax.ShapeDtypeStruct((n_tok * TOKEN_ROWS, LANES), jnp.uint32),
        compiler_params=_cparams(("arbitrary",)),
        name="moe_dispatch",
    )(ztile, pos.reshape(t // tm, 1, tm), rows)


def _expert_kernel(ea_ref, eb_ref, blk_ref, nused_ref, x_ref, wga, wgb, wua, wub, wda, wdb, y_ref):
    @pl.when(pl.program_id(0) < nused_ref[0])
    def _():
        h = _unpack_bf16_pairs(_slabs(x_ref, MOE_TILE, PACKED_SLABS)).astype(BF16)
        extras = x_ref[pl.ds(PACKED_SLABS, MOE_TILE, stride=TOKEN_ROWS), :]
        gate_a = lax.bitcast_convert_type(extras[:, 0:1], F32)
        gate_b = lax.bitcast_convert_type(extras[:, 1:2], F32)
        y = jnp.zeros((MOE_TILE, D_MODEL), F32)
        for gate, wg, wu, wd in ((gate_a, wga, wua, wda), (gate_b, wgb, wub, wdb)):
            hg = jnp.dot(h, wg[0], preferred_element_type=F32)
            hu = jnp.dot(h, wu[0], preferred_element_type=F32)
            act = (_silu(hg) * hu * gate).astype(BF16)
            y = y + jnp.dot(act, wd[0], preferred_element_type=F32)
        for s in range(D_MODEL // LANES):
            y_ref[pl.ds(s, MOE_TILE, stride=TOKEN_ROWS), :] = y[:, s * LANES:(s + 1) * LANES]

    @pl.when(pl.program_id(0) >= nused_ref[0])
    def _():
        y_ref[...] = jnp.zeros_like(y_ref)


def _experts(xs, tile_ea, tile_eb, tile_blk, n_used, wg, wu, wd):
    n_tiles = xs.shape[0] // (MOE_TILE * TOKEN_ROWS)
    up = lambda sel: pl.BlockSpec((1, D_MODEL, D_EXPERT), lambda j, ea, eb, blk, nu: ((ea, eb)[sel][j], 0, 0))
    down = lambda sel: pl.BlockSpec((1, D_EXPERT, D_MODEL), lambda j, ea, eb, blk, nu: ((ea, eb)[sel][j], 0, 0))
    shape = (MOE_TILE * TOKEN_ROWS, LANES)
    return pl.pallas_call(
        _expert_kernel,
        grid_spec=pltpu.PrefetchScalarGridSpec(
            num_scalar_prefetch=4,
            grid=(n_tiles,),
            in_specs=[pl.BlockSpec(shape, lambda j, ea, eb, blk, nu: (blk[j], 0)),
                      up(0), up(1), up(0), up(1), down(0), down(1)],
            out_specs=pl.BlockSpec(shape, lambda j, ea, eb, blk, nu: (j, 0))),
        out_shape=jax.ShapeDtypeStruct(xs.shape, F32),
        compiler_params=_cparams(("arbitrary",)),
        name="moe_experts",
    )(tile_ea, tile_eb, tile_blk, n_used, xs, wg, wg, wu, wu, wd, wd)


def _final_kernel(pos_ref, pos_next_ref, ys_hbm, x_ref, gt_ref, fin_ref, o_ref, ybuf, sems, *, tm):
    y_moe = _gather_moe_rows(pos_ref, pos_next_ref, ys_hbm, ybuf, sems, pl.program_id(0), pl.num_programs(0), tm)
    x2 = x_ref[...] + gt_ref[0, 0] * y_moe
    ms = jnp.mean(x2 * x2, axis=-1, keepdims=True)
    o_ref[...] = x2 * lax.rsqrt(ms + EPS) * fin_ref[...]


def _final(ys, pos, x, gt, fin_gain, seq, tm=512):
    t = x.shape[0]
    tiles_per_batch = seq // tm
    n_steps = t // tm
    return pl.pallas_call(
        functools.partial(_final_kernel, tm=tm),
        grid=(n_steps,),
        in_specs=_gather_specs(n_steps, tm, lambda i: i) + [
                  pl.BlockSpec((tm, D_MODEL), lambda i: (i, 0)),
                  pl.BlockSpec((1, 1, 1, D_MODEL), lambda i: (i // tiles_per_batch, 0, 0, 0)),
                  pl.BlockSpec((1, D_MODEL), lambda i: (0, 0))],
        out_specs=pl.BlockSpec((tm, D_MODEL), lambda i: (i, 0)),
        out_shape=jax.ShapeDtypeStruct((t, D_MODEL), F32),
        scratch_shapes=_gather_scratch(tm),
        compiler_params=_cparams(("arbitrary",)),
        name="moe_combine_final_norm",
    )(pos.reshape(n_steps, 1, tm), pos.reshape(n_steps, 1, tm), ys, x, gt, fin_gain)


_PAIR_A = np.array([0, 0, 0, 1, 1, 2], np.int32)
_PAIR_B = np.array([1, 2, 3, 2, 3, 3], np.int32)


def _moe_sorted(rows, route, counts, wg, wu, wd):
    t = rows.shape[0] // TOKEN_ROWS
    cls = route[:, 0, :].reshape(t)
    rank = route[:, 1, :].reshape(t)
    cnt = counts[:N_CLASSES, 0].astype(jnp.int32)
    tiles = jnp.maximum((cnt + MOE_TILE - 1) // MOE_TILE, 1)
    tile_end = jnp.cumsum(tiles)
    row_off = (tile_end - tiles) * MOE_TILE
    classes = jnp.arange(N_CLASSES, dtype=jnp.int32)
    pos = jnp.sum(jnp.where(cls[:, None] == classes[None, :], row_off[None, :], 0), axis=1) + rank
    n_tiles = t // MOE_TILE + N_CLASSES
    n_used = tile_end[-1]
    ztile = jnp.concatenate([(tile_end - 1) * MOE_TILE, n_used.reshape(1)])
    j = jnp.minimum(jnp.arange(n_tiles, dtype=jnp.int32), n_used - 1)
    tile_cls = jnp.sum((tile_end[None, :] <= j[:, None]).astype(jnp.int32), axis=1)
    pair = tile_cls % PAIRS_PER_GROUP
    pair_a = jnp.sum(jnp.where(pair[:, None] == jnp.arange(PAIRS_PER_GROUP)[None, :], _PAIR_A[None, :], 0), axis=1)
    pair_b = jnp.sum(jnp.where(pair[:, None] == jnp.arange(PAIRS_PER_GROUP)[None, :], _PAIR_B[None, :], 0), axis=1)
    base = (tile_cls // PAIRS_PER_GROUP) * EXPERTS_PER_GROUP
    xs = _dispatch(rows, pos, ztile, n_tiles * MOE_TILE)
    ys = _experts(xs, base + pair_a, base + pair_b, j, n_used.reshape(1), wg, wu, wd)
    return ys, pos


def _rope_tables(seq):
    inv_freq = ROPE_THETA ** (-jnp.arange(0, ROT_DIM, 2, dtype=F32) / ROT_DIM)
    ang = jnp.arange(seq).astype(F32)[:, None] * inv_freq[None, :]
    cos, sin = jnp.cos(ang), jnp.sin(ang)
    ones = jnp.ones((seq, HEAD_DIM - ROT_DIM), F32)
    cos_h = jnp.concatenate([cos, cos, ones], axis=1)
    sgn_h = jnp.concatenate([-sin, sin, 0.0 * ones], axis=1)
    reps = LANES // HEAD_DIM
    return jnp.tile(cos_h, (1, reps)), jnp.tile(sgn_h, (1, reps))


def _pad_heads(w, axis):
    pad = [(0, 0)] * w.ndim
    pad[axis] = (0, GROUP_WIDTH - HEADS_PER_GROUP * HEAD_DIM)
    return jnp.pad(w, pad)


def _router_weights(w_group, b_group, w_expert, b_expert):
    we = jnp.transpose(w_expert, (1, 0, 2)).reshape(D_MODEL, N_EXPERTS)
    w = jnp.concatenate([w_group, we], axis=1)
    w = jnp.pad(w, ((0, 0), (0, ROUTER_ROWS - w.shape[1])))
    bias = jnp.pad(jnp.concatenate([b_group, b_expert.reshape(-1)]), (0, ROUTER_ROWS - 20))
    return w.T.astype(BF16), bias.reshape(ROUTER_ROWS, 1)


def kernel(x, c, norm_mix, norm_ffn, w_ada, b_ada, a_w_qkv, a_w_o, b_w_in, b_v_gain, b_w_s, b_b_s, b_w_o,
           r_w_group, r_b_group, r_w_expert, r_b_expert, e_w_gate, e_w_up, e_w_down, final_norm):
    b, s, _ = x.shape
    mod = _ada_mod(c, w_ada, b_ada)
    sh_m, sc_m, gt_m, sh_f, sc_f, gt_f = [mod[:, :, i:i + 1] for i in range(6)]
    row = lambda v: v.reshape(1, -1)
    gw = HEADS_PER_GROUP * HEAD_DIM

    wq, wk, wv = [a_w_qkv[0][:, i * 3 * gw:(i + 1) * 3 * gw] for i in range(3)]
    wq = wq * (HEAD_DIM ** -0.5)
    w_groups = jnp.stack([
        jnp.concatenate([_pad_heads(w[:, g * gw:(g + 1) * gw], 1) for w in (wq, wk, wv)], axis=1)
        for g in range(N_ATTN_GROUPS)]).astype(BF16)
    wo_groups = jnp.stack([_pad_heads(a_w_o[0][g * gw:(g + 1) * gw], 0)
                           for g in range(N_ATTN_GROUPS)]).astype(BF16)
    expand = (jnp.arange(LANES)[:, None] == jnp.arange(GROUP_WIDTH)[None, :] // HEAD_DIM).astype(F32)
    cos_t, sgn_t = _rope_tables(s)
    qkv = _qkv(x, row(norm_mix[0]), sh_m[0], sc_m[0], w_groups, cos_t, sgn_t)
    o_list, l_list = [], []
    for g in range(N_ATTN_GROUPS):
        o, l = _attention(*qkv[3 * g:3 * g + 3])
        o_list.append(o)
        l_list.append(l)
    wr_t, br = _router_weights(r_w_group[0], r_b_group[0], r_w_expert[0], r_b_expert[0])
    x1, rows, route, counts = _attn_out(o_list, l_list, x, wo_groups, expand, gt_m[0], row(norm_ffn[0]),
                                        sh_f[0], sc_f[0], wr_t, br)
    t = b * s
    flat = lambda r: r.reshape(-1, r.shape[-2], r.shape[-1])
    ys, pos = _moe_sorted(rows, flat(route), counts, e_w_gate[0].astype(BF16), e_w_up[0].astype(BF16),
                          e_w_down[0].astype(BF16))

    wr_t, br = _router_weights(r_w_group[1], r_b_group[1], r_w_expert[1], r_b_expert[1])
    x3, rows, route, counts = _gmlp(ys, pos, gt_f[0], x1, row(norm_mix[1]), sh_m[1], sc_m[1],
                                    b_w_in[0].astype(BF16), row(b_v_gain[0]), b_w_s[0].astype(BF16), b_b_s[0].T,
                                    b_w_o[0].astype(BF16), gt_m[1], row(norm_ffn[1]), sh_f[1], sc_f[1], wr_t, br)
    ys, pos = _moe_sorted(rows, flat(route), counts, e_w_gate[1].astype(BF16), e_w_up[1].astype(BF16),
                          e_w_down[1].astype(BF16))
    out = _final(ys, pos, x3.reshape(t, D_MODEL), gt_f[1], row(final_norm), s)
    return out.reshape(b, s, D_MODEL)
```

```python
import functools

import jax
import jax.numpy as jnp
import numpy as np
from jax import lax
from jax.experimental import pallas as pl
from jax.experimental.pallas import tpu as pltpu

D_MODEL = 1024
DEPTH = 2
HEAD_DIM = 64
HEADS_PER_GROUP = 5
N_ATTN_GROUPS = 3
DILATIONS = (1, 4, 16)
BAND_RADIUS = 64
ROT_DIM = HEAD_DIM // 4
ROPE_THETA = 500000.0
NEG_INF = -1e30
CHUNK = 128
GMLP_HALF = 2 * D_MODEL
GMLP_GROUPS = 8
N_EXPERT_GROUPS = 4
EXPERTS_PER_GROUP = 4
N_EXPERTS = 16
D_EXPERT = 256
EPS = 1e-6

LANES = 128
GROUP_WIDTH = 384
ROUTER_ROWS = 32
PAIRS_PER_GROUP = 6
N_CLASSES = N_EXPERT_GROUPS * PAIRS_PER_GROUP
CLASS_ROWS = 32
HALF_WORDS = D_MODEL // 2
SUBLANES = 8
TOKEN_ROWS = SUBLANES
PACKED_SLABS = HALF_WORDS // LANES
MOE_TILE = 512

F32 = jnp.float32
BF16 = jnp.bfloat16
VMEM_LIMIT = 56 * 1024 * 1024


def _cparams(sem):
    return pltpu.CompilerParams(dimension_semantics=sem, vmem_limit_bytes=VMEM_LIMIT)


def _const_spec(shape, index_map):
    return pl.BlockSpec(shape, index_map, pipeline_mode=pl.Buffered(1))


def _silu(x):
    return x * (1.0 / (1.0 + jnp.exp(-x)))


def _gelu(x):
    return 0.5 * x * (1.0 + lax.erf(x * (2.0 ** -0.5)))


def _norm_modulate(x, gain, shift, scale):
    ms = jnp.mean(x * x, axis=-1, keepdims=True)
    return (x * lax.rsqrt(ms + EPS) * gain) * (1.0 + scale) + shift


def _ada_kernel(c_ref, w_ref, b_ref, o_ref):
    ca = _silu(c_ref[...])
    o_ref[0] = jnp.dot(ca, w_ref[0], preferred_element_type=F32,
                       precision=lax.Precision.HIGHEST) + b_ref[0]


def _ada_mod(c, w_ada, b_ada):
    b = c.shape[0]
    tn = 1536
    c8 = jnp.pad(c, ((0, 8 - b), (0, 0)))
    out = pl.pallas_call(
        _ada_kernel,
        grid=(DEPTH, 6 * D_MODEL // tn),
        in_specs=[pl.BlockSpec((8, D_MODEL), lambda l, j: (0, 0)),
                  pl.BlockSpec((1, D_MODEL, tn), lambda l, j: (l, 0, j)),
                  pl.BlockSpec((1, 1, tn), lambda l, j: (l, 0, j))],
        out_specs=pl.BlockSpec((1, 8, tn), lambda l, j: (l, 0, j)),
        out_shape=jax.ShapeDtypeStruct((DEPTH, 8, 6 * D_MODEL), F32),
        compiler_params=_cparams(("arbitrary", "arbitrary")),
        name="ada_mod",
    )(c8, w_ada, b_ada.reshape(DEPTH, 1, 6 * D_MODEL))
    return out[:, :b].reshape(DEPTH, b, 6, 1, D_MODEL)


def _pack_bf16_pairs(x):
    n = x.shape[1] // 2
    lo = lax.bitcast_convert_type(x[:, :n].astype(BF16).astype(F32), jnp.uint32)
    hi = lax.bitcast_convert_type(x[:, n:].astype(BF16).astype(F32), jnp.uint32)
    return (lo >> 16) | (hi & jnp.uint32(0xFFFF0000))


def _unpack_bf16_pairs(w):
    lo = lax.bitcast_convert_type(w << 16, F32)
    hi = lax.bitcast_convert_type(w & jnp.uint32(0xFFFF0000), F32)
    return jnp.concatenate([lo, hi], axis=1)


def _ffn_prep(x1, gain, shift, scale, wr_ref, br_ref, carry_ref, first_step):
    tm = x1.shape[0]
    hf32 = _norm_modulate(x1, gain, shift, scale)
    hf = hf32.astype(BF16)
    lt = lax.dot_general(wr_ref[...], hf, (((1,), (1,)), ((), ())),
                         preferred_element_type=F32) + br_ref[...]
    gl = [lt[i:i + 1, :] for i in range(N_EXPERT_GROUPS)]
    gmax = jnp.maximum(jnp.maximum(gl[0], gl[1]), jnp.maximum(gl[2], gl[3]))
    ge = [jnp.exp(g - gmax) for g in gl]
    gsum = ge[0] + ge[1] + ge[2] + ge[3]
    gp = [e / gsum for e in ge]
    g_val = jnp.maximum(jnp.maximum(gp[0], gp[1]), jnp.maximum(gp[2], gp[3]))
    g_idx = jnp.where(gp[0] == g_val, 0, jnp.where(gp[1] == g_val, 1, jnp.where(gp[2] == g_val, 2, 3)))
    el = []
    for j in range(EXPERTS_PER_GROUP):
        rows = [lt[4 + 4 * g + j:5 + 4 * g + j, :] for g in range(N_EXPERT_GROUPS)]
        el.append(jnp.where(g_idx == 0, rows[0], jnp.where(g_idx == 1, rows[1],
                  jnp.where(g_idx == 2, rows[2], rows[3]))))
    v1 = jnp.maximum(jnp.maximum(el[0], el[1]), jnp.maximum(el[2], el[3]))
    i1 = jnp.where(el[0] == v1, 0, jnp.where(el[1] == v1, 1, jnp.where(el[2] == v1, 2, 3)))
    el2 = [jnp.where(i1 == j, -jnp.inf, el[j]) for j in range(EXPERTS_PER_GROUP)]
    v2 = jnp.maximum(jnp.maximum(el2[0], el2[1]), jnp.maximum(el2[2], el2[3]))
    i2 = jnp.where((el2[0] == v2) & (i1 != 0), 0,
                   jnp.where((el2[1] == v2) & (i1 != 1), 1,
                             jnp.where((el2[2] == v2) & (i1 != 2), 2, 3)))
    e21 = jnp.exp(v2 - v1)
    den = 1.0 + e21
    w1 = g_val * (1.0 / den)
    w2 = g_val * (e21 / den)
    swap = i2 < i1
    a = jnp.where(swap, i2, i1)
    bb = jnp.where(swap, i1, i2)
    w_a = jnp.where(swap, w2, w1)
    w_b = jnp.where(swap, w1, w2)
    cls = g_idx * PAIRS_PER_GROUP + ((a * (7 - a)) >> 1) + (bb - a - 1)
    row = lax.broadcasted_iota(jnp.int32, (LANES, tm), 0)
    extras_t = jnp.where(row == 0, w_a, jnp.where(row == 1, w_b, 0.0))
    extras = lax.bitcast_convert_type(extras_t.T, jnp.uint32)
    packed = _pack_bf16_pairs(hf32)

    @pl.when(first_step)
    def _():
        carry_ref[...] = jnp.zeros_like(carry_ref)

    crow = lax.broadcasted_iota(jnp.int32, (CLASS_ROWS, tm), 0)
    onehot = (crow == cls).astype(F32)
    si = lax.broadcasted_iota(jnp.int32, (tm, tm), 0)
    ti = lax.broadcasted_iota(jnp.int32, (tm, tm), 1)
    before = (si < ti).astype(BF16)
    prefix = jnp.dot(onehot.astype(BF16), before, preferred_element_type=F32)
    carry = carry_ref[:, 0:1]
    rank = jnp.sum(onehot * (prefix + carry), axis=0, keepdims=True).astype(jnp.int32)
    carry_ref[...] = carry_ref[...] + jnp.sum(onehot, axis=1, keepdims=True)
    return packed, extras, cls, rank


def _rope(t, cos, sgn, first_half):
    out = []
    for c in range(GROUP_WIDTH // LANES):
        xc = t[:, c * LANES:(c + 1) * LANES]
        other = jnp.where(first_half, pltpu.roll(xc, LANES - ROT_DIM // 2, 1),
                          pltpu.roll(xc, ROT_DIM // 2, 1))
        out.append(xc * cos + other * sgn)
    return jnp.concatenate(out, axis=1)


def _qkv_kernel(x_ref, gain_ref, sh_ref, sc_ref, w_ref, cos_ref, sgn_ref, *rest, tm):
    out_refs, hn_scr = rest[:9], rest[9]
    hn = _norm_modulate(x_ref[0], gain_ref[...], sh_ref[0, 0], sc_ref[0, 0])
    n_slab = D_MODEL // LANES
    lane = lax.broadcasted_iota(jnp.int32, (tm, LANES), 1) % HEAD_DIM
    first_half = lane < ROT_DIM // 2
    for g, d in enumerate(DILATIONS):
        rows = tm // d
        if d == 1:
            hp, cos, sgn = hn, cos_ref[...], sgn_ref[...]
        else:
            if g == 1:
                for s in range(n_slab):
                    hn_scr[s] = hn[:, s * LANES:(s + 1) * LANES]
            hp = jnp.concatenate(
                [jnp.concatenate([hn_scr[s, pl.ds(r, rows, stride=d), :] for s in range(n_slab)], axis=1)
                 for r in range(d)], axis=0)
            cos = jnp.concatenate([cos_ref[pl.ds(r, rows, stride=d), :] for r in range(d)], axis=0)
            sgn = jnp.concatenate([sgn_ref[pl.ds(r, rows, stride=d), :] for r in range(d)], axis=0)
        res = jnp.dot(hp.astype(BF16), w_ref[g], preferred_element_type=F32)
        q = _rope(res[:, :GROUP_WIDTH], cos, sgn, first_half)
        k = _rope(res[:, GROUP_WIDTH:2 * GROUP_WIDTH], cos, sgn, first_half)
        v = res[:, 2 * GROUP_WIDTH:]
        for t, o_ref in zip((q, k, v), out_refs[3 * g:3 * g + 3]):
            o_ref[0] = t.astype(BF16).reshape(d, rows, GROUP_WIDTH)


def _qkv(x, gain, shift, scale, w_groups, cos_t, sgn_t, tm=512):
    b, s, _ = x.shape
    out_shape, out_specs = [], []
    for d in DILATIONS:
        for _ in range(3):
            out_shape.append(jax.ShapeDtypeStruct((b, d, s // d, GROUP_WIDTH), BF16))
            out_specs.append(pl.BlockSpec((1, d, tm // d, GROUP_WIDTH), lambda bi, i: (bi, 0, i, 0)))
    return pl.pallas_call(
        functools.partial(_qkv_kernel, tm=tm),
        grid=(b, s // tm),
        in_specs=[pl.BlockSpec((1, tm, D_MODEL), lambda bi, i: (bi, i, 0)),
                  _const_spec((1, D_MODEL), lambda bi, i: (0, 0)),
                  pl.BlockSpec((1, 1, 1, D_MODEL), lambda bi, i: (bi, 0, 0, 0)),
                  pl.BlockSpec((1, 1, 1, D_MODEL), lambda bi, i: (bi, 0, 0, 0)),
                  _const_spec((N_ATTN_GROUPS, D_MODEL, 3 * GROUP_WIDTH), lambda bi, i: (0, 0, 0)),
                  pl.BlockSpec((tm, LANES), lambda bi, i: (i, 0)),
                  pl.BlockSpec((tm, LANES), lambda bi, i: (i, 0))],
        out_specs=out_specs,
        out_shape=out_shape,
        scratch_shapes=[pltpu.VMEM((D_MODEL // LANES, tm, LANES), F32)],
        compiler_params=_cparams(("arbitrary", "arbitrary")),
        name="qkv_rope",
    )(x, gain, shift, scale, w_groups, cos_t, sgn_t)


def _attn_kernel(q_ref, kp_ref, kc_ref, kn_ref, vp_ref, vc_ref, vn_ref, o_ref, lse_ref,
                 kbuf, vbuf, *, seq, tq):
    i = pl.program_id(2)
    r = BAND_RADIUS
    kbuf[0:r] = kp_ref[0, 0]
    kbuf[r:r + tq] = kc_ref[0, 0]
    kbuf[r + tq:] = kn_ref[0, 0]
    vbuf[0:r] = vp_ref[0, 0]
    vbuf[r:r + tq] = vc_ref[0, 0]
    vbuf[r + tq:] = vn_ref[0, 0]
    qb = 2 * r
    lane = lax.broadcasted_iota(jnp.int32, (qb, LANES), 1)
    low = lane < HEAD_DIM
    tidx = lax.broadcasted_iota(jnp.int32, (qb, 2 * qb), 0)
    kidx = lax.broadcasted_iota(jnp.int32, (qb, 2 * qb), 1)
    rel = kidx - tidx
    band = (rel >= 0) & (rel <= 2 * r)
    for j in range(tq // qb):
        kpos = i * tq + (j * qb - r) + kidx
        mask = band & (kpos >= 0) & (kpos < seq)
        q = q_ref[0, 0, j * qb:(j + 1) * qb, :]
        k = kbuf[j * qb:(j + 2) * qb, :]
        v = vbuf[j * qb:(j + 2) * qb, :]
        lse_tile = jnp.zeros((qb, LANES), F32)
        o_chunks = []
        for c in range(GROUP_WIDTH // LANES):
            qc = q[:, c * LANES:(c + 1) * LANES]
            kc = k[:, c * LANES:(c + 1) * LANES]
            vc = v[:, c * LANES:(c + 1) * LANES]
            outs = []
            for hh in range(2):
                head = 2 * c + hh
                if head >= HEADS_PER_GROUP:
                    outs.append(jnp.zeros((qb, LANES), F32))
                    continue
                qm = jnp.where(low if hh == 0 else jnp.logical_not(low), qc, jnp.zeros_like(qc))
                s = lax.dot_general(qm, kc, (((1,), (1,)), ((), ())), preferred_element_type=F32)
                s = jnp.where(mask, s, NEG_INF)
                m = jnp.max(s, axis=-1, keepdims=True)
                p = jnp.exp(s - m)
                den = jnp.sum(p, axis=-1, keepdims=True)
                o = jnp.dot(p.astype(BF16), vc, preferred_element_type=F32) / den
                lse_tile = jnp.where(lane == head, m + jnp.log(den), lse_tile)
                outs.append(o)
            o_chunks.append(jnp.where(low, outs[0], outs[1]))
        o_ref[0, 0, j * qb:(j + 1) * qb, :] = jnp.concatenate(o_chunks, axis=1).astype(BF16)
        lse_ref[0, 0, j * qb:(j + 1) * qb, :] = lse_tile


def _attention(q, k, v, tq=512):
    b, d, seq, w = q.shape
    r = BAND_RADIUS
    nb = seq // r
    cur = pl.BlockSpec((1, 1, tq, w), lambda bi, ri, i: (bi, ri, i, 0))
    prev = pl.BlockSpec((1, 1, r, w), lambda bi, ri, i: (bi, ri, jnp.maximum(i * (tq // r) - 1, 0), 0))
    nxt = pl.BlockSpec((1, 1, r, w), lambda bi, ri, i: (bi, ri, jnp.minimum((i + 1) * (tq // r), nb - 1), 0))
    return pl.pallas_call(
        functools.partial(_attn_kernel, seq=seq, tq=tq),
        grid=(b, d, seq // tq),
        in_specs=[cur, prev, cur, nxt, prev, cur, nxt],
        out_specs=[pl.BlockSpec((1, 1, tq, w), lambda bi, ri, i: (bi, ri, i, 0)),
                   pl.BlockSpec((1, 1, tq, LANES), lambda bi, ri, i: (bi, ri, i, 0))],
        out_shape=[jax.ShapeDtypeStruct((b, d, seq, w), BF16),
                   jax.ShapeDtypeStruct((b, d, seq, LANES), F32)],
        scratch_shapes=[pltpu.VMEM((tq + 2 * r, w), BF16), pltpu.VMEM((tq + 2 * r, w), BF16)],
        compiler_params=_cparams(("arbitrary", "arbitrary", "arbitrary")),
        name=f"band_attn_d{d}",
    )(q, k, k, k, v, v, v)


def _attn_out_kernel(o0, o1, o2, l0, l1, l2, x_ref, wo_ref, exp_ref, gt_ref, gain_ref, sh_ref, sc_ref,
                     wr_ref, br_ref, x1_ref, rows_ref, route_ref, cnt_ref, oscr, lscr, carry_ref, *, tm):
    n_slab = GROUP_WIDTH // LANES
    for g, (d, o_ref, l_ref) in enumerate(zip(DILATIONS, (o0, o1, o2), (l0, l1, l2))):
        rows = tm // d
        for r in range(d):
            dst = slice(None) if d == 1 else pl.ds(r, rows, stride=d)
            blk = o_ref[0, r].astype(F32)
            for c in range(n_slab):
                oscr[g * n_slab + c, dst, :] = blk[:, c * LANES:(c + 1) * LANES]
            lscr[g, dst, :] = l_ref[0, r]
    lse = [lscr[g] for g in range(N_ATTN_GROUPS)]
    mx = jnp.maximum(jnp.maximum(lse[0], lse[1]), lse[2])
    ex = [jnp.exp(l - mx) for l in lse]
    tot = ex[0] + ex[1] + ex[2]
    y = jnp.zeros((tm, D_MODEL), F32)
    for g in range(N_ATTN_GROUPS):
        alpha = ex[g] / tot
        a_wide = jnp.dot(alpha, exp_ref[...], preferred_element_type=F32,
                         precision=lax.Precision.HIGHEST)
        og = jnp.concatenate([oscr[g * n_slab + c] for c in range(n_slab)], axis=1)
        y = y + jnp.dot((og * a_wide).astype(BF16), wo_ref[g], preferred_element_type=F32)
    x1 = x_ref[0] + gt_ref[0, 0] * y
    x1_ref[0] = x1
    _route_outputs(x1, gain_ref, sh_ref, sc_ref, wr_ref, br_ref, carry_ref, rows_ref, route_ref, cnt_ref)


def _route_outputs(x1, gain_ref, sh_ref, sc_ref, wr_ref, br_ref, carry_ref, rows_ref, route_ref, cnt_ref):
    first = (pl.program_id(0) == 0) & (pl.program_id(1) == 0)
    packed, extras, cls, rank = _ffn_prep(x1, gain_ref[...], sh_ref[0, 0], sc_ref[0, 0], wr_ref, br_ref,
                                          carry_ref, first)
    tm = x1.shape[0]
    for s in range(PACKED_SLABS):
        rows_ref[pl.ds(s, tm, stride=TOKEN_ROWS), :] = packed[:, s * LANES:(s + 1) * LANES]
    rows_ref[pl.ds(PACKED_SLABS, tm, stride=TOKEN_ROWS), :] = extras
    for s in range(PACKED_SLABS + 1, TOKEN_ROWS):
        rows_ref[pl.ds(s, tm, stride=TOKEN_ROWS), :] = jnp.zeros((tm, LANES), jnp.uint32)
    r8 = lax.broadcasted_iota(jnp.int32, (SUBLANES, tm), 0)
    route_ref[0, 0] = jnp.where(r8 == 0, cls, jnp.where(r8 == 1, rank, 0))
    cnt_ref[...] = carry_ref[...]


def _route_out_specs(b, s, tm):
    specs = [pl.BlockSpec((1, tm, D_MODEL), lambda bi, i: (bi, i, 0)),
             pl.BlockSpec((tm * TOKEN_ROWS, LANES), lambda bi, i: (bi * (s // tm) + i, 0)),
             pl.BlockSpec((1, 1, SUBLANES, tm), lambda bi, i: (bi, i, 0, 0)),
             pl.BlockSpec((CLASS_ROWS, LANES), lambda bi, i: (0, 0))]
    shapes = [jax.ShapeDtypeStruct((b, s, D_MODEL), F32),
              jax.ShapeDtypeStruct((b * s * TOKEN_ROWS, LANES), jnp.uint32),
              jax.ShapeDtypeStruct((b, s // tm, SUBLANES, tm), jnp.int32),
              jax.ShapeDtypeStruct((CLASS_ROWS, LANES), F32)]
    return specs, shapes


def _mod_spec():
    return pl.BlockSpec((1, 1, 1, D_MODEL), lambda bi, i: (bi, 0, 0, 0))


def _attn_out(o_list, l_list, x, wo_groups, expand, gt, gain, shift, scale, wr_t, br, tm=512):
    b, s, _ = x.shape
    o_specs = [pl.BlockSpec((1, d, tm // d, GROUP_WIDTH), lambda bi, i: (bi, 0, i, 0)) for d in DILATIONS]
    l_specs = [pl.BlockSpec((1, d, tm // d, LANES), lambda bi, i: (bi, 0, i, 0)) for d in DILATIONS]
    out_specs, out_shape = _route_out_specs(b, s, tm)
    return pl.pallas_call(
        functools.partial(_attn_out_kernel, tm=tm),
        grid=(b, s // tm),
        in_specs=o_specs + l_specs + [
            pl.BlockSpec((1, tm, D_MODEL), lambda bi, i: (bi, i, 0)),
            _const_spec((N_ATTN_GROUPS, GROUP_WIDTH, D_MODEL), lambda bi, i: (0, 0, 0)),
            _const_spec((LANES, GROUP_WIDTH), lambda bi, i: (0, 0)),
            _mod_spec(),
            _const_spec((1, D_MODEL), lambda bi, i: (0, 0)),
            _mod_spec(), _mod_spec(),
            _const_spec((ROUTER_ROWS, D_MODEL), lambda bi, i: (0, 0)),
            _const_spec((ROUTER_ROWS, 1), lambda bi, i: (0, 0))],
        out_specs=out_specs,
        out_shape=out_shape,
        scratch_shapes=[pltpu.VMEM((N_ATTN_GROUPS * GROUP_WIDTH // LANES, tm, LANES), F32),
                        pltpu.VMEM((N_ATTN_GROUPS, tm, LANES), F32),
                        pltpu.VMEM((CLASS_ROWS, LANES), F32)],
        compiler_params=_cparams(("arbitrary", "arbitrary")),
        name="attn_merge_proj",
    )(*o_list, *l_list, x, wo_groups, expand, gt, gain, shift, scale, wr_t, br)


DMA_BURST_UNROLL = 8


def _issue_burst(n, issue):
    def body(i, carry):
        for k in range(DMA_BURST_UNROLL):
            issue(i * DMA_BURST_UNROLL + k, k)
        return carry

    lax.fori_loop(0, n // DMA_BURST_UNROLL, body, 0)


def _gather_moe_rows(pos_ref, pos_next_ref, ys_hbm, ybuf, sems, step, n_steps, tm):
    rows = tm * TOKEN_ROWS

    def request(p_ref, slot):
        def issue(r, k):
            _token_copy(ys_hbm, p_ref[0, 0, r], ybuf, slot * tm + r, sems.at[slot]).start(priority=k % 2)

        _issue_burst(tm, issue)

    @pl.when(step == 0)
    def _():
        request(pos_ref, 0)

    @pl.when(step + 1 < n_steps)
    def _():
        request(pos_next_ref, (step + 1) % 2)

    slot = step % 2
    base = pl.multiple_of(slot * rows, rows)
    pltpu.make_async_copy(ys_hbm.at[pl.ds(0, rows), :], ybuf.at[pl.ds(base, rows), :], sems.at[slot]).wait()
    return jnp.concatenate([ybuf[pl.ds(base + s, tm, stride=TOKEN_ROWS), :] for s in range(D_MODEL // LANES)],
                           axis=1)


def _gather_specs(n_steps, tm, index_of):
    return [pl.BlockSpec((1, 1, tm), lambda *g: (index_of(*g), 0, 0), memory_space=pltpu.SMEM),
            pl.BlockSpec((1, 1, tm), lambda *g: (jnp.minimum(index_of(*g) + 1, n_steps - 1), 0, 0),
                         memory_space=pltpu.SMEM),
            pl.BlockSpec(memory_space=pl.ANY)]


def _gather_scratch(tm):
    return [pltpu.VMEM((2 * tm * TOKEN_ROWS, LANES), F32), pltpu.SemaphoreType.DMA((2,))]


def _gmlp_kernel(pos_ref, pos_next_ref, ys_hbm, gtf_ref, x_ref, gain_ref, sh_ref, sc_ref, win_ref, vg_ref, ws_ref,
                 bs_ref, wo_ref, gt_ref, fgain_ref, fsh_ref, fsc_ref, wr_ref, br_ref, x1_ref, rows_ref, route_ref,
                 cnt_ref, gated_scr, carry_ref, ybuf, sems, *, tm):
    step = pl.program_id(0) * pl.num_programs(1) + pl.program_id(1)
    n_steps = pl.num_programs(0) * pl.num_programs(1)
    y_moe = _gather_moe_rows(pos_ref, pos_next_ref, ys_hbm, ybuf, sems, step, n_steps, tm)
    x = x_ref[0] + gtf_ref[0, 0] * y_moe
    hn =_norm_modulate(x, gain_ref[...], sh_ref[0, 0], sc_ref[0, 0]).astype(BF16)
    u = _gelu(jnp.dot(hn, win_ref[:, :GMLP_HALF], preferred_element_type=F32))
    v = _gelu(jnp.dot(hn, win_ref[:, GMLP_HALF:], preferred_element_type=F32))
    mu = jnp.mean(v, axis=-1, keepdims=True)
    vc = v - mu
    vn = (vc * lax.rsqrt(jnp.mean(vc * vc, axis=-1, keepdims=True) + EPS) * vg_ref[...]).astype(BF16)
    gw = GMLP_HALF // GMLP_GROUPS
    for c in range(tm // CHUNK):
        rs = slice(c * CHUNK, (c + 1) * CHUNK)
        for g in range(GMLP_GROUPS):
            cs = slice(g * gw, (g + 1) * gw)
            vs = jnp.dot(ws_ref[g], vn[rs, cs], preferred_element_type=F32) + bs_ref[:, g:g + 1]
            gated_scr[rs, cs] = (u[rs, cs] * vs).astype(BF16)
    y = jnp.dot(gated_scr[...], wo_ref[...], preferred_element_type=F32)
    x1 = x + gt_ref[0, 0] * y
    x1_ref[0] = x1
    _route_outputs(x1, fgain_ref, fsh_ref, fsc_ref, wr_ref, br_ref, carry_ref, rows_ref, route_ref, cnt_ref)


def _gmlp(ys, pos, gt_prev, x, gain, shift, scale, w_in, v_gain, w_s, b_s_t, w_o, gt, fgain, fshift, fscale,
          wr_t, br, tm=512):
    b, s, _ = x.shape
    tok = lambda w: pl.BlockSpec((1, tm, w), lambda bi, i: (bi, i, 0))
    c2 = lambda shape: _const_spec(shape, lambda bi, i: (0,) * len(shape))
    out_specs, out_shape = _route_out_specs(b, s, tm)
    n_steps = b * s // tm
    return pl.pallas_call(
        functools.partial(_gmlp_kernel, tm=tm),
        grid=(b, s // tm),
        in_specs=_gather_specs(n_steps, tm, lambda bi, i: bi * (s // tm) + i) + [
                  _mod_spec(),
                  tok(D_MODEL), c2((1, D_MODEL)), _mod_spec(), _mod_spec(),
                  c2((D_MODEL, 2 * GMLP_HALF)), c2((1, GMLP_HALF)),
                  c2((GMLP_GROUPS, CHUNK, CHUNK)), c2((CHUNK, GMLP_GROUPS)),
                  c2((GMLP_HALF, D_MODEL)), _mod_spec(),
                  c2((1, D_MODEL)), _mod_spec(), _mod_spec(),
                  c2((ROUTER_ROWS, D_MODEL)), c2((ROUTER_ROWS, 1))],
        out_specs=out_specs,
        out_shape=out_shape,
        scratch_shapes=[pltpu.VMEM((tm, GMLP_HALF), BF16), pltpu.VMEM((CLASS_ROWS, LANES), F32)]
        + _gather_scratch(tm),
        compiler_params=_cparams(("arbitrary", "arbitrary")),
        name="gmlp",
    )(pos.reshape(n_steps, 1, tm), pos.reshape(n_steps, 1, tm), ys, gt_prev,
      x, gain, shift, scale, w_in, v_gain, w_s, b_s_t, w_o, gt, fgain, fshift, fscale, wr_t, br)


def _token_copy(src, src_tok, dst, dst_tok, sem):
    s0 = pl.multiple_of(src_tok * TOKEN_ROWS, TOKEN_ROWS)
    d0 = pl.multiple_of(dst_tok * TOKEN_ROWS, TOKEN_ROWS)
    return pltpu.make_async_copy(src.at[pl.ds(s0, TOKEN_ROWS), :], dst.at[pl.ds(d0, TOKEN_ROWS), :], sem)


def _slabs(ref, n_tok, n_slab):
    return jnp.concatenate([ref[pl.ds(s, n_tok, stride=TOKEN_ROWS), :] for s in range(n_slab)], axis=1)


def _dispatch_kernel(ztile_ref, pos_ref, rows_ref, out_hbm, zbuf, sem, *, tm):
    zrows = MOE_TILE * TOKEN_ROWS

    def zero_copy(tok):
        z0 = pl.multiple_of(tok * TOKEN_ROWS, zrows)
        return pltpu.make_async_copy(zbuf, out_hbm.at[pl.ds(z0, zrows), :], sem)

    @pl.when(pl.program_id(0) == 0)
    def _():
        zbuf[...] = jnp.zeros_like(zbuf)
        for c in range(N_CLASSES):
            zero_copy(ztile_ref[c]).start()
        for c in range(N_CLASSES):
            zero_copy(ztile_ref[c]).wait()

        def clear_idle(j, carry):
            cp = zero_copy(j * MOE_TILE)
            cp.start()
            cp.wait()
            return carry

        lax.fori_loop(ztile_ref[N_CLASSES], out_hbm.shape[0] // zrows, clear_idle, 0)

    def issue(r, k):
        _token_copy(rows_ref, r, out_hbm, pos_ref[0, 0, r], sem).start(priority=k % 2)

    _issue_burst(tm, issue)
    pltpu.make_async_copy(rows_ref, out_hbm.at[pl.ds(0, tm * TOKEN_ROWS), :], sem).wait()


def _dispatch(rows, pos, ztile, n_tok, tm=512):
    t = pos.shape[0]
    return pl.pallas_call(
        functools.partial(_dispatch_kernel, tm=tm),
        grid_spec=pltpu.PrefetchScalarGridSpec(
            num_scalar_prefetch=1,
            grid=(t // tm,),
            in_specs=[pl.BlockSpec((1, 1, tm), lambda i, z: (i, 0, 0), memory_space=pltpu.SMEM),
                      pl.BlockSpec((tm * TOKEN_ROWS, LANES), lambda i, z: (i, 0))],
            out_specs=pl.BlockSpec(memory_space=pl.ANY),
            scratch_shapes=[pltpu.VMEM((MOE_TILE * TOKEN_ROWS, LANES), jnp.uint32), pltpu.SemaphoreType.DMA]),
        out_shape=jax.ShapeDtypeStruct((n_tok * TOKEN_ROWS, LANES), jnp.uint32),
        compiler_params=_cparams(("arbitrary",)),
        name="moe_dispatch",
    )(ztile, pos.reshape(t // tm, 1, tm), rows)


def _expert_kernel(ea_ref, eb_ref, blk_ref, nused_ref, x_ref, wga, wgb, wua, wub, wda, wdb, y_ref):
    @pl.when(pl.program_id(0) < nused_ref[0])
    def _():
        h = _unpack_bf16_pairs(_slabs(x_ref, MOE_TILE, PACKED_SLABS)).astype(BF16)
        extras = x_ref[pl.ds(PACKED_SLABS, MOE_TILE, stride=TOKEN_ROWS), :]
        gate_a = lax.bitcast_convert_type(extras[:, 0:1], F32)
        gate_b = lax.bitcast_convert_type(extras[:, 1:2], F32)
        y = jnp.zeros((MOE_TILE, D_MODEL), F32)
        for gate, wg, wu, wd in ((gate_a, wga, wua, wda), (gate_b, wgb, wub, wdb)):
            hg = jnp.dot(h, wg[0], preferred_element_type=F32)
            hu = jnp.dot(h, wu[0], preferred_element_type=F32)
            act = (_silu(hg) * hu * gate).astype(BF16)
            y = y + jnp.dot(act, wd[0], preferred_element_type=F32)
        for s in range(D_MODEL // LANES):
            y_ref[pl.ds(s, MOE_TILE, stride=TOKEN_ROWS), :] = y[:, s * LANES:(s + 1) * LANES]

    @pl.when(pl.program_id(0) >= nused_ref[0])
    def _():
        y_ref[...] = jnp.zeros_like(y_ref)


def _experts(xs, tile_ea, tile_eb, tile_blk, n_used, wg, wu, wd):
    n_tiles = xs.shape[0] // (MOE_TILE * TOKEN_ROWS)
    up = lambda sel: pl.BlockSpec((1, D_MODEL, D_EXPERT), lambda j, ea, eb, blk, nu: ((ea, eb)[sel][j], 0, 0))
    down = lambda sel: pl.BlockSpec((1, D_EXPERT, D_MODEL), lambda j, ea, eb, blk, nu: ((ea, eb)[sel][j], 0, 0))
    shape = (MOE_TILE * TOKEN_ROWS, LANES)
    return pl.pallas_call(
        _expert_kernel,
        grid_spec=pltpu.PrefetchScalarGridSpec(
            num_scalar_prefetch=4,
            grid=(n_tiles,),
            in_specs=[pl.BlockSpec(shape, lambda j, ea, eb, blk, nu: (blk[j], 0)),
                      up(0), up(1), up(0), up(1), down(0), down(1)],
            out_specs=pl.BlockSpec(shape, lambda j, ea, eb, blk, nu: (j, 0))),
        out_shape=jax.ShapeDtypeStruct(xs.shape, F32),
        compiler_params=_cparams(("arbitrary",)),
        name="moe_experts",
    )(tile_ea, tile_eb, tile_blk, n_used, xs, wg, wg, wu, wu, wd, wd)


def _final_kernel(pos_ref, pos_next_ref, ys_hbm, x_ref, gt_ref, fin_ref, o_ref, ybuf, sems, *, tm):
    y_moe = _gather_moe_rows(pos_ref, pos_next_ref, ys_hbm, ybuf, sems, pl.program_id(0), pl.num_programs(0), tm)
    x2 = x_ref[...] + gt_ref[0, 0] * y_moe
    ms = jnp.mean(x2 * x2, axis=-1, keepdims=True)
    o_ref[...] = x2 * lax.rsqrt(ms + EPS) * fin_ref[...]


def _final(ys, pos, x, gt, fin_gain, seq, tm=512):
    t = x.shape[0]
    tiles_per_batch = seq // tm
    n_steps = t // tm
    return pl.pallas_call(
        functools.partial(_final_kernel, tm=tm),
        grid=(n_steps,),
        in_specs=_gather_specs(n_steps, tm, lambda i: i) + [
                  pl.BlockSpec((tm, D_MODEL), lambda i: (i, 0)),
                  pl.BlockSpec((1, 1, 1, D_MODEL), lambda i: (i // tiles_per_batch, 0, 0, 0)),
                  pl.BlockSpec((1, D_MODEL), lambda i: (0, 0))],
        out_specs=pl.BlockSpec((tm, D_MODEL), lambda i: (i, 0)),
        out_shape=jax.ShapeDtypeStruct((t, D_MODEL), F32),
        scratch_shapes=_gather_scratch(tm),
        compiler_params=_cparams(("arbitrary",)),
        name="moe_combine_final_norm",
    )(pos.reshape(n_steps, 1, tm), pos.reshape(n_steps, 1, tm), ys, x, gt, fin_gain)


_PAIR_A = np.array([0, 0, 0, 1, 1, 2], np.int32)
_PAIR_B = np.array([1, 2, 3, 2, 3, 3], np.int32)


def _moe_sorted(rows, route, counts, wg, wu, wd):
    t = rows.shape[0] // TOKEN_ROWS
    cls = route[:, 0, :].reshape(t)
    rank = route[:, 1, :].reshape(t)
    cnt = counts[:N_CLASSES, 0].astype(jnp.int32)
    tiles = jnp.maximum((cnt + MOE_TILE - 1) // MOE_TILE, 1)
    tile_end = jnp.cumsum(tiles)
    row_off = (tile_end - tiles) * MOE_TILE
    classes = jnp.arange(N_CLASSES, dtype=jnp.int32)
    pos = jnp.sum(jnp.where(cls[:, None] == classes[None, :], row_off[None, :], 0), axis=1) + rank
    n_tiles = t // MOE_TILE + N_CLASSES
    n_used = tile_end[-1]
    ztile = jnp.concatenate([(tile_end - 1) * MOE_TILE, n_used.reshape(1)])
    j = jnp.minimum(jnp.arange(n_tiles, dtype=jnp.int32), n_used - 1)
    tile_cls = jnp.sum((tile_end[None, :] <= j[:, None]).astype(jnp.int32), axis=1)
    pair = tile_cls % PAIRS_PER_GROUP
    pair_a = jnp.sum(jnp.where(pair[:, None] == jnp.arange(PAIRS_PER_GROUP)[None, :], _PAIR_A[None, :], 0), axis=1)
    pair_b = jnp.sum(jnp.where(pair[:, None] == jnp.arange(PAIRS_PER_GROUP)[None, :], _PAIR_B[None, :], 0), axis=1)
    base = (tile_cls // PAIRS_PER_GROUP) * EXPERTS_PER_GROUP
    xs = _dispatch(rows, pos, ztile, n_tiles * MOE_TILE)
    ys = _experts(xs, base + pair_a, base + pair_b, j, n_used.reshape(1), wg, wu, wd)
    return ys, pos


def _rope_tables(seq):
    inv_freq = ROPE_THETA ** (-jnp.arange(0, ROT_DIM, 2, dtype=F32) / ROT_DIM)
    ang = jnp.arange(seq).astype(F32)[:, None] * inv_freq[None, :]
    cos, sin = jnp.cos(ang), jnp.sin(ang)
    ones = jnp.ones((seq, HEAD_DIM - ROT_DIM), F32)
    cos_h = jnp.concatenate([cos, cos, ones], axis=1)
    sgn_h = jnp.concatenate([-sin, sin, 0.0 * ones], axis=1)
    reps = LANES // HEAD_DIM
    return jnp.tile(cos_h, (1, reps)), jnp.tile(sgn_h, (1, reps))


def _pad_heads(w, axis):
    pad = [(0, 0)] * w.ndim
    pad[axis] = (0, GROUP_WIDTH - HEADS_PER_GROUP * HEAD_DIM)
    return jnp.pad(w, pad)


def _router_weights(w_group, b_group, w_expert, b_expert):
    we = jnp.transpose(w_expert, (1, 0, 2)).reshape(D_MODEL, N_EXPERTS)
    w = jnp.concatenate([w_group, we], axis=1)
    w = jnp.pad(w, ((0, 0), (0, ROUTER_ROWS - w.shape[1])))
    bias = jnp.pad(jnp.concatenate([b_group, b_expert.reshape(-1)]), (0, ROUTER_ROWS - 20))
    return w.T.astype(BF16), bias.reshape(ROUTER_ROWS, 1)


def kernel(x, c, norm_mix, norm_ffn, w_ada, b_ada, a_w_qkv, a_w_o, b_w_in, b_v_gain, b_w_s, b_b_s, b_w_o,
           r_w_group, r_b_group, r_w_expert, r_b_expert, e_w_gate, e_w_up, e_w_down, final_norm):
    b, s, _ = x.shape
    mod = _ada_mod(c, w_ada, b_ada)
    sh_m, sc_m, gt_m, sh_f, sc_f, gt_f = [mod[:, :, i:i + 1] for i in range(6)]
    row = lambda v: v.reshape(1, -1)
    gw = HEADS_PER_GROUP * HEAD_DIM

    wq, wk, wv = [a_w_qkv[0][:, i * 3 * gw:(i + 1) * 3 * gw] for i in range(3)]
    wq = wq * (HEAD_DIM ** -0.5)
    w_groups = jnp.stack([
        jnp.concatenate([_pad_heads(w[:, g * gw:(g + 1) * gw], 1) for w in (wq, wk, wv)], axis=1)
        for g in range(N_ATTN_GROUPS)]).astype(BF16)
    wo_groups = jnp.stack([_pad_heads(a_w_o[0][g * gw:(g + 1) * gw], 0)
                           for g in range(N_ATTN_GROUPS)]).astype(BF16)
    expand = (jnp.arange(LANES)[:, None] == jnp.arange(GROUP_WIDTH)[None, :] // HEAD_DIM).astype(F32)
    cos_t, sgn_t = _rope_tables(s)
    qkv = _qkv(x, row(norm_mix[0]), sh_m[0], sc_m[0], w_groups, cos_t, sgn_t)
    o_list, l_list = [], []
    for g in range(N_ATTN_GROUPS):
        o, l = _attention(*qkv[3 * g:3 * g + 3])
        o_list.append(o)
        l_list.append(l)
    wr_t, br = _router_weights(r_w_group[0], r_b_group[0], r_w_expert[0], r_b_expert[0])
    x1, rows, route, counts = _attn_out(o_list, l_list, x, wo_groups, expand, gt_m[0], row(norm_ffn[0]),
                                        sh_f[0], sc_f[0], wr_t, br)
    t = b * s
    flat = lambda r: r.reshape(-1, r.shape[-2], r.shape[-1])
    ys, pos = _moe_sorted(rows, flat(route), counts, e_w_gate[0].astype(BF16), e_w_up[0].astype(BF16),
                          e_w_down[0].astype(BF16))

    wr_t, br = _router_weights(r_w_group[1], r_b_group[1], r_w_expert[1], r_b_expert[1])
    x3, rows, route, counts = _gmlp(ys, pos, gt_f[0], x1, row(norm_mix[1]), sh_m[1], sc_m[1],
                                    b_w_in[0].astype(BF16), row(b_v_gain[0]), b_w_s[0].astype(BF16), b_b_s[0].T,
                                    b_w_o[0].astype(BF16), gt_m[1], row(norm_ffn[1]), sh_f[1], sc_f[1], wr_t, br)
    ys, pos = _moe_sorted(rows, flat(route), counts, e_w_gate[1].astype(BF16), e_w_up[1].astype(BF16),
                          e_w_down[1].astype(BF16))
    out = _final(ys, pos, x3.reshape(t, D_MODEL), gt_f[1], row(final_norm), s)
    return out.reshape(b, s, D_MODEL)
```

```python
import functools

import jax
import jax.numpy as jnp
import numpy as np
from jax import lax
from jax.experimental import pallas as pl
from jax.experimental.pallas import tpu as pltpu

D_MODEL = 1024
DEPTH = 2
HEAD_DIM = 64
HEADS_PER_GROUP = 5
N_ATTN_GROUPS = 3
DILATIONS = (1, 4, 16)
BAND_RADIUS = 64
ROT_DIM = HEAD_DIM // 4
ROPE_THETA = 500000.0
NEG_INF = -1e30
CHUNK = 128
GMLP_HALF = 2 * D_MODEL
GMLP_GROUPS = 8
N_EXPERT_GROUPS = 4
EXPERTS_PER_GROUP = 4
N_EXPERTS = 16
D_EXPERT = 256
EPS = 1e-6

LANES = 128
GROUP_WIDTH = 384
ROUTER_ROWS = 32
PAIRS_PER_GROUP = 6
N_CLASSES = N_EXPERT_GROUPS * PAIRS_PER_GROUP
CLASS_ROWS = 32
SUBLANES = 8
TOKEN_ROWS = D_MODEL // LANES
MOE_TILE = 512

F32 = jnp.float32
BF16 = jnp.bfloat16
VMEM_LIMIT = 56 * 1024 * 1024


def _cparams(sem):
    return pltpu.CompilerParams(dimension_semantics=sem, vmem_limit_bytes=VMEM_LIMIT)


def _const_spec(shape, index_map):
    return pl.BlockSpec(shape, index_map, pipeline_mode=pl.Buffered(1))


def _silu(x):
    return x * (1.0 / (1.0 + jnp.exp(-x)))


def _gelu(x):
    return 0.5 * x * (1.0 + lax.erf(x * (2.0 ** -0.5)))


def _norm_modulate(x, gain, shift, scale):
    ms = jnp.mean(x * x, axis=-1, keepdims=True)
    return (x * lax.rsqrt(ms + EPS) * gain) * (1.0 + scale) + shift


def _split_bf16(x):
    hi = x.astype(BF16)
    return hi, (x - hi.astype(F32)).astype(BF16)


def _ada_kernel(c_ref, w_ref, b_ref, o_ref):
    ca_hi, ca_lo = _split_bf16(_silu(c_ref[...]))
    w_hi, w_lo = _split_bf16(w_ref[0])
    dot = functools.partial(jnp.dot, preferred_element_type=F32)
    o_ref[0] = dot(ca_hi, w_hi) + dot(ca_lo, w_hi) + dot(ca_hi, w_lo) + b_ref[0]


def _ada_mod(c, w_ada, b_ada):
    b = c.shape[0]
    tn = 1536
    c8 = jnp.pad(c, ((0, 8 - b), (0, 0)))
    out = pl.pallas_call(
        _ada_kernel,
        grid=(DEPTH, 6 * D_MODEL // tn),
        in_specs=[pl.BlockSpec((8, D_MODEL), lambda l, j: (0, 0)),
                  pl.BlockSpec((1, D_MODEL, tn), lambda l, j: (l, 0, j)),
                  pl.BlockSpec((1, 1, tn), lambda l, j: (l, 0, j))],
        out_specs=pl.BlockSpec((1, 8, tn), lambda l, j: (l, 0, j)),
        out_shape=jax.ShapeDtypeStruct((DEPTH, 8, 6 * D_MODEL), F32),
        compiler_params=_cparams(("arbitrary", "arbitrary")),
        name="ada_mod",
    )(c8, w_ada, b_ada.reshape(DEPTH, 1, 6 * D_MODEL))
    return out[:, :b].reshape(DEPTH, b, 6, 1, D_MODEL)


def _router_logits(hf, wr_ref, br_ref):
    return lax.dot_general(wr_ref[...], hf, (((1,), (1,)), ((), ())), preferred_element_type=F32) + br_ref[...]


def _group_choice(lt):
    gl = [lt[i:i + 1, :] for i in range(N_EXPERT_GROUPS)]
    gmax = jnp.maximum(jnp.maximum(gl[0], gl[1]), jnp.maximum(gl[2], gl[3]))
    ge = [jnp.exp(g - gmax) for g in gl]
    gsum = ge[0] + ge[1] + ge[2] + ge[3]
    gp = [e / gsum for e in ge]
    g_val = jnp.maximum(jnp.maximum(gp[0], gp[1]), jnp.maximum(gp[2], gp[3]))
    g_idx = jnp.where(gp[0] == g_val, 0, jnp.where(gp[1] == g_val, 1, jnp.where(gp[2] == g_val, 2, 3)))
    return g_val, g_idx


def _ffn_prep(x1, gain, shift, scale, wr_ref, br_ref, carry_ref, first_step):
    tm = x1.shape[0]
    hf32 = _norm_modulate(x1, gain, shift, scale)
    lt = _router_logits(hf32.astype(BF16), wr_ref, br_ref)
    _, g_idx = _group_choice(lt)
    el = []
    for j in range(EXPERTS_PER_GROUP):
        rows = [lt[4 + 4 * g + j:5 + 4 * g + j, :] for g in range(N_EXPERT_GROUPS)]
        el.append(jnp.where(g_idx == 0, rows[0], jnp.where(g_idx == 1, rows[1],
                  jnp.where(g_idx == 2, rows[2], rows[3]))))
    v1 = jnp.maximum(jnp.maximum(el[0], el[1]), jnp.maximum(el[2], el[3]))
    i1 = jnp.where(el[0] == v1, 0, jnp.where(el[1] == v1, 1, jnp.where(el[2] == v1, 2, 3)))
    el2 = [jnp.where(i1 == j, -jnp.inf, el[j]) for j in range(EXPERTS_PER_GROUP)]
    v2 = jnp.maximum(jnp.maximum(el2[0], el2[1]), jnp.maximum(el2[2], el2[3]))
    i2 = jnp.where((el2[0] == v2) & (i1 != 0), 0,
                   jnp.where((el2[1] == v2) & (i1 != 1), 1,
                             jnp.where((el2[2] == v2) & (i1 != 2), 2, 3)))
    a = jnp.minimum(i1, i2)
    bb = jnp.maximum(i1, i2)
    cls = g_idx * PAIRS_PER_GROUP + ((a * (7 - a)) >> 1) + (bb - a - 1)

    @pl.when(first_step)
    def _():
        carry_ref[...] = jnp.zeros_like(carry_ref)

    crow = lax.broadcasted_iota(jnp.int32, (CLASS_ROWS, tm), 0)
    onehot = (crow == cls).astype(F32)
    si = lax.broadcasted_iota(jnp.int32, (tm, tm), 0)
    ti = lax.broadcasted_iota(jnp.int32, (tm, tm), 1)
    before = (si < ti).astype(BF16)
    prefix = jnp.dot(onehot.astype(BF16), before, preferred_element_type=F32)
    carry = carry_ref[:, 0:1]
    rank = jnp.sum(onehot * (prefix + carry), axis=0, keepdims=True).astype(jnp.int32)
    carry_ref[...] = carry_ref[...] + jnp.sum(onehot, axis=1, keepdims=True)
    return hf32, cls, rank


def _rope(t, cos, sgn, first_half):
    out = []
    for c in range(GROUP_WIDTH // LANES):
        xc = t[:, c * LANES:(c + 1) * LANES]
        other = jnp.where(first_half, pltpu.roll(xc, LANES - ROT_DIM // 2, 1),
                          pltpu.roll(xc, ROT_DIM // 2, 1))
        out.append(xc * cos + other * sgn)
    return jnp.concatenate(out, axis=1)


def _qkv_kernel(x_ref, gain_ref, sh_ref, sc_ref, w_ref, cos_ref, sgn_ref, *rest, tm):
    out_refs, hn_scr = rest[:9], rest[9]
    hn = _norm_modulate(x_ref[0], gain_ref[...], sh_ref[0, 0], sc_ref[0, 0])
    n_slab = D_MODEL // LANES
    lane = lax.broadcasted_iota(jnp.int32, (tm, LANES), 1) % HEAD_DIM
    first_half = lane < ROT_DIM // 2
    for g, d in enumerate(DILATIONS):
        rows = tm // d
        if d == 1:
            hp, cos, sgn = hn, cos_ref[...], sgn_ref[...]
        else:
            if g == 1:
                for s in range(n_slab):
                    hn_scr[s] = hn[:, s * LANES:(s + 1) * LANES]
            hp = jnp.concatenate(
                [jnp.concatenate([hn_scr[s, pl.ds(r, rows, stride=d), :] for s in range(n_slab)], axis=1)
                 for r in range(d)], axis=0)
            cos = jnp.concatenate([cos_ref[pl.ds(r, rows, stride=d), :] for r in range(d)], axis=0)
            sgn = jnp.concatenate([sgn_ref[pl.ds(r, rows, stride=d), :] for r in range(d)], axis=0)
        res = jnp.dot(hp.astype(BF16), w_ref[g], preferred_element_type=F32)
        q = _rope(res[:, :GROUP_WIDTH], cos, sgn, first_half)
        k = _rope(res[:, GROUP_WIDTH:2 * GROUP_WIDTH], cos, sgn, first_half)
        v = res[:, 2 * GROUP_WIDTH:]
        for t, o_ref in zip((q, k, v), out_refs[3 * g:3 * g + 3]):
            o_ref[0] = t.astype(BF16).reshape(d, rows, GROUP_WIDTH)


def _qkv(x, gain, shift, scale, w_groups, cos_t, sgn_t, tm=512):
    b, s, _ = x.shape
    out_shape, out_specs = [], []
    for d in DILATIONS:
        for _ in range(3):
            out_shape.append(jax.ShapeDtypeStruct((b, d, s // d, GROUP_WIDTH), BF16))
            out_specs.append(pl.BlockSpec((1, d, tm // d, GROUP_WIDTH), lambda bi, i: (bi, 0, i, 0)))
    return pl.pallas_call(
        functools.partial(_qkv_kernel, tm=tm),
        grid=(b, s // tm),
        in_specs=[pl.BlockSpec((1, tm, D_MODEL), lambda bi, i: (bi, i, 0)),
                  _const_spec((1, D_MODEL), lambda bi, i: (0, 0)),
                  pl.BlockSpec((1, 1, 1, D_MODEL), lambda bi, i: (bi, 0, 0, 0)),
                  pl.BlockSpec((1, 1, 1, D_MODEL), lambda bi, i: (bi, 0, 0, 0)),
                  _const_spec((N_ATTN_GROUPS, D_MODEL, 3 * GROUP_WIDTH), lambda bi, i: (0, 0, 0)),
                  pl.BlockSpec((tm, LANES), lambda bi, i: (i, 0)),
                  pl.BlockSpec((tm, LANES), lambda bi, i: (i, 0))],
        out_specs=out_specs,
        out_shape=out_shape,
        scratch_shapes=[pltpu.VMEM((D_MODEL // LANES, tm, LANES), F32)],
        compiler_params=_cparams(("arbitrary", "arbitrary")),
        name="qkv_rope",
    )(x, gain, shift, scale, w_groups, cos_t, sgn_t)


def _attn_kernel(q_ref, kp_ref, kc_ref, kn_ref, vp_ref, vc_ref, vn_ref, o_ref, lse_ref,
                 kbuf, vbuf, *, seq, tq):
    i = pl.program_id(2)
    r = BAND_RADIUS
    kbuf[0:r] = kp_ref[0, 0]
    kbuf[r:r + tq] = kc_ref[0, 0]
    kbuf[r + tq:] = kn_ref[0, 0]
    vbuf[0:r] = vp_ref[0, 0]
    vbuf[r:r + tq] = vc_ref[0, 0]
    vbuf[r + tq:] = vn_ref[0, 0]
    qb = 2 * r
    lane = lax.broadcasted_iota(jnp.int32, (qb, LANES), 1)
    low = lane < HEAD_DIM
    tidx = lax.broadcasted_iota(jnp.int32, (qb, 2 * qb), 0)
    kidx = lax.broadcasted_iota(jnp.int32, (qb, 2 * qb), 1)
    rel = kidx - tidx
    band = (rel >= 0) & (rel <= 2 * r)
    for j in range(tq // qb):
        kpos = i * tq + (j * qb - r) + kidx
        mask = band & (kpos >= 0) & (kpos < seq)
        q = q_ref[0, 0, j * qb:(j + 1) * qb, :]
        k = kbuf[j * qb:(j + 2) * qb, :]
        v = vbuf[j * qb:(j + 2) * qb, :]
        lse_tile = jnp.zeros((qb, LANES), F32)
        o_chunks = []
        for c in range(GROUP_WIDTH // LANES):
            qc = q[:, c * LANES:(c + 1) * LANES]
            kc = k[:, c * LANES:(c + 1) * LANES]
            vc = v[:, c * LANES:(c + 1) * LANES]
            outs = []
            for hh in range(2):
                head = 2 * c + hh
                if head >= HEADS_PER_GROUP:
                    outs.append(jnp.zeros((qb, LANES), F32))
                    continue
                qm = jnp.where(low if hh == 0 else jnp.logical_not(low), qc, jnp.zeros_like(qc))
                s = lax.dot_general(qm, kc, (((1,), (1,)), ((), ())), preferred_element_type=F32)
                s = jnp.where(mask, s, NEG_INF)
                m = jnp.max(s, axis=-1, keepdims=True)
                p = jnp.exp(s - m)
                den = jnp.sum(p, axis=-1, keepdims=True)
                o = jnp.dot(p.astype(BF16), vc, preferred_element_type=F32) / den
                lse_tile = jnp.where(lane == head, m + jnp.log(den), lse_tile)
                outs.append(o)
            o_chunks.append(jnp.where(low, outs[0], outs[1]))
        o_ref[0, 0, j * qb:(j + 1) * qb, :] = jnp.concatenate(o_chunks, axis=1).astype(BF16)
        lse_ref[0, 0, j * qb:(j + 1) * qb, :] = lse_tile


def _attention(q, k, v, tq=512):
    b, d, seq, w = q.shape
    r = BAND_RADIUS
    nb = seq // r
    cur = pl.BlockSpec((1, 1, tq, w), lambda bi, ri, i: (bi, ri, i, 0))
    prev = pl.BlockSpec((1, 1, r, w), lambda bi, ri, i: (bi, ri, jnp.maximum(i * (tq // r) - 1, 0), 0))
    nxt = pl.BlockSpec((1, 1, r, w), lambda bi, ri, i: (bi, ri, jnp.minimum((i + 1) * (tq // r), nb - 1), 0))
    return pl.pallas_call(
        functools.partial(_attn_kernel, seq=seq, tq=tq),
        grid=(b, d, seq // tq),
        in_specs=[cur, prev, cur, nxt, prev, cur, nxt],
        out_specs=[pl.BlockSpec((1, 1, tq, w), lambda bi, ri, i: (bi, ri, i, 0)),
                   pl.BlockSpec((1, 1, tq, LANES), lambda bi, ri, i: (bi, ri, i, 0))],
        out_shape=[jax.ShapeDtypeStruct((b, d, seq, w), BF16),
                   jax.ShapeDtypeStruct((b, d, seq, LANES), F32)],
        scratch_shapes=[pltpu.VMEM((tq + 2 * r, w), BF16), pltpu.VMEM((tq + 2 * r, w), BF16)],
        compiler_params=_cparams(("arbitrary", "arbitrary", "arbitrary")),
        name=f"band_attn_d{d}",
    )(q, k, k, k, v, v, v)


def _attn_out_kernel(o0, o1, o2, l0, l1, l2, x_ref, wo_ref, exp_ref, gt_ref, gain_ref, sh_ref, sc_ref,
                     wr_ref, br_ref, x1_ref, rows_ref, route_ref, cnt_ref, oscr, lscr, carry_ref, *, tm):
    n_slab = GROUP_WIDTH // LANES
    for g, (d, o_ref, l_ref) in enumerate(zip(DILATIONS, (o0, o1, o2), (l0, l1, l2))):
        rows = tm // d
        for r in range(d):
            dst = slice(None) if d == 1 else pl.ds(r, rows, stride=d)
            blk = o_ref[0, r].astype(F32)
            for c in range(n_slab):
                oscr[g * n_slab + c, dst, :] = blk[:, c * LANES:(c + 1) * LANES]
            lscr[g, dst, :] = l_ref[0, r]
    lse = [lscr[g] for g in range(N_ATTN_GROUPS)]
    mx = jnp.maximum(jnp.maximum(lse[0], lse[1]), lse[2])
    ex = [jnp.exp(l - mx) for l in lse]
    tot = ex[0] + ex[1] + ex[2]
    y = jnp.zeros((tm, D_MODEL), F32)
    for g in range(N_ATTN_GROUPS):
        alpha = ex[g] / tot
        a_hi, a_lo = _split_bf16(alpha)
        a_wide = (jnp.dot(a_hi, exp_ref[...], preferred_element_type=F32)
                  + jnp.dot(a_lo, exp_ref[...], preferred_element_type=F32))
        og = jnp.concatenate([oscr[g * n_slab + c] for c in range(n_slab)], axis=1)
        y = y + jnp.dot((og * a_wide).astype(BF16), wo_ref[g], preferred_element_type=F32)
    x1 = x_ref[0] + gt_ref[0, 0] * y
    x1_ref[0] = x1
    _route_outputs(x1, gain_ref, sh_ref, sc_ref, wr_ref, br_ref, carry_ref, rows_ref, route_ref, cnt_ref)


def _route_outputs(x1, gain_ref, sh_ref, sc_ref, wr_ref, br_ref, carry_ref, rows_ref, route_ref, cnt_ref):
    first = (pl.program_id(0) == 0) & (pl.program_id(1) == 0)
    hf32, cls, rank = _ffn_prep(x1, gain_ref[...], sh_ref[0, 0], sc_ref[0, 0], wr_ref, br_ref, carry_ref, first)
    tm = x1.shape[0]
    for s in range(TOKEN_ROWS):
        rows_ref[pl.ds(s, tm, stride=TOKEN_ROWS), :] = hf32[:, s * LANES:(s + 1) * LANES]
    r8 = lax.broadcasted_iota(jnp.int32, (SUBLANES, tm), 0)
    route_ref[0, 0] = jnp.where(r8 == 0, cls, jnp.where(r8 == 1, rank, 0))
    cnt_ref[...] = carry_ref[...]


def _route_out_specs(b, s, tm):
    specs = [pl.BlockSpec((1, tm, D_MODEL), lambda bi, i: (bi, i, 0)),
             pl.BlockSpec((tm * TOKEN_ROWS, LANES), lambda bi, i: (bi * (s // tm) + i, 0)),
             pl.BlockSpec((1, 1, SUBLANES, tm), lambda bi, i: (bi, i, 0, 0)),
             pl.BlockSpec((CLASS_ROWS, LANES), lambda bi, i: (0, 0))]
    shapes = [jax.ShapeDtypeStruct((b, s, D_MODEL), F32),
              jax.ShapeDtypeStruct((b * s * TOKEN_ROWS, LANES), F32),
              jax.ShapeDtypeStruct((b, s // tm, SUBLANES, tm), jnp.int32),
              jax.ShapeDtypeStruct((CLASS_ROWS, LANES), F32)]
    return specs, shapes


def _mod_spec():
    return pl.BlockSpec((1, 1, 1, D_MODEL), lambda bi, i: (bi, 0, 0, 0))


def _attn_out(o_list, l_list, x, wo_groups, expand, gt, gain, shift, scale, wr_t, br, tm=512):
    b, s, _ = x.shape
    o_specs = [pl.BlockSpec((1, d, tm // d, GROUP_WIDTH), lambda bi, i: (bi, 0, i, 0)) for d in DILATIONS]
    l_specs = [pl.BlockSpec((1, d, tm // d, LANES), lambda bi, i: (bi, 0, i, 0)) for d in DILATIONS]
    out_specs, out_shape = _route_out_specs(b, s, tm)
    return pl.pallas_call(
        functools.partial(_attn_out_kernel, tm=tm),
        grid=(b, s // tm),
        in_specs=o_specs + l_specs + [
            pl.BlockSpec((1, tm, D_MODEL), lambda bi, i: (bi, i, 0)),
            _const_spec((N_ATTN_GROUPS, GROUP_WIDTH, D_MODEL), lambda bi, i: (0, 0, 0)),
            _const_spec((LANES, GROUP_WIDTH), lambda bi, i: (0, 0)),
            _mod_spec(),
            _const_spec((1, D_MODEL), lambda bi, i: (0, 0)),
            _mod_spec(), _mod_spec(),
            _const_spec((ROUTER_ROWS, D_MODEL), lambda bi, i: (0, 0)),
            _const_spec((ROUTER_ROWS, 1), lambda bi, i: (0, 0))],
        out_specs=out_specs,
        out_shape=out_shape,
        scratch_shapes=[pltpu.VMEM((N_ATTN_GROUPS * GROUP_WIDTH // LANES, tm, LANES), F32),
                        pltpu.VMEM((N_ATTN_GROUPS, tm, LANES), F32),
                        pltpu.VMEM((CLASS_ROWS, LANES), F32)],
        compiler_params=_cparams(("arbitrary", "arbitrary")),
        name="attn_merge_proj",
    )(*o_list, *l_list, x, wo_groups, expand, gt, gain, shift, scale, wr_t, br)


DMA_BURST_UNROLL = 8


def _issue_burst(n, issue):
    def body(i, carry):
        for k in range(DMA_BURST_UNROLL):
            issue(i * DMA_BURST_UNROLL + k, k)
        return carry

    lax.fori_loop(0, n // DMA_BURST_UNROLL, body, 0)


def _gather_moe_rows(pos_ref, pos_next_ref, ys_hbm, ybuf, sems, step, n_steps, tm):
    rows = tm * TOKEN_ROWS

    def request(p_ref, slot):
        def issue(r, k):
            _token_copy(ys_hbm, p_ref[0, 0, r], ybuf, slot * tm + r, sems.at[slot]).start(priority=k % 2)

        _issue_burst(tm, issue)

    @pl.when(step == 0)
    def _():
        request(pos_ref, 0)

    @pl.when(step + 1 < n_steps)
    def _():
        request(pos_next_ref, (step + 1) % 2)

    slot = step % 2
    base = pl.multiple_of(slot * rows, rows)
    pltpu.make_async_copy(ys_hbm.at[pl.ds(0, rows), :], ybuf.at[pl.ds(base, rows), :], sems.at[slot]).wait()
    return jnp.concatenate([ybuf[pl.ds(base + s, tm, stride=TOKEN_ROWS), :] for s in range(D_MODEL // LANES)],
                           axis=1)


def _gather_specs(n_steps, tm, index_of):
    return [pl.BlockSpec((1, 1, tm), lambda *g: (index_of(*g), 0, 0), memory_space=pltpu.SMEM),
            pl.BlockSpec((1, 1, tm), lambda *g: (jnp.minimum(index_of(*g) + 1, n_steps - 1), 0, 0),
                         memory_space=pltpu.SMEM),
            pl.BlockSpec(memory_space=pl.ANY)]


def _gather_scratch(tm):
    return [pltpu.VMEM((2 * tm * TOKEN_ROWS, LANES), F32), pltpu.SemaphoreType.DMA((2,))]


def _gmlp_kernel(pos_ref, pos_next_ref, ys_hbm, gtf_ref, x_ref, gain_ref, sh_ref, sc_ref, win_ref, vg_ref, ws_ref,
                 bs_ref, wo_ref, gt_ref, fgain_ref, fsh_ref, fsc_ref, wr_ref, br_ref, x1_ref, rows_ref, route_ref,
                 cnt_ref, gated_scr, carry_ref, ybuf, sems, *, tm):
    step = pl.program_id(0) * pl.num_programs(1) + pl.program_id(1)
    n_steps = pl.num_programs(0) * pl.num_programs(1)
    y_moe = _gather_moe_rows(pos_ref, pos_next_ref, ys_hbm, ybuf, sems, step, n_steps, tm)
    x = x_ref[0] + gtf_ref[0, 0] * y_moe
    hn = _norm_modulate(x, gain_ref[...], sh_ref[0, 0], sc_ref[0, 0]).astype(BF16)
    u = _gelu(jnp.dot(hn, win_ref[:, :GMLP_HALF], preferred_element_type=F32))
    v = _gelu(jnp.dot(hn, win_ref[:, GMLP_HALF:], preferred_element_type=F32))
    mu = jnp.mean(v, axis=-1, keepdims=True)
    vc = v - mu
    vn = (vc * lax.rsqrt(jnp.mean(vc * vc, axis=-1, keepdims=True) + EPS) * vg_ref[...]).astype(BF16)
    gw = GMLP_HALF // GMLP_GROUPS
    for c in range(tm // CHUNK):
        rs = slice(c * CHUNK, (c + 1) * CHUNK)
        for g in range(GMLP_GROUPS):
            cs = slice(g * gw, (g + 1) * gw)
            vs = jnp.dot(ws_ref[g], vn[rs, cs], preferred_element_type=F32) + bs_ref[:, g:g + 1]
            gated_scr[rs, cs] = (u[rs, cs] * vs).astype(BF16)
    y = jnp.dot(gated_scr[...], wo_ref[...], preferred_element_type=F32)
    x1 = x + gt_ref[0, 0] * y
    x1_ref[0] = x1
    _route_outputs(x1, fgain_ref, fsh_ref, fsc_ref, wr_ref, br_ref, carry_ref, rows_ref, route_ref, cnt_ref)


def _gmlp(ys, pos, gt_prev, x, gain, shift, scale, w_in, v_gain, w_s, b_s_t, w_o, gt, fgain, fshift, fscale,
          wr_t, br, tm=512):
    b, s, _ = x.shape
    tok = lambda w: pl.BlockSpec((1, tm, w), lambda bi, i: (bi, i, 0))
    c2 = lambda shape: _const_spec(shape, lambda bi, i: (0,) * len(shape))
    out_specs, out_shape = _route_out_specs(b, s, tm)
    n_steps = b * s // tm
    return pl.pallas_call(
        functools.partial(_gmlp_kernel, tm=tm),
        grid=(b, s // tm),
        in_specs=_gather_specs(n_steps, tm, lambda bi, i: bi * (s // tm) + i) + [
                  _mod_spec(),
                  tok(D_MODEL), c2((1, D_MODEL)), _mod_spec(), _mod_spec(),
                  c2((D_MODEL, 2 * GMLP_HALF)), c2((1, GMLP_HALF)),
                  c2((GMLP_GROUPS, CHUNK, CHUNK)), c2((CHUNK, GMLP_GROUPS)),
                  c2((GMLP_HALF, D_MODEL)), _mod_spec(),
                  c2((1, D_MODEL)), _mod_spec(), _mod_spec(),
                  c2((ROUTER_ROWS, D_MODEL)), c2((ROUTER_ROWS, 1))],
        out_specs=out_specs,
        out_shape=out_shape,
        scratch_shapes=[pltpu.VMEM((tm, GMLP_HALF), BF16), pltpu.VMEM((CLASS_ROWS, LANES), F32)]
        + _gather_scratch(tm),
        compiler_params=_cparams(("arbitrary", "arbitrary")),
        name="gmlp",
    )(pos.reshape(n_steps, 1, tm), pos.reshape(n_steps, 1, tm), ys, gt_prev,
      x, gain, shift, scale, w_in, v_gain, w_s, b_s_t, w_o, gt, fgain, fshift, fscale, wr_t, br)


def _token_copy(src, src_tok, dst, dst_tok, sem):
    s0 = pl.multiple_of(src_tok * TOKEN_ROWS, TOKEN_ROWS)
    d0 = pl.multiple_of(dst_tok * TOKEN_ROWS, TOKEN_ROWS)
    return pltpu.make_async_copy(src.at[pl.ds(s0, TOKEN_ROWS), :], dst.at[pl.ds(d0, TOKEN_ROWS), :], sem)


def _slabs(ref, n_tok, n_slab):
    return jnp.concatenate([ref[pl.ds(s, n_tok, stride=TOKEN_ROWS), :] for s in range(n_slab)], axis=1)


def _dispatch_kernel(ztile_ref, pos_ref, rows_ref, out_hbm, zbuf, sem, *, tm):
    zrows = MOE_TILE * TOKEN_ROWS

    def zero_copy(tok):
        z0 = pl.multiple_of(tok * TOKEN_ROWS, zrows)
        return pltpu.make_async_copy(zbuf, out_hbm.at[pl.ds(z0, zrows), :], sem)

    @pl.when(pl.program_id(0) == 0)
    def _():
        zbuf[...] = jnp.zeros_like(zbuf)
        for c in range(N_CLASSES):
            zero_copy(ztile_ref[c]).start()
        for c in range(N_CLASSES):
            zero_copy(ztile_ref[c]).wait()

        def clear_idle(j, carry):
            cp = zero_copy(j * MOE_TILE)
            cp.start()
            cp.wait()
            return carry

        lax.fori_loop(ztile_ref[N_CLASSES], out_hbm.shape[0] // zrows, clear_idle, 0)

    def issue(r, k):
        _token_copy(rows_ref, r, out_hbm, pos_ref[0, 0, r], sem).start(priority=k % 2)

    _issue_burst(tm, issue)
    pltpu.make_async_copy(rows_ref, out_hbm.at[pl.ds(0, tm * TOKEN_ROWS), :], sem).wait()


def _dispatch(rows, pos, ztile, n_tok, tm=512):
    t = pos.shape[0]
    return pl.pallas_call(
        functools.partial(_dispatch_kernel, tm=tm),
        grid_spec=pltpu.PrefetchScalarGridSpec(
            num_scalar_prefetch=1,
            grid=(t // tm,),
            in_specs=[pl.BlockSpec((1, 1, tm), lambda i, z: (i, 0, 0), memory_space=pltpu.SMEM),
                      pl.BlockSpec((tm * TOKEN_ROWS, LANES), lambda i, z: (i, 0))],
            out_specs=pl.BlockSpec(memory_space=pl.ANY),
            scratch_shapes=[pltpu.VMEM((MOE_TILE * TOKEN_ROWS, LANES), F32), pltpu.SemaphoreType.DMA]),
        out_shape=jax.ShapeDtypeStruct((n_tok * TOKEN_ROWS, LANES), F32),
        compiler_params=_cparams(("arbitrary",)),
        name="moe_dispatch",
    )(ztile, pos.reshape(t // tm, 1, tm), rows)


def _expert_kernel(ea_ref, eb_ref, blk_ref, nused_ref, x_ref, wr_ref, br_ref, wga, wgb, wua, wub, wda, wdb, y_ref,
                   lt_scr):
    j = pl.program_id(0)

    @pl.when(j < nused_ref[0])
    def _():
        h = _slabs(x_ref, MOE_TILE, TOKEN_ROWS).astype(BF16)
        lt = _router_logits(h, wr_ref, br_ref)
        g_val, _ = _group_choice(lt)
        lt_scr[...] = lt
        el_a = lt_scr[pl.ds(N_EXPERT_GROUPS + ea_ref[j], 1), :]
        el_b = lt_scr[pl.ds(N_EXPERT_GROUPS + eb_ref[j], 1), :]
        top = jnp.maximum(el_a, el_b)
        p_a = jnp.exp(el_a - top)
        p_b = jnp.exp(el_b - top)
        row = lax.broadcasted_iota(jnp.int32, (LANES, MOE_TILE), 0)
        gates_t = jnp.where(row == 0, g_val * (p_a / (p_a + p_b)), jnp.where(row == 1, g_val * (p_b / (p_a + p_b)), 0.0))
        gates = gates_t.T
        y = jnp.zeros((MOE_TILE, D_MODEL), F32)
        for gate, wg, wu, wd in ((gates[:, 0:1], wga, wua, wda), (gates[:, 1:2], wgb, wub, wdb)):
            hg = jnp.dot(h, wg[0], preferred_element_type=F32)
            hu = jnp.dot(h, wu[0], preferred_element_type=F32)
            act = (_silu(hg) * hu * gate).astype(BF16)
            y = y + jnp.dot(act, wd[0], preferred_element_type=F32)
        for s in range(TOKEN_ROWS):
            y_ref[pl.ds(s, MOE_TILE, stride=TOKEN_ROWS), :] = y[:, s * LANES:(s + 1) * LANES]

    @pl.when(j >= nused_ref[0])
    def _():
        y_ref[...] = jnp.zeros_like(y_ref)


def _experts(xs, tile_ea, tile_eb, tile_blk, n_used, wr_t, br, wg, wu, wd):
    n_tiles = xs.shape[0] // (MOE_TILE * TOKEN_ROWS)
    up = lambda sel: pl.BlockSpec((1, D_MODEL, D_EXPERT), lambda j, ea, eb, blk, nu: ((ea, eb)[sel][j], 0, 0))
    down = lambda sel: pl.BlockSpec((1, D_EXPERT, D_MODEL), lambda j, ea, eb, blk, nu: ((ea, eb)[sel][j], 0, 0))
    shape = (MOE_TILE * TOKEN_ROWS, LANES)
    return pl.pallas_call(
        _expert_kernel,
        grid_spec=pltpu.PrefetchScalarGridSpec(
            num_scalar_prefetch=4,
            grid=(n_tiles,),
            in_specs=[pl.BlockSpec(shape, lambda j, ea, eb, blk, nu: (blk[j], 0)),
                      pl.BlockSpec((ROUTER_ROWS, D_MODEL), lambda j, ea, eb, blk, nu: (0, 0)),
                      pl.BlockSpec((ROUTER_ROWS, 1), lambda j, ea, eb, blk, nu: (0, 0)),
                      up(0), up(1), up(0), up(1), down(0), down(1)],
            out_specs=pl.BlockSpec(shape, lambda j, ea, eb, blk, nu: (j, 0)),
            scratch_shapes=[pltpu.VMEM((ROUTER_ROWS, MOE_TILE), F32)]),
        out_shape=jax.ShapeDtypeStruct(xs.shape, F32),
        compiler_params=_cparams(("arbitrary",)),
        name="moe_experts",
    )(tile_ea, tile_eb, tile_blk, n_used, xs, wr_t, br, wg, wg, wu, wu, wd, wd)


def _final_kernel(pos_ref, pos_next_ref, ys_hbm, x_ref, gt_ref, fin_ref, o_ref, ybuf, sems, *, tm):
    y_moe = _gather_moe_rows(pos_ref, pos_next_ref, ys_hbm, ybuf, sems, pl.program_id(0), pl.num_programs(0), tm)
    x2 = x_ref[...] + gt_ref[0, 0] * y_moe
    ms = jnp.mean(x2 * x2, axis=-1, keepdims=True)
    o_ref[...] = x2 * lax.rsqrt(ms + EPS) * fin_ref[...]


def _final(ys, pos, x, gt, fin_gain, seq, tm=512):
    t = x.shape[0]
    tiles_per_batch = seq // tm
    n_steps = t // tm
    return pl.pallas_call(
        functools.partial(_final_kernel, tm=tm),
        grid=(n_steps,),
        in_specs=_gather_specs(n_steps, tm, lambda i: i) + [
                  pl.BlockSpec((tm, D_MODEL), lambda i: (i, 0)),
                  pl.BlockSpec((1, 1, 1, D_MODEL), lambda i: (i // tiles_per_batch, 0, 0, 0)),
                  pl.BlockSpec((1, D_MODEL), lambda i: (0, 0))],
        out_specs=pl.BlockSpec((tm, D_MODEL), lambda i: (i, 0)),
        out_shape=jax.ShapeDtypeStruct((t, D_MODEL), F32),
        scratch_shapes=_gather_scratch(tm),
        compiler_params=_cparams(("arbitrary",)),
        name="moe_combine_final_norm",
    )(pos.reshape(n_steps, 1, tm), pos.reshape(n_steps, 1, tm), ys, x, gt, fin_gain)


_PAIR_A = np.array([0, 0, 0, 1, 1, 2], np.int32)
_PAIR_B = np.array([1, 2, 3, 2, 3, 3], np.int32)


def _moe_sorted(rows, route, counts, wr_t, br, wg, wu, wd):
    t = rows.shape[0] // TOKEN_ROWS
    cls = route[:, 0, :].reshape(t)
    rank = route[:, 1, :].reshape(t)
    cnt = counts[:N_CLASSES, 0].astype(jnp.int32)
    tiles = jnp.maximum((cnt + MOE_TILE - 1) // MOE_TILE, 1)
    tile_end = jnp.cumsum(tiles)
    row_off = (tile_end - tiles) * MOE_TILE
    classes = jnp.arange(N_CLASSES, dtype=jnp.int32)
    pos = jnp.sum(jnp.where(cls[:, None] == classes[None, :], row_off[None, :], 0), axis=1) + rank
    n_tiles = t // MOE_TILE + N_CLASSES
    n_used = tile_end[-1]
    ztile = jnp.concatenate([(tile_end - 1) * MOE_TILE, n_used.reshape(1)])
    j = jnp.minimum(jnp.arange(n_tiles, dtype=jnp.int32), n_used - 1)
    tile_cls = jnp.sum((tile_end[None, :] <= j[:, None]).astype(jnp.int32), axis=1)
    pair = tile_cls % PAIRS_PER_GROUP
    pair_a = jnp.sum(jnp.where(pair[:, None] == jnp.arange(PAIRS_PER_GROUP)[None, :], _PAIR_A[None, :], 0), axis=1)
    pair_b = jnp.sum(jnp.where(pair[:, None] == jnp.arange(PAIRS_PER_GROUP)[None, :], _PAIR_B[None, :], 0), axis=1)
    base = (tile_cls // PAIRS_PER_GROUP) * EXPERTS_PER_GROUP
    xs = _dispatch(rows, pos, ztile, n_tiles * MOE_TILE)
    ys = _experts(xs, base + pair_a, base + pair_b, j, n_used.reshape(1), wr_t, br, wg, wu, wd)
    return ys, pos


def _rope_tables(seq):
    inv_freq = ROPE_THETA ** (-jnp.arange(0, ROT_DIM, 2, dtype=jnp.float32) / ROT_DIM)
    ang = jnp.arange(seq).astype(jnp.float32)[:, None] * inv_freq[None, :]
    cos, sin = jnp.cos(ang), jnp.sin(ang)
    ones = jnp.ones((seq, HEAD_DIM - ROT_DIM), jnp.float32)
    cos_h = jnp.concatenate([cos, cos, ones], axis=1)
    sgn_h = jnp.concatenate([-sin, sin, 0.0 * ones], axis=1)
    reps = LANES // HEAD_DIM
    return jnp.tile(cos_h, (1, reps)).astype(F32), jnp.tile(sgn_h, (1, reps)).astype(F32)


def _pad_heads(w, axis):
    pad = [(0, 0)] * w.ndim
    pad[axis] = (0, GROUP_WIDTH - HEADS_PER_GROUP * HEAD_DIM)
    return jnp.pad(w, pad)


def _router_weights(w_group, b_group, w_expert, b_expert):
    we = jnp.transpose(w_expert, (1, 0, 2)).reshape(D_MODEL, N_EXPERTS)
    w = jnp.concatenate([w_group, we], axis=1)
    w = jnp.pad(w, ((0, 0), (0, ROUTER_ROWS - w.shape[1])))
    bias = jnp.pad(jnp.concatenate([b_group, b_expert.reshape(-1)]), (0, ROUTER_ROWS - 20))
    return w.T.astype(BF16), bias.reshape(ROUTER_ROWS, 1)


def kernel(x, c, norm_mix, norm_ffn, w_ada, b_ada, a_w_qkv, a_w_o, b_w_in, b_v_gain, b_w_s, b_b_s, b_w_o,
           r_w_group, r_b_group, r_w_expert, r_b_expert, e_w_gate, e_w_up, e_w_down, final_norm):
    b, s, _ = x.shape
    mod = _ada_mod(c, w_ada, b_ada)
    sh_m, sc_m, gt_m, sh_f, sc_f, gt_f = [mod[:, :, i:i + 1] for i in range(6)]
    row = lambda v: v.reshape(1, -1)
    gw = HEADS_PER_GROUP * HEAD_DIM

    wq, wk, wv = [a_w_qkv[0][:, i * 3 * gw:(i + 1) * 3 * gw] for i in range(3)]
    wq = wq * (HEAD_DIM ** -0.5)
    w_groups = jnp.stack([
        jnp.concatenate([_pad_heads(w[:, g * gw:(g + 1) * gw], 1) for w in (wq, wk, wv)], axis=1)
        for g in range(N_ATTN_GROUPS)]).astype(BF16)
    wo_groups = jnp.stack([_pad_heads(a_w_o[0][g * gw:(g + 1) * gw], 0)
                           for g in range(N_ATTN_GROUPS)]).astype(BF16)
    expand = (jnp.arange(LANES)[:, None] == jnp.arange(GROUP_WIDTH)[None, :] // HEAD_DIM).astype(BF16)
    cos_t, sgn_t = _rope_tables(s)
    qkv = _qkv(x, row(norm_mix[0]), sh_m[0], sc_m[0], w_groups, cos_t, sgn_t)
    o_list, l_list = [], []
    for g in range(N_ATTN_GROUPS):
        o, l = _attention(*qkv[3 * g:3 * g + 3])
        o_list.append(o)
        l_list.append(l)
    wr_t, br = _router_weights(r_w_group[0], r_b_group[0], r_w_expert[0], r_b_expert[0])
    x1, rows, route, counts = _attn_out(o_list, l_list, x, wo_groups, expand, gt_m[0], row(norm_ffn[0]),
                                        sh_f[0], sc_f[0], wr_t, br)
    t = b * s
    flat = lambda r: r.reshape(-1, r.shape[-2], r.shape[-1])
    ys, pos = _moe_sorted(rows, flat(route), counts, wr_t, br, e_w_gate[0].astype(BF16), e_w_up[0].astype(BF16),
                          e_w_down[0].astype(BF16))

    wr_t, br = _router_weights(r_w_group[1], r_b_group[1], r_w_expert[1], r_b_expert[1])
    x3, rows, route, counts = _gmlp(ys, pos, gt_f[0], x1, row(norm_mix[1]), sh_m[1], sc_m[1],
                                    b_w_in[0].astype(BF16), row(b_v_gain[0]), b_w_s[0].astype(BF16), b_b_s[0].T,
                                    b_w_o[0].astype(BF16), gt_m[1], row(norm_ffn[1]), sh_f[1], sc_f[1], wr_t, br)
    ys, pos = _moe_sorted(rows, flat(route), counts, wr_t, br, e_w_gate[1].astype(BF16), e_w_up[1].astype(BF16),
                          e_w_down[1].astype(BF16))
    out = _final(ys, pos, x3.reshape(t, D_MODEL), gt_f[1], row(final_norm), s)
    return out.reshape(b, s, D_MODEL)
```

```python
import functools

import jax
import jax.numpy as jnp
import numpy as np
from jax import lax
from jax.experimental import pallas as pl
from jax.experimental.pallas import tpu as pltpu

D_MODEL = 1024
DEPTH = 2
HEAD_DIM = 64
HEADS_PER_GROUP = 5
N_ATTN_GROUPS = 3
DILATIONS = (1, 4, 16)
BAND_RADIUS = 64
ROT_DIM = HEAD_DIM // 4
ROPE_THETA = 500000.0
NEG_INF = -1e30
CHUNK = 128
GMLP_HALF = 2 * D_MODEL
GMLP_GROUPS = 8
N_EXPERT_GROUPS = 4
EXPERTS_PER_GROUP = 4
N_EXPERTS = 16
D_EXPERT = 256
EPS = 1e-6

LANES = 128
GROUP_WIDTH = 384
ROUTER_ROWS = 32
PAIRS_PER_GROUP = 6
N_CLASSES = N_EXPERT_GROUPS * PAIRS_PER_GROUP
CLASS_ROWS = 32
SUBLANES = 8
TOKEN_ROWS = D_MODEL // LANES
MOE_TILE = 512

F32 = jnp.float32
BF16 = jnp.bfloat16
VMEM_LIMIT = 56 * 1024 * 1024


def _cparams(sem):
    return pltpu.CompilerParams(dimension_semantics=sem, vmem_limit_bytes=VMEM_LIMIT)


def _const_spec(shape, index_map):
    return pl.BlockSpec(shape, index_map, pipeline_mode=pl.Buffered(1))


def _silu(x):
    return x * (1.0 / (1.0 + jnp.exp(-x)))


def _gelu(x):
    return 0.5 * x * (1.0 + lax.erf(x * (2.0 ** -0.5)))


def _norm_modulate(x, gain, shift, scale):
    ms = jnp.mean(x * x, axis=-1, keepdims=True)
    return (x * lax.rsqrt(ms + EPS) * gain) * (1.0 + scale) + shift


def _split_bf16(x):
    hi = x.astype(BF16)
    return hi, (x - hi.astype(F32)).astype(BF16)


def _ada_kernel(c_ref, w_ref, b_ref, o_ref):
    ca_hi, ca_lo = _split_bf16(_silu(c_ref[...]))
    w_hi, w_lo = _split_bf16(w_ref[0])
    dot = functools.partial(jnp.dot, preferred_element_type=F32)
    o_ref[0] = dot(ca_hi, w_hi) + dot(ca_lo, w_hi) + dot(ca_hi, w_lo) + b_ref[0]


def _ada_mod(c, w_ada, b_ada):
    b = c.shape[0]
    tn = 1536
    c8 = jnp.pad(c, ((0, 8 - b), (0, 0)))
    out = pl.pallas_call(
        _ada_kernel,
        grid=(DEPTH, 6 * D_MODEL // tn),
        in_specs=[pl.BlockSpec((8, D_MODEL), lambda l, j: (0, 0)),
                  pl.BlockSpec((1, D_MODEL, tn), lambda l, j: (l, 0, j)),
                  pl.BlockSpec((1, 1, tn), lambda l, j: (l, 0, j))],
        out_specs=pl.BlockSpec((1, 8, tn), lambda l, j: (l, 0, j)),
        out_shape=jax.ShapeDtypeStruct((DEPTH, 8, 6 * D_MODEL), F32),
        compiler_params=_cparams(("arbitrary", "arbitrary")),
        name="ada_mod",
    )(c8, w_ada, b_ada.reshape(DEPTH, 1, 6 * D_MODEL))
    return out[:, :b].reshape(DEPTH, b, 6, 1, D_MODEL)


def _router_logits(hf, wr_ref, br_ref):
    return lax.dot_general(wr_ref[...], hf, (((1,), (1,)), ((), ())), preferred_element_type=F32) + br_ref[...]


def _group_choice(lt):
    gl = [lt[i:i + 1, :] for i in range(N_EXPERT_GROUPS)]
    gmax = jnp.maximum(jnp.maximum(gl[0], gl[1]), jnp.maximum(gl[2], gl[3]))
    ge = [jnp.exp(g - gmax) for g in gl]
    gsum = ge[0] + ge[1] + ge[2] + ge[3]
    gp = [e / gsum for e in ge]
    g_val = jnp.maximum(jnp.maximum(gp[0], gp[1]), jnp.maximum(gp[2], gp[3]))
    g_idx = jnp.where(gp[0] == g_val, 0, jnp.where(gp[1] == g_val, 1, jnp.where(gp[2] == g_val, 2, 3)))
    return g_val, g_idx


def _ffn_prep(x1, gain, shift, scale, wr_ref, br_ref):
    hf32 = _norm_modulate(x1, gain, shift, scale)
    lt = _router_logits(hf32.astype(BF16), wr_ref, br_ref)
    _, g_idx = _group_choice(lt)
    el = []
    for j in range(EXPERTS_PER_GROUP):
        rows = [lt[4 + 4 * g + j:5 + 4 * g + j, :] for g in range(N_EXPERT_GROUPS)]
        el.append(jnp.where(g_idx == 0, rows[0], jnp.where(g_idx == 1, rows[1],
                  jnp.where(g_idx == 2, rows[2], rows[3]))))
    v1 = jnp.maximum(jnp.maximum(el[0], el[1]), jnp.maximum(el[2], el[3]))
    i1 = jnp.where(el[0] == v1, 0, jnp.where(el[1] == v1, 1, jnp.where(el[2] == v1, 2, 3)))
    el2 = [jnp.where(i1 == j, -jnp.inf, el[j]) for j in range(EXPERTS_PER_GROUP)]
    v2 = jnp.maximum(jnp.maximum(el2[0], el2[1]), jnp.maximum(el2[2], el2[3]))
    i2 = jnp.where((el2[0] == v2) & (i1 != 0), 0,
                   jnp.where((el2[1] == v2) & (i1 != 1), 1,
                             jnp.where((el2[2] == v2) & (i1 != 2), 2, 3)))
    a = jnp.minimum(i1, i2)
    bb = jnp.maximum(i1, i2)
    cls = g_idx * PAIRS_PER_GROUP + ((a * (7 - a)) >> 1) + (bb - a - 1)
    return hf32, cls


def _place_tokens(cls, carry_ref, cur_ref, tab_ref, first_step):
    tm = cls.shape[1]

    @pl.when(first_step)
    def _():
        carry_ref[...] = jnp.zeros_like(carry_ref)
        cur_ref[...] = jnp.zeros_like(cur_ref)
        tab_ref[...] = jnp.zeros_like(tab_ref)

    crow = lax.broadcasted_iota(jnp.int32, (CLASS_ROWS, tm), 0)
    onehot = (crow == cls).astype(F32)
    si = lax.broadcasted_iota(jnp.int32, (tm, tm), 0)
    ti = lax.broadcasted_iota(jnp.int32, (tm, tm), 1)
    before = (si < ti).astype(BF16)
    prefix = jnp.dot(onehot.astype(BF16), before, preferred_element_type=F32)
    seen = prefix + carry_ref[:, 0:1]
    rank = jnp.sum(onehot * seen, axis=0, keepdims=True)
    inv_tile = 1.0 / MOE_TILE
    rem = rank - MOE_TILE * jnp.floor(rank * inv_tile)
    opens = (rem == 0.0).astype(F32)
    opened_before = jnp.sum(jnp.floor((seen + (MOE_TILE - 1)) * inv_tile), axis=0, keepdims=True)
    sel = onehot * opens
    opened = jnp.sum(sel, axis=1, keepdims=True) > 0.0
    new_tile = jnp.sum(sel * opened_before, axis=1, keepdims=True)
    first_rank = jnp.sum(sel * rank, axis=1, keepdims=True)
    tile = jnp.sum(onehot * jnp.where(opened & (seen >= first_rank), new_tile, cur_ref[:, 0:1]),
                   axis=0, keepdims=True)
    carry_ref[...] = carry_ref[...] + jnp.sum(onehot, axis=1, keepdims=True)
    cur_ref[...] = jnp.where(opened, new_tile, cur_ref[...])
    lane = lax.broadcasted_iota(jnp.int32, (CLASS_ROWS, LANES), 1).astype(F32)
    cidx = lax.broadcasted_iota(jnp.int32, (CLASS_ROWS, LANES), 0).astype(F32)
    tab_ref[...] = tab_ref[...] + jnp.where(opened & (lane == new_tile), cidx, 0.0)
    return (tile * MOE_TILE + rem).astype(jnp.int32)


def _rope(t, cos, sgn, first_half):
    out = []
    for c in range(GROUP_WIDTH // LANES):
        xc = t[:, c * LANES:(c + 1) * LANES]
        other = jnp.where(first_half, pltpu.roll(xc, LANES - ROT_DIM // 2, 1),
                          pltpu.roll(xc, ROT_DIM // 2, 1))
        out.append(xc * cos + other * sgn)
    return jnp.concatenate(out, axis=1)


def _qkv_kernel(x_ref, gain_ref, sh_ref, sc_ref, w_ref, cos_ref, sgn_ref, *rest, tm):
    out_refs, hn_scr = rest[:9], rest[9]
    hn = _norm_modulate(x_ref[0], gain_ref[...], sh_ref[0, 0], sc_ref[0, 0])
    n_slab = D_MODEL // LANES
    lane = lax.broadcasted_iota(jnp.int32, (tm, LANES), 1) % HEAD_DIM
    first_half = lane < ROT_DIM // 2
    for g, d in enumerate(DILATIONS):
        rows = tm // d
        if d == 1:
            hp, cos, sgn = hn, cos_ref[...], sgn_ref[...]
        else:
            if g == 1:
                for s in range(n_slab):
                    hn_scr[s] = hn[:, s * LANES:(s + 1) * LANES]
            hp = jnp.concatenate(
                [jnp.concatenate([hn_scr[s, pl.ds(r, rows, stride=d), :] for s in range(n_slab)], axis=1)
                 for r in range(d)], axis=0)
            cos = jnp.concatenate([cos_ref[pl.ds(r, rows, stride=d), :] for r in range(d)], axis=0)
            sgn = jnp.concatenate([sgn_ref[pl.ds(r, rows, stride=d), :] for r in range(d)], axis=0)
        res = jnp.dot(hp.astype(BF16), w_ref[g], preferred_element_type=F32)
        q = _rope(res[:, :GROUP_WIDTH], cos, sgn, first_half)
        k = _rope(res[:, GROUP_WIDTH:2 * GROUP_WIDTH], cos, sgn, first_half)
        v = res[:, 2 * GROUP_WIDTH:]
        for t, o_ref in zip((q, k, v), out_refs[3 * g:3 * g + 3]):
            o_ref[0] = t.astype(BF16).reshape(d, rows, GROUP_WIDTH)


def _qkv(x, gain, shift, scale, w_groups, cos_t, sgn_t, tm=512):
    b, s, _ = x.shape
    out_shape, out_specs = [], []
    for d in DILATIONS:
        for _ in range(3):
            out_shape.append(jax.ShapeDtypeStruct((b, d, s // d, GROUP_WIDTH), BF16))
            out_specs.append(pl.BlockSpec((1, d, tm // d, GROUP_WIDTH), lambda bi, i: (bi, 0, i, 0)))
    return pl.pallas_call(
        functools.partial(_qkv_kernel, tm=tm),
        grid=(b, s // tm),
        in_specs=[pl.BlockSpec((1, tm, D_MODEL), lambda bi, i: (bi, i, 0)),
                  _const_spec((1, D_MODEL), lambda bi, i: (0, 0)),
                  pl.BlockSpec((1, 1, 1, D_MODEL), lambda bi, i: (bi, 0, 0, 0)),
                  pl.BlockSpec((1, 1, 1, D_MODEL), lambda bi, i: (bi, 0, 0, 0)),
                  _const_spec((N_ATTN_GROUPS, D_MODEL, 3 * GROUP_WIDTH), lambda bi, i: (0, 0, 0)),
                  pl.BlockSpec((tm, LANES), lambda bi, i: (i, 0)),
                  pl.BlockSpec((tm, LANES), lambda bi, i: (i, 0))],
        out_specs=out_specs,
        out_shape=out_shape,
        scratch_shapes=[pltpu.VMEM((D_MODEL // LANES, tm, LANES), F32)],
        compiler_params=_cparams(("arbitrary", "arbitrary")),
        name="qkv_rope",
    )(x, gain, shift, scale, w_groups, cos_t, sgn_t)


def _attn_kernel(q_ref, kp_ref, kc_ref, kn_ref, vp_ref, vc_ref, vn_ref, o_ref, lse_ref,
                 kbuf, vbuf, *, seq, tq):
    i = pl.program_id(2)
    r = BAND_RADIUS
    kbuf[0:r] = kp_ref[0, 0]
    kbuf[r:r + tq] = kc_ref[0, 0]
    kbuf[r + tq:] = kn_ref[0, 0]
    vbuf[0:r] = vp_ref[0, 0]
    vbuf[r:r + tq] = vc_ref[0, 0]
    vbuf[r + tq:] = vn_ref[0, 0]
    qb = 2 * r
    lane = lax.broadcasted_iota(jnp.int32, (qb, LANES), 1)
    low = lane < HEAD_DIM
    tidx = lax.broadcasted_iota(jnp.int32, (qb, 2 * qb), 0)
    kidx = lax.broadcasted_iota(jnp.int32, (qb, 2 * qb), 1)
    rel = kidx - tidx
    band = (rel >= 0) & (rel <= 2 * r)
    for j in range(tq // qb):
        kpos = i * tq + (j * qb - r) + kidx
        mask = band & (kpos >= 0) & (kpos < seq)
        q = q_ref[0, 0, j * qb:(j + 1) * qb, :]
        k = kbuf[j * qb:(j + 2) * qb, :]
        v = vbuf[j * qb:(j + 2) * qb, :]
        lse_tile = jnp.zeros((qb, LANES), F32)
        o_chunks = []
        for c in range(GROUP_WIDTH // LANES):
            qc = q[:, c * LANES:(c + 1) * LANES]
            kc = k[:, c * LANES:(c + 1) * LANES]
            vc = v[:, c * LANES:(c + 1) * LANES]
            outs = []
            for hh in range(2):
                head = 2 * c + hh
                if head >= HEADS_PER_GROUP:
                    outs.append(jnp.zeros((qb, LANES), F32))
                    continue
                qm = jnp.where(low if hh == 0 else jnp.logical_not(low), qc, jnp.zeros_like(qc))
                s = lax.dot_general(qm, kc, (((1,), (1,)), ((), ())), preferred_element_type=F32)
                s = jnp.where(mask, s, NEG_INF)
                m = jnp.max(s, axis=-1, keepdims=True)
                p = jnp.exp(s - m)
                den = jnp.sum(p, axis=-1, keepdims=True)
                o = jnp.dot(p.astype(BF16), vc, preferred_element_type=F32) / den
                lse_tile = jnp.where(lane == head, m + jnp.log(den), lse_tile)
                outs.append(o)
            o_chunks.append(jnp.where(low, outs[0], outs[1]))
        o_ref[0, 0, j * qb:(j + 1) * qb, :] = jnp.concatenate(o_chunks, axis=1).astype(BF16)
        lse_ref[0, 0, j * qb:(j + 1) * qb, :] = lse_tile


def _attention(q, k, v, tq=512):
    b, d, seq, w = q.shape
    r = BAND_RADIUS
    nb = seq // r
    cur = pl.BlockSpec((1, 1, tq, w), lambda bi, ri, i: (bi, ri, i, 0))
    prev = pl.BlockSpec((1, 1, r, w), lambda bi, ri, i: (bi, ri, jnp.maximum(i * (tq // r) - 1, 0), 0))
    nxt = pl.BlockSpec((1, 1, r, w), lambda bi, ri, i: (bi, ri, jnp.minimum((i + 1) * (tq // r), nb - 1), 0))
    return pl.pallas_call(
        functools.partial(_attn_kernel, seq=seq, tq=tq),
        grid=(b, d, seq // tq),
        in_specs=[cur, prev, cur, nxt, prev, cur, nxt],
        out_specs=[pl.BlockSpec((1, 1, tq, w), lambda bi, ri, i: (bi, ri, i, 0)),
                   pl.BlockSpec((1, 1, tq, LANES), lambda bi, ri, i: (bi, ri, i, 0))],
        out_shape=[jax.ShapeDtypeStruct((b, d, seq, w), BF16),
                   jax.ShapeDtypeStruct((b, d, seq, LANES), F32)],
        scratch_shapes=[pltpu.VMEM((tq + 2 * r, w), BF16), pltpu.VMEM((tq + 2 * r, w), BF16)],
        compiler_params=_cparams(("arbitrary", "arbitrary", "arbitrary")),
        name=f"band_attn_d{d}",
    )(q, k, k, k, v, v, v)


def _attn_out_kernel(o0, o1, o2, l0, l1, l2, x_ref, wo_ref, exp_ref, gt_ref, gain_ref, sh_ref, sc_ref,
                     wr_ref, br_ref, x1_ref, xs_hbm, pos_ref, tab_out, cnt_out, oscr, lscr, *route_scratch,
                     tm, n_steps):
    n_slab = GROUP_WIDTH // LANES
    for g, (d, o_ref, l_ref) in enumerate(zip(DILATIONS, (o0, o1, o2), (l0, l1, l2))):
        rows = tm // d
        for r in range(d):
            dst = slice(None) if d == 1 else pl.ds(r, rows, stride=d)
            blk = o_ref[0, r].astype(F32)
            for c in range(n_slab):
                oscr[g * n_slab + c, dst, :] = blk[:, c * LANES:(c + 1) * LANES]
            lscr[g, dst, :] = l_ref[0, r]
    lse = [lscr[g] for g in range(N_ATTN_GROUPS)]
    mx = jnp.maximum(jnp.maximum(lse[0], lse[1]), lse[2])
    ex = [jnp.exp(l - mx) for l in lse]
    tot = ex[0] + ex[1] + ex[2]
    y = jnp.zeros((tm, D_MODEL), F32)
    for g in range(N_ATTN_GROUPS):
        alpha = ex[g] / tot
        a_hi, a_lo = _split_bf16(alpha)
        a_wide = (jnp.dot(a_hi, exp_ref[...], preferred_element_type=F32)
                  + jnp.dot(a_lo, exp_ref[...], preferred_element_type=F32))
        og = jnp.concatenate([oscr[g * n_slab + c] for c in range(n_slab)], axis=1)
        y = y + jnp.dot((og * a_wide).astype(BF16), wo_ref[g], preferred_element_type=F32)
    x1 = x_ref[0] + gt_ref[0, 0] * y
    x1_ref[0] = x1
    _route_outputs(x1, gain_ref, sh_ref, sc_ref, wr_ref, br_ref, xs_hbm, pos_ref, tab_out, cnt_out, route_scratch,
                   n_steps)


ZERO_TOKENS = MOE_TILE // 2


def _route_outputs(x1, gain_ref, sh_ref, sc_ref, wr_ref, br_ref, xs_hbm, pos_ref, tab_out, cnt_out, scratch,
                   n_steps):
    (carry_ref, cur_ref, tab_ref, rows_scr, pos_vmem, pos_smem, fin_vmem, fin_smem, zbuf, sems, aux_sem) = scratch
    step = pl.program_id(0) * pl.num_programs(1) + pl.program_id(1)
    tm = x1.shape[0]
    rows = tm * TOKEN_ROWS
    hf32, cls = _ffn_prep(x1, gain_ref[...], sh_ref[0, 0], sc_ref[0, 0], wr_ref, br_ref)
    pos = _place_tokens(cls, carry_ref, cur_ref, tab_ref, step == 0)
    r8 = lax.broadcasted_iota(jnp.int32, (SUBLANES, tm), 0)
    pos8 = jnp.where(r8 == 0, pos, 0)
    pos_ref[0, 0] = pos8
    pos_vmem[...] = pos8
    to_smem = pltpu.make_async_copy(pos_vmem, pos_smem, aux_sem)
    to_smem.start()

    def drain(slot):
        pltpu.make_async_copy(rows_scr.at[pl.ds(0, rows), :], xs_hbm.at[pl.ds(0, rows), :], sems.at[slot]).wait()

    slot = step % 2

    @pl.when(step >= 2)
    def _():
        drain(slot)

    base = pl.multiple_of(slot * rows, rows)
    for s in range(TOKEN_ROWS):
        rows_scr[pl.ds(base + s, tm, stride=TOKEN_ROWS), :] = hf32[:, s * LANES:(s + 1) * LANES]
    to_smem.wait()

    def issue(r, k):
        _token_copy(rows_scr, slot * tm + r, xs_hbm, pos_smem[0, r], sems.at[slot]).start(priority=k % 2)

    _issue_burst(tm, issue)
    tab_out[...] = tab_ref[...]
    cnt_out[...] = carry_ref[...]

    @pl.when(step == n_steps - 1)
    def _():
        drain(slot)
        if n_steps > 1:
            drain(1 - slot)
        _clear_unused_rows(carry_ref, cur_ref, xs_hbm, fin_vmem, fin_smem, zbuf, aux_sem)


def _clear_unused_rows(carry_ref, cur_ref, xs_hbm, fin_vmem, fin_smem, zbuf, sem):
    lane = lax.broadcasted_iota(jnp.int32, (CLASS_ROWS, LANES), 1)
    crow = lax.broadcasted_iota(jnp.int32, (CLASS_ROWS, LANES), 0)
    on_lane = lambda col: jnp.sum(jnp.where(lane == crow, col, 0.0), axis=0, keepdims=True)
    counts = carry_ref[...]
    used = jnp.sum(jnp.floor((counts + (MOE_TILE - 1)) * (1.0 / MOE_TILE)), axis=0, keepdims=True)
    r8 = lax.broadcasted_iota(jnp.int32, (SUBLANES, LANES), 0)
    fin = jnp.where(r8 == 0, on_lane(cur_ref[...]), jnp.where(r8 == 1, on_lane(counts), jnp.where(r8 == 2, used, 0.0)))
    fin_vmem[...] = fin.astype(jnp.int32)
    to_smem = pltpu.make_async_copy(fin_vmem, fin_smem, sem)
    to_smem.start()
    zbuf[...] = jnp.zeros_like(zbuf)
    to_smem.wait()

    def zero_copy(first_tok, n_tok):
        d0 = pl.multiple_of(first_tok * TOKEN_ROWS, TOKEN_ROWS)
        return pltpu.make_async_copy(zbuf.at[pl.ds(0, n_tok * TOKEN_ROWS), :],
                                     xs_hbm.at[pl.ds(d0, n_tok * TOKEN_ROWS), :], sem)

    for wait in (False, True):
        for c in range(N_CLASSES):
            count = fin_smem[1, c]
            fill = jnp.where(count > 0, (MOE_TILE - count % MOE_TILE) % MOE_TILE, 0)
            first = fin_smem[0, c] * MOE_TILE + (MOE_TILE - fill)
            piece = ZERO_TOKENS
            while piece >= 1:
                done = (fill // (2 * piece)) * (2 * piece)

                @pl.when((fill // piece) % 2 == 1)
                def _(first=first, done=done, piece=piece):
                    cp = zero_copy(first + done, piece)
                    cp.wait() if wait else cp.start()

                piece //= 2

    def clear_tile(j, carry):
        for half in range(MOE_TILE // ZERO_TOKENS):
            cp = zero_copy(j * MOE_TILE + half * ZERO_TOKENS, ZERO_TOKENS)
            cp.start()
            cp.wait()
        return carry

    lax.fori_loop(fin_smem[2, 0], xs_hbm.shape[0] // (MOE_TILE * TOKEN_ROWS), clear_tile, 0)


def _route_scratch(tm):
    return [pltpu.VMEM((CLASS_ROWS, LANES), F32), pltpu.VMEM((CLASS_ROWS, LANES), F32),
            pltpu.VMEM((CLASS_ROWS, LANES), F32),
            pltpu.VMEM((2 * tm * TOKEN_ROWS, LANES), F32),
            pltpu.VMEM((SUBLANES, tm), jnp.int32), pltpu.SMEM((SUBLANES, tm), jnp.int32),
            pltpu.VMEM((SUBLANES, LANES), jnp.int32), pltpu.SMEM((SUBLANES, LANES), jnp.int32),
            pltpu.VMEM((ZERO_TOKENS * TOKEN_ROWS, LANES), F32),
            pltpu.SemaphoreType.DMA((2,)), pltpu.SemaphoreType.DMA]


N_ROUTE_SCRATCH = 11


def _route_out_specs(b, s, tm):
    n_tiles = b * s // MOE_TILE + N_CLASSES
    specs = [pl.BlockSpec((1, tm, D_MODEL), lambda bi, i: (bi, i, 0)),
             pl.BlockSpec(memory_space=pl.ANY),
             pl.BlockSpec((1, 1, SUBLANES, tm), lambda bi, i: (bi, i, 0, 0)),
             pl.BlockSpec((CLASS_ROWS, LANES), lambda bi, i: (0, 0)),
             pl.BlockSpec((CLASS_ROWS, LANES), lambda bi, i: (0, 0))]
    shapes = [jax.ShapeDtypeStruct((b, s, D_MODEL), F32),
              jax.ShapeDtypeStruct((n_tiles * MOE_TILE * TOKEN_ROWS, LANES), F32),
              jax.ShapeDtypeStruct((b, s // tm, SUBLANES, tm), jnp.int32),
              jax.ShapeDtypeStruct((CLASS_ROWS, LANES), F32),
              jax.ShapeDtypeStruct((CLASS_ROWS, LANES), F32)]
    return specs, shapes


def _mod_spec():
    return pl.BlockSpec((1, 1, 1, D_MODEL), lambda bi, i: (bi, 0, 0, 0))


def _attn_out(o_list, l_list, x, wo_groups, expand, gt, gain, shift, scale, wr_t, br, tm=512):
    b, s, _ = x.shape
    o_specs = [pl.BlockSpec((1, d, tm // d, GROUP_WIDTH), lambda bi, i: (bi, 0, i, 0)) for d in DILATIONS]
    l_specs = [pl.BlockSpec((1, d, tm // d, LANES), lambda bi, i: (bi, 0, i, 0)) for d in DILATIONS]
    out_specs, out_shape = _route_out_specs(b, s, tm)
    return pl.pallas_call(
        functools.partial(_attn_out_kernel, tm=tm, n_steps=b * s // tm),
        grid=(b, s // tm),
        in_specs=o_specs + l_specs + [
            pl.BlockSpec((1, tm, D_MODEL), lambda bi, i: (bi, i, 0)),
            _const_spec((N_ATTN_GROUPS, GROUP_WIDTH, D_MODEL), lambda bi, i: (0, 0, 0)),
            _const_spec((LANES, GROUP_WIDTH), lambda bi, i: (0, 0)),
            _mod_spec(),
            _const_spec((1, D_MODEL), lambda bi, i: (0, 0)),
            _mod_spec(), _mod_spec(),
            _const_spec((ROUTER_ROWS, D_MODEL), lambda bi, i: (0, 0)),
            _const_spec((ROUTER_ROWS, 1), lambda bi, i: (0, 0))],
        out_specs=out_specs,
        out_shape=out_shape,
        scratch_shapes=[pltpu.VMEM((N_ATTN_GROUPS * GROUP_WIDTH // LANES, tm, LANES), F32),
                        pltpu.VMEM((N_ATTN_GROUPS, tm, LANES), F32)] + _route_scratch(tm),
        compiler_params=_cparams(("arbitrary", "arbitrary")),
        name="attn_merge_proj",
    )(*o_list, *l_list, x, wo_groups, expand, gt, gain, shift, scale, wr_t, br)


DMA_BURST_UNROLL = 8


def _issue_burst(n, issue):
    def body(i, carry):
        for k in range(DMA_BURST_UNROLL):
            issue(i * DMA_BURST_UNROLL + k, k)
        return carry

    lax.fori_loop(0, n // DMA_BURST_UNROLL, body, 0)


def _gather_moe_rows(pos_ref, pos_next_ref, ys_hbm, ybuf, sems, step, n_steps, tm):
    rows = tm * TOKEN_ROWS

    def request(p_ref, slot):
        def issue(r, k):
            _token_copy(ys_hbm, p_ref[0, 0, r], ybuf, slot * tm + r, sems.at[slot]).start(priority=k % 2)

        _issue_burst(tm, issue)

    @pl.when(step == 0)
    def _():
        request(pos_ref, 0)

    @pl.when(step + 1 < n_steps)
    def _():
        request(pos_next_ref, (step + 1) % 2)

    slot = step % 2
    base = pl.multiple_of(slot * rows, rows)
    pltpu.make_async_copy(ys_hbm.at[pl.ds(0, rows), :], ybuf.at[pl.ds(base, rows), :], sems.at[slot]).wait()
    return jnp.concatenate([ybuf[pl.ds(base + s, tm, stride=TOKEN_ROWS), :] for s in range(D_MODEL // LANES)],
                           axis=1)


def _gather_specs(n_steps, tm, index_of):
    return [pl.BlockSpec((1, 1, tm), lambda *g: (index_of(*g), 0, 0), memory_space=pltpu.SMEM),
            pl.BlockSpec((1, 1, tm), lambda *g: (jnp.minimum(index_of(*g) + 1, n_steps - 1), 0, 0),
                         memory_space=pltpu.SMEM),
            pl.BlockSpec(memory_space=pl.ANY)]


def _gather_scratch(tm):
    return [pltpu.VMEM((2 * tm * TOKEN_ROWS, LANES), F32), pltpu.SemaphoreType.DMA((2,))]


def _gmlp_kernel(pos_ref, pos_next_ref, ys_hbm, gtf_ref, x_ref, gain_ref, sh_ref, sc_ref, win_ref, vg_ref, ws_ref,
                 bs_ref, wo_ref, gt_ref, fgain_ref, fsh_ref, fsc_ref, wr_ref, br_ref, x1_ref, xs_hbm, pos_out, tab_out,
                 cnt_out, gated_scr, ybuf, sems, *route_scratch, tm, n_steps):
    step = pl.program_id(0) * pl.num_programs(1) + pl.program_id(1)
    y_moe = _gather_moe_rows(pos_ref, pos_next_ref, ys_hbm, ybuf, sems, step, n_steps, tm)
    x = x_ref[0] + gtf_ref[0, 0] * y_moe
    hn = _norm_modulate(x, gain_ref[...], sh_ref[0, 0], sc_ref[0, 0]).astype(BF16)
    v = _gelu(jnp.dot(hn, win_ref[:, GMLP_HALF:], preferred_element_type=F32))
    u = _gelu(jnp.dot(hn, win_ref[:, :GMLP_HALF], preferred_element_type=F32))
    mu = jnp.mean(v, axis=-1, keepdims=True)
    vc = v - mu
    vn = (vc * lax.rsqrt(jnp.mean(vc * vc, axis=-1, keepdims=True) + EPS) * vg_ref[...]).astype(BF16)
    gw = GMLP_HALF // GMLP_GROUPS
    for c in range(tm // CHUNK):
        rs = slice(c * CHUNK, (c + 1) * CHUNK)
        for g in range(GMLP_GROUPS):
            cs = slice(g * gw, (g + 1) * gw)
            vs = jnp.dot(ws_ref[g], vn[rs, cs], preferred_element_type=F32) + bs_ref[:, g:g + 1]
            gated_scr[rs, cs] = (u[rs, cs] * vs).astype(BF16)
    y = jnp.dot(gated_scr[...], wo_ref[...], preferred_element_type=F32)
    x1 = x + gt_ref[0, 0] * y
    x1_ref[0] = x1
    _route_outputs(x1, fgain_ref, fsh_ref, fsc_ref, wr_ref, br_ref, xs_hbm, pos_out, tab_out, cnt_out,
                   route_scratch, n_steps)


def _gmlp(ys, pos, gt_prev, x, gain, shift, scale, w_in, v_gain, w_s, b_s_t, w_o, gt, fgain, fshift, fscale,
          wr_t, br, tm=512):
    b, s, _ = x.shape
    tok = lambda w: pl.BlockSpec((1, tm, w), lambda bi, i: (bi, i, 0))
    c2 = lambda shape: _const_spec(shape, lambda bi, i: (0,) * len(shape))
    out_specs, out_shape = _route_out_specs(b, s, tm)
    n_steps = b * s // tm
    return pl.pallas_call(
        functools.partial(_gmlp_kernel, tm=tm, n_steps=n_steps),
        grid=(b, s // tm),
        in_specs=_gather_specs(n_steps, tm, lambda bi, i: bi * (s // tm) + i) + [
                  _mod_spec(),
                  tok(D_MODEL), c2((1, D_MODEL)), _mod_spec(), _mod_spec(),
                  c2((D_MODEL, 2 * GMLP_HALF)), c2((1, GMLP_HALF)),
                  c2((GMLP_GROUPS, CHUNK, CHUNK)), c2((CHUNK, GMLP_GROUPS)),
                  c2((GMLP_HALF, D_MODEL)), _mod_spec(),
                  c2((1, D_MODEL)), _mod_spec(), _mod_spec(),
                  c2((ROUTER_ROWS, D_MODEL)), c2((ROUTER_ROWS, 1))],
        out_specs=out_specs,
        out_shape=out_shape,
        scratch_shapes=[pltpu.VMEM((tm, GMLP_HALF), BF16)] + _gather_scratch(tm) + _route_scratch(tm),
        compiler_params=_cparams(("arbitrary", "arbitrary")),
        name="gmlp",
    )(pos.reshape(n_steps, 1, tm), pos.reshape(n_steps, 1, tm), ys, gt_prev,
      x, gain, shift, scale, w_in, v_gain, w_s, b_s_t, w_o, gt, fgain, fshift, fscale, wr_t, br)


def _token_copy(src, src_tok, dst, dst_tok, sem):
    s0 = pl.multiple_of(src_tok * TOKEN_ROWS, TOKEN_ROWS)
    d0 = pl.multiple_of(dst_tok * TOKEN_ROWS, TOKEN_ROWS)
    return pltpu.make_async_copy(src.at[pl.ds(s0, TOKEN_ROWS), :], dst.at[pl.ds(d0, TOKEN_ROWS), :], sem)


def _slabs(ref, n_tok, n_slab):
    return jnp.concatenate([ref[pl.ds(s, n_tok, stride=TOKEN_ROWS), :] for s in range(n_slab)], axis=1)


def _expert_kernel(ea_ref, eb_ref, blk_ref, nused_ref, x_ref, wr_ref, br_ref, wga, wgb, wua, wub, wda, wdb, y_ref,
                   lt_scr):
    j = pl.program_id(0)

    @pl.when(j < nused_ref[0])
    def _():
        h = _slabs(x_ref, MOE_TILE, TOKEN_ROWS).astype(BF16)
        lt = _router_logits(h, wr_ref, br_ref)
        g_val, _ = _group_choice(lt)
        lt_scr[...] = lt
        el_a = lt_scr[pl.ds(N_EXPERT_GROUPS + ea_ref[j], 1), :]
        el_b = lt_scr[pl.ds(N_EXPERT_GROUPS + eb_ref[j], 1), :]
        top = jnp.maximum(el_a, el_b)
        p_a = jnp.exp(el_a - top)
        p_b = jnp.exp(el_b - top)
        row = lax.broadcasted_iota(jnp.int32, (LANES, MOE_TILE), 0)
        gates_t = jnp.where(row == 0, g_val * (p_a / (p_a + p_b)), jnp.where(row == 1, g_val * (p_b / (p_a + p_b)), 0.0))
        gates = gates_t.T
        y = jnp.zeros((MOE_TILE, D_MODEL), F32)
        for gate, wg, wu, wd in ((gates[:, 0:1], wga, wua, wda), (gates[:, 1:2], wgb, wub, wdb)):
            hg = jnp.dot(h, wg[0], preferred_element_type=F32)
            hu = jnp.dot(h, wu[0], preferred_element_type=F32)
            act = (_silu(hg) * hu * gate).astype(BF16)
            y = y + jnp.dot(act, wd[0], preferred_element_type=F32)
        for s in range(TOKEN_ROWS):
            y_ref[pl.ds(s, MOE_TILE, stride=TOKEN_ROWS), :] = y[:, s * LANES:(s + 1) * LANES]

    @pl.when(j >= nused_ref[0])
    def _():
        y_ref[...] = jnp.zeros_like(y_ref)


def _experts(xs, tile_ea, tile_eb, tile_blk, n_used, wr_t, br, wg, wu, wd):
    n_tiles = xs.shape[0] // (MOE_TILE * TOKEN_ROWS)
    up = lambda sel: pl.BlockSpec((1, D_MODEL, D_EXPERT), lambda j, ea, eb, blk, nu: ((ea, eb)[sel][j], 0, 0))
    down = lambda sel: pl.BlockSpec((1, D_EXPERT, D_MODEL), lambda j, ea, eb, blk, nu: ((ea, eb)[sel][j], 0, 0))
    shape = (MOE_TILE * TOKEN_ROWS, LANES)
    return pl.pallas_call(
        _expert_kernel,
        grid_spec=pltpu.PrefetchScalarGridSpec(
            num_scalar_prefetch=4,
            grid=(n_tiles,),
            in_specs=[pl.BlockSpec(shape, lambda j, ea, eb, blk, nu: (blk[j], 0)),
                      pl.BlockSpec((ROUTER_ROWS, D_MODEL), lambda j, ea, eb, blk, nu: (0, 0)),
                      pl.BlockSpec((ROUTER_ROWS, 1), lambda j, ea, eb, blk, nu: (0, 0)),
                      up(0), up(1), up(0), up(1), down(0), down(1)],
            out_specs=pl.BlockSpec(shape, lambda j, ea, eb, blk, nu: (j, 0)),
            scratch_shapes=[pltpu.VMEM((ROUTER_ROWS, MOE_TILE), F32)]),
        out_shape=jax.ShapeDtypeStruct(xs.shape, F32),
        compiler_params=_cparams(("arbitrary",)),
        name="moe_experts",
    )(tile_ea, tile_eb, tile_blk, n_used, xs, wr_t, br, wg, wg, wu, wu, wd, wd)


def _final_kernel(pos_ref, pos_next_ref, ys_hbm, x_ref, gt_ref, fin_ref, o_ref, ybuf, sems, *, tm):
    y_moe = _gather_moe_rows(pos_ref, pos_next_ref, ys_hbm, ybuf, sems, pl.program_id(0), pl.num_programs(0), tm)
    x2 = x_ref[...] + gt_ref[0, 0] * y_moe
    ms = jnp.mean(x2 * x2, axis=-1, keepdims=True)
    o_ref[...] = x2 * lax.rsqrt(ms + EPS) * fin_ref[...]


def _final(ys, pos, x, gt, fin_gain, seq, tm=512):
    t = x.shape[0]
    tiles_per_batch = seq // tm
    n_steps = t // tm
    return pl.pallas_call(
        functools.partial(_final_kernel, tm=tm),
        grid=(n_steps,),
        in_specs=_gather_specs(n_steps, tm, lambda i: i) + [
                  pl.BlockSpec((tm, D_MODEL), lambda i: (i, 0)),
                  pl.BlockSpec((1, 1, 1, D_MODEL), lambda i: (i // tiles_per_batch, 0, 0, 0)),
                  pl.BlockSpec((1, D_MODEL), lambda i: (0, 0))],
        out_specs=pl.BlockSpec((tm, D_MODEL), lambda i: (i, 0)),
        out_shape=jax.ShapeDtypeStruct((t, D_MODEL), F32),
        scratch_shapes=_gather_scratch(tm),
        compiler_params=_cparams(("arbitrary",)),
        name="moe_combine_final_norm",
    )(pos.reshape(n_steps, 1, tm), pos.reshape(n_steps, 1, tm), ys, x, gt, fin_gain)


_PAIR_A = np.array([0, 0, 0, 1, 1, 2], np.int32)
_PAIR_B = np.array([1, 2, 3, 2, 3, 3], np.int32)


def _moe_sorted(xs, tab, counts, wr_t, br, wg, wu, wd):
    n_tiles = xs.shape[0] // (MOE_TILE * TOKEN_ROWS)
    cnt = counts[:N_CLASSES, 0].astype(jnp.int32)
    n_used = jnp.sum((cnt + MOE_TILE - 1) // MOE_TILE)
    ids = jnp.arange(n_tiles, dtype=jnp.int32)
    tile_cls = jnp.sum(tab, axis=0)[:n_tiles].astype(jnp.int32)
    last_cls = jnp.sum(jnp.where(ids == n_used - 1, tile_cls, 0))
    tile_cls = jnp.where(ids < n_used, tile_cls, last_cls)
    j = jnp.minimum(ids, n_used - 1)
    pair = tile_cls % PAIRS_PER_GROUP
    pair_a = jnp.sum(jnp.where(pair[:, None] == jnp.arange(PAIRS_PER_GROUP)[None, :], _PAIR_A[None, :], 0), axis=1)
    pair_b = jnp.sum(jnp.where(pair[:, None] == jnp.arange(PAIRS_PER_GROUP)[None, :], _PAIR_B[None, :], 0), axis=1)
    base = (tile_cls // PAIRS_PER_GROUP) * EXPERTS_PER_GROUP
    return _experts(xs, base + pair_a, base + pair_b, j, n_used.reshape(1), wr_t, br, wg, wu, wd)


def _rope_tables(seq):
    inv_freq = ROPE_THETA ** (-jnp.arange(0, ROT_DIM, 2, dtype=jnp.float32) / ROT_DIM)
    ang = jnp.arange(seq).astype(jnp.float32)[:, None] * inv_freq[None, :]
    cos, sin = jnp.cos(ang), jnp.sin(ang)
    ones = jnp.ones((seq, HEAD_DIM - ROT_DIM), jnp.float32)
    cos_h = jnp.concatenate([cos, cos, ones], axis=1)
    sgn_h = jnp.concatenate([-sin, sin, 0.0 * ones], axis=1)
    reps = LANES // HEAD_DIM
    return jnp.tile(cos_h, (1, reps)).astype(F32), jnp.tile(sgn_h, (1, reps)).astype(F32)


def _pad_heads(w, axis):
    pad = [(0, 0)] * w.ndim
    pad[axis] = (0, GROUP_WIDTH - HEADS_PER_GROUP * HEAD_DIM)
    return jnp.pad(w, pad)


def _router_weights(w_group, b_group, w_expert, b_expert):
    we = jnp.transpose(w_expert, (1, 0, 2)).reshape(D_MODEL, N_EXPERTS)
    w = jnp.concatenate([w_group, we], axis=1)
    w = jnp.pad(w, ((0, 0), (0, ROUTER_ROWS - w.shape[1])))
    bias = jnp.pad(jnp.concatenate([b_group, b_expert.reshape(-1)]), (0, ROUTER_ROWS - 20))
    return w.T.astype(BF16), bias.reshape(ROUTER_ROWS, 1)


def kernel(x, c, norm_mix, norm_ffn, w_ada, b_ada, a_w_qkv, a_w_o, b_w_in, b_v_gain, b_w_s, b_b_s, b_w_o,
           r_w_group, r_b_group, r_w_expert, r_b_expert, e_w_gate, e_w_up, e_w_down, final_norm):
    b, s, _ = x.shape
    mod = _ada_mod(c, w_ada, b_ada)
    sh_m, sc_m, gt_m, sh_f, sc_f, gt_f = [mod[:, :, i:i + 1] for i in range(6)]
    row = lambda v: v.reshape(1, -1)
    gw = HEADS_PER_GROUP * HEAD_DIM

    wq, wk, wv = [a_w_qkv[0][:, i * 3 * gw:(i + 1) * 3 * gw] for i in range(3)]
    wq = wq * (HEAD_DIM ** -0.5)
    w_groups = jnp.stack([
        jnp.concatenate([_pad_heads(w[:, g * gw:(g + 1) * gw], 1) for w in (wq, wk, wv)], axis=1)
        for g in range(N_ATTN_GROUPS)]).astype(BF16)
    wo_groups = jnp.stack([_pad_heads(a_w_o[0][g * gw:(g + 1) * gw], 0)
                           for g in range(N_ATTN_GROUPS)]).astype(BF16)
    expand = (jnp.arange(LANES)[:, None] == jnp.arange(GROUP_WIDTH)[None, :] // HEAD_DIM).astype(BF16)
    cos_t, sgn_t = _rope_tables(s)
    qkv = _qkv(x, row(norm_mix[0]), sh_m[0], sc_m[0], w_groups, cos_t, sgn_t)
    o_list, l_list = [], []
    for g in range(N_ATTN_GROUPS):
        o, l = _attention(*qkv[3 * g:3 * g + 3])
        o_list.append(o)
        l_list.append(l)
    wr_t, br = _router_weights(r_w_group[0], r_b_group[0], r_w_expert[0], r_b_expert[0])
    x1, xs, pos, tab, counts = _attn_out(o_list, l_list, x, wo_groups, expand, gt_m[0], row(norm_ffn[0]),
                                         sh_f[0], sc_f[0], wr_t, br)
    t = b * s
    token_order = lambda p: p[:, :, 0, :].reshape(t)
    ys = _moe_sorted(xs, tab, counts, wr_t, br, e_w_gate[0].astype(BF16), e_w_up[0].astype(BF16),
                     e_w_down[0].astype(BF16))

    wr_t, br = _router_weights(r_w_group[1], r_b_group[1], r_w_expert[1], r_b_expert[1])
    x3, xs, pos, tab, counts = _gmlp(ys, token_order(pos), gt_f[0], x1, row(norm_mix[1]), sh_m[1], sc_m[1],
                                     b_w_in[0].astype(BF16), row(b_v_gain[0]), b_w_s[0].astype(BF16), b_b_s[0].T,
                                     b_w_o[0].astype(BF16), gt_m[1], row(norm_ffn[1]), sh_f[1], sc_f[1], wr_t, br)
    ys = _moe_sorted(xs, tab, counts, wr_t, br, e_w_gate[1].astype(BF16), e_w_up[1].astype(BF16),
                     e_w_down[1].astype(BF16))
    out = _final(ys, token_order(pos), x3.reshape(t, D_MODEL), gt_f[1], row(final_norm), s)
    return out.reshape(b, s, D_MODEL)
```

```python
import functools

import jax
import jax.numpy as jnp
import numpy as np
from jax import lax
from jax.experimental import pallas as pl
from jax.experimental.pallas import tpu as pltpu

D_MODEL = 1024
DEPTH = 2
HEAD_DIM = 64
HEADS_PER_GROUP = 5
N_ATTN_GROUPS = 3
DILATIONS = (1, 4, 16)
BAND_RADIUS = 64
ROT_DIM = HEAD_DIM // 4
ROPE_THETA = 500000.0
NEG_INF = -1e30
CHUNK = 128
GMLP_HALF = 2 * D_MODEL
GMLP_GROUPS = 8
N_EXPERT_GROUPS = 4
EXPERTS_PER_GROUP = 4
N_EXPERTS = 16
D_EXPERT = 256
EPS = 1e-6

LANES = 128
GROUP_WIDTH = 384
ROUTER_ROWS = 32
PAIRS_PER_GROUP = 6
N_CLASSES = N_EXPERT_GROUPS * PAIRS_PER_GROUP
CLASS_ROWS = 32
SUBLANES = 8
TOKEN_ROWS = D_MODEL // LANES
MOE_TILE = 512

F32 = jnp.float32
BF16 = jnp.bfloat16
VMEM_LIMIT = 56 * 1024 * 1024


def _cparams(sem):
    return pltpu.CompilerParams(dimension_semantics=sem, vmem_limit_bytes=VMEM_LIMIT)


def _const_spec(shape, index_map):
    return pl.BlockSpec(shape, index_map, pipeline_mode=pl.Buffered(1))


def _silu(x):
    return x * (1.0 / (1.0 + jnp.exp(-x)))


def _gelu(x):
    return 0.5 * x * (1.0 + lax.erf(x * (2.0 ** -0.5)))


def _norm_modulate(x, gain, shift, scale):
    ms = jnp.mean(x * x, axis=-1, keepdims=True)
    return (x * lax.rsqrt(ms + EPS) * gain) * (1.0 + scale) + shift


def _split_bf16(x):
    hi = x.astype(BF16)
    return hi, (x - hi.astype(F32)).astype(BF16)


def _ada_kernel(c_ref, w_ref, b_ref, o_ref):
    ca_hi, ca_lo = _split_bf16(_silu(c_ref[...]))
    w_hi, w_lo = _split_bf16(w_ref[0])
    dot = functools.partial(jnp.dot, preferred_element_type=F32)
    o_ref[0] = dot(ca_hi, w_hi) + dot(ca_lo, w_hi) + dot(ca_hi, w_lo) + b_ref[0]


def _ada_mod(c, w_ada, b_ada):
    b = c.shape[0]
    tn = 1536
    c8 = jnp.pad(c, ((0, 8 - b), (0, 0)))
    out = pl.pallas_call(
        _ada_kernel,
        grid=(DEPTH, 6 * D_MODEL // tn),
        in_specs=[pl.BlockSpec((8, D_MODEL), lambda l, j: (0, 0)),
                  pl.BlockSpec((1, D_MODEL, tn), lambda l, j: (l, 0, j)),
                  pl.BlockSpec((1, 1, tn), lambda l, j: (l, 0, j))],
        out_specs=pl.BlockSpec((1, 8, tn), lambda l, j: (l, 0, j)),
        out_shape=jax.ShapeDtypeStruct((DEPTH, 8, 6 * D_MODEL), F32),
        compiler_params=_cparams(("arbitrary", "arbitrary")),
        name="ada_mod",
    )(c8, w_ada, b_ada.reshape(DEPTH, 1, 6 * D_MODEL))
    return out[:, :b].reshape(DEPTH, b, 6, 1, D_MODEL)


def _router_logits(hf, wr_ref, br_ref):
    return lax.dot_general(wr_ref[...], hf, (((1,), (1,)), ((), ())), preferred_element_type=F32) + br_ref[...]


def _group_choice(lt):
    gl = [lt[i:i + 1, :] for i in range(N_EXPERT_GROUPS)]
    gmax = jnp.maximum(jnp.maximum(gl[0], gl[1]), jnp.maximum(gl[2], gl[3]))
    ge = [jnp.exp(g - gmax) for g in gl]
    gsum = ge[0] + ge[1] + ge[2] + ge[3]
    gp = [e / gsum for e in ge]
    g_val = jnp.maximum(jnp.maximum(gp[0], gp[1]), jnp.maximum(gp[2], gp[3]))
    g_idx = jnp.where(gp[0] == g_val, 0, jnp.where(gp[1] == g_val, 1, jnp.where(gp[2] == g_val, 2, 3)))
    return g_val, g_idx


def _ffn_prep(x1, gain, shift, scale, wr_ref, br_ref):
    hf32 = _norm_modulate(x1, gain, shift, scale)
    lt = _router_logits(hf32.astype(BF16), wr_ref, br_ref)
    _, g_idx = _group_choice(lt)
    el = []
    for j in range(EXPERTS_PER_GROUP):
        rows = [lt[4 + 4 * g + j:5 + 4 * g + j, :] for g in range(N_EXPERT_GROUPS)]
        el.append(jnp.where(g_idx == 0, rows[0], jnp.where(g_idx == 1, rows[1],
                  jnp.where(g_idx == 2, rows[2], rows[3]))))
    v1 = jnp.maximum(jnp.maximum(el[0], el[1]), jnp.maximum(el[2], el[3]))
    i1 = jnp.where(el[0] == v1, 0, jnp.where(el[1] == v1, 1, jnp.where(el[2] == v1, 2, 3)))
    el2 = [jnp.where(i1 == j, -jnp.inf, el[j]) for j in range(EXPERTS_PER_GROUP)]
    v2 = jnp.maximum(jnp.maximum(el2[0], el2[1]), jnp.maximum(el2[2], el2[3]))
    i2 = jnp.where((el2[0] == v2) & (i1 != 0), 0,
                   jnp.where((el2[1] == v2) & (i1 != 1), 1,
                             jnp.where((el2[2] == v2) & (i1 != 2), 2, 3)))
    a = jnp.minimum(i1, i2)
    bb = jnp.maximum(i1, i2)
    cls = g_idx * PAIRS_PER_GROUP + ((a * (7 - a)) >> 1) + (bb - a - 1)
    return hf32, cls


def _place_tokens(cls, carry_ref, cur_ref, tab_ref, first_step):
    tm = cls.shape[1]

    @pl.when(first_step)
    def _():
        carry_ref[...] = jnp.zeros_like(carry_ref)
        cur_ref[...] = jnp.zeros_like(cur_ref)
        tab_ref[...] = jnp.zeros_like(tab_ref)

    crow = lax.broadcasted_iota(jnp.int32, (CLASS_ROWS, tm), 0)
    onehot = (crow == cls).astype(F32)
    si = lax.broadcasted_iota(jnp.int32, (tm, tm), 0)
    ti = lax.broadcasted_iota(jnp.int32, (tm, tm), 1)
    before = (si < ti).astype(BF16)
    prefix = jnp.dot(onehot.astype(BF16), before, preferred_element_type=F32)
    seen = prefix + carry_ref[:, 0:1]
    rank = jnp.sum(onehot * seen, axis=0, keepdims=True)
    inv_tile = 1.0 / MOE_TILE
    rem = rank - MOE_TILE * jnp.floor(rank * inv_tile)
    opens = (rem == 0.0).astype(F32)
    opened_before = jnp.sum(jnp.floor((seen + (MOE_TILE - 1)) * inv_tile), axis=0, keepdims=True)
    sel = onehot * opens
    opened = jnp.sum(sel, axis=1, keepdims=True) > 0.0
    new_tile = jnp.sum(sel * opened_before, axis=1, keepdims=True)
    first_rank = jnp.sum(sel * rank, axis=1, keepdims=True)
    tile = jnp.sum(onehot * jnp.where(opened & (seen >= first_rank), new_tile, cur_ref[:, 0:1]),
                   axis=0, keepdims=True)
    carry_ref[...] = carry_ref[...] + jnp.sum(onehot, axis=1, keepdims=True)
    cur_ref[...] = jnp.where(opened, new_tile, cur_ref[...])
    lane = lax.broadcasted_iota(jnp.int32, (CLASS_ROWS, LANES), 1).astype(F32)
    cidx = lax.broadcasted_iota(jnp.int32, (CLASS_ROWS, LANES), 0).astype(F32)
    tab_ref[...] = tab_ref[...] + jnp.where(opened & (lane == new_tile), cidx, 0.0)
    return (tile * MOE_TILE + rem).astype(jnp.int32)


def _rope(t, cos, sgn, first_half):
    out = []
    for c in range(GROUP_WIDTH // LANES):
        xc = t[:, c * LANES:(c + 1) * LANES]
        other = jnp.where(first_half, pltpu.roll(xc, LANES - ROT_DIM // 2, 1),
                          pltpu.roll(xc, ROT_DIM // 2, 1))
        out.append(xc * cos + other * sgn)
    return jnp.concatenate(out, axis=1)


def _qkv_kernel(x_ref, gain_ref, sh_ref, sc_ref, w_ref, cos_ref, sgn_ref, *rest, tm):
    out_refs, hn_scr = rest[:9], rest[9]
    hn = _norm_modulate(x_ref[0], gain_ref[...], sh_ref[0, 0], sc_ref[0, 0])
    n_slab = D_MODEL // LANES
    lane = lax.broadcasted_iota(jnp.int32, (tm, LANES), 1) % HEAD_DIM
    first_half = lane < ROT_DIM // 2
    for g, d in enumerate(DILATIONS):
        rows = tm // d
        if d == 1:
            hp, cos, sgn = hn, cos_ref[...], sgn_ref[...]
        else:
            if g == 1:
                for s in range(n_slab):
                    hn_scr[s] = hn[:, s * LANES:(s + 1) * LANES]
            hp = jnp.concatenate(
                [jnp.concatenate([hn_scr[s, pl.ds(r, rows, stride=d), :] for s in range(n_slab)], axis=1)
                 for r in range(d)], axis=0)
            cos = jnp.concatenate([cos_ref[pl.ds(r, rows, stride=d), :] for r in range(d)], axis=0)
            sgn = jnp.concatenate([sgn_ref[pl.ds(r, rows, stride=d), :] for r in range(d)], axis=0)
        res = jnp.dot(hp.astype(BF16), w_ref[g], preferred_element_type=F32)
        q = _rope(res[:, :GROUP_WIDTH], cos, sgn, first_half)
        k = _rope(res[:, GROUP_WIDTH:2 * GROUP_WIDTH], cos, sgn, first_half)
        v = res[:, 2 * GROUP_WIDTH:]
        for t, o_ref in zip((q, k, v), out_refs[3 * g:3 * g + 3]):
            o_ref[0] = t.astype(BF16).reshape(d, rows, GROUP_WIDTH)


def _qkv(x, gain, shift, scale, w_groups, cos_t, sgn_t, tm=512):
    b, s, _ = x.shape
    out_shape, out_specs = [], []
    for d in DILATIONS:
        for _ in range(3):
            out_shape.append(jax.ShapeDtypeStruct((b, d, s // d, GROUP_WIDTH), BF16))
            out_specs.append(pl.BlockSpec((1, d, tm // d, GROUP_WIDTH), lambda bi, i: (bi, 0, i, 0)))
    return pl.pallas_call(
        functools.partial(_qkv_kernel, tm=tm),
        grid=(b, s // tm),
        in_specs=[pl.BlockSpec((1, tm, D_MODEL), lambda bi, i: (bi, i, 0)),
                  _const_spec((1, D_MODEL), lambda bi, i: (0, 0)),
                  pl.BlockSpec((1, 1, 1, D_MODEL), lambda bi, i: (bi, 0, 0, 0)),
                  pl.BlockSpec((1, 1, 1, D_MODEL), lambda bi, i: (bi, 0, 0, 0)),
                  _const_spec((N_ATTN_GROUPS, D_MODEL, 3 * GROUP_WIDTH), lambda bi, i: (0, 0, 0)),
                  pl.BlockSpec((tm, LANES), lambda bi, i: (i, 0)),
                  pl.BlockSpec((tm, LANES), lambda bi, i: (i, 0))],
        out_specs=out_specs,
        out_shape=out_shape,
        scratch_shapes=[pltpu.VMEM((D_MODEL // LANES, tm, LANES), F32)],
        compiler_params=_cparams(("arbitrary", "arbitrary")),
        name="qkv_rope",
    )(x, gain, shift, scale, w_groups, cos_t, sgn_t)


def _attn_kernel(q_ref, kp_ref, kc_ref, kn_ref, vp_ref, vc_ref, vn_ref, o_ref, lse_ref,
                 kbuf, vbuf, *, seq, tq):
    i = pl.program_id(2)
    r = BAND_RADIUS
    kbuf[0:r] = kp_ref[0, 0]
    kbuf[r:r + tq] = kc_ref[0, 0]
    kbuf[r + tq:] = kn_ref[0, 0]
    vbuf[0:r] = vp_ref[0, 0]
    vbuf[r:r + tq] = vc_ref[0, 0]
    vbuf[r + tq:] = vn_ref[0, 0]
    qb = 2 * r
    lane = lax.broadcasted_iota(jnp.int32, (qb, LANES), 1)
    low = lane < HEAD_DIM
    tidx = lax.broadcasted_iota(jnp.int32, (qb, 2 * qb), 0)
    kidx = lax.broadcasted_iota(jnp.int32, (qb, 2 * qb), 1)
    rel = kidx - tidx
    band = (rel >= 0) & (rel <= 2 * r)
    for j in range(tq // qb):
        kpos = i * tq + (j * qb - r) + kidx
        mask = band & (kpos >= 0) & (kpos < seq)
        q = q_ref[0, 0, j * qb:(j + 1) * qb, :]
        k = kbuf[j * qb:(j + 2) * qb, :]
        v = vbuf[j * qb:(j + 2) * qb, :]
        lse_tile = jnp.zeros((qb, LANES), F32)
        o_chunks = []
        for c in range(GROUP_WIDTH // LANES):
            qc = q[:, c * LANES:(c + 1) * LANES]
            kc = k[:, c * LANES:(c + 1) * LANES]
            vc = v[:, c * LANES:(c + 1) * LANES]
            outs = []
            for hh in range(2):
                head = 2 * c + hh
                if head >= HEADS_PER_GROUP:
                    outs.append(jnp.zeros((qb, LANES), F32))
                    continue
                qm = jnp.where(low if hh == 0 else jnp.logical_not(low), qc, jnp.zeros_like(qc))
                s = lax.dot_general(qm, kc, (((1,), (1,)), ((), ())), preferred_element_type=F32)
                s = jnp.where(mask, s, NEG_INF)
                m = jnp.max(s, axis=-1, keepdims=True)
                p = jnp.exp(s - m)
                den = jnp.sum(p, axis=-1, keepdims=True)
                o = jnp.dot(p.astype(BF16), vc, preferred_element_type=F32) / den
                lse_tile = jnp.where(lane == head, m + jnp.log(den), lse_tile)
                outs.append(o)
            o_chunks.append(jnp.where(low, outs[0], outs[1]))
        o_ref[0, 0, j * qb:(j + 1) * qb, :] = jnp.concatenate(o_chunks, axis=1).astype(BF16)
        lse_ref[0, 0, j * qb:(j + 1) * qb, :] = lse_tile


def _attention(q, k, v, tq=512):
    b, d, seq, w = q.shape
    r = BAND_RADIUS
    nb = seq // r
    cur = pl.BlockSpec((1, 1, tq, w), lambda bi, ri, i: (bi, ri, i, 0))
    prev = pl.BlockSpec((1, 1, r, w), lambda bi, ri, i: (bi, ri, jnp.maximum(i * (tq // r) - 1, 0), 0))
    nxt = pl.BlockSpec((1, 1, r, w), lambda bi, ri, i: (bi, ri, jnp.minimum((i + 1) * (tq // r), nb - 1), 0))
    return pl.pallas_call(
        functools.partial(_attn_kernel, seq=seq, tq=tq),
        grid=(b, d, seq // tq),
        in_specs=[cur, prev, cur, nxt, prev, cur, nxt],
        out_specs=[pl.BlockSpec((1, 1, tq, w), lambda bi, ri, i: (bi, ri, i, 0)),
                   pl.BlockSpec((1, 1, tq, LANES), lambda bi, ri, i: (bi, ri, i, 0))],
        out_shape=[jax.ShapeDtypeStruct((b, d, seq, w), BF16),
                   jax.ShapeDtypeStruct((b, d, seq, LANES), F32)],
        scratch_shapes=[pltpu.VMEM((tq + 2 * r, w), BF16), pltpu.VMEM((tq + 2 * r, w), BF16)],
        compiler_params=_cparams(("arbitrary", "arbitrary", "arbitrary")),
        name=f"band_attn_d{d}",
    )(q, k, k, k, v, v, v)


def _attn_out_kernel(o0, o1, o2, l0, l1, l2, x_ref, wo_ref, exp_ref, gt_ref, gain_ref, sh_ref, sc_ref,
                     wr_ref, br_ref, x1_ref, xs_hbm, pos_ref, tab_out, cnt_out, oscr, lscr, *route_scratch,
                     tm, n_steps):
    n_slab = GROUP_WIDTH // LANES
    for g, (d, o_ref, l_ref) in enumerate(zip(DILATIONS, (o0, o1, o2), (l0, l1, l2))):
        rows = tm // d
        for r in range(d):
            dst = slice(None) if d == 1 else pl.ds(r, rows, stride=d)
            blk = o_ref[0, r].astype(F32)
            for c in range(n_slab):
                oscr[g * n_slab + c, dst, :] = blk[:, c * LANES:(c + 1) * LANES]
            lscr[g, dst, :] = l_ref[0, r]
    lse = [lscr[g] for g in range(N_ATTN_GROUPS)]
    mx = jnp.maximum(jnp.maximum(lse[0], lse[1]), lse[2])
    ex = [jnp.exp(l - mx) for l in lse]
    tot = ex[0] + ex[1] + ex[2]
    y = jnp.zeros((tm, D_MODEL), F32)
    for g in range(N_ATTN_GROUPS):
        alpha = ex[g] / tot
        a_hi, a_lo = _split_bf16(alpha)
        a_wide = (jnp.dot(a_hi, exp_ref[...], preferred_element_type=F32)
                  + jnp.dot(a_lo, exp_ref[...], preferred_element_type=F32))
        og = jnp.concatenate([oscr[g * n_slab + c] for c in range(n_slab)], axis=1)
        y = y + jnp.dot((og * a_wide).astype(BF16), wo_ref[g], preferred_element_type=F32)
    x1 = x_ref[0] + gt_ref[0, 0] * y
    x1_ref[0] = x1
    _route_outputs(x1, gain_ref, sh_ref, sc_ref, wr_ref, br_ref, xs_hbm, pos_ref, tab_out, cnt_out, route_scratch,
                   n_steps)


ZERO_TOKENS = MOE_TILE // 2


def _route_outputs(x1, gain_ref, sh_ref, sc_ref, wr_ref, br_ref, xs_hbm, pos_ref, tab_out, cnt_out, scratch,
                   n_steps):
    (carry_ref, cur_ref, tab_ref, rows_scr, pos_vmem, pos_smem, fin_vmem, fin_smem, zbuf, sems, aux_sem) = scratch
    step = pl.program_id(0) * pl.num_programs(1) + pl.program_id(1)
    tm = x1.shape[0]
    rows = tm * TOKEN_ROWS
    hf32, cls = _ffn_prep(x1, gain_ref[...], sh_ref[0, 0], sc_ref[0, 0], wr_ref, br_ref)
    pos = _place_tokens(cls, carry_ref, cur_ref, tab_ref, step == 0)
    r8 = lax.broadcasted_iota(jnp.int32, (SUBLANES, tm), 0)
    pos8 = jnp.where(r8 == 0, pos, 0)
    pos_ref[0, 0] = pos8
    pos_vmem[...] = pos8
    to_smem = pltpu.make_async_copy(pos_vmem, pos_smem, aux_sem)
    to_smem.start()

    def drain(slot):
        pltpu.make_async_copy(rows_scr.at[pl.ds(0, rows), :], xs_hbm.at[pl.ds(0, rows), :], sems.at[slot]).wait()

    slot = step % 2

    @pl.when(step >= 2)
    def _():
        drain(slot)

    base = pl.multiple_of(slot * rows, rows)
    for s in range(TOKEN_ROWS):
        rows_scr[pl.ds(base + s, tm, stride=TOKEN_ROWS), :] = hf32[:, s * LANES:(s + 1) * LANES]
    to_smem.wait()

    def issue(r, priority):
        _token_copy(rows_scr, slot * tm + r, xs_hbm, pos_smem[0, r], sems.at[slot]).start(priority=priority)

    _issue_burst(tm, issue, spread=False)
    tab_out[...] = tab_ref[...]
    cnt_out[...] = carry_ref[...]

    @pl.when(step == n_steps - 1)
    def _():
        drain(slot)
        if n_steps > 1:
            drain(1 - slot)
        _clear_unused_rows(carry_ref, cur_ref, xs_hbm, fin_vmem, fin_smem, zbuf, aux_sem)


def _clear_unused_rows(carry_ref, cur_ref, xs_hbm, fin_vmem, fin_smem, zbuf, sem):
    lane = lax.broadcasted_iota(jnp.int32, (CLASS_ROWS, LANES), 1)
    crow = lax.broadcasted_iota(jnp.int32, (CLASS_ROWS, LANES), 0)
    on_lane = lambda col: jnp.sum(jnp.where(lane == crow, col, 0.0), axis=0, keepdims=True)
    counts = carry_ref[...]
    used = jnp.sum(jnp.floor((counts + (MOE_TILE - 1)) * (1.0 / MOE_TILE)), axis=0, keepdims=True)
    r8 = lax.broadcasted_iota(jnp.int32, (SUBLANES, LANES), 0)
    fin = jnp.where(r8 == 0, on_lane(cur_ref[...]), jnp.where(r8 == 1, on_lane(counts), jnp.where(r8 == 2, used, 0.0)))
    fin_vmem[...] = fin.astype(jnp.int32)
    to_smem = pltpu.make_async_copy(fin_vmem, fin_smem, sem)
    to_smem.start()
    zbuf[...] = jnp.zeros_like(zbuf)
    to_smem.wait()

    def zero_copy(first_tok, n_tok):
        d0 = pl.multiple_of(first_tok * TOKEN_ROWS, TOKEN_ROWS)
        return pltpu.make_async_copy(zbuf.at[pl.ds(0, n_tok * TOKEN_ROWS), :],
                                     xs_hbm.at[pl.ds(d0, n_tok * TOKEN_ROWS), :], sem)

    for wait in (False, True):
        for c in range(N_CLASSES):
            count = fin_smem[1, c]
            fill = jnp.where(count > 0, (MOE_TILE - count % MOE_TILE) % MOE_TILE, 0)
            first = fin_smem[0, c] * MOE_TILE + (MOE_TILE - fill)
            piece = ZERO_TOKENS
            while piece >= 1:
                done = (fill // (2 * piece)) * (2 * piece)

                @pl.when((fill // piece) % 2 == 1)
                def _(first=first, done=done, piece=piece):
                    cp = zero_copy(first + done, piece)
                    cp.wait() if wait else cp.start()

                piece //= 2

    def clear_tile(j, carry):
        for half in range(MOE_TILE // ZERO_TOKENS):
            cp = zero_copy(j * MOE_TILE + half * ZERO_TOKENS, ZERO_TOKENS)
            cp.start()
            cp.wait()
        return carry

    lax.fori_loop(fin_smem[2, 0], xs_hbm.shape[0] // (MOE_TILE * TOKEN_ROWS), clear_tile, 0)


def _route_scratch(tm):
    return [pltpu.VMEM((CLASS_ROWS, LANES), F32), pltpu.VMEM((CLASS_ROWS, LANES), F32),
            pltpu.VMEM((CLASS_ROWS, LANES), F32),
            pltpu.VMEM((2 * tm * TOKEN_ROWS, LANES), F32),
            pltpu.VMEM((SUBLANES, tm), jnp.int32), pltpu.SMEM((SUBLANES, tm), jnp.int32),
            pltpu.VMEM((SUBLANES, LANES), jnp.int32), pltpu.SMEM((SUBLANES, LANES), jnp.int32),
            pltpu.VMEM((ZERO_TOKENS * TOKEN_ROWS, LANES), F32),
            pltpu.SemaphoreType.DMA((2,)), pltpu.SemaphoreType.DMA]


N_ROUTE_SCRATCH = 11


def _route_out_specs(b, s, tm):
    n_tiles = b * s // MOE_TILE + N_CLASSES
    specs = [pl.BlockSpec((1, tm, D_MODEL), lambda bi, i: (bi, i, 0)),
             pl.BlockSpec(memory_space=pl.ANY),
             pl.BlockSpec((1, 1, SUBLANES, tm), lambda bi, i: (bi, i, 0, 0)),
             pl.BlockSpec((CLASS_ROWS, LANES), lambda bi, i: (0, 0)),
             pl.BlockSpec((CLASS_ROWS, LANES), lambda bi, i: (0, 0))]
    shapes = [jax.ShapeDtypeStruct((b, s, D_MODEL), F32),
              jax.ShapeDtypeStruct((n_tiles * MOE_TILE * TOKEN_ROWS, LANES), F32),
              jax.ShapeDtypeStruct((b, s // tm, SUBLANES, tm), jnp.int32),
              jax.ShapeDtypeStruct((CLASS_ROWS, LANES), F32),
              jax.ShapeDtypeStruct((CLASS_ROWS, LANES), F32)]
    return specs, shapes


def _mod_spec():
    return pl.BlockSpec((1, 1, 1, D_MODEL), lambda bi, i: (bi, 0, 0, 0))


def _attn_out(o_list, l_list, x, wo_groups, expand, gt, gain, shift, scale, wr_t, br, tm=512):
    b, s, _ = x.shape
    o_specs = [pl.BlockSpec((1, d, tm // d, GROUP_WIDTH), lambda bi, i: (bi, 0, i, 0)) for d in DILATIONS]
    l_specs = [pl.BlockSpec((1, d, tm // d, LANES), lambda bi, i: (bi, 0, i, 0)) for d in DILATIONS]
    out_specs, out_shape = _route_out_specs(b, s, tm)
    return pl.pallas_call(
        functools.partial(_attn_out_kernel, tm=tm, n_steps=b * s // tm),
        grid=(b, s // tm),
        in_specs=o_specs + l_specs + [
            pl.BlockSpec((1, tm, D_MODEL), lambda bi, i: (bi, i, 0)),
            _const_spec((N_ATTN_GROUPS, GROUP_WIDTH, D_MODEL), lambda bi, i: (0, 0, 0)),
            _const_spec((LANES, GROUP_WIDTH), lambda bi, i: (0, 0)),
            _mod_spec(),
            _const_spec((1, D_MODEL), lambda bi, i: (0, 0)),
            _mod_spec(), _mod_spec(),
            _const_spec((ROUTER_ROWS, D_MODEL), lambda bi, i: (0, 0)),
            _const_spec((ROUTER_ROWS, 1), lambda bi, i: (0, 0))],
        out_specs=out_specs,
        out_shape=out_shape,
        scratch_shapes=[pltpu.VMEM((N_ATTN_GROUPS * GROUP_WIDTH // LANES, tm, LANES), F32),
                        pltpu.VMEM((N_ATTN_GROUPS, tm, LANES), F32)] + _route_scratch(tm),
        compiler_params=_cparams(("arbitrary", "arbitrary")),
        name="attn_merge_proj",
    )(*o_list, *l_list, x, wo_groups, expand, gt, gain, shift, scale, wr_t, br)


DMA_BURST_UNROLL = 8


BACKGROUND_PRIORITY = 1


def _issue_burst(n, issue, spread=True):
    def body(i, carry):
        for k in range(DMA_BURST_UNROLL):
            issue(i * DMA_BURST_UNROLL + k, k % 2 if spread else BACKGROUND_PRIORITY)
        return carry

    lax.fori_loop(0, n // DMA_BURST_UNROLL, body, 0)


def _gather_moe_rows(pos_ref, pos_next_ref, ys_hbm, ybuf, sems, step, n_steps, tm, spread):
    rows = tm * TOKEN_ROWS

    def request(p_ref, slot):
        def issue(r, priority):
            _token_copy(ys_hbm, p_ref[0, 0, r], ybuf, slot * tm + r, sems.at[slot]).start(priority=priority)

        _issue_burst(tm, issue, spread)

    @pl.when(step == 0)
    def _():
        request(pos_ref, 0)

    @pl.when(step + 1 < n_steps)
    def _():
        request(pos_next_ref, (step + 1) % 2)

    slot = step % 2
    base = pl.multiple_of(slot * rows, rows)
    pltpu.make_async_copy(ys_hbm.at[pl.ds(0, rows), :], ybuf.at[pl.ds(base, rows), :], sems.at[slot]).wait()
    return jnp.concatenate([ybuf[pl.ds(base + s, tm, stride=TOKEN_ROWS), :] for s in range(D_MODEL // LANES)],
                           axis=1)


def _gather_specs(n_steps, tm, index_of):
    return [pl.BlockSpec((1, 1, tm), lambda *g: (index_of(*g), 0, 0), memory_space=pltpu.SMEM),
            pl.BlockSpec((1, 1, tm), lambda *g: (jnp.minimum(index_of(*g) + 1, n_steps - 1), 0, 0),
                         memory_space=pltpu.SMEM),
            pl.BlockSpec(memory_space=pl.ANY)]


def _gather_scratch(tm):
    return [pltpu.VMEM((2 * tm * TOKEN_ROWS, LANES), F32), pltpu.SemaphoreType.DMA((2,))]


def _gmlp_kernel(pos_ref, pos_next_ref, ys_hbm, gtf_ref, x_ref, gain_ref, sh_ref, sc_ref, win_ref, vg_ref, ws_ref,
                 bs_ref, wo_ref, gt_ref, fgain_ref, fsh_ref, fsc_ref, wr_ref, br_ref, x1_ref, xs_hbm, pos_out, tab_out,
                 cnt_out, gated_scr, ybuf, sems, *route_scratch, tm, n_steps):
    step = pl.program_id(0) * pl.num_programs(1) + pl.program_id(1)
    y_moe = _gather_moe_rows(pos_ref, pos_next_ref, ys_hbm, ybuf, sems, step, n_steps, tm, spread=False)
    x = x_ref[0] + gtf_ref[0, 0] * y_moe
    hn = _norm_modulate(x, gain_ref[...], sh_ref[0, 0], sc_ref[0, 0]).astype(BF16)
    v = _gelu(jnp.dot(hn, win_ref[:, GMLP_HALF:], preferred_element_type=F32))
    u = _gelu(jnp.dot(hn, win_ref[:, :GMLP_HALF], preferred_element_type=F32))
    mu = jnp.mean(v, axis=-1, keepdims=True)
    vc = v - mu
    vn = (vc * lax.rsqrt(jnp.mean(vc * vc, axis=-1, keepdims=True) + EPS) * vg_ref[...]).astype(BF16)
    gw = GMLP_HALF // GMLP_GROUPS
    for c in range(tm // CHUNK):
        rs = slice(c * CHUNK, (c + 1) * CHUNK)
        for g in range(GMLP_GROUPS):
            cs = slice(g * gw, (g + 1) * gw)
            vs = jnp.dot(ws_ref[g], vn[rs, cs], preferred_element_type=F32) + bs_ref[:, g:g + 1]
            gated_scr[rs, cs] = (u[rs, cs] * vs).astype(BF16)
    y = jnp.dot(gated_scr[...], wo_ref[...], preferred_element_type=F32)
    x1 = x + gt_ref[0, 0] * y
    x1_ref[0] = x1
    _route_outputs(x1, fgain_ref, fsh_ref, fsc_ref, wr_ref, br_ref, xs_hbm, pos_out, tab_out, cnt_out,
                   route_scratch, n_steps)


def _gmlp(ys, pos, gt_prev, x, gain, shift, scale, w_in, v_gain, w_s, b_s_t, w_o, gt, fgain, fshift, fscale,
          wr_t, br, tm=512):
    b, s, _ = x.shape
    tok = lambda w: pl.BlockSpec((1, tm, w), lambda bi, i: (bi, i, 0))
    c2 = lambda shape: _const_spec(shape, lambda bi, i: (0,) * len(shape))
    out_specs, out_shape = _route_out_specs(b, s, tm)
    n_steps = b * s // tm
    return pl.pallas_call(
        functools.partial(_gmlp_kernel, tm=tm, n_steps=n_steps),
        grid=(b, s // tm),
        in_specs=_gather_specs(n_steps, tm, lambda bi, i: bi * (s // tm) + i) + [
                  _mod_spec(),
                  tok(D_MODEL), c2((1, D_MODEL)), _mod_spec(), _mod_spec(),
                  c2((D_MODEL, 2 * GMLP_HALF)), c2((1, GMLP_HALF)),
                  c2((GMLP_GROUPS, CHUNK, CHUNK)), c2((CHUNK, GMLP_GROUPS)),
                  c2((GMLP_HALF, D_MODEL)), _mod_spec(),
                  c2((1, D_MODEL)), _mod_spec(), _mod_spec(),
                  c2((ROUTER_ROWS, D_MODEL)), c2((ROUTER_ROWS, 1))],
        out_specs=out_specs,
        out_shape=out_shape,
        scratch_shapes=[pltpu.VMEM((tm, GMLP_HALF), BF16)] + _gather_scratch(tm) + _route_scratch(tm),
        compiler_params=_cparams(("arbitrary", "arbitrary")),
        name="gmlp",
    )(pos.reshape(n_steps, 1, tm), pos.reshape(n_steps, 1, tm), ys, gt_prev,
      x, gain, shift, scale, w_in, v_gain, w_s, b_s_t, w_o, gt, fgain, fshift, fscale, wr_t, br)


def _token_copy(src, src_tok, dst, dst_tok, sem):
    s0 = pl.multiple_of(src_tok * TOKEN_ROWS, TOKEN_ROWS)
    d0 = pl.multiple_of(dst_tok * TOKEN_ROWS, TOKEN_ROWS)
    return pltpu.make_async_copy(src.at[pl.ds(s0, TOKEN_ROWS), :], dst.at[pl.ds(d0, TOKEN_ROWS), :], sem)


def _slabs(ref, n_tok, n_slab):
    return jnp.concatenate([ref[pl.ds(s, n_tok, stride=TOKEN_ROWS), :] for s in range(n_slab)], axis=1)


def _expert_kernel(ea_ref, eb_ref, blk_ref, nused_ref, x_ref, wr_ref, br_ref, wga, wgb, wua, wub, wda, wdb, y_ref,
                   lt_scr):
    j = pl.program_id(0)

    @pl.when(j < nused_ref[0])
    def _():
        h = _slabs(x_ref, MOE_TILE, TOKEN_ROWS).astype(BF16)
        lt = _router_logits(h, wr_ref, br_ref)
        g_val, _ = _group_choice(lt)
        lt_scr[...] = lt
        el_a = lt_scr[pl.ds(N_EXPERT_GROUPS + ea_ref[j], 1), :]
        el_b = lt_scr[pl.ds(N_EXPERT_GROUPS + eb_ref[j], 1), :]
        top = jnp.maximum(el_a, el_b)
        p_a = jnp.exp(el_a - top)
        p_b = jnp.exp(el_b - top)
        row = lax.broadcasted_iota(jnp.int32, (LANES, MOE_TILE), 0)
        gates_t = jnp.where(row == 0, g_val * (p_a / (p_a + p_b)), jnp.where(row == 1, g_val * (p_b / (p_a + p_b)), 0.0))
        gates = gates_t.T
        y = jnp.zeros((MOE_TILE, D_MODEL), F32)
        for gate, wg, wu, wd in ((gates[:, 0:1], wga, wua, wda), (gates[:, 1:2], wgb, wub, wdb)):
            hg = jnp.dot(h, wg[0], preferred_element_type=F32)
            hu = jnp.dot(h, wu[0], preferred_element_type=F32)
            act = (_silu(hg) * hu * gate).astype(BF16)
            y = y + jnp.dot(act, wd[0], preferred_element_type=F32)
        for s in range(TOKEN_ROWS):
            y_ref[pl.ds(s, MOE_TILE, stride=TOKEN_ROWS), :] = y[:, s * LANES:(s + 1) * LANES]

    @pl.when(j >= nused_ref[0])
    def _():
        y_ref[...] = jnp.zeros_like(y_ref)


def _experts(xs, tile_ea, tile_eb, tile_blk, n_used, wr_t, br, wg, wu, wd):
    n_tiles = xs.shape[0] // (MOE_TILE * TOKEN_ROWS)
    up = lambda sel: pl.BlockSpec((1, D_MODEL, D_EXPERT), lambda j, ea, eb, blk, nu: ((ea, eb)[sel][j], 0, 0))
    down = lambda sel: pl.BlockSpec((1, D_EXPERT, D_MODEL), lambda j, ea, eb, blk, nu: ((ea, eb)[sel][j], 0, 0))
    shape = (MOE_TILE * TOKEN_ROWS, LANES)
    return pl.pallas_call(
        _expert_kernel,
        grid_spec=pltpu.PrefetchScalarGridSpec(
            num_scalar_prefetch=4,
            grid=(n_tiles,),
            in_specs=[pl.BlockSpec(shape, lambda j, ea, eb, blk, nu: (blk[j], 0)),
                      pl.BlockSpec((ROUTER_ROWS, D_MODEL), lambda j, ea, eb, blk, nu: (0, 0)),
                      pl.BlockSpec((ROUTER_ROWS, 1), lambda j, ea, eb, blk, nu: (0, 0)),
                      up(0), up(1), up(0), up(1), down(0), down(1)],
            out_specs=pl.BlockSpec(shape, lambda j, ea, eb, blk, nu: (j, 0)),
            scratch_shapes=[pltpu.VMEM((ROUTER_ROWS, MOE_TILE), F32)]),
        out_shape=jax.ShapeDtypeStruct(xs.shape, F32),
        compiler_params=_cparams(("arbitrary",)),
        name="moe_experts",
    )(tile_ea, tile_eb, tile_blk, n_used, xs, wr_t, br, wg, wg, wu, wu, wd, wd)


def _final_kernel(pos_ref, pos_next_ref, ys_hbm, x_ref, gt_ref, fin_ref, o_ref, ybuf, sems, *, tm):
    y_moe = _gather_moe_rows(pos_ref, pos_next_ref, ys_hbm, ybuf, sems, pl.program_id(0), pl.num_programs(0), tm,
                             spread=True)
    x2 = x_ref[...] + gt_ref[0, 0] * y_moe
    ms = jnp.mean(x2 * x2, axis=-1, keepdims=True)
    o_ref[...] = x2 * lax.rsqrt(ms + EPS) * fin_ref[...]


def _final(ys, pos, x, gt, fin_gain, seq, tm=512):
    t = x.shape[0]
    tiles_per_batch = seq // tm
    n_steps = t // tm
    return pl.pallas_call(
        functools.partial(_final_kernel, tm=tm),
        grid=(n_steps,),
        in_specs=_gather_specs(n_steps, tm, lambda i: i) + [
                  pl.BlockSpec((tm, D_MODEL), lambda i: (i, 0)),
                  pl.BlockSpec((1, 1, 1, D_MODEL), lambda i: (i // tiles_per_batch, 0, 0, 0)),
                  pl.BlockSpec((1, D_MODEL), lambda i: (0, 0))],
        out_specs=pl.BlockSpec((tm, D_MODEL), lambda i: (i, 0)),
        out_shape=jax.ShapeDtypeStruct((t, D_MODEL), F32),
        scratch_shapes=_gather_scratch(tm),
        compiler_params=_cparams(("arbitrary",)),
        name="moe_combine_final_norm",
    )(pos.reshape(n_steps, 1, tm), pos.reshape(n_steps, 1, tm), ys, x, gt, fin_gain)


_PAIR_A = np.array([0, 0, 0, 1, 1, 2], np.int32)
_PAIR_B = np.array([1, 2, 3, 2, 3, 3], np.int32)


def _moe_sorted(xs, tab, counts, wr_t, br, wg, wu, wd):
    n_tiles = xs.shape[0] // (MOE_TILE * TOKEN_ROWS)
    cnt = counts[:N_CLASSES, 0].astype(jnp.int32)
    n_used = jnp.sum((cnt + MOE_TILE - 1) // MOE_TILE)
    ids = jnp.arange(n_tiles, dtype=jnp.int32)
    tile_cls = jnp.sum(tab, axis=0)[:n_tiles].astype(jnp.int32)
    last_cls = jnp.sum(jnp.where(ids == n_used - 1, tile_cls, 0))
    tile_cls = jnp.where(ids < n_used, tile_cls, last_cls)
    j = jnp.minimum(ids, n_used - 1)
    pair = tile_cls % PAIRS_PER_GROUP
    pair_a = jnp.sum(jnp.where(pair[:, None] == jnp.arange(PAIRS_PER_GROUP)[None, :], _PAIR_A[None, :], 0), axis=1)
    pair_b = jnp.sum(jnp.where(pair[:, None] == jnp.arange(PAIRS_PER_GROUP)[None, :], _PAIR_B[None, :], 0), axis=1)
    base = (tile_cls // PAIRS_PER_GROUP) * EXPERTS_PER_GROUP
    return _experts(xs, base + pair_a, base + pair_b, j, n_used.reshape(1), wr_t, br, wg, wu, wd)


def _rope_tables(seq):
    inv_freq = ROPE_THETA ** (-jnp.arange(0, ROT_DIM, 2, dtype=jnp.float32) / ROT_DIM)
    ang = jnp.arange(seq).astype(jnp.float32)[:, None] * inv_freq[None, :]
    cos, sin = jnp.cos(ang), jnp.sin(ang)
    ones = jnp.ones((seq, HEAD_DIM - ROT_DIM), jnp.float32)
    cos_h = jnp.concatenate([cos, cos, ones], axis=1)
    sgn_h = jnp.concatenate([-sin, sin, 0.0 * ones], axis=1)
    reps = LANES // HEAD_DIM
    return jnp.tile(cos_h, (1, reps)).astype(F32), jnp.tile(sgn_h, (1, reps)).astype(F32)


def _pad_heads(w, axis):
    pad = [(0, 0)] * w.ndim
    pad[axis] = (0, GROUP_WIDTH - HEADS_PER_GROUP * HEAD_DIM)
    return jnp.pad(w, pad)


def _router_weights(w_group, b_group, w_expert, b_expert):
    we = jnp.transpose(w_expert, (1, 0, 2)).reshape(D_MODEL, N_EXPERTS)
    w = jnp.concatenate([w_group, we], axis=1)
    w = jnp.pad(w, ((0, 0), (0, ROUTER_ROWS - w.shape[1])))
    bias = jnp.pad(jnp.concatenate([b_group, b_expert.reshape(-1)]), (0, ROUTER_ROWS - 20))
    return w.T.astype(BF16), bias.reshape(ROUTER_ROWS, 1)


def kernel(x, c, norm_mix, norm_ffn, w_ada, b_ada, a_w_qkv, a_w_o, b_w_in, b_v_gain, b_w_s, b_b_s, b_w_o,
           r_w_group, r_b_group, r_w_expert, r_b_expert, e_w_gate, e_w_up, e_w_down, final_norm):
    b, s, _ = x.shape
    mod = _ada_mod(c, w_ada, b_ada)
    sh_m, sc_m, gt_m, sh_f, sc_f, gt_f = [mod[:, :, i:i + 1] for i in range(6)]
    row = lambda v: v.reshape(1, -1)
    gw = HEADS_PER_GROUP * HEAD_DIM

    wq, wk, wv = [a_w_qkv[0][:, i * 3 * gw:(i + 1) * 3 * gw] for i in range(3)]
    wq = wq * (HEAD_DIM ** -0.5)
    w_groups = jnp.stack([
        jnp.concatenate([_pad_heads(w[:, g * gw:(g + 1) * gw], 1) for w in (wq, wk, wv)], axis=1)
        for g in range(N_ATTN_GROUPS)]).astype(BF16)
    wo_groups = jnp.stack([_pad_heads(a_w_o[0][g * gw:(g + 1) * gw], 0)
                           for g in range(N_ATTN_GROUPS)]).astype(BF16)
    expand = (jnp.arange(LANES)[:, None] == jnp.arange(GROUP_WIDTH)[None, :] // HEAD_DIM).astype(BF16)
    cos_t, sgn_t = _rope_tables(s)
    qkv = _qkv(x, row(norm_mix[0]), sh_m[0], sc_m[0], w_groups, cos_t, sgn_t)
    o_list, l_list = [], []
    for g in range(N_ATTN_GROUPS):
        o, l = _attention(*qkv[3 * g:3 * g + 3])
        o_list.append(o)
        l_list.append(l)
    wr_t, br = _router_weights(r_w_group[0], r_b_group[0], r_w_expert[0], r_b_expert[0])
    x1, xs, pos, tab, counts = _attn_out(o_list, l_list, x, wo_groups, expand, gt_m[0], row(norm_ffn[0]),
                                         sh_f[0], sc_f[0], wr_t, br)
    t = b * s
    token_order = lambda p: p[:, :, 0, :].reshape(t)
    ys = _moe_sorted(xs, tab, counts, wr_t, br, e_w_gate[0].astype(BF16), e_w_up[0].astype(BF16),
                     e_w_down[0].astype(BF16))

    wr_t, br = _router_weights(r_w_group[1], r_b_group[1], r_w_expert[1], r_b_expert[1])
    x3, xs, pos, tab, counts = _gmlp(ys, token_order(pos), gt_f[0], x1, row(norm_mix[1]), sh_m[1], sc_m[1],
                                     b_w_in[0].astype(BF16), row(b_v_gain[0]), b_w_s[0].astype(BF16), b_b_s[0].T,
                                     b_w_o[0].astype(BF16), gt_m[1], row(norm_ffn[1]), sh_f[1], sc_f[1], wr_t, br)
    ys = _moe_sorted(xs, tab, counts, wr_t, br, e_w_gate[1].astype(BF16), e_w_up[1].astype(BF16),
                     e_w_down[1].astype(BF16))
    out = _final(ys, token_order(pos), x3.reshape(t, D_MODEL), gt_f[1], row(final_norm), s)
    return out.reshape(b, s, D_MODEL)
```

```python
import functools

import jax
import jax.numpy as jnp
import numpy as np
from jax import lax
from jax.experimental import pallas as pl
from jax.experimental.pallas import tpu as pltpu

D_MODEL = 1024
DEPTH = 2
HEAD_DIM = 64
HEADS_PER_GROUP = 5
N_ATTN_GROUPS = 3
DILATIONS = (1, 4, 16)
BAND_RADIUS = 64
ROT_DIM = HEAD_DIM // 4
ROPE_THETA = 500000.0
NEG_INF = -1e30
CHUNK = 128
GMLP_HALF = 2 * D_MODEL
GMLP_GROUPS = 8
N_EXPERT_GROUPS = 4
EXPERTS_PER_GROUP = 4
N_EXPERTS = 16
D_EXPERT = 256
EPS = 1e-6

LANES = 128
GROUP_WIDTH = 384
ROUTER_ROWS = 32
PAIRS_PER_GROUP = 6
N_CLASSES = N_EXPERT_GROUPS * PAIRS_PER_GROUP
CLASS_ROWS = 32
SUBLANES = 8
TOKEN_ROWS = D_MODEL // LANES
MOE_TILE = 512

F32 = jnp.float32
BF16 = jnp.bfloat16
VMEM_LIMIT = 56 * 1024 * 1024


def _cparams(sem):
    return pltpu.CompilerParams(dimension_semantics=sem, vmem_limit_bytes=VMEM_LIMIT)


def _const_spec(shape, index_map):
    return pl.BlockSpec(shape, index_map, pipeline_mode=pl.Buffered(1))


def _silu(x):
    return x * (1.0 / (1.0 + jnp.exp(-x)))


def _gelu(x):
    return 0.5 * x * (1.0 + lax.erf(x * (2.0 ** -0.5)))


def _norm_modulate(x, gain, shift, scale):
    ms = jnp.mean(x * x, axis=-1, keepdims=True)
    return (x * lax.rsqrt(ms + EPS) * gain) * (1.0 + scale) + shift


def _split_bf16(x):
    hi = x.astype(BF16)
    return hi, (x - hi.astype(F32)).astype(BF16)


def _ada_kernel(c_ref, w_ref, b_ref, o_ref):
    ca_hi, ca_lo = _split_bf16(_silu(c_ref[...]))
    w_hi, w_lo = _split_bf16(w_ref[0])
    dot = functools.partial(jnp.dot, preferred_element_type=F32)
    o_ref[0] = dot(ca_hi, w_hi) + dot(ca_lo, w_hi) + dot(ca_hi, w_lo) + b_ref[0]


def _ada_mod(c, w_ada, b_ada):
    b = c.shape[0]
    tn = 1536
    c8 = jnp.pad(c, ((0, 8 - b), (0, 0)))
    out = pl.pallas_call(
        _ada_kernel,
        grid=(DEPTH, 6 * D_MODEL // tn),
        in_specs=[pl.BlockSpec((8, D_MODEL), lambda l, j: (0, 0)),
                  pl.BlockSpec((1, D_MODEL, tn), lambda l, j: (l, 0, j)),
                  pl.BlockSpec((1, 1, tn), lambda l, j: (l, 0, j))],
        out_specs=pl.BlockSpec((1, 8, tn), lambda l, j: (l, 0, j)),
        out_shape=jax.ShapeDtypeStruct((DEPTH, 8, 6 * D_MODEL), F32),
        compiler_params=_cparams(("arbitrary", "arbitrary")),
        name="ada_mod",
    )(c8, w_ada, b_ada.reshape(DEPTH, 1, 6 * D_MODEL))
    return out[:, :b].reshape(DEPTH, b, 6, 1, D_MODEL)


def _router_logits(hf, wr_ref, br_ref):
    return lax.dot_general(wr_ref[...], hf, (((1,), (1,)), ((), ())), preferred_element_type=F32) + br_ref[...]


def _group_choice(lt):
    gl = [lt[i:i + 1, :] for i in range(N_EXPERT_GROUPS)]
    gmax = jnp.maximum(jnp.maximum(gl[0], gl[1]), jnp.maximum(gl[2], gl[3]))
    ge = [jnp.exp(g - gmax) for g in gl]
    gsum = ge[0] + ge[1] + ge[2] + ge[3]
    gp = [e / gsum for e in ge]
    g_val = jnp.maximum(jnp.maximum(gp[0], gp[1]), jnp.maximum(gp[2], gp[3]))
    g_idx = jnp.where(gp[0] == g_val, 0, jnp.where(gp[1] == g_val, 1, jnp.where(gp[2] == g_val, 2, 3)))
    return g_val, g_idx


def _ffn_prep(x1, gain, shift, scale, wr_ref, br_ref):
    hf32 = _norm_modulate(x1, gain, shift, scale)
    lt = _router_logits(hf32.astype(BF16), wr_ref, br_ref)
    _, g_idx = _group_choice(lt)
    el = []
    for j in range(EXPERTS_PER_GROUP):
        rows = [lt[4 + 4 * g + j:5 + 4 * g + j, :] for g in range(N_EXPERT_GROUPS)]
        el.append(jnp.where(g_idx == 0, rows[0], jnp.where(g_idx == 1, rows[1],
                  jnp.where(g_idx == 2, rows[2], rows[3]))))
    v1 = jnp.maximum(jnp.maximum(el[0], el[1]), jnp.maximum(el[2], el[3]))
    i1 = jnp.where(el[0] == v1, 0, jnp.where(el[1] == v1, 1, jnp.where(el[2] == v1, 2, 3)))
    el2 = [jnp.where(i1 == j, -jnp.inf, el[j]) for j in range(EXPERTS_PER_GROUP)]
    v2 = jnp.maximum(jnp.maximum(el2[0], el2[1]), jnp.maximum(el2[2], el2[3]))
    i2 = jnp.where((el2[0] == v2) & (i1 != 0), 0,
                   jnp.where((el2[1] == v2) & (i1 != 1), 1,
                             jnp.where((el2[2] == v2) & (i1 != 2), 2, 3)))
    a = jnp.minimum(i1, i2)
    bb = jnp.maximum(i1, i2)
    cls = g_idx * PAIRS_PER_GROUP + ((a * (7 - a)) >> 1) + (bb - a - 1)
    return hf32, cls


def _place_tokens(cls, carry_ref, cur_ref, tab_ref, first_step):
    tm = cls.shape[1]

    @pl.when(first_step)
    def _():
        carry_ref[...] = jnp.zeros_like(carry_ref)
        cur_ref[...] = jnp.zeros_like(cur_ref)
        tab_ref[...] = jnp.zeros_like(tab_ref)

    crow = lax.broadcasted_iota(jnp.int32, (CLASS_ROWS, tm), 0)
    onehot = (crow == cls).astype(F32)
    si = lax.broadcasted_iota(jnp.int32, (tm, tm), 0)
    ti = lax.broadcasted_iota(jnp.int32, (tm, tm), 1)
    before = (si < ti).astype(BF16)
    prefix = jnp.dot(onehot.astype(BF16), before, preferred_element_type=F32)
    seen = prefix + carry_ref[:, 0:1]
    rank = jnp.sum(onehot * seen, axis=0, keepdims=True)
    inv_tile = 1.0 / MOE_TILE
    rem = rank - MOE_TILE * jnp.floor(rank * inv_tile)
    opens = (rem == 0.0).astype(F32)
    opened_before = jnp.sum(jnp.floor((seen + (MOE_TILE - 1)) * inv_tile), axis=0, keepdims=True)
    sel = onehot * opens
    opened = jnp.sum(sel, axis=1, keepdims=True) > 0.0
    new_tile = jnp.sum(sel * opened_before, axis=1, keepdims=True)
    first_rank = jnp.sum(sel * rank, axis=1, keepdims=True)
    tile = jnp.sum(onehot * jnp.where(opened & (seen >= first_rank), new_tile, cur_ref[:, 0:1]),
                   axis=0, keepdims=True)
    carry_ref[...] = carry_ref[...] + jnp.sum(onehot, axis=1, keepdims=True)
    cur_ref[...] = jnp.where(opened, new_tile, cur_ref[...])
    lane = lax.broadcasted_iota(jnp.int32, (CLASS_ROWS, LANES), 1).astype(F32)
    cidx = lax.broadcasted_iota(jnp.int32, (CLASS_ROWS, LANES), 0).astype(F32)
    tab_ref[...] = tab_ref[...] + jnp.where(opened & (lane == new_tile), cidx, 0.0)
    return (tile * MOE_TILE + rem).astype(jnp.int32)


def _rope(t, cos, sgn, first_half):
    out = []
    for c in range(GROUP_WIDTH // LANES):
        xc = t[:, c * LANES:(c + 1) * LANES]
        other = jnp.where(first_half, pltpu.roll(xc, LANES - ROT_DIM // 2, 1),
                          pltpu.roll(xc, ROT_DIM // 2, 1))
        out.append(xc * cos + other * sgn)
    return jnp.concatenate(out, axis=1)


def _qkv_kernel(x_ref, gain_ref, sh_ref, sc_ref, w_ref, cos_ref, sgn_ref, *rest, tm):
    out_refs, hn_scr = rest[:9], rest[9]
    hn = _norm_modulate(x_ref[0], gain_ref[...], sh_ref[0, 0], sc_ref[0, 0])
    n_slab = D_MODEL // LANES
    lane = lax.broadcasted_iota(jnp.int32, (tm, LANES), 1) % HEAD_DIM
    first_half = lane < ROT_DIM // 2
    for g, d in enumerate(DILATIONS):
        rows = tm // d
        if d == 1:
            hp, cos, sgn = hn, cos_ref[...], sgn_ref[...]
        else:
            if g == 1:
                for s in range(n_slab):
                    hn_scr[s] = hn[:, s * LANES:(s + 1) * LANES]
            hp = jnp.concatenate(
                [jnp.concatenate([hn_scr[s, pl.ds(r, rows, stride=d), :] for s in range(n_slab)], axis=1)
                 for r in range(d)], axis=0)
            cos = jnp.concatenate([cos_ref[pl.ds(r, rows, stride=d), :] for r in range(d)], axis=0)
            sgn = jnp.concatenate([sgn_ref[pl.ds(r, rows, stride=d), :] for r in range(d)], axis=0)
        res = jnp.dot(hp.astype(BF16), w_ref[g], preferred_element_type=F32)
        q = _rope(res[:, :GROUP_WIDTH], cos, sgn, first_half)
        k = _rope(res[:, GROUP_WIDTH:2 * GROUP_WIDTH], cos, sgn, first_half)
        v = res[:, 2 * GROUP_WIDTH:]
        for t, o_ref in zip((q, k, v), out_refs[3 * g:3 * g + 3]):
            o_ref[0] = t.astype(BF16).reshape(d, rows, GROUP_WIDTH)


def _qkv(x, gain, shift, scale, w_groups, cos_t, sgn_t, tm=512):
    b, s, _ = x.shape
    out_shape, out_specs = [], []
    for d in DILATIONS:
        for _ in range(3):
            out_shape.append(jax.ShapeDtypeStruct((b, d, s // d, GROUP_WIDTH), BF16))
            out_specs.append(pl.BlockSpec((1, d, tm // d, GROUP_WIDTH), lambda bi, i: (bi, 0, i, 0)))
    return pl.pallas_call(
        functools.partial(_qkv_kernel, tm=tm),
        grid=(b, s // tm),
        in_specs=[pl.BlockSpec((1, tm, D_MODEL), lambda bi, i: (bi, i, 0)),
                  _const_spec((1, D_MODEL), lambda bi, i: (0, 0)),
                  pl.BlockSpec((1, 1, 1, D_MODEL), lambda bi, i: (bi, 0, 0, 0)),
                  pl.BlockSpec((1, 1, 1, D_MODEL), lambda bi, i: (bi, 0, 0, 0)),
                  _const_spec((N_ATTN_GROUPS, D_MODEL, 3 * GROUP_WIDTH), lambda bi, i: (0, 0, 0)),
                  pl.BlockSpec((tm, LANES), lambda bi, i: (i, 0)),
                  pl.BlockSpec((tm, LANES), lambda bi, i: (i, 0))],
        out_specs=out_specs,
        out_shape=out_shape,
        scratch_shapes=[pltpu.VMEM((D_MODEL // LANES, tm, LANES), F32)],
        compiler_params=_cparams(("arbitrary", "arbitrary")),
        name="qkv_rope",
    )(x, gain, shift, scale, w_groups, cos_t, sgn_t)


def _attn_kernel(q_ref, kp_ref, kc_ref, kn_ref, vp_ref, vc_ref, vn_ref, o_ref, lse_ref,
                 kbuf, vbuf, *, seq, tq):
    i = pl.program_id(2)
    r = BAND_RADIUS
    kbuf[0:r] = kp_ref[0, 0]
    kbuf[r:r + tq] = kc_ref[0, 0]
    kbuf[r + tq:] = kn_ref[0, 0]
    vbuf[0:r] = vp_ref[0, 0]
    vbuf[r:r + tq] = vc_ref[0, 0]
    vbuf[r + tq:] = vn_ref[0, 0]
    qb = 2 * r
    lane = lax.broadcasted_iota(jnp.int32, (qb, LANES), 1)
    low = lane < HEAD_DIM
    tidx = lax.broadcasted_iota(jnp.int32, (qb, 2 * qb), 0)
    kidx = lax.broadcasted_iota(jnp.int32, (qb, 2 * qb), 1)
    rel = kidx - tidx
    band = (rel >= 0) & (rel <= 2 * r)
    for j in range(tq // qb):
        kpos = i * tq + (j * qb - r) + kidx
        mask = band & (kpos >= 0) & (kpos < seq)
        q = q_ref[0, 0, j * qb:(j + 1) * qb, :]
        k = kbuf[j * qb:(j + 2) * qb, :]
        v = vbuf[j * qb:(j + 2) * qb, :]
        lse_tile = jnp.zeros((qb, LANES), F32)
        o_chunks = []
        for c in range(GROUP_WIDTH // LANES):
            qc = q[:, c * LANES:(c + 1) * LANES]
            kc = k[:, c * LANES:(c + 1) * LANES]
            vc = v[:, c * LANES:(c + 1) * LANES]
            outs = []
            for hh in range(2):
                head = 2 * c + hh
                if head >= HEADS_PER_GROUP:
                    outs.append(jnp.zeros((qb, LANES), F32))
                    continue
                qm = jnp.where(low if hh == 0 else jnp.logical_not(low), qc, jnp.zeros_like(qc))
                s = lax.dot_general(qm, kc, (((1,), (1,)), ((), ())), preferred_element_type=F32)
                s = jnp.where(mask, s, NEG_INF)
                m = jnp.max(s, axis=-1, keepdims=True)
                p = jnp.exp(s - m)
                den = jnp.sum(p, axis=-1, keepdims=True)
                o = jnp.dot(p.astype(BF16), vc, preferred_element_type=F32) / den
                lse_tile = jnp.where(lane == head, m + jnp.log(den), lse_tile)
                outs.append(o)
            o_chunks.append(jnp.where(low, outs[0], outs[1]))
        o_ref[0, 0, j * qb:(j + 1) * qb, :] = jnp.concatenate(o_chunks, axis=1).astype(BF16)
        lse_ref[0, 0, j * qb:(j + 1) * qb, :] = lse_tile


def _attention(q, k, v, tq=512):
    b, d, seq, w = q.shape
    r = BAND_RADIUS
    nb = seq // r
    cur = pl.BlockSpec((1, 1, tq, w), lambda bi, ri, i: (bi, ri, i, 0))
    prev = pl.BlockSpec((1, 1, r, w), lambda bi, ri, i: (bi, ri, jnp.maximum(i * (tq // r) - 1, 0), 0))
    nxt = pl.BlockSpec((1, 1, r, w), lambda bi, ri, i: (bi, ri, jnp.minimum((i + 1) * (tq // r), nb - 1), 0))
    return pl.pallas_call(
        functools.partial(_attn_kernel, seq=seq, tq=tq),
        grid=(b, d, seq // tq),
        in_specs=[cur, prev, cur, nxt, prev, cur, nxt],
        out_specs=[pl.BlockSpec((1, 1, tq, w), lambda bi, ri, i: (bi, ri, i, 0)),
                   pl.BlockSpec((1, 1, tq, LANES), lambda bi, ri, i: (bi, ri, i, 0))],
        out_shape=[jax.ShapeDtypeStruct((b, d, seq, w), BF16),
                   jax.ShapeDtypeStruct((b, d, seq, LANES), F32)],
        scratch_shapes=[pltpu.VMEM((tq + 2 * r, w), BF16), pltpu.VMEM((tq + 2 * r, w), BF16)],
        compiler_params=_cparams(("arbitrary", "arbitrary", "arbitrary")),
        name=f"band_attn_d{d}",
    )(q, k, k, k, v, v, v)


def _attn_out_kernel(o0, o1, o2, l0, l1, l2, x_ref, wo_ref, exp_ref, gt_ref, gain_ref, sh_ref, sc_ref,
                     wr_ref, br_ref, x1_ref, xs_hbm, pos_ref, tab_out, cnt_out, oscr, lscr, *route_scratch,
                     tm, n_steps):
    n_slab = GROUP_WIDTH // LANES
    for g, (d, o_ref, l_ref) in enumerate(zip(DILATIONS, (o0, o1, o2), (l0, l1, l2))):
        rows = tm // d
        for r in range(d):
            dst = slice(None) if d == 1 else pl.ds(r, rows, stride=d)
            blk = o_ref[0, r].astype(F32)
            for c in range(n_slab):
                oscr[g * n_slab + c, dst, :] = blk[:, c * LANES:(c + 1) * LANES]
            lscr[g, dst, :] = l_ref[0, r]
    lse = [lscr[g] for g in range(N_ATTN_GROUPS)]
    mx = jnp.maximum(jnp.maximum(lse[0], lse[1]), lse[2])
    ex = [jnp.exp(l - mx) for l in lse]
    tot = ex[0] + ex[1] + ex[2]
    y = jnp.zeros((tm, D_MODEL), F32)
    for g in range(N_ATTN_GROUPS):
        alpha = ex[g] / tot
        a_hi, a_lo = _split_bf16(alpha)
        a_wide = (jnp.dot(a_hi, exp_ref[...], preferred_element_type=F32)
                  + jnp.dot(a_lo, exp_ref[...], preferred_element_type=F32))
        og = jnp.concatenate([oscr[g * n_slab + c] for c in range(n_slab)], axis=1)
        y = y + jnp.dot((og * a_wide).astype(BF16), wo_ref[g], preferred_element_type=F32)
    x1 = x_ref[0] + gt_ref[0, 0] * y
    x1_ref[0] = x1
    _route_outputs(x1, gain_ref, sh_ref, sc_ref, wr_ref, br_ref, xs_hbm, pos_ref, tab_out, cnt_out, route_scratch,
                   n_steps)


ZERO_TOKENS = MOE_TILE // 2


def _route_outputs(x1, gain_ref, sh_ref, sc_ref, wr_ref, br_ref, xs_hbm, pos_ref, tab_out, cnt_out, scratch,
                   n_steps):
    (carry_ref, cur_ref, tab_ref, rows_scr, pos_vmem, pos_smem, fin_vmem, fin_smem, zbuf, sems, aux_sem) = scratch
    step = pl.program_id(0) * pl.num_programs(1) + pl.program_id(1)
    tm = x1.shape[0]
    rows = tm * TOKEN_ROWS
    hf32, cls = _ffn_prep(x1, gain_ref[...], sh_ref[0, 0], sc_ref[0, 0], wr_ref, br_ref)
    pos = _place_tokens(cls, carry_ref, cur_ref, tab_ref, step == 0)
    r8 = lax.broadcasted_iota(jnp.int32, (SUBLANES, tm), 0)
    pos8 = jnp.where(r8 == 0, pos, 0)
    pos_ref[0, 0] = pos8
    pos_vmem[...] = pos8
    to_smem = pltpu.make_async_copy(pos_vmem, pos_smem, aux_sem)
    to_smem.start()

    def drain(slot):
        pltpu.make_async_copy(rows_scr.at[pl.ds(0, rows), :], xs_hbm.at[pl.ds(0, rows), :], sems.at[slot]).wait()

    slot = step % 2

    @pl.when(step >= 2)
    def _():
        drain(slot)

    base = pl.multiple_of(slot * rows, rows)
    for s in range(TOKEN_ROWS):
        rows_scr[pl.ds(base + s, tm, stride=TOKEN_ROWS), :] = hf32[:, s * LANES:(s + 1) * LANES]
    to_smem.wait()

    def issue(r, priority):
        _token_copy(rows_scr, slot * tm + r, xs_hbm, pos_smem[0, r], sems.at[slot]).start(priority=priority)

    _issue_burst(0, tm, issue, spread=False)
    tab_out[...] = tab_ref[...]
    cnt_out[...] = carry_ref[...]

    @pl.when(step == n_steps - 1)
    def _():
        drain(slot)
        if n_steps > 1:
            drain(1 - slot)
        _clear_unused_rows(carry_ref, cur_ref, xs_hbm, fin_vmem, fin_smem, zbuf, aux_sem)


def _clear_unused_rows(carry_ref, cur_ref, xs_hbm, fin_vmem, fin_smem, zbuf, sem):
    lane = lax.broadcasted_iota(jnp.int32, (CLASS_ROWS, LANES), 1)
    crow = lax.broadcasted_iota(jnp.int32, (CLASS_ROWS, LANES), 0)
    on_lane = lambda col: jnp.sum(jnp.where(lane == crow, col, 0.0), axis=0, keepdims=True)
    counts = carry_ref[...]
    used = jnp.sum(jnp.floor((counts + (MOE_TILE - 1)) * (1.0 / MOE_TILE)), axis=0, keepdims=True)
    r8 = lax.broadcasted_iota(jnp.int32, (SUBLANES, LANES), 0)
    fin = jnp.where(r8 == 0, on_lane(cur_ref[...]), jnp.where(r8 == 1, on_lane(counts), jnp.where(r8 == 2, used, 0.0)))
    fin_vmem[...] = fin.astype(jnp.int32)
    to_smem = pltpu.make_async_copy(fin_vmem, fin_smem, sem)
    to_smem.start()
    zbuf[...] = jnp.zeros_like(zbuf)
    to_smem.wait()

    def zero_copy(first_tok, n_tok):
        d0 = pl.multiple_of(first_tok * TOKEN_ROWS, TOKEN_ROWS)
        return pltpu.make_async_copy(zbuf.at[pl.ds(0, n_tok * TOKEN_ROWS), :],
                                     xs_hbm.at[pl.ds(d0, n_tok * TOKEN_ROWS), :], sem)

    for wait in (False, True):
        for c in range(N_CLASSES):
            count = fin_smem[1, c]
            fill = jnp.where(count > 0, (MOE_TILE - count % MOE_TILE) % MOE_TILE, 0)
            first = fin_smem[0, c] * MOE_TILE + (MOE_TILE - fill)
            piece = ZERO_TOKENS
            while piece >= 1:
                done = (fill // (2 * piece)) * (2 * piece)

                @pl.when((fill // piece) % 2 == 1)
                def _(first=first, done=done, piece=piece):
                    cp = zero_copy(first + done, piece)
                    cp.wait() if wait else cp.start()

                piece //= 2

    def clear_tile(j, carry):
        for half in range(MOE_TILE // ZERO_TOKENS):
            cp = zero_copy(j * MOE_TILE + half * ZERO_TOKENS, ZERO_TOKENS)
            cp.start()
            cp.wait()
        return carry

    lax.fori_loop(fin_smem[2, 0], xs_hbm.shape[0] // (MOE_TILE * TOKEN_ROWS), clear_tile, 0)


def _route_scratch(tm):
    return [pltpu.VMEM((CLASS_ROWS, LANES), F32), pltpu.VMEM((CLASS_ROWS, LANES), F32),
            pltpu.VMEM((CLASS_ROWS, LANES), F32),
            pltpu.VMEM((2 * tm * TOKEN_ROWS, LANES), F32),
            pltpu.VMEM((SUBLANES, tm), jnp.int32), pltpu.SMEM((SUBLANES, tm), jnp.int32),
            pltpu.VMEM((SUBLANES, LANES), jnp.int32), pltpu.SMEM((SUBLANES, LANES), jnp.int32),
            pltpu.VMEM((ZERO_TOKENS * TOKEN_ROWS, LANES), F32),
            pltpu.SemaphoreType.DMA((2,)), pltpu.SemaphoreType.DMA]


N_ROUTE_SCRATCH = 11


def _route_out_specs(b, s, tm):
    n_tiles = b * s // MOE_TILE + N_CLASSES
    specs = [pl.BlockSpec((1, tm, D_MODEL), lambda bi, i: (bi, i, 0)),
             pl.BlockSpec(memory_space=pl.ANY),
             pl.BlockSpec((1, 1, SUBLANES, tm), lambda bi, i: (bi, i, 0, 0)),
             pl.BlockSpec((CLASS_ROWS, LANES), lambda bi, i: (0, 0)),
             pl.BlockSpec((CLASS_ROWS, LANES), lambda bi, i: (0, 0))]
    shapes = [jax.ShapeDtypeStruct((b, s, D_MODEL), F32),
              jax.ShapeDtypeStruct((n_tiles * MOE_TILE * TOKEN_ROWS, LANES), F32),
              jax.ShapeDtypeStruct((b, s // tm, SUBLANES, tm), jnp.int32),
              jax.ShapeDtypeStruct((CLASS_ROWS, LANES), F32),
              jax.ShapeDtypeStruct((CLASS_ROWS, LANES), F32)]
    return specs, shapes


def _mod_spec():
    return pl.BlockSpec((1, 1, 1, D_MODEL), lambda bi, i: (bi, 0, 0, 0))


def _attn_out(o_list, l_list, x, wo_groups, expand, gt, gain, shift, scale, wr_t, br, tm=512):
    b, s, _ = x.shape
    o_specs = [pl.BlockSpec((1, d, tm // d, GROUP_WIDTH), lambda bi, i: (bi, 0, i, 0)) for d in DILATIONS]
    l_specs = [pl.BlockSpec((1, d, tm // d, LANES), lambda bi, i: (bi, 0, i, 0)) for d in DILATIONS]
    out_specs, out_shape = _route_out_specs(b, s, tm)
    return pl.pallas_call(
        functools.partial(_attn_out_kernel, tm=tm, n_steps=b * s // tm),
        grid=(b, s // tm),
        in_specs=o_specs + l_specs + [
            pl.BlockSpec((1, tm, D_MODEL), lambda bi, i: (bi, i, 0)),
            _const_spec((N_ATTN_GROUPS, GROUP_WIDTH, D_MODEL), lambda bi, i: (0, 0, 0)),
            _const_spec((LANES, GROUP_WIDTH), lambda bi, i: (0, 0)),
            _mod_spec(),
            _const_spec((1, D_MODEL), lambda bi, i: (0, 0)),
            _mod_spec(), _mod_spec(),
            _const_spec((ROUTER_ROWS, D_MODEL), lambda bi, i: (0, 0)),
            _const_spec((ROUTER_ROWS, 1), lambda bi, i: (0, 0))],
        out_specs=out_specs,
        out_shape=out_shape,
        scratch_shapes=[pltpu.VMEM((N_ATTN_GROUPS * GROUP_WIDTH // LANES, tm, LANES), F32),
                        pltpu.VMEM((N_ATTN_GROUPS, tm, LANES), F32)] + _route_scratch(tm),
        compiler_params=_cparams(("arbitrary", "arbitrary")),
        name="attn_merge_proj",
    )(*o_list, *l_list, x, wo_groups, expand, gt, gain, shift, scale, wr_t, br)


DMA_BURST_UNROLL = 8


BACKGROUND_PRIORITY = 1


def _issue_burst(lo, hi, issue, spread=True):
    def body(i, carry):
        for k in range(DMA_BURST_UNROLL):
            issue(lo + i * DMA_BURST_UNROLL + k, k % 2 if spread else BACKGROUND_PRIORITY)
        return carry

    lax.fori_loop(0, (hi - lo) // DMA_BURST_UNROLL, body, 0)


class _RowGather:
    def __init__(self, pos_ref, pos_next_ref, ys_hbm, ybuf, sems, step, n_steps, tm, spread):
        self.pos_ref, self.pos_next_ref, self.ys_hbm, self.ybuf, self.sems = pos_ref, pos_next_ref, ys_hbm, ybuf, sems
        self.step, self.n_steps, self.tm, self.spread = step, n_steps, tm, spread

    def _request(self, p_ref, slot, lo, hi):
        def issue(r, priority):
            _token_copy(self.ys_hbm, p_ref[0, 0, r], self.ybuf, slot * self.tm + r,
                        self.sems.at[slot]).start(priority=priority)

        _issue_burst(lo, hi, issue, self.spread)

    def request_next(self, piece=0, n_pieces=1):
        size = self.tm // n_pieces

        @pl.when(self.step + 1 < self.n_steps)
        def _():
            self._request(self.pos_next_ref, (self.step + 1) % 2, piece * size, (piece + 1) * size)

    def current(self):
        rows = self.tm * TOKEN_ROWS

        @pl.when(self.step == 0)
        def _():
            self._request(self.pos_ref, 0, 0, self.tm)

        slot = self.step % 2
        base = pl.multiple_of(slot * rows, rows)
        pltpu.make_async_copy(self.ys_hbm.at[pl.ds(0, rows), :], self.ybuf.at[pl.ds(base, rows), :],
                              self.sems.at[slot]).wait()
        return jnp.concatenate([self.ybuf[pl.ds(base + s, self.tm, stride=TOKEN_ROWS), :]
                                for s in range(D_MODEL // LANES)], axis=1)


def _gather_specs(n_steps, tm, index_of):
    return [pl.BlockSpec((1, 1, tm), lambda *g: (index_of(*g), 0, 0), memory_space=pltpu.SMEM),
            pl.BlockSpec((1, 1, tm), lambda *g: (jnp.minimum(index_of(*g) + 1, n_steps - 1), 0, 0),
                         memory_space=pltpu.SMEM),
            pl.BlockSpec(memory_space=pl.ANY)]


def _gather_scratch(tm):
    return [pltpu.VMEM((2 * tm * TOKEN_ROWS, LANES), F32), pltpu.SemaphoreType.DMA((2,))]


def _gmlp_kernel(pos_ref, pos_next_ref, ys_hbm, gtf_ref, x_ref, gain_ref, sh_ref, sc_ref, win_ref, vg_ref, ws_ref,
                 bs_ref, wo_ref, gt_ref, fgain_ref, fsh_ref, fsc_ref, wr_ref, br_ref, x1_ref, xs_hbm, pos_out, tab_out,
                 cnt_out, gated_scr, ybuf, sems, *route_scratch, tm, n_steps):
    step = pl.program_id(0) * pl.num_programs(1) + pl.program_id(1)
    gather = _RowGather(pos_ref, pos_next_ref, ys_hbm, ybuf, sems, step, n_steps, tm, spread=False)
    n_pieces = 4 + tm // CHUNK
    x = x_ref[0] + gtf_ref[0, 0] * gather.current()
    gather.request_next(0, n_pieces)
    hn = _norm_modulate(x, gain_ref[...], sh_ref[0, 0], sc_ref[0, 0]).astype(BF16)
    gather.request_next(1, n_pieces)
    v = _gelu(jnp.dot(hn, win_ref[:, GMLP_HALF:], preferred_element_type=F32))
    u = _gelu(jnp.dot(hn, win_ref[:, :GMLP_HALF], preferred_element_type=F32))
    mu = jnp.mean(v, axis=-1, keepdims=True)
    vc = v - mu
    vn = (vc * lax.rsqrt(jnp.mean(vc * vc, axis=-1, keepdims=True) + EPS) * vg_ref[...]).astype(BF16)
    gather.request_next(2, n_pieces)
    gw = GMLP_HALF // GMLP_GROUPS
    for c in range(tm // CHUNK):
        rs = slice(c * CHUNK, (c + 1) * CHUNK)
        for g in range(GMLP_GROUPS):
            cs = slice(g * gw, (g + 1) * gw)
            vs = jnp.dot(ws_ref[g], vn[rs, cs], preferred_element_type=F32) + bs_ref[:, g:g + 1]
            gated_scr[rs, cs] = (u[rs, cs] * vs).astype(BF16)
        gather.request_next(3 + c, n_pieces)
    y = jnp.dot(gated_scr[...], wo_ref[...], preferred_element_type=F32)
    gather.request_next(n_pieces - 1, n_pieces)
    x1 = x + gt_ref[0, 0] * y
    x1_ref[0] = x1
    _route_outputs(x1, fgain_ref, fsh_ref, fsc_ref, wr_ref, br_ref, xs_hbm, pos_out, tab_out, cnt_out,
                   route_scratch, n_steps)


def _gmlp(ys, pos, gt_prev, x, gain, shift, scale, w_in, v_gain, w_s, b_s_t, w_o, gt, fgain, fshift, fscale,
          wr_t, br, tm=512):
    b, s, _ = x.shape
    tok = lambda w: pl.BlockSpec((1, tm, w), lambda bi, i: (bi, i, 0))
    c2 = lambda shape: _const_spec(shape, lambda bi, i: (0,) * len(shape))
    out_specs, out_shape = _route_out_specs(b, s, tm)
    n_steps = b * s // tm
    return pl.pallas_call(
        functools.partial(_gmlp_kernel, tm=tm, n_steps=n_steps),
        grid=(b, s // tm),
        in_specs=_gather_specs(n_steps, tm, lambda bi, i: bi * (s // tm) + i) + [
                  _mod_spec(),
                  tok(D_MODEL), c2((1, D_MODEL)), _mod_spec(), _mod_spec(),
                  c2((D_MODEL, 2 * GMLP_HALF)), c2((1, GMLP_HALF)),
                  c2((GMLP_GROUPS, CHUNK, CHUNK)), c2((CHUNK, GMLP_GROUPS)),
                  c2((GMLP_HALF, D_MODEL)), _mod_spec(),
                  c2((1, D_MODEL)), _mod_spec(), _mod_spec(),
                  c2((ROUTER_ROWS, D_MODEL)), c2((ROUTER_ROWS, 1))],
        out_specs=out_specs,
        out_shape=out_shape,
        scratch_shapes=[pltpu.VMEM((tm, GMLP_HALF), BF16)] + _gather_scratch(tm) + _route_scratch(tm),
        compiler_params=_cparams(("arbitrary", "arbitrary")),
        name="gmlp",
    )(pos.reshape(n_steps, 1, tm), pos.reshape(n_steps, 1, tm), ys, gt_prev,
      x, gain, shift, scale, w_in, v_gain, w_s, b_s_t, w_o, gt, fgain, fshift, fscale, wr_t, br)


def _token_copy(src, src_tok, dst, dst_tok, sem):
    s0 = pl.multiple_of(src_tok * TOKEN_ROWS, TOKEN_ROWS)
    d0 = pl.multiple_of(dst_tok * TOKEN_ROWS, TOKEN_ROWS)
    return pltpu.make_async_copy(src.at[pl.ds(s0, TOKEN_ROWS), :], dst.at[pl.ds(d0, TOKEN_ROWS), :], sem)


def _slabs(ref, n_tok, n_slab):
    return jnp.concatenate([ref[pl.ds(s, n_tok, stride=TOKEN_ROWS), :] for s in range(n_slab)], axis=1)


def _expert_kernel(ea_ref, eb_ref, blk_ref, nused_ref, x_ref, wr_ref, br_ref, wga, wgb, wua, wub, wda, wdb, y_ref,
                   lt_scr):
    j = pl.program_id(0)

    @pl.when(j < nused_ref[0])
    def _():
        h = _slabs(x_ref, MOE_TILE, TOKEN_ROWS).astype(BF16)
        lt = _router_logits(h, wr_ref, br_ref)
        g_val, _ = _group_choice(lt)
        lt_scr[...] = lt
        el_a = lt_scr[pl.ds(N_EXPERT_GROUPS + ea_ref[j], 1), :]
        el_b = lt_scr[pl.ds(N_EXPERT_GROUPS + eb_ref[j], 1), :]
        top = jnp.maximum(el_a, el_b)
        p_a = jnp.exp(el_a - top)
        p_b = jnp.exp(el_b - top)
        row = lax.broadcasted_iota(jnp.int32, (LANES, MOE_TILE), 0)
        gates_t = jnp.where(row == 0, g_val * (p_a / (p_a + p_b)), jnp.where(row == 1, g_val * (p_b / (p_a + p_b)), 0.0))
        gates = gates_t.T
        y = jnp.zeros((MOE_TILE, D_MODEL), F32)
        for gate, wg, wu, wd in ((gates[:, 0:1], wga, wua, wda), (gates[:, 1:2], wgb, wub, wdb)):
            hg = jnp.dot(h, wg[0], preferred_element_type=F32)
            hu = jnp.dot(h, wu[0], preferred_element_type=F32)
            act = (_silu(hg) * hu * gate).astype(BF16)
            y = y + jnp.dot(act, wd[0], preferred_element_type=F32)
        for s in range(TOKEN_ROWS):
            y_ref[pl.ds(s, MOE_TILE, stride=TOKEN_ROWS), :] = y[:, s * LANES:(s + 1) * LANES]

    @pl.when(j >= nused_ref[0])
    def _():
        y_ref[...] = jnp.zeros_like(y_ref)


def _experts(xs, tile_ea, tile_eb, tile_blk, n_used, wr_t, br, wg, wu, wd):
    n_tiles = xs.shape[0] // (MOE_TILE * TOKEN_ROWS)
    up = lambda sel: pl.BlockSpec((1, D_MODEL, D_EXPERT), lambda j, ea, eb, blk, nu: ((ea, eb)[sel][j], 0, 0))
    down = lambda sel: pl.BlockSpec((1, D_EXPERT, D_MODEL), lambda j, ea, eb, blk, nu: ((ea, eb)[sel][j], 0, 0))
    shape = (MOE_TILE * TOKEN_ROWS, LANES)
    return pl.pallas_call(
        _expert_kernel,
        grid_spec=pltpu.PrefetchScalarGridSpec(
            num_scalar_prefetch=4,
            grid=(n_tiles,),
            in_specs=[pl.BlockSpec(shape, lambda j, ea, eb, blk, nu: (blk[j], 0)),
                      pl.BlockSpec((ROUTER_ROWS, D_MODEL), lambda j, ea, eb, blk, nu: (0, 0)),
                      pl.BlockSpec((ROUTER_ROWS, 1), lambda j, ea, eb, blk, nu: (0, 0)),
                      up(0), up(1), up(0), up(1), down(0), down(1)],
            out_specs=pl.BlockSpec(shape, lambda j, ea, eb, blk, nu: (j, 0)),
            scratch_shapes=[pltpu.VMEM((ROUTER_ROWS, MOE_TILE), F32)]),
        out_shape=jax.ShapeDtypeStruct(xs.shape, F32),
        compiler_params=_cparams(("arbitrary",)),
        name="moe_experts",
    )(tile_ea, tile_eb, tile_blk, n_used, xs, wr_t, br, wg, wg, wu, wu, wd, wd)


def _final_kernel(pos_ref, pos_next_ref, ys_hbm, x_ref, gt_ref, fin_ref, o_ref, ybuf, sems, *, tm):
    gather = _RowGather(pos_ref, pos_next_ref, ys_hbm, ybuf, sems, pl.program_id(0), pl.num_programs(0), tm,
                        spread=True)
    gather.request_next()
    y_moe = gather.current()
    x2 = x_ref[...] + gt_ref[0, 0] * y_moe
    ms = jnp.mean(x2 * x2, axis=-1, keepdims=True)
    o_ref[...] = x2 * lax.rsqrt(ms + EPS) * fin_ref[...]


def _final(ys, pos, x, gt, fin_gain, seq, tm=512):
    t = x.shape[0]
    tiles_per_batch = seq // tm
    n_steps = t // tm
    return pl.pallas_call(
        functools.partial(_final_kernel, tm=tm),
        grid=(n_steps,),
        in_specs=_gather_specs(n_steps, tm, lambda i: i) + [
                  pl.BlockSpec((tm, D_MODEL), lambda i: (i, 0)),
                  pl.BlockSpec((1, 1, 1, D_MODEL), lambda i: (i // tiles_per_batch, 0, 0, 0)),
                  pl.BlockSpec((1, D_MODEL), lambda i: (0, 0))],
        out_specs=pl.BlockSpec((tm, D_MODEL), lambda i: (i, 0)),
        out_shape=jax.ShapeDtypeStruct((t, D_MODEL), F32),
        scratch_shapes=_gather_scratch(tm),
        compiler_params=_cparams(("arbitrary",)),
        name="moe_combine_final_norm",
    )(pos.reshape(n_steps, 1, tm), pos.reshape(n_steps, 1, tm), ys, x, gt, fin_gain)


_PAIR_A = np.array([0, 0, 0, 1, 1, 2], np.int32)
_PAIR_B = np.array([1, 2, 3, 2, 3, 3], np.int32)


def _moe_sorted(xs, tab, counts, wr_t, br, wg, wu, wd):
    n_tiles = xs.shape[0] // (MOE_TILE * TOKEN_ROWS)
    cnt = counts[:N_CLASSES, 0].astype(jnp.int32)
    n_used = jnp.sum((cnt + MOE_TILE - 1) // MOE_TILE)
    ids = jnp.arange(n_tiles, dtype=jnp.int32)
    tile_cls = jnp.sum(tab, axis=0)[:n_tiles].astype(jnp.int32)
    last_cls = jnp.sum(jnp.where(ids == n_used - 1, tile_cls, 0))
    tile_cls = jnp.where(ids < n_used, tile_cls, last_cls)
    j = jnp.minimum(ids, n_used - 1)
    pair = tile_cls % PAIRS_PER_GROUP
    pair_a = jnp.sum(jnp.where(pair[:, None] == jnp.arange(PAIRS_PER_GROUP)[None, :], _PAIR_A[None, :], 0), axis=1)
    pair_b = jnp.sum(jnp.where(pair[:, None] == jnp.arange(PAIRS_PER_GROUP)[None, :], _PAIR_B[None, :], 0), axis=1)
    base = (tile_cls // PAIRS_PER_GROUP) * EXPERTS_PER_GROUP
    return _experts(xs, base + pair_a, base + pair_b, j, n_used.reshape(1), wr_t, br, wg, wu, wd)


def _rope_tables(seq):
    inv_freq = ROPE_THETA ** (-jnp.arange(0, ROT_DIM, 2, dtype=jnp.float32) / ROT_DIM)
    ang = jnp.arange(seq).astype(jnp.float32)[:, None] * inv_freq[None, :]
    cos, sin = jnp.cos(ang), jnp.sin(ang)
    ones = jnp.ones((seq, HEAD_DIM - ROT_DIM), jnp.float32)
    cos_h = jnp.concatenate([cos, cos, ones], axis=1)
    sgn_h = jnp.concatenate([-sin, sin, 0.0 * ones], axis=1)
    reps = LANES // HEAD_DIM
    return jnp.tile(cos_h, (1, reps)).astype(F32), jnp.tile(sgn_h, (1, reps)).astype(F32)


def _pad_heads(w, axis):
    pad = [(0, 0)] * w.ndim
    pad[axis] = (0, GROUP_WIDTH - HEADS_PER_GROUP * HEAD_DIM)
    return jnp.pad(w, pad)


def _router_weights(w_group, b_group, w_expert, b_expert):
    we = jnp.transpose(w_expert, (1, 0, 2)).reshape(D_MODEL, N_EXPERTS)
    w = jnp.concatenate([w_group, we], axis=1)
    w = jnp.pad(w, ((0, 0), (0, ROUTER_ROWS - w.shape[1])))
    bias = jnp.pad(jnp.concatenate([b_group, b_expert.reshape(-1)]), (0, ROUTER_ROWS - 20))
    return w.T.astype(BF16), bias.reshape(ROUTER_ROWS, 1)


def kernel(x, c, norm_mix, norm_ffn, w_ada, b_ada, a_w_qkv, a_w_o, b_w_in, b_v_gain, b_w_s, b_b_s, b_w_o,
           r_w_group, r_b_group, r_w_expert, r_b_expert, e_w_gate, e_w_up, e_w_down, final_norm):
    b, s, _ = x.shape
    mod = _ada_mod(c, w_ada, b_ada)
    sh_m, sc_m, gt_m, sh_f, sc_f, gt_f = [mod[:, :, i:i + 1] for i in range(6)]
    row = lambda v: v.reshape(1, -1)
    gw = HEADS_PER_GROUP * HEAD_DIM

    wq, wk, wv = [a_w_qkv[0][:, i * 3 * gw:(i + 1) * 3 * gw] for i in range(3)]
    wq = wq * (HEAD_DIM ** -0.5)
    w_groups = jnp.stack([
        jnp.concatenate([_pad_heads(w[:, g * gw:(g + 1) * gw], 1) for w in (wq, wk, wv)], axis=1)
        for g in range(N_ATTN_GROUPS)]).astype(BF16)
    wo_groups = jnp.stack([_pad_heads(a_w_o[0][g * gw:(g + 1) * gw], 0)
                           for g in range(N_ATTN_GROUPS)]).astype(BF16)
    expand = (jnp.arange(LANES)[:, None] == jnp.arange(GROUP_WIDTH)[None, :] // HEAD_DIM).astype(BF16)
    cos_t, sgn_t = _rope_tables(s)
    qkv = _qkv(x, row(norm_mix[0]), sh_m[0], sc_m[0], w_groups, cos_t, sgn_t)
    o_list, l_list = [], []
    for g in range(N_ATTN_GROUPS):
        o, l = _attention(*qkv[3 * g:3 * g + 3])
        o_list.append(o)
        l_list.append(l)
    wr_t, br = _router_weights(r_w_group[0], r_b_group[0], r_w_expert[0], r_b_expert[0])
    x1, xs, pos, tab, counts = _attn_out(o_list, l_list, x, wo_groups, expand, gt_m[0], row(norm_ffn[0]),
                                         sh_f[0], sc_f[0], wr_t, br)
    t = b * s
    token_order = lambda p: p[:, :, 0, :].reshape(t)
    ys = _moe_sorted(xs, tab, counts, wr_t, br, e_w_gate[0].astype(BF16), e_w_up[0].astype(BF16),
                     e_w_down[0].astype(BF16))

    wr_t, br = _router_weights(r_w_group[1], r_b_group[1], r_w_expert[1], r_b_expert[1])
    x3, xs, pos, tab, counts = _gmlp(ys, token_order(pos), gt_f[0], x1, row(norm_mix[1]), sh_m[1], sc_m[1],
                                     b_w_in[0].astype(BF16), row(b_v_gain[0]), b_w_s[0].astype(BF16), b_b_s[0].T,
                                     b_w_o[0].astype(BF16), gt_m[1], row(norm_ffn[1]), sh_f[1], sc_f[1], wr_t, br)
    ys = _moe_sorted(xs, tab, counts, wr_t, br, e_w_gate[1].astype(BF16), e_w_up[1].astype(BF16),
                     e_w_down[1].astype(BF16))
    out = _final(ys, token_order(pos), x3.reshape(t, D_MODEL), gt_f[1], row(final_norm), s)
    return out.reshape(b, s, D_MODEL)
```

```python
import functools

import jax
import jax.numpy as jnp
import numpy as np
from jax import lax
from jax.experimental import pallas as pl
from jax.experimental.pallas import tpu as pltpu

D_MODEL = 1024
DEPTH = 2
HEAD_DIM = 64
HEADS_PER_GROUP = 5
N_ATTN_GROUPS = 3
DILATIONS = (1, 4, 16)
BAND_RADIUS = 64
ROT_DIM = HEAD_DIM // 4
ROPE_THETA = 500000.0
NEG_INF = -1e30
CHUNK = 128
GMLP_HALF = 2 * D_MODEL
GMLP_GROUPS = 8
N_EXPERT_GROUPS = 4
EXPERTS_PER_GROUP = 4
N_EXPERTS = 16
D_EXPERT = 256
EPS = 1e-6

LANES = 128
GROUP_WIDTH = 384
ROUTER_ROWS = 32
PAIRS_PER_GROUP = 6
N_CLASSES = N_EXPERT_GROUPS * PAIRS_PER_GROUP
CLASS_ROWS = 32
SUBLANES = 8
TOKEN_ROWS = D_MODEL // LANES
MOE_TILE = 512

F32 = jnp.float32
BF16 = jnp.bfloat16
VMEM_LIMIT = 56 * 1024 * 1024


def _cparams(sem):
    return pltpu.CompilerParams(dimension_semantics=sem, vmem_limit_bytes=VMEM_LIMIT)


def _const_spec(shape, index_map):
    return pl.BlockSpec(shape, index_map, pipeline_mode=pl.Buffered(1))


def _silu(x):
    return x * (1.0 / (1.0 + jnp.exp(-x)))


def _gelu(x):
    return 0.5 * x * (1.0 + lax.erf(x * (2.0 ** -0.5)))


def _norm_modulate(x, gain, shift, scale):
    ms = jnp.mean(x * x, axis=-1, keepdims=True)
    return (x * lax.rsqrt(ms + EPS) * gain) * (1.0 + scale) + shift


def _split_bf16(x):
    hi = x.astype(BF16)
    return hi, (x - hi.astype(F32)).astype(BF16)


def _ada_kernel(c_ref, w_ref, b_ref, o_ref):
    ca_hi, ca_lo = _split_bf16(_silu(c_ref[...]))
    w_hi, w_lo = _split_bf16(w_ref[0])
    dot = functools.partial(jnp.dot, preferred_element_type=F32)
    o_ref[0] = dot(ca_hi, w_hi) + dot(ca_lo, w_hi) + dot(ca_hi, w_lo) + b_ref[0]


def _ada_mod(c, w_ada, b_ada):
    b = c.shape[0]
    tn = 1536
    c8 = jnp.pad(c, ((0, 8 - b), (0, 0)))
    out = pl.pallas_call(
        _ada_kernel,
        grid=(DEPTH, 6 * D_MODEL // tn),
        in_specs=[pl.BlockSpec((8, D_MODEL), lambda l, j: (0, 0)),
                  pl.BlockSpec((1, D_MODEL, tn), lambda l, j: (l, 0, j)),
                  pl.BlockSpec((1, 1, tn), lambda l, j: (l, 0, j))],
        out_specs=pl.BlockSpec((1, 8, tn), lambda l, j: (l, 0, j)),
        out_shape=jax.ShapeDtypeStruct((DEPTH, 8, 6 * D_MODEL), F32),
        compiler_params=_cparams(("arbitrary", "arbitrary")),
        name="ada_mod",
    )(c8, w_ada, b_ada.reshape(DEPTH, 1, 6 * D_MODEL))
    return out[:, :b].reshape(DEPTH, b, 6, 1, D_MODEL)


def _router_logits(hf, wr_ref, br_ref):
    return lax.dot_general(wr_ref[...], hf, (((1,), (1,)), ((), ())), preferred_element_type=F32) + br_ref[...]


def _group_choice(lt):
    gl = [lt[i:i + 1, :] for i in range(N_EXPERT_GROUPS)]
    gmax = jnp.maximum(jnp.maximum(gl[0], gl[1]), jnp.maximum(gl[2], gl[3]))
    ge = [jnp.exp(g - gmax) for g in gl]
    gsum = ge[0] + ge[1] + ge[2] + ge[3]
    gp = [e / gsum for e in ge]
    g_val = jnp.maximum(jnp.maximum(gp[0], gp[1]), jnp.maximum(gp[2], gp[3]))
    g_idx = jnp.where(gp[0] == g_val, 0, jnp.where(gp[1] == g_val, 1, jnp.where(gp[2] == g_val, 2, 3)))
    return g_val, g_idx


def _ffn_prep(x1, gain, shift, scale, wr_ref, br_ref):
    hf32 = _norm_modulate(x1, gain, shift, scale)
    lt = _router_logits(hf32.astype(BF16), wr_ref, br_ref)
    _, g_idx = _group_choice(lt)
    el = []
    for j in range(EXPERTS_PER_GROUP):
        rows = [lt[4 + 4 * g + j:5 + 4 * g + j, :] for g in range(N_EXPERT_GROUPS)]
        el.append(jnp.where(g_idx == 0, rows[0], jnp.where(g_idx == 1, rows[1],
                  jnp.where(g_idx == 2, rows[2], rows[3]))))
    v1 = jnp.maximum(jnp.maximum(el[0], el[1]), jnp.maximum(el[2], el[3]))
    i1 = jnp.where(el[0] == v1, 0, jnp.where(el[1] == v1, 1, jnp.where(el[2] == v1, 2, 3)))
    el2 = [jnp.where(i1 == j, -jnp.inf, el[j]) for j in range(EXPERTS_PER_GROUP)]
    v2 = jnp.maximum(jnp.maximum(el2[0], el2[1]), jnp.maximum(el2[2], el2[3]))
    i2 = jnp.where((el2[0] == v2) & (i1 != 0), 0,
                   jnp.where((el2[1] == v2) & (i1 != 1), 1,
                             jnp.where((el2[2] == v2) & (i1 != 2), 2, 3)))
    a = jnp.minimum(i1, i2)
    bb = jnp.maximum(i1, i2)
    cls = g_idx * PAIRS_PER_GROUP + ((a * (7 - a)) >> 1) + (bb - a - 1)
    return hf32, cls


def _place_tokens(cls, carry_ref, cur_ref, tab_ref, first_step):
    tm = cls.shape[1]

    @pl.when(first_step)
    def _():
        carry_ref[...] = jnp.zeros_like(carry_ref)
        cur_ref[...] = jnp.zeros_like(cur_ref)
        tab_ref[...] = jnp.zeros_like(tab_ref)

    crow = lax.broadcasted_iota(jnp.int32, (CLASS_ROWS, tm), 0)
    onehot = (crow == cls).astype(F32)
    si = lax.broadcasted_iota(jnp.int32, (tm, tm), 0)
    ti = lax.broadcasted_iota(jnp.int32, (tm, tm), 1)
    before = (si < ti).astype(BF16)
    prefix = jnp.dot(onehot.astype(BF16), before, preferred_element_type=F32)
    seen = prefix + carry_ref[:, 0:1]
    rank = jnp.sum(onehot * seen, axis=0, keepdims=True)
    inv_tile = 1.0 / MOE_TILE
    rem = rank - MOE_TILE * jnp.floor(rank * inv_tile)
    opens = (rem == 0.0).astype(F32)
    opened_before = jnp.sum(jnp.floor((seen + (MOE_TILE - 1)) * inv_tile), axis=0, keepdims=True)
    sel = onehot * opens
    opened = jnp.sum(sel, axis=1, keepdims=True) > 0.0
    new_tile = jnp.sum(sel * opened_before, axis=1, keepdims=True)
    first_rank = jnp.sum(sel * rank, axis=1, keepdims=True)
    tile = jnp.sum(onehot * jnp.where(opened & (seen >= first_rank), new_tile, cur_ref[:, 0:1]),
                   axis=0, keepdims=True)
    carry_ref[...] = carry_ref[...] + jnp.sum(onehot, axis=1, keepdims=True)
    cur_ref[...] = jnp.where(opened, new_tile, cur_ref[...])
    lane = lax.broadcasted_iota(jnp.int32, (CLASS_ROWS, LANES), 1).astype(F32)
    cidx = lax.broadcasted_iota(jnp.int32, (CLASS_ROWS, LANES), 0).astype(F32)
    tab_ref[...] = tab_ref[...] + jnp.where(opened & (lane == new_tile), cidx, 0.0)
    return (tile * MOE_TILE + rem).astype(jnp.int32)


def _rope(t, cos, sgn, first_half):
    out = []
    for c in range(GROUP_WIDTH // LANES):
        xc = t[:, c * LANES:(c + 1) * LANES]
        other = jnp.where(first_half, pltpu.roll(xc, LANES - ROT_DIM // 2, 1),
                          pltpu.roll(xc, ROT_DIM // 2, 1))
        out.append(xc * cos + other * sgn)
    return jnp.concatenate(out, axis=1)


def _qkv_kernel(x_ref, gain_ref, sh_ref, sc_ref, w_ref, cos_ref, sgn_ref, *rest, tm):
    out_refs, hn_scr = rest[:9], rest[9]
    hn = _norm_modulate(x_ref[0], gain_ref[...], sh_ref[0, 0], sc_ref[0, 0])
    n_slab = D_MODEL // LANES
    lane = lax.broadcasted_iota(jnp.int32, (tm, LANES), 1) % HEAD_DIM
    first_half = lane < ROT_DIM // 2
    for g, d in enumerate(DILATIONS):
        rows = tm // d
        if d == 1:
            hp, cos, sgn = hn, cos_ref[...], sgn_ref[...]
        else:
            if g == 1:
                for s in range(n_slab):
                    hn_scr[s] = hn[:, s * LANES:(s + 1) * LANES]
            hp = jnp.concatenate(
                [jnp.concatenate([hn_scr[s, pl.ds(r, rows, stride=d), :] for s in range(n_slab)], axis=1)
                 for r in range(d)], axis=0)
            cos = jnp.concatenate([cos_ref[pl.ds(r, rows, stride=d), :] for r in range(d)], axis=0)
            sgn = jnp.concatenate([sgn_ref[pl.ds(r, rows, stride=d), :] for r in range(d)], axis=0)
        res = jnp.dot(hp.astype(BF16), w_ref[g], preferred_element_type=F32)
        q = _rope(res[:, :GROUP_WIDTH], cos, sgn, first_half)
        k = _rope(res[:, GROUP_WIDTH:2 * GROUP_WIDTH], cos, sgn, first_half)
        v = res[:, 2 * GROUP_WIDTH:]
        for t, o_ref in zip((q, k, v), out_refs[3 * g:3 * g + 3]):
            o_ref[0] = t.astype(BF16).reshape(d, rows, GROUP_WIDTH)


def _qkv(x, gain, shift, scale, w_groups, cos_t, sgn_t, tm=1024):
    b, s, _ = x.shape
    out_shape, out_specs = [], []
    for d in DILATIONS:
        for _ in range(3):
            out_shape.append(jax.ShapeDtypeStruct((b, d, s // d, GROUP_WIDTH), BF16))
            out_specs.append(pl.BlockSpec((1, d, tm // d, GROUP_WIDTH), lambda bi, i: (bi, 0, i, 0)))
    return pl.pallas_call(
        functools.partial(_qkv_kernel, tm=tm),
        grid=(b, s // tm),
        in_specs=[pl.BlockSpec((1, tm, D_MODEL), lambda bi, i: (bi, i, 0)),
                  _const_spec((1, D_MODEL), lambda bi, i: (0, 0)),
                  pl.BlockSpec((1, 1, 1, D_MODEL), lambda bi, i: (bi, 0, 0, 0)),
                  pl.BlockSpec((1, 1, 1, D_MODEL), lambda bi, i: (bi, 0, 0, 0)),
                  _const_spec((N_ATTN_GROUPS, D_MODEL, 3 * GROUP_WIDTH), lambda bi, i: (0, 0, 0)),
                  pl.BlockSpec((tm, LANES), lambda bi, i: (i, 0)),
                  pl.BlockSpec((tm, LANES), lambda bi, i: (i, 0))],
        out_specs=out_specs,
        out_shape=out_shape,
        scratch_shapes=[pltpu.VMEM((D_MODEL // LANES, tm, LANES), F32)],
        compiler_params=_cparams(("arbitrary", "arbitrary")),
        name="qkv_rope",
    )(x, gain, shift, scale, w_groups, cos_t, sgn_t)


def _attn_kernel(q_ref, kp_ref, kc_ref, kn_ref, vp_ref, vc_ref, vn_ref, o_ref, lse_ref,
                 kbuf, vbuf, *, seq, tq):
    i = pl.program_id(2)
    r = BAND_RADIUS
    kbuf[0:r] = kp_ref[0, 0]
    kbuf[r:r + tq] = kc_ref[0, 0]
    kbuf[r + tq:] = kn_ref[0, 0]
    vbuf[0:r] = vp_ref[0, 0]
    vbuf[r:r + tq] = vc_ref[0, 0]
    vbuf[r + tq:] = vn_ref[0, 0]
    qb = 2 * r
    lane = lax.broadcasted_iota(jnp.int32, (qb, LANES), 1)
    low = lane < HEAD_DIM
    tidx = lax.broadcasted_iota(jnp.int32, (qb, 2 * qb), 0)
    kidx = lax.broadcasted_iota(jnp.int32, (qb, 2 * qb), 1)
    rel = kidx - tidx
    band = (rel >= 0) & (rel <= 2 * r)
    for j in range(tq // qb):
        kpos = i * tq + (j * qb - r) + kidx
        mask = band & (kpos >= 0) & (kpos < seq)
        q = q_ref[0, 0, j * qb:(j + 1) * qb, :]
        k = kbuf[j * qb:(j + 2) * qb, :]
        v = vbuf[j * qb:(j + 2) * qb, :]
        lse_tile = jnp.zeros((qb, LANES), F32)
        o_chunks = []
        for c in range(GROUP_WIDTH // LANES):
            qc = q[:, c * LANES:(c + 1) * LANES]
            kc = k[:, c * LANES:(c + 1) * LANES]
            vc = v[:, c * LANES:(c + 1) * LANES]
            outs = []
            for hh in range(2):
                head = 2 * c + hh
                if head >= HEADS_PER_GROUP:
                    outs.append(jnp.zeros((qb, LANES), F32))
                    continue
                qm = jnp.where(low if hh == 0 else jnp.logical_not(low), qc, jnp.zeros_like(qc))
                s = lax.dot_general(qm, kc, (((1,), (1,)), ((), ())), preferred_element_type=F32)
                s = jnp.where(mask, s, NEG_INF)
                m = jnp.max(s, axis=-1, keepdims=True)
                p = jnp.exp(s - m)
                den = jnp.sum(p, axis=-1, keepdims=True)
                o = jnp.dot(p.astype(BF16), vc, preferred_element_type=F32) / den
                lse_tile = jnp.where(lane == head, m + jnp.log(den), lse_tile)
                outs.append(o)
            o_chunks.append(jnp.where(low, outs[0], outs[1]))
        o_ref[0, 0, j * qb:(j + 1) * qb, :] = jnp.concatenate(o_chunks, axis=1).astype(BF16)
        lse_ref[0, 0, j * qb:(j + 1) * qb, :] = lse_tile


def _attention(q, k, v, tq=1024):
    b, d, seq, w = q.shape
    r = BAND_RADIUS
    nb = seq // r
    cur = pl.BlockSpec((1, 1, tq, w), lambda bi, ri, i: (bi, ri, i, 0))
    prev = pl.BlockSpec((1, 1, r, w), lambda bi, ri, i: (bi, ri, jnp.maximum(i * (tq // r) - 1, 0), 0))
    nxt = pl.BlockSpec((1, 1, r, w), lambda bi, ri, i: (bi, ri, jnp.minimum((i + 1) * (tq // r), nb - 1), 0))
    return pl.pallas_call(
        functools.partial(_attn_kernel, seq=seq, tq=tq),
        grid=(b, d, seq // tq),
        in_specs=[cur, prev, cur, nxt, prev, cur, nxt],
        out_specs=[pl.BlockSpec((1, 1, tq, w), lambda bi, ri, i: (bi, ri, i, 0)),
                   pl.BlockSpec((1, 1, tq, LANES), lambda bi, ri, i: (bi, ri, i, 0))],
        out_shape=[jax.ShapeDtypeStruct((b, d, seq, w), BF16),
                   jax.ShapeDtypeStruct((b, d, seq, LANES), F32)],
        scratch_shapes=[pltpu.VMEM((tq + 2 * r, w), BF16), pltpu.VMEM((tq + 2 * r, w), BF16)],
        compiler_params=_cparams(("arbitrary", "arbitrary", "arbitrary")),
        name=f"band_attn_d{d}",
    )(q, k, k, k, v, v, v)


def _attn_out_kernel(o0, o1, o2, l0, l1, l2, x_ref, wo_ref, exp_ref, gt_ref, gain_ref, sh_ref, sc_ref,
                     wr_ref, br_ref, x1_ref, xs_hbm, pos_ref, tab_out, cnt_out, oscr, lscr, *route_scratch,
                     tm, n_steps):
    n_slab = GROUP_WIDTH // LANES
    for g, (d, o_ref, l_ref) in enumerate(zip(DILATIONS, (o0, o1, o2), (l0, l1, l2))):
        rows = tm // d
        for r in range(d):
            dst = slice(None) if d == 1 else pl.ds(r, rows, stride=d)
            blk = o_ref[0, r].astype(F32)
            for c in range(n_slab):
                oscr[g * n_slab + c, dst, :] = blk[:, c * LANES:(c + 1) * LANES]
            lscr[g, dst, :] = l_ref[0, r]
    lse = [lscr[g] for g in range(N_ATTN_GROUPS)]
    mx = jnp.maximum(jnp.maximum(lse[0], lse[1]), lse[2])
    ex = [jnp.exp(l - mx) for l in lse]
    tot = ex[0] + ex[1] + ex[2]
    y = jnp.zeros((tm, D_MODEL), F32)
    for g in range(N_ATTN_GROUPS):
        alpha = ex[g] / tot
        a_hi, a_lo = _split_bf16(alpha)
        a_wide = (jnp.dot(a_hi, exp_ref[...], preferred_element_type=F32)
                  + jnp.dot(a_lo, exp_ref[...], preferred_element_type=F32))
        og = jnp.concatenate([oscr[g * n_slab + c] for c in range(n_slab)], axis=1)
        y = y + jnp.dot((og * a_wide).astype(BF16), wo_ref[g], preferred_element_type=F32)
    x1 = x_ref[0] + gt_ref[0, 0] * y
    x1_ref[0] = x1
    _route_outputs(x1, gain_ref, sh_ref, sc_ref, wr_ref, br_ref, xs_hbm, pos_ref, tab_out, cnt_out, route_scratch,
                   n_steps)


ZERO_TOKENS = MOE_TILE // 2


def _route_outputs(x1, gain_ref, sh_ref, sc_ref, wr_ref, br_ref, xs_hbm, pos_ref, tab_out, cnt_out, scratch,
                   n_steps):
    (carry_ref, cur_ref, tab_ref, rows_scr, pos_vmem, pos_smem, fin_vmem, fin_smem, zbuf, sems, aux_sem) = scratch
    step = pl.program_id(0) * pl.num_programs(1) + pl.program_id(1)
    tm = x1.shape[0]
    rows = tm * TOKEN_ROWS
    hf32, cls = _ffn_prep(x1, gain_ref[...], sh_ref[0, 0], sc_ref[0, 0], wr_ref, br_ref)
    pos = _place_tokens(cls, carry_ref, cur_ref, tab_ref, step == 0)
    pos_lines = _lines(pos)
    pos_ref[0, 0] = pos_lines
    pos_vmem[...] = pos_lines
    to_smem = pltpu.make_async_copy(pos_vmem, pos_smem, aux_sem)
    to_smem.start()

    def drain(slot):
        pltpu.make_async_copy(rows_scr.at[pl.ds(0, rows), :], xs_hbm.at[pl.ds(0, rows), :], sems.at[slot]).wait()

    slot = step % 2

    @pl.when(step >= 2)
    def _():
        drain(slot)

    base = pl.multiple_of(slot * rows, rows)
    for s in range(TOKEN_ROWS):
        rows_scr[pl.ds(base + s, tm, stride=TOKEN_ROWS), :] = hf32[:, s * LANES:(s + 1) * LANES]
    to_smem.wait()

    def issue(line, lane, priority):
        _token_copy(rows_scr, slot * tm + line * LANES + lane, xs_hbm, pos_smem[line, lane],
                    sems.at[slot]).start(priority=priority)

    _issue_burst(tm, issue, spread=False)
    tab_out[...] = tab_ref[...]
    cnt_out[...] = carry_ref[...]

    @pl.when(step == n_steps - 1)
    def _():
        drain(slot)
        if n_steps > 1:
            drain(1 - slot)
        _clear_unused_rows(carry_ref, cur_ref, xs_hbm, fin_vmem, fin_smem, zbuf, aux_sem)


def _clear_unused_rows(carry_ref, cur_ref, xs_hbm, fin_vmem, fin_smem, zbuf, sem):
    lane = lax.broadcasted_iota(jnp.int32, (CLASS_ROWS, LANES), 1)
    crow = lax.broadcasted_iota(jnp.int32, (CLASS_ROWS, LANES), 0)
    on_lane = lambda col: jnp.sum(jnp.where(lane == crow, col, 0.0), axis=0, keepdims=True)
    counts = carry_ref[...]
    used = jnp.sum(jnp.floor((counts + (MOE_TILE - 1)) * (1.0 / MOE_TILE)), axis=0, keepdims=True)
    r8 = lax.broadcasted_iota(jnp.int32, (SUBLANES, LANES), 0)
    fin = jnp.where(r8 == 0, on_lane(cur_ref[...]), jnp.where(r8 == 1, on_lane(counts), jnp.where(r8 == 2, used, 0.0)))
    fin_vmem[...] = fin.astype(jnp.int32)
    to_smem = pltpu.make_async_copy(fin_vmem, fin_smem, sem)
    to_smem.start()
    zbuf[...] = jnp.zeros_like(zbuf)
    to_smem.wait()

    def zero_copy(first_tok, n_tok):
        d0 = pl.multiple_of(first_tok * TOKEN_ROWS, TOKEN_ROWS)
        return pltpu.make_async_copy(zbuf.at[pl.ds(0, n_tok * TOKEN_ROWS), :],
                                     xs_hbm.at[pl.ds(d0, n_tok * TOKEN_ROWS), :], sem)

    for wait in (False, True):
        for c in range(N_CLASSES):
            count = fin_smem[1, c]
            fill = jnp.where(count > 0, (MOE_TILE - count % MOE_TILE) % MOE_TILE, 0)
            first = fin_smem[0, c] * MOE_TILE + (MOE_TILE - fill)
            piece = ZERO_TOKENS
            while piece >= 1:
                done = (fill // (2 * piece)) * (2 * piece)

                @pl.when((fill // piece) % 2 == 1)
                def _(first=first, done=done, piece=piece):
                    cp = zero_copy(first + done, piece)
                    cp.wait() if wait else cp.start()

                piece //= 2

    def clear_tile(j, carry):
        for half in range(MOE_TILE // ZERO_TOKENS):
            cp = zero_copy(j * MOE_TILE + half * ZERO_TOKENS, ZERO_TOKENS)
            cp.start()
            cp.wait()
        return carry

    lax.fori_loop(fin_smem[2, 0], xs_hbm.shape[0] // (MOE_TILE * TOKEN_ROWS), clear_tile, 0)


def _route_scratch(tm):
    return [pltpu.VMEM((CLASS_ROWS, LANES), F32), pltpu.VMEM((CLASS_ROWS, LANES), F32),
            pltpu.VMEM((CLASS_ROWS, LANES), F32),
            pltpu.VMEM((2 * tm * TOKEN_ROWS, LANES), F32),
            pltpu.VMEM((SUBLANES, LANES), jnp.int32), pltpu.SMEM((SUBLANES, LANES), jnp.int32),
            pltpu.VMEM((SUBLANES, LANES), jnp.int32), pltpu.SMEM((SUBLANES, LANES), jnp.int32),
            pltpu.VMEM((ZERO_TOKENS * TOKEN_ROWS, LANES), F32),
            pltpu.SemaphoreType.DMA((2,)), pltpu.SemaphoreType.DMA]


N_ROUTE_SCRATCH = 11


def _route_out_specs(b, s, tm):
    n_tiles = b * s // MOE_TILE + N_CLASSES
    specs = [pl.BlockSpec((1, tm, D_MODEL), lambda bi, i: (bi, i, 0)),
             pl.BlockSpec(memory_space=pl.ANY),
             pl.BlockSpec((1, 1, SUBLANES, LANES), lambda bi, i: (bi, i, 0, 0)),
             pl.BlockSpec((CLASS_ROWS, LANES), lambda bi, i: (0, 0)),
             pl.BlockSpec((CLASS_ROWS, LANES), lambda bi, i: (0, 0))]
    shapes = [jax.ShapeDtypeStruct((b, s, D_MODEL), F32),
              jax.ShapeDtypeStruct((n_tiles * MOE_TILE * TOKEN_ROWS, LANES), F32),
              jax.ShapeDtypeStruct((b, s // tm, SUBLANES, LANES), jnp.int32),
              jax.ShapeDtypeStruct((CLASS_ROWS, LANES), F32),
              jax.ShapeDtypeStruct((CLASS_ROWS, LANES), F32)]
    return specs, shapes


def _mod_spec():
    return pl.BlockSpec((1, 1, 1, D_MODEL), lambda bi, i: (bi, 0, 0, 0))


def _attn_out(o_list, l_list, x, wo_groups, expand, gt, gain, shift, scale, wr_t, br, tm=512):
    b, s, _ = x.shape
    o_specs = [pl.BlockSpec((1, d, tm // d, GROUP_WIDTH), lambda bi, i: (bi, 0, i, 0)) for d in DILATIONS]
    l_specs = [pl.BlockSpec((1, d, tm // d, LANES), lambda bi, i: (bi, 0, i, 0)) for d in DILATIONS]
    out_specs, out_shape = _route_out_specs(b, s, tm)
    return pl.pallas_call(
        functools.partial(_attn_out_kernel, tm=tm, n_steps=b * s // tm),
        grid=(b, s // tm),
        in_specs=o_specs + l_specs + [
            pl.BlockSpec((1, tm, D_MODEL), lambda bi, i: (bi, i, 0)),
            _const_spec((N_ATTN_GROUPS, GROUP_WIDTH, D_MODEL), lambda bi, i: (0, 0, 0)),
            _const_spec((LANES, GROUP_WIDTH), lambda bi, i: (0, 0)),
            _mod_spec(),
            _const_spec((1, D_MODEL), lambda bi, i: (0, 0)),
            _mod_spec(), _mod_spec(),
            _const_spec((ROUTER_ROWS, D_MODEL), lambda bi, i: (0, 0)),
            _const_spec((ROUTER_ROWS, 1), lambda bi, i: (0, 0))],
        out_specs=out_specs,
        out_shape=out_shape,
        scratch_shapes=[pltpu.VMEM((N_ATTN_GROUPS * GROUP_WIDTH // LANES, tm, LANES), F32),
                        pltpu.VMEM((N_ATTN_GROUPS, tm, LANES), F32)] + _route_scratch(tm),
        compiler_params=_cparams(("arbitrary", "arbitrary")),
        name="attn_merge_proj",
    )(*o_list, *l_list, x, wo_groups, expand, gt, gain, shift, scale, wr_t, br)


DMA_BURST_UNROLL = 16


BACKGROUND_PRIORITY = 1


def _issue_burst(n, issue, spread=True):
    for line in range(n // LANES):
        def body(i, carry, line=line):
            for k in range(DMA_BURST_UNROLL):
                issue(line, i * DMA_BURST_UNROLL + k, k % 2 if spread else BACKGROUND_PRIORITY)
            return carry

        lax.fori_loop(0, LANES // DMA_BURST_UNROLL, body, 0)


def _lines(row):
    n = row.shape[1]
    line = lax.broadcasted_iota(jnp.int32, (SUBLANES, LANES), 0)
    out = jnp.zeros((SUBLANES, LANES), jnp.int32)
    for j in range(n // LANES):
        out = jnp.where(line == j, row[:, j * LANES:(j + 1) * LANES], out)
    return out


class _RowGather:
    def __init__(self, pos_ref, pos_next_ref, ys_hbm, ybuf, sems, step, n_steps, tm, spread):
        self.pos_ref, self.pos_next_ref, self.ys_hbm, self.ybuf, self.sems = pos_ref, pos_next_ref, ys_hbm, ybuf, sems
        self.step, self.n_steps, self.tm, self.spread = step, n_steps, tm, spread

    def _request(self, p_ref, slot):
        def issue(line, lane, priority):
            _token_copy(self.ys_hbm, p_ref[0, line, lane], self.ybuf, slot * self.tm + line * LANES + lane,
                        self.sems.at[slot]).start(priority=priority)

        _issue_burst(self.tm, issue, self.spread)

    def request_next(self):
        @pl.when(self.step + 1 < self.n_steps)
        def _():
            self._request(self.pos_next_ref, (self.step + 1) % 2)

    def current(self):
        rows = self.tm * TOKEN_ROWS

        @pl.when(self.step == 0)
        def _():
            self._request(self.pos_ref, 0)

        slot = self.step % 2
        base = pl.multiple_of(slot * rows, rows)
        pltpu.make_async_copy(self.ys_hbm.at[pl.ds(0, rows), :], self.ybuf.at[pl.ds(base, rows), :],
                              self.sems.at[slot]).wait()
        return jnp.concatenate([self.ybuf[pl.ds(base + s, self.tm, stride=TOKEN_ROWS), :]
                                for s in range(D_MODEL // LANES)], axis=1)


def _gather_specs(n_steps, tm, index_of):
    return [pl.BlockSpec((1, SUBLANES, LANES), lambda *g: (index_of(*g), 0, 0), memory_space=pltpu.SMEM),
            pl.BlockSpec((1, SUBLANES, LANES), lambda *g: (jnp.minimum(index_of(*g) + 1, n_steps - 1), 0, 0),
                         memory_space=pltpu.SMEM),
            pl.BlockSpec(memory_space=pl.ANY)]


def _pos_lines(pos, tm):
    lines = pos.reshape(-1, tm // LANES, LANES)
    return jnp.pad(lines, ((0, 0), (0, SUBLANES - tm // LANES), (0, 0)))


def _gather_scratch(tm):
    return [pltpu.VMEM((2 * tm * TOKEN_ROWS, LANES), F32), pltpu.SemaphoreType.DMA((2,))]


def _gmlp_kernel(pos_ref, pos_next_ref, ys_hbm, gtf_ref, x_ref, gain_ref, sh_ref, sc_ref, win_ref, vg_ref, ws_ref,
                 bs_ref, wo_ref, gt_ref, fgain_ref, fsh_ref, fsc_ref, wr_ref, br_ref, x1_ref, xs_hbm, pos_out, tab_out,
                 cnt_out, gated_scr, ybuf, sems, *route_scratch, tm, n_steps):
    step = pl.program_id(0) * pl.num_programs(1) + pl.program_id(1)
    gather = _RowGather(pos_ref, pos_next_ref, ys_hbm, ybuf, sems, step, n_steps, tm, spread=False)
    gather.request_next()
    x = x_ref[0] + gtf_ref[0, 0] * gather.current()
    hn = _norm_modulate(x, gain_ref[...], sh_ref[0, 0], sc_ref[0, 0]).astype(BF16)
    v = _gelu(jnp.dot(hn, win_ref[:, GMLP_HALF:], preferred_element_type=F32))
    u = _gelu(jnp.dot(hn, win_ref[:, :GMLP_HALF], preferred_element_type=F32))
    mu = jnp.mean(v, axis=-1, keepdims=True)
    vc = v - mu
    vn = (vc * lax.rsqrt(jnp.mean(vc * vc, axis=-1, keepdims=True) + EPS) * vg_ref[...]).astype(BF16)
    gw = GMLP_HALF // GMLP_GROUPS
    for c in range(tm // CHUNK):
        rs = slice(c * CHUNK, (c + 1) * CHUNK)
        for g in range(GMLP_GROUPS):
            cs = slice(g * gw, (g + 1) * gw)
            vs = jnp.dot(ws_ref[g], vn[rs, cs], preferred_element_type=F32) + bs_ref[:, g:g + 1]
            gated_scr[rs, cs] = (u[rs, cs] * vs).astype(BF16)
    y = jnp.dot(gated_scr[...], wo_ref[...], preferred_element_type=F32)
    x1 = x + gt_ref[0, 0] * y
    x1_ref[0] = x1
    _route_outputs(x1, fgain_ref, fsh_ref, fsc_ref, wr_ref, br_ref, xs_hbm, pos_out, tab_out, cnt_out,
                   route_scratch, n_steps)


def _gmlp(ys, pos, gt_prev, x, gain, shift, scale, w_in, v_gain, w_s, b_s_t, w_o, gt, fgain, fshift, fscale,
          wr_t, br, tm=512):
    b, s, _ = x.shape
    tok = lambda w: pl.BlockSpec((1, tm, w), lambda bi, i: (bi, i, 0))
    c2 = lambda shape: _const_spec(shape, lambda bi, i: (0,) * len(shape))
    out_specs, out_shape = _route_out_specs(b, s, tm)
    n_steps = b * s // tm
    return pl.pallas_call(
        functools.partial(_gmlp_kernel, tm=tm, n_steps=n_steps),
        grid=(b, s // tm),
        in_specs=_gather_specs(n_steps, tm, lambda bi, i: bi * (s // tm) + i) + [
                  _mod_spec(),
                  tok(D_MODEL), c2((1, D_MODEL)), _mod_spec(), _mod_spec(),
                  c2((D_MODEL, 2 * GMLP_HALF)), c2((1, GMLP_HALF)),
                  c2((GMLP_GROUPS, CHUNK, CHUNK)), c2((CHUNK, GMLP_GROUPS)),
                  c2((GMLP_HALF, D_MODEL)), _mod_spec(),
                  c2((1, D_MODEL)), _mod_spec(), _mod_spec(),
                  c2((ROUTER_ROWS, D_MODEL)), c2((ROUTER_ROWS, 1))],
        out_specs=out_specs,
        out_shape=out_shape,
        scratch_shapes=[pltpu.VMEM((tm, GMLP_HALF), BF16)] + _gather_scratch(tm) + _route_scratch(tm),
        compiler_params=_cparams(("arbitrary", "arbitrary")),
        name="gmlp",
    )(_pos_lines(pos, tm), _pos_lines(pos, tm), ys, gt_prev,
      x, gain, shift, scale, w_in, v_gain, w_s, b_s_t, w_o, gt, fgain, fshift, fscale, wr_t, br)


def _token_copy(src, src_tok, dst, dst_tok, sem):
    s0 = pl.multiple_of(src_tok * TOKEN_ROWS, TOKEN_ROWS)
    d0 = pl.multiple_of(dst_tok * TOKEN_ROWS, TOKEN_ROWS)
    return pltpu.make_async_copy(src.at[pl.ds(s0, TOKEN_ROWS), :], dst.at[pl.ds(d0, TOKEN_ROWS), :], sem)


def _slabs(ref, n_tok, n_slab):
    return jnp.concatenate([ref[pl.ds(s, n_tok, stride=TOKEN_ROWS), :] for s in range(n_slab)], axis=1)


def _expert_kernel(ea_ref, eb_ref, blk_ref, nused_ref, x_ref, wr_ref, br_ref, wga, wgb, wua, wub, wda, wdb, y_ref,
                   lt_scr, wg_scr, wu_scr, wd_scr):
    j = pl.program_id(0)
    prev = jnp.maximum(j - 1, 0)
    new_pair = (j == 0) | (ea_ref[j] != ea_ref[prev]) | (eb_ref[j] != eb_ref[prev])

    @pl.when(new_pair & (j < nused_ref[0]))
    def _():
        for slot, (wg, wu, wd) in enumerate(((wga, wua, wda), (wgb, wub, wdb))):
            wg_scr[slot] = wg[0, 0].astype(BF16)
            wu_scr[slot] = wu[0, 0].astype(BF16)
            wd_scr[slot] = wd[0, 0].astype(BF16)

    @pl.when(j < nused_ref[0])
    def _():
        h = _slabs(x_ref, MOE_TILE, TOKEN_ROWS).astype(BF16)
        lt = _router_logits(h, wr_ref, br_ref)
        g_val, _ = _group_choice(lt)
        lt_scr[...] = lt
        el_a = lt_scr[pl.ds(N_EXPERT_GROUPS + ea_ref[j], 1), :]
        el_b = lt_scr[pl.ds(N_EXPERT_GROUPS + eb_ref[j], 1), :]
        top = jnp.maximum(el_a, el_b)
        p_a = jnp.exp(el_a - top)
        p_b = jnp.exp(el_b - top)
        row = lax.broadcasted_iota(jnp.int32, (LANES, MOE_TILE), 0)
        gates_t = jnp.where(row == 0, g_val * (p_a / (p_a + p_b)), jnp.where(row == 1, g_val * (p_b / (p_a + p_b)), 0.0))
        gates = gates_t.T
        y = jnp.zeros((MOE_TILE, D_MODEL), F32)
        for slot in range(2):
            hg = jnp.dot(h, wg_scr[slot], preferred_element_type=F32)
            hu = jnp.dot(h, wu_scr[slot], preferred_element_type=F32)
            act = (_silu(hg) * hu * gates[:, slot:slot + 1]).astype(BF16)
            y = y + jnp.dot(act, wd_scr[slot], preferred_element_type=F32)
        for s in range(TOKEN_ROWS):
            y_ref[pl.ds(s, MOE_TILE, stride=TOKEN_ROWS), :] = y[:, s * LANES:(s + 1) * LANES]

    @pl.when(j >= nused_ref[0])
    def _():
        y_ref[...] = jnp.zeros_like(y_ref)


def _experts(xs, tile_ea, tile_eb, tile_blk, n_used, wr_t, br, wg, wu, wd, layer):
    n_tiles = xs.shape[0] // (MOE_TILE * TOKEN_ROWS)
    up = lambda sel: pl.BlockSpec((1, 1, D_MODEL, D_EXPERT),
                                  lambda j, ea, eb, blk, nu: (layer, (ea, eb)[sel][j], 0, 0))
    down = lambda sel: pl.BlockSpec((1, 1, D_EXPERT, D_MODEL),
                                    lambda j, ea, eb, blk, nu: (layer, (ea, eb)[sel][j], 0, 0))
    shape = (MOE_TILE * TOKEN_ROWS, LANES)
    return pl.pallas_call(
        _expert_kernel,
        grid_spec=pltpu.PrefetchScalarGridSpec(
            num_scalar_prefetch=4,
            grid=(n_tiles,),
            in_specs=[pl.BlockSpec(shape, lambda j, ea, eb, blk, nu: (blk[j], 0)),
                      pl.BlockSpec((ROUTER_ROWS, D_MODEL), lambda j, ea, eb, blk, nu: (0, 0)),
                      pl.BlockSpec((ROUTER_ROWS, 1), lambda j, ea, eb, blk, nu: (0, 0)),
                      up(0), up(1), up(0), up(1), down(0), down(1)],
            out_specs=pl.BlockSpec(shape, lambda j, ea, eb, blk, nu: (j, 0)),
            scratch_shapes=[pltpu.VMEM((ROUTER_ROWS, MOE_TILE), F32),
                            pltpu.VMEM((2, D_MODEL, D_EXPERT), BF16), pltpu.VMEM((2, D_MODEL, D_EXPERT), BF16),
                            pltpu.VMEM((2, D_EXPERT, D_MODEL), BF16)]),
        out_shape=jax.ShapeDtypeStruct(xs.shape, F32),
        compiler_params=_cparams(("arbitrary",)),
        name="moe_experts",
    )(tile_ea, tile_eb, tile_blk, n_used, xs, wr_t, br, wg, wg, wu, wu, wd, wd)


def _final_kernel(pos_ref, pos_next_ref, ys_hbm, x_ref, gt_ref, fin_ref, o_ref, ybuf, sems, *, tm):
    gather = _RowGather(pos_ref, pos_next_ref, ys_hbm, ybuf, sems, pl.program_id(0), pl.num_programs(0), tm,
                        spread=True)
    gather.request_next()
    y_moe = gather.current()
    x2 = x_ref[...] + gt_ref[0, 0] * y_moe
    ms = jnp.mean(x2 * x2, axis=-1, keepdims=True)
    o_ref[...] = x2 * lax.rsqrt(ms + EPS) * fin_ref[...]


def _final(ys, pos, x, gt, fin_gain, seq, tm=512):
    t = x.shape[0]
    tiles_per_batch = seq // tm
    n_steps = t // tm
    return pl.pallas_call(
        functools.partial(_final_kernel, tm=tm),
        grid=(n_steps,),
        in_specs=_gather_specs(n_steps, tm, lambda i: i) + [
                  pl.BlockSpec((tm, D_MODEL), lambda i: (i, 0)),
                  pl.BlockSpec((1, 1, 1, D_MODEL), lambda i: (i // tiles_per_batch, 0, 0, 0)),
                  pl.BlockSpec((1, D_MODEL), lambda i: (0, 0))],
        out_specs=pl.BlockSpec((tm, D_MODEL), lambda i: (i, 0)),
        out_shape=jax.ShapeDtypeStruct((t, D_MODEL), F32),
        scratch_shapes=_gather_scratch(tm),
        compiler_params=_cparams(("arbitrary",)),
        name="moe_combine_final_norm",
    )(_pos_lines(pos, tm), _pos_lines(pos, tm), ys, x, gt, fin_gain)


_PAIR_A = np.array([0, 0, 0, 1, 1, 2], np.int32)
_PAIR_B = np.array([1, 2, 3, 2, 3, 3], np.int32)


def _moe_sorted(xs, tab, counts, wr_t, br, wg, wu, wd, layer):
    n_tiles = xs.shape[0] // (MOE_TILE * TOKEN_ROWS)
    cnt = counts[:N_CLASSES, 0].astype(jnp.int32)
    n_used = jnp.sum((cnt + MOE_TILE - 1) // MOE_TILE)
    ids = jnp.arange(n_tiles, dtype=jnp.int32)
    tile_cls = jnp.sum(tab, axis=0)[:n_tiles].astype(jnp.int32)
    last_cls = jnp.sum(jnp.where(ids == n_used - 1, tile_cls, 0))
    tile_cls = jnp.where(ids < n_used, tile_cls, last_cls)
    j = jnp.minimum(ids, n_used - 1)
    pair = tile_cls % PAIRS_PER_GROUP
    pair_a = jnp.sum(jnp.where(pair[:, None] == jnp.arange(PAIRS_PER_GROUP)[None, :], _PAIR_A[None, :], 0), axis=1)
    pair_b = jnp.sum(jnp.where(pair[:, None] == jnp.arange(PAIRS_PER_GROUP)[None, :], _PAIR_B[None, :], 0), axis=1)
    base = (tile_cls // PAIRS_PER_GROUP) * EXPERTS_PER_GROUP
    return _experts(xs, base + pair_a, base + pair_b, j, n_used.reshape(1), wr_t, br, wg, wu, wd, layer)


def _rope_tables(seq):
    inv_freq = ROPE_THETA ** (-jnp.arange(0, ROT_DIM, 2, dtype=jnp.float32) / ROT_DIM)
    ang = jnp.arange(seq).astype(jnp.float32)[:, None] * inv_freq[None, :]
    cos, sin = jnp.cos(ang), jnp.sin(ang)
    ones = jnp.ones((seq, HEAD_DIM - ROT_DIM), jnp.float32)
    cos_h = jnp.concatenate([cos, cos, ones], axis=1)
    sgn_h = jnp.concatenate([-sin, sin, 0.0 * ones], axis=1)
    reps = LANES // HEAD_DIM
    return jnp.tile(cos_h, (1, reps)).astype(F32), jnp.tile(sgn_h, (1, reps)).astype(F32)


def _pad_heads(w, axis):
    pad = [(0, 0)] * w.ndim
    pad[axis] = (0, GROUP_WIDTH - HEADS_PER_GROUP * HEAD_DIM)
    return jnp.pad(w, pad)


def _router_weights(w_group, b_group, w_expert, b_expert):
    we = jnp.transpose(w_expert, (1, 0, 2)).reshape(D_MODEL, N_EXPERTS)
    w = jnp.concatenate([w_group, we], axis=1)
    w = jnp.pad(w, ((0, 0), (0, ROUTER_ROWS - w.shape[1])))
    bias = jnp.pad(jnp.concatenate([b_group, b_expert.reshape(-1)]), (0, ROUTER_ROWS - 20))
    return w.T.astype(BF16), bias.reshape(ROUTER_ROWS, 1)


def kernel(x, c, norm_mix, norm_ffn, w_ada, b_ada, a_w_qkv, a_w_o, b_w_in, b_v_gain, b_w_s, b_b_s, b_w_o,
           r_w_group, r_b_group, r_w_expert, r_b_expert, e_w_gate, e_w_up, e_w_down, final_norm):
    b, s, _ = x.shape
    mod = _ada_mod(c, w_ada, b_ada)
    sh_m, sc_m, gt_m, sh_f, sc_f, gt_f = [mod[:, :, i:i + 1] for i in range(6)]
    row = lambda v: v.reshape(1, -1)
    gw = HEADS_PER_GROUP * HEAD_DIM

    w = a_w_qkv[0].reshape(D_MODEL, 3, N_ATTN_GROUPS, gw)
    w = w * jnp.asarray([HEAD_DIM ** -0.5, 1.0, 1.0], F32)[None, :, None, None]
    w_groups = jnp.transpose(_pad_heads(w, 3), (2, 0, 1, 3)).reshape(
        N_ATTN_GROUPS, D_MODEL, 3 * GROUP_WIDTH).astype(BF16)
    wo_groups = _pad_heads(a_w_o[0].reshape(N_ATTN_GROUPS, gw, D_MODEL), 1).astype(BF16)
    expand = (jnp.arange(LANES)[:, None] == jnp.arange(GROUP_WIDTH)[None, :] // HEAD_DIM).astype(BF16)
    cos_t, sgn_t = _rope_tables(s)
    qkv = _qkv(x, row(norm_mix[0]), sh_m[0], sc_m[0], w_groups, cos_t, sgn_t)
    o_list, l_list = [], []
    for g in range(N_ATTN_GROUPS):
        o, l = _attention(*qkv[3 * g:3 * g + 3])
        o_list.append(o)
        l_list.append(l)
    wr_t, br = _router_weights(r_w_group[0], r_b_group[0], r_w_expert[0], r_b_expert[0])
    x1, xs, pos, tab, counts = _attn_out(o_list, l_list, x, wo_groups, expand, gt_m[0], row(norm_ffn[0]),
                                         sh_f[0], sc_f[0], wr_t, br)
    t = b * s
    token_order = lambda p: p[:, :, :t // (p.shape[0] * p.shape[1] * LANES), :].reshape(t)
    ys = _moe_sorted(xs, tab, counts, wr_t, br, e_w_gate, e_w_up, e_w_down, 0)

    wr_t, br = _router_weights(r_w_group[1], r_b_group[1], r_w_expert[1], r_b_expert[1])
    x3, xs, pos, tab, counts = _gmlp(ys, token_order(pos), gt_f[0], x1, row(norm_mix[1]), sh_m[1], sc_m[1],
                                     b_w_in[0].astype(BF16), row(b_v_gain[0]), b_w_s[0].astype(BF16), b_b_s[0].T,
                                     b_w_o[0].astype(BF16), gt_m[1], row(norm_ffn[1]), sh_f[1], sc_f[1], wr_t, br)
    ys = _moe_sorted(xs, tab, counts, wr_t, br, e_w_gate, e_w_up, e_w_down, 1)
    out = _final(ys, token_order(pos), x3.reshape(t, D_MODEL), gt_f[1], row(final_norm), s)
    return out.reshape(b, s, D_MODEL)
```

```python
import functools

import jax
import jax.numpy as jnp
import numpy as np
from jax import lax
from jax.experimental import pallas as pl
from jax.experimental.pallas import tpu as pltpu

D_MODEL = 1024
DEPTH = 2
HEAD_DIM = 64
HEADS_PER_GROUP = 5
N_ATTN_GROUPS = 3
DILATIONS = (1, 4, 16)
BAND_RADIUS = 64
ROT_DIM = HEAD_DIM // 4
ROPE_THETA = 500000.0
NEG_INF = -1e30
CHUNK = 128
GMLP_HALF = 2 * D_MODEL
GMLP_GROUPS = 8
N_EXPERT_GROUPS = 4
EXPERTS_PER_GROUP = 4
N_EXPERTS = 16
D_EXPERT = 256
EPS = 1e-6

LANES = 128
GROUP_WIDTH = 384
ROUTER_ROWS = 32
PAIRS_PER_GROUP = 6
N_CLASSES = N_EXPERT_GROUPS * PAIRS_PER_GROUP
CLASS_ROWS = 32
SUBLANES = 8
TOKEN_ROWS = D_MODEL // LANES
MOE_TILE = 512

F32 = jnp.float32
BF16 = jnp.bfloat16
VMEM_LIMIT = 56 * 1024 * 1024


def _cparams(sem):
    return pltpu.CompilerParams(dimension_semantics=sem, vmem_limit_bytes=VMEM_LIMIT)


def _const_spec(shape, index_map):
    return pl.BlockSpec(shape, index_map, pipeline_mode=pl.Buffered(1))


def _silu(x):
    return x * (1.0 / (1.0 + jnp.exp(-x)))


def _gelu(x):
    return 0.5 * x * (1.0 + lax.erf(x * (2.0 ** -0.5)))


def _norm_modulate(x, gain, shift, scale):
    ms = jnp.mean(x * x, axis=-1, keepdims=True)
    return (x * lax.rsqrt(ms + EPS) * gain) * (1.0 + scale) + shift


def _split_bf16(x):
    hi = x.astype(BF16)
    return hi, (x - hi.astype(F32)).astype(BF16)


def _ada_kernel(c_ref, w_ref, b_ref, o_ref):
    ca_hi, ca_lo = _split_bf16(_silu(c_ref[...]))
    w_hi, w_lo = _split_bf16(w_ref[0])
    dot = functools.partial(jnp.dot, preferred_element_type=F32)
    o_ref[0] = dot(ca_hi, w_hi) + dot(ca_lo, w_hi) + dot(ca_hi, w_lo) + b_ref[0]


def _ada_mod(c, w_ada, b_ada):
    b = c.shape[0]
    tn = 1536
    c8 = jnp.pad(c, ((0, 8 - b), (0, 0)))
    out = pl.pallas_call(
        _ada_kernel,
        grid=(DEPTH, 6 * D_MODEL // tn),
        in_specs=[pl.BlockSpec((8, D_MODEL), lambda l, j: (0, 0)),
                  pl.BlockSpec((1, D_MODEL, tn), lambda l, j: (l, 0, j)),
                  pl.BlockSpec((1, 1, tn), lambda l, j: (l, 0, j))],
        out_specs=pl.BlockSpec((1, 8, tn), lambda l, j: (l, 0, j)),
        out_shape=jax.ShapeDtypeStruct((DEPTH, 8, 6 * D_MODEL), F32),
        compiler_params=_cparams(("arbitrary", "arbitrary")),
        name="ada_mod",
    )(c8, w_ada, b_ada.reshape(DEPTH, 1, 6 * D_MODEL))
    return out[:, :b].reshape(DEPTH, b, 6, 1, D_MODEL)


def _router_logits(hf, wr_ref, br_ref):
    return lax.dot_general(wr_ref[...], hf, (((1,), (1,)), ((), ())), preferred_element_type=F32) + br_ref[...]


def _group_choice(lt):
    gl = [lt[i:i + 1, :] for i in range(N_EXPERT_GROUPS)]
    gmax = jnp.maximum(jnp.maximum(gl[0], gl[1]), jnp.maximum(gl[2], gl[3]))
    ge = [jnp.exp(g - gmax) for g in gl]
    gsum = ge[0] + ge[1] + ge[2] + ge[3]
    gp = [e / gsum for e in ge]
    g_val = jnp.maximum(jnp.maximum(gp[0], gp[1]), jnp.maximum(gp[2], gp[3]))
    g_idx = jnp.where(gp[0] == g_val, 0, jnp.where(gp[1] == g_val, 1, jnp.where(gp[2] == g_val, 2, 3)))
    return g_val, g_idx


def _ffn_prep(x1, gain, shift, scale, wr_ref, br_ref):
    hf32 = _norm_modulate(x1, gain, shift, scale)
    lt = _router_logits(hf32.astype(BF16), wr_ref, br_ref)
    _, g_idx = _group_choice(lt)
    el = []
    for j in range(EXPERTS_PER_GROUP):
        rows = [lt[4 + 4 * g + j:5 + 4 * g + j, :] for g in range(N_EXPERT_GROUPS)]
        el.append(jnp.where(g_idx == 0, rows[0], jnp.where(g_idx == 1, rows[1],
                  jnp.where(g_idx == 2, rows[2], rows[3]))))
    v1 = jnp.maximum(jnp.maximum(el[0], el[1]), jnp.maximum(el[2], el[3]))
    i1 = jnp.where(el[0] == v1, 0, jnp.where(el[1] == v1, 1, jnp.where(el[2] == v1, 2, 3)))
    el2 = [jnp.where(i1 == j, -jnp.inf, el[j]) for j in range(EXPERTS_PER_GROUP)]
    v2 = jnp.maximum(jnp.maximum(el2[0], el2[1]), jnp.maximum(el2[2], el2[3]))
    i2 = jnp.where((el2[0] == v2) & (i1 != 0), 0,
                   jnp.where((el2[1] == v2) & (i1 != 1), 1,
                             jnp.where((el2[2] == v2) & (i1 != 2), 2, 3)))
    a = jnp.minimum(i1, i2)
    bb = jnp.maximum(i1, i2)
    cls = g_idx * PAIRS_PER_GROUP + ((a * (7 - a)) >> 1) + (bb - a - 1)
    return hf32, cls


def _place_tokens(cls, carry_ref, cur_ref, tab_ref, before_ref, first_step):
    tm = cls.shape[1]

    @pl.when(first_step)
    def _():
        carry_ref[...] = jnp.zeros_like(carry_ref)
        cur_ref[...] = jnp.zeros_like(cur_ref)
        tab_ref[...] = jnp.zeros_like(tab_ref)
        si = lax.broadcasted_iota(jnp.int32, (tm, tm), 0)
        ti = lax.broadcasted_iota(jnp.int32, (tm, tm), 1)
        before_ref[...] = (si < ti).astype(BF16)

    crow = lax.broadcasted_iota(jnp.int32, (CLASS_ROWS, tm), 0)
    onehot = (crow == cls).astype(F32)
    prefix = jnp.dot(onehot.astype(BF16), before_ref[...], preferred_element_type=F32)
    seen = prefix + carry_ref[:, 0:1]
    rank = jnp.sum(onehot * seen, axis=0, keepdims=True)
    inv_tile = 1.0 / MOE_TILE
    rem = rank - MOE_TILE * jnp.floor(rank * inv_tile)
    opens = (rem == 0.0).astype(F32)
    opened_before = jnp.sum(jnp.floor((seen + (MOE_TILE - 1)) * inv_tile), axis=0, keepdims=True)
    sel = onehot * opens
    opened = jnp.sum(sel, axis=1, keepdims=True) > 0.0
    new_tile = jnp.sum(sel * opened_before, axis=1, keepdims=True)
    first_rank = jnp.sum(sel * rank, axis=1, keepdims=True)
    tile = jnp.sum(onehot * jnp.where(opened & (seen >= first_rank), new_tile, cur_ref[:, 0:1]),
                   axis=0, keepdims=True)
    carry_ref[...] = carry_ref[...] + jnp.sum(onehot, axis=1, keepdims=True)
    cur_ref[...] = jnp.where(opened, new_tile, cur_ref[...])
    lane = lax.broadcasted_iota(jnp.int32, (CLASS_ROWS, LANES), 1).astype(F32)
    cidx = lax.broadcasted_iota(jnp.int32, (CLASS_ROWS, LANES), 0).astype(F32)
    tab_ref[...] = tab_ref[...] + jnp.where(opened & (lane == new_tile), cidx, 0.0)
    return (tile * MOE_TILE + rem).astype(jnp.int32)


def _rope(t, cos, sgn, first_half):
    out = []
    for c in range(GROUP_WIDTH // LANES):
        xc = t[:, c * LANES:(c + 1) * LANES]
        other = jnp.where(first_half, pltpu.roll(xc, LANES - ROT_DIM // 2, 1),
                          pltpu.roll(xc, ROT_DIM // 2, 1))
        out.append(xc * cos + other * sgn)
    return jnp.concatenate(out, axis=1)


def _qkv_kernel(x_ref, gain_ref, sh_ref, sc_ref, w_ref, cos_ref, sgn_ref, *rest, tm):
    out_refs, hn_scr = rest[:9], rest[9]
    hn = _norm_modulate(x_ref[0], gain_ref[...], sh_ref[0, 0], sc_ref[0, 0])
    n_slab = D_MODEL // LANES
    lane = lax.broadcasted_iota(jnp.int32, (tm, LANES), 1) % HEAD_DIM
    first_half = lane < ROT_DIM // 2
    for g, d in enumerate(DILATIONS):
        rows = tm // d
        if d == 1:
            hp, cos, sgn = hn, cos_ref[...], sgn_ref[...]
        else:
            if g == 1:
                for s in range(n_slab):
                    hn_scr[s] = hn[:, s * LANES:(s + 1) * LANES]
            hp = jnp.concatenate(
                [jnp.concatenate([hn_scr[s, pl.ds(r, rows, stride=d), :] for s in range(n_slab)], axis=1)
                 for r in range(d)], axis=0)
            cos = jnp.concatenate([cos_ref[pl.ds(r, rows, stride=d), :] for r in range(d)], axis=0)
            sgn = jnp.concatenate([sgn_ref[pl.ds(r, rows, stride=d), :] for r in range(d)], axis=0)
        res = jnp.dot(hp.astype(BF16), w_ref[g], preferred_element_type=F32)
        q = _rope(res[:, :GROUP_WIDTH], cos, sgn, first_half)
        k = _rope(res[:, GROUP_WIDTH:2 * GROUP_WIDTH], cos, sgn, first_half)
        v = res[:, 2 * GROUP_WIDTH:]
        for t, o_ref in zip((q, k, v), out_refs[3 * g:3 * g + 3]):
            o_ref[0] = t.astype(BF16).reshape(d, rows, GROUP_WIDTH)


def _qkv(x, gain, shift, scale, w_groups, cos_t, sgn_t, tm=1024):
    b, s, _ = x.shape
    out_shape, out_specs = [], []
    for d in DILATIONS:
        for _ in range(3):
            out_shape.append(jax.ShapeDtypeStruct((b, d, s // d, GROUP_WIDTH), BF16))
            out_specs.append(pl.BlockSpec((1, d, tm // d, GROUP_WIDTH), lambda bi, i: (bi, 0, i, 0)))
    return pl.pallas_call(
        functools.partial(_qkv_kernel, tm=tm),
        grid=(b, s // tm),
        in_specs=[pl.BlockSpec((1, tm, D_MODEL), lambda bi, i: (bi, i, 0)),
                  _const_spec((1, D_MODEL), lambda bi, i: (0, 0)),
                  pl.BlockSpec((1, 1, 1, D_MODEL), lambda bi, i: (bi, 0, 0, 0)),
                  pl.BlockSpec((1, 1, 1, D_MODEL), lambda bi, i: (bi, 0, 0, 0)),
                  _const_spec((N_ATTN_GROUPS, D_MODEL, 3 * GROUP_WIDTH), lambda bi, i: (0, 0, 0)),
                  pl.BlockSpec((tm, LANES), lambda bi, i: (i, 0)),
                  pl.BlockSpec((tm, LANES), lambda bi, i: (i, 0))],
        out_specs=out_specs,
        out_shape=out_shape,
        scratch_shapes=[pltpu.VMEM((D_MODEL // LANES, tm, LANES), F32)],
        compiler_params=_cparams(("arbitrary", "arbitrary")),
        name="qkv_rope",
    )(x, gain, shift, scale, w_groups, cos_t, sgn_t)


def _attn_kernel(q_ref, kp_ref, kc_ref, kn_ref, vp_ref, vc_ref, vn_ref, o_ref, lse_ref,
                 kbuf, vbuf, *, seq, tq):
    i = pl.program_id(2)
    r = BAND_RADIUS
    kbuf[0:r] = kp_ref[0, 0]
    kbuf[r:r + tq] = kc_ref[0, 0]
    kbuf[r + tq:] = kn_ref[0, 0]
    vbuf[0:r] = vp_ref[0, 0]
    vbuf[r:r + tq] = vc_ref[0, 0]
    vbuf[r + tq:] = vn_ref[0, 0]
    qb = 2 * r
    lane = lax.broadcasted_iota(jnp.int32, (qb, LANES), 1)
    low = lane < HEAD_DIM
    tidx = lax.broadcasted_iota(jnp.int32, (qb, 2 * qb), 0)
    kidx = lax.broadcasted_iota(jnp.int32, (qb, 2 * qb), 1)
    rel = kidx - tidx
    band = (rel >= 0) & (rel <= 2 * r)
    for j in range(tq // qb):
        kpos = i * tq + (j * qb - r) + kidx
        mask = band & (kpos >= 0) & (kpos < seq)
        q = q_ref[0, 0, j * qb:(j + 1) * qb, :]
        k = kbuf[j * qb:(j + 2) * qb, :]
        v = vbuf[j * qb:(j + 2) * qb, :]
        lse_tile = jnp.zeros((qb, LANES), F32)
        o_chunks = []
        for c in range(GROUP_WIDTH // LANES):
            qc = q[:, c * LANES:(c + 1) * LANES]
            kc = k[:, c * LANES:(c + 1) * LANES]
            vc = v[:, c * LANES:(c + 1) * LANES]
            outs = []
            for hh in range(2):
                head = 2 * c + hh
                if head >= HEADS_PER_GROUP:
                    outs.append(jnp.zeros((qb, LANES), F32))
                    continue
                qm = jnp.where(low if hh == 0 else jnp.logical_not(low), qc, jnp.zeros_like(qc))
                s = lax.dot_general(qm, kc, (((1,), (1,)), ((), ())), preferred_element_type=F32)
                s = jnp.where(mask, s, NEG_INF)
                m = jnp.max(s, axis=-1, keepdims=True)
                p = jnp.exp(s - m)
                den = jnp.sum(p, axis=-1, keepdims=True)
                o = jnp.dot(p.astype(BF16), vc, preferred_element_type=F32) / den
                lse_tile = jnp.where(lane == head, m + jnp.log(den), lse_tile)
                outs.append(o)
            o_chunks.append(jnp.where(low, outs[0], outs[1]))
        o_ref[0, 0, j * qb:(j + 1) * qb, :] = jnp.concatenate(o_chunks, axis=1).astype(BF16)
        lse_ref[0, 0, j * qb:(j + 1) * qb, :] = lse_tile


def _attention(q, k, v, tq=1024):
    b, d, seq, w = q.shape
    r = BAND_RADIUS
    nb = seq // r
    cur = pl.BlockSpec((1, 1, tq, w), lambda bi, ri, i: (bi, ri, i, 0))
    prev = pl.BlockSpec((1, 1, r, w), lambda bi, ri, i: (bi, ri, jnp.maximum(i * (tq // r) - 1, 0), 0))
    nxt = pl.BlockSpec((1, 1, r, w), lambda bi, ri, i: (bi, ri, jnp.minimum((i + 1) * (tq // r), nb - 1), 0))
    return pl.pallas_call(
        functools.partial(_attn_kernel, seq=seq, tq=tq),
        grid=(b, d, seq // tq),
        in_specs=[cur, prev, cur, nxt, prev, cur, nxt],
        out_specs=[pl.BlockSpec((1, 1, tq, w), lambda bi, ri, i: (bi, ri, i, 0)),
                   pl.BlockSpec((1, 1, tq, LANES), lambda bi, ri, i: (bi, ri, i, 0))],
        out_shape=[jax.ShapeDtypeStruct((b, d, seq, w), BF16),
                   jax.ShapeDtypeStruct((b, d, seq, LANES), F32)],
        scratch_shapes=[pltpu.VMEM((tq + 2 * r, w), BF16), pltpu.VMEM((tq + 2 * r, w), BF16)],
        compiler_params=_cparams(("arbitrary", "arbitrary", "arbitrary")),
        name=f"band_attn_d{d}",
    )(q, k, k, k, v, v, v)


def _attn_out_kernel(o0, o1, o2, l0, l1, l2, x_ref, wo_ref, exp_ref, gt_ref, gain_ref, sh_ref, sc_ref,
                     wr_ref, br_ref, x1_ref, xs_hbm, pos_ref, tab_out, cnt_out, oscr, lscr, *route_scratch,
                     tm, n_steps):
    n_slab = GROUP_WIDTH // LANES
    for g, (d, o_ref, l_ref) in enumerate(zip(DILATIONS, (o0, o1, o2), (l0, l1, l2))):
        rows = tm // d
        for r in range(d):
            dst = slice(None) if d == 1 else pl.ds(r, rows, stride=d)
            blk = o_ref[0, r].astype(F32)
            for c in range(n_slab):
                oscr[g * n_slab + c, dst, :] = blk[:, c * LANES:(c + 1) * LANES]
            lscr[g, dst, :] = l_ref[0, r]
    lse = [lscr[g] for g in range(N_ATTN_GROUPS)]
    mx = jnp.maximum(jnp.maximum(lse[0], lse[1]), lse[2])
    ex = [jnp.exp(l - mx) for l in lse]
    tot = ex[0] + ex[1] + ex[2]
    y = jnp.zeros((tm, D_MODEL), F32)
    for g in range(N_ATTN_GROUPS):
        alpha = ex[g] / tot
        a_hi, a_lo = _split_bf16(alpha)
        a_wide = (jnp.dot(a_hi, exp_ref[...], preferred_element_type=F32)
                  + jnp.dot(a_lo, exp_ref[...], preferred_element_type=F32))
        og = jnp.concatenate([oscr[g * n_slab + c] for c in range(n_slab)], axis=1)
        y = y + jnp.dot((og * a_wide).astype(BF16), wo_ref[g], preferred_element_type=F32)
    x1 = x_ref[0] + gt_ref[0, 0] * y
    x1_ref[0] = x1
    _route_outputs(x1, gain_ref, sh_ref, sc_ref, wr_ref, br_ref, xs_hbm, pos_ref, tab_out, cnt_out, route_scratch,
                   n_steps)


ZERO_TOKENS = MOE_TILE // 2


def _route_outputs(x1, gain_ref, sh_ref, sc_ref, wr_ref, br_ref, xs_hbm, pos_ref, tab_out, cnt_out, scratch,
                   n_steps):
    (carry_ref, cur_ref, tab_ref, before_ref, rows_scr, pos_vmem, pos_smem, fin_vmem, fin_smem, zbuf, sems,
     aux_sem) = scratch
    step = pl.program_id(0) * pl.num_programs(1) + pl.program_id(1)
    tm = x1.shape[0]
    rows = tm * TOKEN_ROWS
    hf32, cls = _ffn_prep(x1, gain_ref[...], sh_ref[0, 0], sc_ref[0, 0], wr_ref, br_ref)
    pos = _place_tokens(cls, carry_ref, cur_ref, tab_ref, before_ref, step == 0)
    pos_lines = _lines(pos)
    pos_ref[0, 0] = pos_lines
    pos_vmem[...] = pos_lines
    to_smem = pltpu.make_async_copy(pos_vmem, pos_smem, aux_sem)
    to_smem.start()

    def drain(slot):
        pltpu.make_async_copy(rows_scr.at[pl.ds(0, rows), :], xs_hbm.at[pl.ds(0, rows), :], sems.at[slot]).wait()

    slot = step % 2

    @pl.when(step >= 2)
    def _():
        drain(slot)

    base = pl.multiple_of(slot * rows, rows)
    for s in range(TOKEN_ROWS):
        rows_scr[pl.ds(base + s, tm, stride=TOKEN_ROWS), :] = hf32[:, s * LANES:(s + 1) * LANES]
    to_smem.wait()

    def issue(line, lane, priority):
        _token_copy(rows_scr, slot * tm + line * LANES + lane, xs_hbm, pos_smem[line, lane],
                    sems.at[slot]).start(priority=priority)

    _issue_burst(tm, issue, spread=False)
    tab_out[...] = tab_ref[...]
    cnt_out[...] = carry_ref[...]

    @pl.when(step == n_steps - 1)
    def _():
        drain(slot)
        if n_steps > 1:
            drain(1 - slot)
        _clear_unused_rows(carry_ref, cur_ref, xs_hbm, fin_vmem, fin_smem, zbuf, aux_sem)


def _clear_unused_rows(carry_ref, cur_ref, xs_hbm, fin_vmem, fin_smem, zbuf, sem):
    lane = lax.broadcasted_iota(jnp.int32, (CLASS_ROWS, LANES), 1)
    crow = lax.broadcasted_iota(jnp.int32, (CLASS_ROWS, LANES), 0)
    on_lane = lambda col: jnp.sum(jnp.where(lane == crow, col, 0.0), axis=0, keepdims=True)
    counts = carry_ref[...]
    used = jnp.sum(jnp.floor((counts + (MOE_TILE - 1)) * (1.0 / MOE_TILE)), axis=0, keepdims=True)
    r8 = lax.broadcasted_iota(jnp.int32, (SUBLANES, LANES), 0)
    fin = jnp.where(r8 == 0, on_lane(cur_ref[...]), jnp.where(r8 == 1, on_lane(counts), jnp.where(r8 == 2, used, 0.0)))
    fin_vmem[...] = fin.astype(jnp.int32)
    to_smem = pltpu.make_async_copy(fin_vmem, fin_smem, sem)
    to_smem.start()
    zbuf[...] = jnp.zeros_like(zbuf)
    to_smem.wait()

    def zero_copy(first_tok, n_tok):
        d0 = pl.multiple_of(first_tok * TOKEN_ROWS, TOKEN_ROWS)
        return pltpu.make_async_copy(zbuf.at[pl.ds(0, n_tok * TOKEN_ROWS), :],
                                     xs_hbm.at[pl.ds(d0, n_tok * TOKEN_ROWS), :], sem)

    for wait in (False, True):
        for c in range(N_CLASSES):
            count = fin_smem[1, c]
            fill = jnp.where(count > 0, (MOE_TILE - count % MOE_TILE) % MOE_TILE, 0)
            first = fin_smem[0, c] * MOE_TILE + (MOE_TILE - fill)
            piece = ZERO_TOKENS
            while piece >= 1:
                done = (fill // (2 * piece)) * (2 * piece)

                @pl.when((fill // piece) % 2 == 1)
                def _(first=first, done=done, piece=piece):
                    cp = zero_copy(first + done, piece)
                    cp.wait() if wait else cp.start()

                piece //= 2

    def clear_tile(j, carry):
        for half in range(MOE_TILE // ZERO_TOKENS):
            cp = zero_copy(j * MOE_TILE + half * ZERO_TOKENS, ZERO_TOKENS)
            cp.start()
            cp.wait()
        return carry

    lax.fori_loop(fin_smem[2, 0], xs_hbm.shape[0] // (MOE_TILE * TOKEN_ROWS), clear_tile, 0)


def _route_scratch(tm):
    assert tm <= MOE_TILE, "a class may open at most one tile per grid step"
    return [pltpu.VMEM((CLASS_ROWS, LANES), F32), pltpu.VMEM((CLASS_ROWS, LANES), F32),
            pltpu.VMEM((CLASS_ROWS, LANES), F32),
            pltpu.VMEM((tm, tm), BF16),
            pltpu.VMEM((2 * tm * TOKEN_ROWS, LANES), F32),
            pltpu.VMEM((SUBLANES, LANES), jnp.int32), pltpu.SMEM((SUBLANES, LANES), jnp.int32),
            pltpu.VMEM((SUBLANES, LANES), jnp.int32), pltpu.SMEM((SUBLANES, LANES), jnp.int32),
            pltpu.VMEM((ZERO_TOKENS * TOKEN_ROWS, LANES), F32),
            pltpu.SemaphoreType.DMA((2,)), pltpu.SemaphoreType.DMA]


def _route_out_specs(b, s, tm):
    n_tiles = b * s // MOE_TILE + N_CLASSES
    specs = [pl.BlockSpec((1, tm, D_MODEL), lambda bi, i: (bi, i, 0)),
             pl.BlockSpec(memory_space=pl.ANY),
             pl.BlockSpec((1, 1, SUBLANES, LANES), lambda bi, i: (bi, i, 0, 0)),
             pl.BlockSpec((CLASS_ROWS, LANES), lambda bi, i: (0, 0)),
             pl.BlockSpec((CLASS_ROWS, LANES), lambda bi, i: (0, 0))]
    shapes = [jax.ShapeDtypeStruct((b, s, D_MODEL), F32),
              jax.ShapeDtypeStruct((n_tiles * MOE_TILE * TOKEN_ROWS, LANES), F32),
              jax.ShapeDtypeStruct((b, s // tm, SUBLANES, LANES), jnp.int32),
              jax.ShapeDtypeStruct((CLASS_ROWS, LANES), F32),
              jax.ShapeDtypeStruct((CLASS_ROWS, LANES), F32)]
    return specs, shapes


def _mod_spec():
    return pl.BlockSpec((1, 1, 1, D_MODEL), lambda bi, i: (bi, 0, 0, 0))


def _attn_out(o_list, l_list, x, wo_groups, expand, gt, gain, shift, scale, wr_t, br, tm=512):
    b, s, _ = x.shape
    o_specs = [pl.BlockSpec((1, d, tm // d, GROUP_WIDTH), lambda bi, i: (bi, 0, i, 0)) for d in DILATIONS]
    l_specs = [pl.BlockSpec((1, d, tm // d, LANES), lambda bi, i: (bi, 0, i, 0)) for d in DILATIONS]
    out_specs, out_shape = _route_out_specs(b, s, tm)
    return pl.pallas_call(
        functools.partial(_attn_out_kernel, tm=tm, n_steps=b * s // tm),
        grid=(b, s // tm),
        in_specs=o_specs + l_specs + [
            pl.BlockSpec((1, tm, D_MODEL), lambda bi, i: (bi, i, 0)),
            _const_spec((N_ATTN_GROUPS, GROUP_WIDTH, D_MODEL), lambda bi, i: (0, 0, 0)),
            _const_spec((LANES, GROUP_WIDTH), lambda bi, i: (0, 0)),
            _mod_spec(),
            _const_spec((1, D_MODEL), lambda bi, i: (0, 0)),
            _mod_spec(), _mod_spec(),
            _const_spec((ROUTER_ROWS, D_MODEL), lambda bi, i: (0, 0)),
            _const_spec((ROUTER_ROWS, 1), lambda bi, i: (0, 0))],
        out_specs=out_specs,
        out_shape=out_shape,
        scratch_shapes=[pltpu.VMEM((N_ATTN_GROUPS * GROUP_WIDTH // LANES, tm, LANES), F32),
                        pltpu.VMEM((N_ATTN_GROUPS, tm, LANES), F32)] + _route_scratch(tm),
        compiler_params=_cparams(("arbitrary", "arbitrary")),
        name="attn_merge_proj",
    )(*o_list, *l_list, x, wo_groups, expand, gt, gain, shift, scale, wr_t, br)


DMA_BURST_UNROLL = 16


BACKGROUND_PRIORITY = 1


def _issue_burst(n, issue, spread=True):
    for line in range(n // LANES):
        def body(i, carry, line=line):
            for k in range(DMA_BURST_UNROLL):
                issue(line, i * DMA_BURST_UNROLL + k, k % 2 if spread else BACKGROUND_PRIORITY)
            return carry

        lax.fori_loop(0, LANES // DMA_BURST_UNROLL, body, 0)


def _lines(row):
    n = row.shape[1]
    line = lax.broadcasted_iota(jnp.int32, (SUBLANES, LANES), 0)
    out = jnp.zeros((SUBLANES, LANES), jnp.int32)
    for j in range(n // LANES):
        out = jnp.where(line == j, row[:, j * LANES:(j + 1) * LANES], out)
    return out


class _RowGather:
    def __init__(self, pos_ref, pos_next_ref, ys_hbm, ybuf, sems, step, n_steps, tm, spread):
        self.pos_ref, self.pos_next_ref, self.ys_hbm, self.ybuf, self.sems = pos_ref, pos_next_ref, ys_hbm, ybuf, sems
        self.step, self.n_steps, self.tm, self.spread = step, n_steps, tm, spread

    def _request(self, p_ref, slot):
        def issue(line, lane, priority):
            _token_copy(self.ys_hbm, p_ref[0, line, lane], self.ybuf, slot * self.tm + line * LANES + lane,
                        self.sems.at[slot]).start(priority=priority)

        _issue_burst(self.tm, issue, self.spread)

    def request_next(self):
        @pl.when(self.step + 1 < self.n_steps)
        def _():
            self._request(self.pos_next_ref, (self.step + 1) % 2)

    def current(self):
        rows = self.tm * TOKEN_ROWS

        @pl.when(self.step == 0)
        def _():
            self._request(self.pos_ref, 0)

        slot = self.step % 2
        base = pl.multiple_of(slot * rows, rows)
        pltpu.make_async_copy(self.ys_hbm.at[pl.ds(0, rows), :], self.ybuf.at[pl.ds(base, rows), :],
                              self.sems.at[slot]).wait()
        return jnp.concatenate([self.ybuf[pl.ds(base + s, self.tm, stride=TOKEN_ROWS), :]
                                for s in range(D_MODEL // LANES)], axis=1)


def _gather_specs(n_steps, tm, index_of):
    return [pl.BlockSpec((1, SUBLANES, LANES), lambda *g: (index_of(*g), 0, 0), memory_space=pltpu.SMEM),
            pl.BlockSpec((1, SUBLANES, LANES), lambda *g: (jnp.minimum(index_of(*g) + 1, n_steps - 1), 0, 0),
                         memory_space=pltpu.SMEM),
            pl.BlockSpec(memory_space=pl.ANY)]


def _pos_lines(pos, tm):
    lines = pos.reshape(-1, tm // LANES, LANES)
    return jnp.pad(lines, ((0, 0), (0, SUBLANES - tm // LANES), (0, 0)))


def _gather_scratch(tm):
    return [pltpu.VMEM((2 * tm * TOKEN_ROWS, LANES), F32), pltpu.SemaphoreType.DMA((2,))]


def _gmlp_kernel(pos_ref, pos_next_ref, ys_hbm, gtf_ref, x_ref, gain_ref, sh_ref, sc_ref, win_ref, vg_ref, ws_ref,
                 bs_ref, wo_ref, gt_ref, fgain_ref, fsh_ref, fsc_ref, wr_ref, br_ref, x1_ref, xs_hbm, pos_out, tab_out,
                 cnt_out, gated_scr, ybuf, sems, *route_scratch, tm, n_steps):
    step = pl.program_id(0) * pl.num_programs(1) + pl.program_id(1)
    gather = _RowGather(pos_ref, pos_next_ref, ys_hbm, ybuf, sems, step, n_steps, tm, spread=False)
    gather.request_next()
    x = x_ref[0] + gtf_ref[0, 0] * gather.current()
    hn = _norm_modulate(x, gain_ref[...], sh_ref[0, 0], sc_ref[0, 0]).astype(BF16)
    v = _gelu(jnp.dot(hn, win_ref[:, GMLP_HALF:], preferred_element_type=F32))
    u = _gelu(jnp.dot(hn, win_ref[:, :GMLP_HALF], preferred_element_type=F32))
    mu = jnp.mean(v, axis=-1, keepdims=True)
    vc = v - mu
    vn = (vc * lax.rsqrt(jnp.mean(vc * vc, axis=-1, keepdims=True) + EPS) * vg_ref[...]).astype(BF16)
    gw = GMLP_HALF // GMLP_GROUPS
    for c in range(tm // CHUNK):
        rs = slice(c * CHUNK, (c + 1) * CHUNK)
        for g in range(GMLP_GROUPS):
            cs = slice(g * gw, (g + 1) * gw)
            vs = jnp.dot(ws_ref[g], vn[rs, cs], preferred_element_type=F32) + bs_ref[:, g:g + 1]
            gated_scr[rs, cs] = (u[rs, cs] * vs).astype(BF16)
    y = jnp.dot(gated_scr[...], wo_ref[...], preferred_element_type=F32)
    x1 = x + gt_ref[0, 0] * y
    x1_ref[0] = x1
    _route_outputs(x1, fgain_ref, fsh_ref, fsc_ref, wr_ref, br_ref, xs_hbm, pos_out, tab_out, cnt_out,
                   route_scratch, n_steps)


def _gmlp(ys, pos, gt_prev, x, gain, shift, scale, w_in, v_gain, w_s, b_s_t, w_o, gt, fgain, fshift, fscale,
          wr_t, br, tm=512):
    b, s, _ = x.shape
    tok = lambda w: pl.BlockSpec((1, tm, w), lambda bi, i: (bi, i, 0))
    c2 = lambda shape: _const_spec(shape, lambda bi, i: (0,) * len(shape))
    out_specs, out_shape = _route_out_specs(b, s, tm)
    n_steps = b * s // tm
    return pl.pallas_call(
        functools.partial(_gmlp_kernel, tm=tm, n_steps=n_steps),
        grid=(b, s // tm),
        in_specs=_gather_specs(n_steps, tm, lambda bi, i: bi * (s // tm) + i) + [
                  _mod_spec(),
                  tok(D_MODEL), c2((1, D_MODEL)), _mod_spec(), _mod_spec(),
                  c2((D_MODEL, 2 * GMLP_HALF)), c2((1, GMLP_HALF)),
                  c2((GMLP_GROUPS, CHUNK, CHUNK)), c2((CHUNK, GMLP_GROUPS)),
                  c2((GMLP_HALF, D_MODEL)), _mod_spec(),
                  c2((1, D_MODEL)), _mod_spec(), _mod_spec(),
                  c2((ROUTER_ROWS, D_MODEL)), c2((ROUTER_ROWS, 1))],
        out_specs=out_specs,
        out_shape=out_shape,
        scratch_shapes=[pltpu.VMEM((tm, GMLP_HALF), BF16)] + _gather_scratch(tm) + _route_scratch(tm),
        compiler_params=_cparams(("arbitrary", "arbitrary")),
        name="gmlp",
    )(_pos_lines(pos, tm), _pos_lines(pos, tm), ys, gt_prev,
      x, gain, shift, scale, w_in, v_gain, w_s, b_s_t, w_o, gt, fgain, fshift, fscale, wr_t, br)


def _token_copy(src, src_tok, dst, dst_tok, sem):
    s0 = pl.multiple_of(src_tok * TOKEN_ROWS, TOKEN_ROWS)
    d0 = pl.multiple_of(dst_tok * TOKEN_ROWS, TOKEN_ROWS)
    return pltpu.make_async_copy(src.at[pl.ds(s0, TOKEN_ROWS), :], dst.at[pl.ds(d0, TOKEN_ROWS), :], sem)


def _slabs(ref, n_tok, n_slab):
    return jnp.concatenate([ref[pl.ds(s, n_tok, stride=TOKEN_ROWS), :] for s in range(n_slab)], axis=1)


def _expert_kernel(ea_ref, eb_ref, blk_ref, nused_ref, x_ref, wr_ref, br_ref, wga, wgb, wua, wub, wda, wdb, y_ref,
                   lt_scr):
    j = pl.program_id(0)

    @pl.when(j < nused_ref[0])
    def _():
        h = _slabs(x_ref, MOE_TILE, TOKEN_ROWS).astype(BF16)
        lt = _router_logits(h, wr_ref, br_ref)
        g_val, _ = _group_choice(lt)
        lt_scr[...] = lt
        el_a = lt_scr[pl.ds(N_EXPERT_GROUPS + ea_ref[j], 1), :]
        el_b = lt_scr[pl.ds(N_EXPERT_GROUPS + eb_ref[j], 1), :]
        top = jnp.maximum(el_a, el_b)
        p_a = jnp.exp(el_a - top)
        p_b = jnp.exp(el_b - top)
        row = lax.broadcasted_iota(jnp.int32, (LANES, MOE_TILE), 0)
        gates_t = jnp.where(row == 0, g_val * (p_a / (p_a + p_b)), jnp.where(row == 1, g_val * (p_b / (p_a + p_b)), 0.0))
        gates = gates_t.T
        y = jnp.zeros((MOE_TILE, D_MODEL), F32)
        for gate, wg, wu, wd in ((gates[:, 0:1], wga, wua, wda), (gates[:, 1:2], wgb, wub, wdb)):
            hg = jnp.dot(h, wg[0, 0], preferred_element_type=F32)
            hu = jnp.dot(h, wu[0, 0], preferred_element_type=F32)
            act = (_silu(hg) * hu * gate).astype(BF16)
            y = y + jnp.dot(act, wd[0, 0], preferred_element_type=F32)
        for s in range(TOKEN_ROWS):
            y_ref[pl.ds(s, MOE_TILE, stride=TOKEN_ROWS), :] = y[:, s * LANES:(s + 1) * LANES]

    @pl.when(j >= nused_ref[0])
    def _():
        y_ref[...] = jnp.zeros_like(y_ref)


def _experts(xs, tile_ea, tile_eb, tile_blk, n_used, wr_t, br, wg, wu, wd, layer):
    n_tiles = xs.shape[0] // (MOE_TILE * TOKEN_ROWS)
    up = lambda sel: pl.BlockSpec((1, 1, D_MODEL, D_EXPERT),
                                  lambda j, ea, eb, blk, nu: (layer, (ea, eb)[sel][j], 0, 0))
    down = lambda sel: pl.BlockSpec((1, 1, D_EXPERT, D_MODEL),
                                    lambda j, ea, eb, blk, nu: (layer, (ea, eb)[sel][j], 0, 0))
    shape = (MOE_TILE * TOKEN_ROWS, LANES)
    return pl.pallas_call(
        _expert_kernel,
        grid_spec=pltpu.PrefetchScalarGridSpec(
            num_scalar_prefetch=4,
            grid=(n_tiles,),
            in_specs=[pl.BlockSpec(shape, lambda j, ea, eb, blk, nu: (blk[j], 0)),
                      pl.BlockSpec((ROUTER_ROWS, D_MODEL), lambda j, ea, eb, blk, nu: (0, 0)),
                      pl.BlockSpec((ROUTER_ROWS, 1), lambda j, ea, eb, blk, nu: (0, 0)),
                      up(0), up(1), up(0), up(1), down(0), down(1)],
            out_specs=pl.BlockSpec(shape, lambda j, ea, eb, blk, nu: (j, 0)),
            scratch_shapes=[pltpu.VMEM((ROUTER_ROWS, MOE_TILE), F32)]),
        out_shape=jax.ShapeDtypeStruct(xs.shape, F32),
        compiler_params=_cparams(("arbitrary",)),
        name="moe_experts",
    )(tile_ea, tile_eb, tile_blk, n_used, xs, wr_t, br, wg, wg, wu, wu, wd, wd)


def _final_kernel(pos_ref, pos_next_ref, ys_hbm, x_ref, gt_ref, fin_ref, o_ref, ybuf, sems, *, tm):
    gather = _RowGather(pos_ref, pos_next_ref, ys_hbm, ybuf, sems, pl.program_id(0), pl.num_programs(0), tm,
                        spread=True)
    gather.request_next()
    y_moe = gather.current()
    x2 = x_ref[...] + gt_ref[0, 0] * y_moe
    ms = jnp.mean(x2 * x2, axis=-1, keepdims=True)
    o_ref[...] = x2 * lax.rsqrt(ms + EPS) * fin_ref[...]


def _final(ys, pos, x, gt, fin_gain, seq, tm=1024):
    t = x.shape[0]
    tiles_per_batch = seq // tm
    n_steps = t // tm
    return pl.pallas_call(
        functools.partial(_final_kernel, tm=tm),
        grid=(n_steps,),
        in_specs=_gather_specs(n_steps, tm, lambda i: i) + [
                  pl.BlockSpec((tm, D_MODEL), lambda i: (i, 0)),
                  pl.BlockSpec((1, 1, 1, D_MODEL), lambda i: (i // tiles_per_batch, 0, 0, 0)),
                  pl.BlockSpec((1, D_MODEL), lambda i: (0, 0))],
        out_specs=pl.BlockSpec((tm, D_MODEL), lambda i: (i, 0)),
        out_shape=jax.ShapeDtypeStruct((t, D_MODEL), F32),
        scratch_shapes=_gather_scratch(tm),
        compiler_params=_cparams(("arbitrary",)),
        name="moe_combine_final_norm",
    )(_pos_lines(pos, tm), _pos_lines(pos, tm), ys, x, gt, fin_gain)


_PAIR_A = np.array([0, 0, 0, 1, 1, 2], np.int32)
_PAIR_B = np.array([1, 2, 3, 2, 3, 3], np.int32)


def _moe_sorted(xs, tab, counts, wr_t, br, wg, wu, wd, layer):
    n_tiles = xs.shape[0] // (MOE_TILE * TOKEN_ROWS)
    cnt = counts[:N_CLASSES, 0].astype(jnp.int32)
    n_used = jnp.sum((cnt + MOE_TILE - 1) // MOE_TILE)
    ids = jnp.arange(n_tiles, dtype=jnp.int32)
    tile_cls = jnp.sum(tab, axis=0)[:n_tiles].astype(jnp.int32)
    last_cls = jnp.sum(jnp.where(ids == n_used - 1, tile_cls, 0))
    tile_cls = jnp.where(ids < n_used, tile_cls, last_cls)
    j = jnp.minimum(ids, n_used - 1)
    pair = tile_cls % PAIRS_PER_GROUP
    pair_a = jnp.sum(jnp.where(pair[:, None] == jnp.arange(PAIRS_PER_GROUP)[None, :], _PAIR_A[None, :], 0), axis=1)
    pair_b = jnp.sum(jnp.where(pair[:, None] == jnp.arange(PAIRS_PER_GROUP)[None, :], _PAIR_B[None, :], 0), axis=1)
    base = (tile_cls // PAIRS_PER_GROUP) * EXPERTS_PER_GROUP
    return _experts(xs, base + pair_a, base + pair_b, j, n_used.reshape(1), wr_t, br, wg, wu, wd, layer)


def _rope_tables(seq):
    inv_freq = ROPE_THETA ** (-jnp.arange(0, ROT_DIM, 2, dtype=jnp.float32) / ROT_DIM)
    ang = jnp.arange(seq).astype(jnp.float32)[:, None] * inv_freq[None, :]
    cos, sin = jnp.cos(ang), jnp.sin(ang)
    ones = jnp.ones((seq, HEAD_DIM - ROT_DIM), jnp.float32)
    cos_h = jnp.concatenate([cos, cos, ones], axis=1)
    sgn_h = jnp.concatenate([-sin, sin, 0.0 * ones], axis=1)
    reps = LANES // HEAD_DIM
    return jnp.tile(cos_h, (1, reps)).astype(F32), jnp.tile(sgn_h, (1, reps)).astype(F32)


def _pad_heads(w, axis):
    pad = [(0, 0)] * w.ndim
    pad[axis] = (0, GROUP_WIDTH - HEADS_PER_GROUP * HEAD_DIM)
    return jnp.pad(w, pad)


def _router_weights(w_group, b_group, w_expert, b_expert):
    we = jnp.transpose(w_expert, (1, 0, 2)).reshape(D_MODEL, N_EXPERTS)
    w = jnp.concatenate([w_group, we], axis=1)
    w = jnp.pad(w, ((0, 0), (0, ROUTER_ROWS - w.shape[1])))
    bias = jnp.pad(jnp.concatenate([b_group, b_expert.reshape(-1)]), (0, ROUTER_ROWS - 20))
    return w.T.astype(BF16), bias.reshape(ROUTER_ROWS, 1)


def kernel(x, c, norm_mix, norm_ffn, w_ada, b_ada, a_w_qkv, a_w_o, b_w_in, b_v_gain, b_w_s, b_b_s, b_w_o,
           r_w_group, r_b_group, r_w_expert, r_b_expert, e_w_gate, e_w_up, e_w_down, final_norm):
    b, s, _ = x.shape
    mod = _ada_mod(c, w_ada, b_ada)
    sh_m, sc_m, gt_m, sh_f, sc_f, gt_f = [mod[:, :, i:i + 1] for i in range(6)]
    row = lambda v: v.reshape(1, -1)
    gw = HEADS_PER_GROUP * HEAD_DIM
    e_w_gate, e_w_up, e_w_down = [w.astype(BF16) for w in (e_w_gate, e_w_up, e_w_down)]

    w = a_w_qkv[0].reshape(D_MODEL, 3, N_ATTN_GROUPS, gw)
    w = w * jnp.asarray([HEAD_DIM ** -0.5, 1.0, 1.0], F32)[None, :, None, None]
    w_groups = jnp.transpose(_pad_heads(w, 3), (2, 0, 1, 3)).reshape(
        N_ATTN_GROUPS, D_MODEL, 3 * GROUP_WIDTH).astype(BF16)
    wo_groups = _pad_heads(a_w_o[0].reshape(N_ATTN_GROUPS, gw, D_MODEL), 1).astype(BF16)
    expand = (jnp.arange(LANES)[:, None] == jnp.arange(GROUP_WIDTH)[None, :] // HEAD_DIM).astype(BF16)
    cos_t, sgn_t = _rope_tables(s)
    qkv = _qkv(x, row(norm_mix[0]), sh_m[0], sc_m[0], w_groups, cos_t, sgn_t)
    o_list, l_list = [], []
    for g in range(N_ATTN_GROUPS):
        o, l = _attention(*qkv[3 * g:3 * g + 3])
        o_list.append(o)
        l_list.append(l)
    wr_t, br = _router_weights(r_w_group[0], r_b_group[0], r_w_expert[0], r_b_expert[0])
    x1, xs, pos, tab, counts = _attn_out(o_list, l_list, x, wo_groups, expand, gt_m[0], row(norm_ffn[0]),
                                         sh_f[0], sc_f[0], wr_t, br)
    t = b * s
    token_order = lambda p: p[:, :, :t // (p.shape[0] * p.shape[1] * LANES), :].reshape(t)
    ys = _moe_sorted(xs, tab, counts, wr_t, br, e_w_gate, e_w_up, e_w_down, 0)

    wr_t, br = _router_weights(r_w_group[1], r_b_group[1], r_w_expert[1], r_b_expert[1])
    x3, xs, pos, tab, counts = _gmlp(ys, token_order(pos), gt_f[0], x1, row(norm_mix[1]), sh_m[1], sc_m[1],
                                     b_w_in[0].astype(BF16), row(b_v_gain[0]), b_w_s[0].astype(BF16), b_b_s[0].T,
                                     b_w_o[0].astype(BF16), gt_m[1], row(norm_ffn[1]), sh_f[1], sc_f[1], wr_t, br)
    ys = _moe_sorted(xs, tab, counts, wr_t, br, e_w_gate, e_w_up, e_w_down, 1)
    out = _final(ys, token_order(pos), x3.reshape(t, D_MODEL), gt_f[1], row(final_norm), s)
    return out.reshape(b, s, D_MODEL)
```

```python
import functools

import jax
import jax.numpy as jnp
import numpy as np
from jax import lax
from jax.experimental import pallas as pl
from jax.experimental.pallas import tpu as pltpu

D_MODEL = 1024
DEPTH = 2
HEAD_DIM = 64
HEADS_PER_GROUP = 5
N_ATTN_GROUPS = 3
DILATIONS = (1, 4, 16)
BAND_RADIUS = 64
ROT_DIM = HEAD_DIM // 4
ROPE_THETA = 500000.0
NEG_INF = -1e30
CHUNK = 128
GMLP_HALF = 2 * D_MODEL
GMLP_GROUPS = 8
N_EXPERT_GROUPS = 4
EXPERTS_PER_GROUP = 4
N_EXPERTS = 16
D_EXPERT = 256
EPS = 1e-6

LANES = 128
GROUP_WIDTH = 384
ROUTER_ROWS = 32
PAIRS_PER_GROUP = 6
N_CLASSES = N_EXPERT_GROUPS * PAIRS_PER_GROUP
CLASS_ROWS = 32
SUBLANES = 8
TOKEN_ROWS = D_MODEL // LANES
MOE_TILE = 512

F32 = jnp.float32
BF16 = jnp.bfloat16
VMEM_LIMIT = 56 * 1024 * 1024


def _cparams(sem):
    return pltpu.CompilerParams(dimension_semantics=sem, vmem_limit_bytes=VMEM_LIMIT)


def _const_spec(shape, index_map):
    return pl.BlockSpec(shape, index_map, pipeline_mode=pl.Buffered(1))


def _silu(x):
    return x * (1.0 / (1.0 + jnp.exp(-x)))


def _gelu_x2(x):
    return x * (1.0 + lax.erf(x * (2.0 ** -0.5)))


def _norm_modulate(x, gain, shift, scale):
    ms = jnp.mean(x * x, axis=-1, keepdims=True)
    return (x * lax.rsqrt(ms + EPS) * gain) * (1.0 + scale) + shift


def _split_bf16(x):
    hi = x.astype(BF16)
    return hi, (x - hi.astype(F32)).astype(BF16)


def _ada_kernel(c_ref, w_ref, b_ref, o_ref):
    ca_hi, ca_lo = _split_bf16(_silu(c_ref[...]))
    w_hi, w_lo = _split_bf16(w_ref[0])
    dot = functools.partial(jnp.dot, preferred_element_type=F32)
    o_ref[0] = dot(ca_hi, w_hi) + dot(ca_lo, w_hi) + dot(ca_hi, w_lo) + b_ref[0]


def _ada_mod(c, w_ada, b_ada):
    b = c.shape[0]
    tn = 1536
    c8 = jnp.pad(c, ((0, 8 - b), (0, 0)))
    out = pl.pallas_call(
        _ada_kernel,
        grid=(DEPTH, 6 * D_MODEL // tn),
        in_specs=[pl.BlockSpec((8, D_MODEL), lambda l, j: (0, 0)),
                  pl.BlockSpec((1, D_MODEL, tn), lambda l, j: (l, 0, j)),
                  pl.BlockSpec((1, 1, tn), lambda l, j: (l, 0, j))],
        out_specs=pl.BlockSpec((1, 8, tn), lambda l, j: (l, 0, j)),
        out_shape=jax.ShapeDtypeStruct((DEPTH, 8, 6 * D_MODEL), F32),
        compiler_params=_cparams(("arbitrary", "arbitrary")),
        name="ada_mod",
    )(c8, w_ada, b_ada.reshape(DEPTH, 1, 6 * D_MODEL))
    return out[:, :b].reshape(DEPTH, b, 6, 1, D_MODEL)


def _router_logits(hf, wr_ref, br_ref):
    return lax.dot_general(wr_ref[...], hf, (((1,), (1,)), ((), ())), preferred_element_type=F32) + br_ref[...]


def _group_choice(lt):
    gl = [lt[i:i + 1, :] for i in range(N_EXPERT_GROUPS)]
    gmax = jnp.maximum(jnp.maximum(gl[0], gl[1]), jnp.maximum(gl[2], gl[3]))
    ge = [jnp.exp(g - gmax) for g in gl]
    gsum = ge[0] + ge[1] + ge[2] + ge[3]
    gp = [e / gsum for e in ge]
    g_val = jnp.maximum(jnp.maximum(gp[0], gp[1]), jnp.maximum(gp[2], gp[3]))
    g_idx = jnp.where(gp[0] == g_val, 0, jnp.where(gp[1] == g_val, 1, jnp.where(gp[2] == g_val, 2, 3)))
    return g_val, g_idx


def _ffn_prep(x1, gain, shift, scale, wr_ref, br_ref):
    hf32 = _norm_modulate(x1, gain, shift, scale)
    lt = _router_logits(hf32.astype(BF16), wr_ref, br_ref)
    _, g_idx = _group_choice(lt)
    el = []
    for j in range(EXPERTS_PER_GROUP):
        rows = [lt[4 + 4 * g + j:5 + 4 * g + j, :] for g in range(N_EXPERT_GROUPS)]
        el.append(jnp.where(g_idx == 0, rows[0], jnp.where(g_idx == 1, rows[1],
                  jnp.where(g_idx == 2, rows[2], rows[3]))))
    v1 = jnp.maximum(jnp.maximum(el[0], el[1]), jnp.maximum(el[2], el[3]))
    i1 = jnp.where(el[0] == v1, 0, jnp.where(el[1] == v1, 1, jnp.where(el[2] == v1, 2, 3)))
    el2 = [jnp.where(i1 == j, -jnp.inf, el[j]) for j in range(EXPERTS_PER_GROUP)]
    v2 = jnp.maximum(jnp.maximum(el2[0], el2[1]), jnp.maximum(el2[2], el2[3]))
    i2 = jnp.where((el2[0] == v2) & (i1 != 0), 0,
                   jnp.where((el2[1] == v2) & (i1 != 1), 1,
                             jnp.where((el2[2] == v2) & (i1 != 2), 2, 3)))
    a = jnp.minimum(i1, i2)
    bb = jnp.maximum(i1, i2)
    cls = g_idx * PAIRS_PER_GROUP + ((a * (7 - a)) >> 1) + (bb - a - 1)
    return hf32, cls


def _place_tokens(cls, carry_ref, cur_ref, tab_ref, first_step):
    tm = cls.shape[1]

    @pl.when(first_step)
    def _():
        carry_ref[...] = jnp.zeros_like(carry_ref)
        cur_ref[...] = jnp.zeros_like(cur_ref)
        tab_ref[...] = jnp.zeros_like(tab_ref)

    crow = lax.broadcasted_iota(jnp.int32, (CLASS_ROWS, tm), 0)
    onehot = (crow == cls).astype(F32)
    si = lax.broadcasted_iota(jnp.int32, (tm, tm), 0)
    ti = lax.broadcasted_iota(jnp.int32, (tm, tm), 1)
    before = (si < ti).astype(BF16)
    prefix = jnp.dot(onehot.astype(BF16), before, preferred_element_type=F32)
    seen = prefix + carry_ref[:, 0:1]
    rank = jnp.sum(onehot * seen, axis=0, keepdims=True)
    inv_tile = 1.0 / MOE_TILE
    rem = rank - MOE_TILE * jnp.floor(rank * inv_tile)
    opens = (rem == 0.0).astype(F32)
    opened_before = jnp.sum(jnp.floor((seen + (MOE_TILE - 1)) * inv_tile), axis=0, keepdims=True)
    sel = onehot * opens
    opened = jnp.sum(sel, axis=1, keepdims=True) > 0.0
    new_tile = jnp.sum(sel * opened_before, axis=1, keepdims=True)
    first_rank = jnp.sum(sel * rank, axis=1, keepdims=True)
    tile = jnp.sum(onehot * jnp.where(opened & (seen >= first_rank), new_tile, cur_ref[:, 0:1]),
                   axis=0, keepdims=True)
    carry_ref[...] = carry_ref[...] + jnp.sum(onehot, axis=1, keepdims=True)
    cur_ref[...] = jnp.where(opened, new_tile, cur_ref[...])
    lane = lax.broadcasted_iota(jnp.int32, (CLASS_ROWS, LANES), 1).astype(F32)
    cidx = lax.broadcasted_iota(jnp.int32, (CLASS_ROWS, LANES), 0).astype(F32)
    tab_ref[...] = tab_ref[...] + jnp.where(opened & (lane == new_tile), cidx, 0.0)
    return (tile * MOE_TILE + rem).astype(jnp.int32)


def _rope(t, cos, sgn, first_half):
    out = []
    for c in range(GROUP_WIDTH // LANES):
        xc = t[:, c * LANES:(c + 1) * LANES]
        other = jnp.where(first_half, pltpu.roll(xc, LANES - ROT_DIM // 2, 1),
                          pltpu.roll(xc, ROT_DIM // 2, 1))
        out.append(xc * cos + other * sgn)
    return jnp.concatenate(out, axis=1)


def _qkv_kernel(x_ref, gain_ref, sh_ref, sc_ref, w_ref, cos_ref, sgn_ref, *rest, tm):
    out_refs, hn_scr = rest[:9], rest[9]
    hn = _norm_modulate(x_ref[0], gain_ref[...], sh_ref[0, 0], sc_ref[0, 0])
    n_slab = D_MODEL // LANES
    lane = lax.broadcasted_iota(jnp.int32, (tm, LANES), 1) % HEAD_DIM
    first_half = lane < ROT_DIM // 2
    for g, d in enumerate(DILATIONS):
        rows = tm // d
        if d == 1:
            hp, cos, sgn = hn, cos_ref[...], sgn_ref[...]
        else:
            if g == 1:
                for s in range(n_slab):
                    hn_scr[s] = hn[:, s * LANES:(s + 1) * LANES]
            hp = jnp.concatenate(
                [jnp.concatenate([hn_scr[s, pl.ds(r, rows, stride=d), :] for s in range(n_slab)], axis=1)
                 for r in range(d)], axis=0)
            cos = jnp.concatenate([cos_ref[pl.ds(r, rows, stride=d), :] for r in range(d)], axis=0)
            sgn = jnp.concatenate([sgn_ref[pl.ds(r, rows, stride=d), :] for r in range(d)], axis=0)
        res = jnp.dot(hp.astype(BF16), w_ref[g], preferred_element_type=F32)
        q = _rope(res[:, :GROUP_WIDTH], cos, sgn, first_half)
        k = _rope(res[:, GROUP_WIDTH:2 * GROUP_WIDTH], cos, sgn, first_half)
        v = res[:, 2 * GROUP_WIDTH:]
        for t, o_ref in zip((q, k, v), out_refs[3 * g:3 * g + 3]):
            o_ref[0] = t.astype(BF16).reshape(d, rows, GROUP_WIDTH)


def _qkv(x, gain, shift, scale, w_groups, cos_t, sgn_t, tm=1024):
    b, s, _ = x.shape
    out_shape, out_specs = [], []
    for d in DILATIONS:
        for _ in range(3):
            out_shape.append(jax.ShapeDtypeStruct((b, d, s // d, GROUP_WIDTH), BF16))
            out_specs.append(pl.BlockSpec((1, d, tm // d, GROUP_WIDTH), lambda bi, i: (bi, 0, i, 0)))
    return pl.pallas_call(
        functools.partial(_qkv_kernel, tm=tm),
        grid=(b, s // tm),
        in_specs=[pl.BlockSpec((1, tm, D_MODEL), lambda bi, i: (bi, i, 0)),
                  _const_spec((1, D_MODEL), lambda bi, i: (0, 0)),
                  pl.BlockSpec((1, 1, 1, D_MODEL), lambda bi, i: (bi, 0, 0, 0)),
                  pl.BlockSpec((1, 1, 1, D_MODEL), lambda bi, i: (bi, 0, 0, 0)),
                  _const_spec((N_ATTN_GROUPS, D_MODEL, 3 * GROUP_WIDTH), lambda bi, i: (0, 0, 0)),
                  pl.BlockSpec((tm, LANES), lambda bi, i: (i, 0)),
                  pl.BlockSpec((tm, LANES), lambda bi, i: (i, 0))],
        out_specs=out_specs,
        out_shape=out_shape,
        scratch_shapes=[pltpu.VMEM((D_MODEL // LANES, tm, LANES), F32)],
        compiler_params=_cparams(("arbitrary", "arbitrary")),
        name="qkv_rope",
    )(x, gain, shift, scale, w_groups, cos_t, sgn_t)


def _attn_kernel(q_ref, kp_ref, kc_ref, kn_ref, vp_ref, vc_ref, vn_ref, o_ref, lse_ref,
                 kbuf, vbuf, *, seq, tq):
    i = pl.program_id(2)
    r = BAND_RADIUS
    kbuf[0:r] = kp_ref[0, 0]
    kbuf[r:r + tq] = kc_ref[0, 0]
    kbuf[r + tq:] = kn_ref[0, 0]
    vbuf[0:r] = vp_ref[0, 0]
    vbuf[r:r + tq] = vc_ref[0, 0]
    vbuf[r + tq:] = vn_ref[0, 0]
    qb = 2 * r
    kw = qb + 2 * r
    lane = lax.broadcasted_iota(jnp.int32, (qb, LANES), 1)
    low = lane < HEAD_DIM
    tidx = lax.broadcasted_iota(jnp.int32, (qb, kw), 0)
    kidx = lax.broadcasted_iota(jnp.int32, (qb, kw), 1)
    rel = kidx - tidx
    band = (rel >= 0) & (rel <= 2 * r)
    for j in range(tq // qb):
        kpos = i * tq + (j * qb - r) + kidx
        mask = band & (kpos >= 0) & (kpos < seq)
        q = q_ref[0, 0, j * qb:(j + 1) * qb, :]
        k = kbuf[j * qb:j * qb + kw, :]
        v = vbuf[j * qb:j * qb + kw, :]
        lse_tile = jnp.zeros((qb, LANES), F32)
        o_chunks = []
        for c in range(GROUP_WIDTH // LANES):
            qc = q[:, c * LANES:(c + 1) * LANES]
            kc = k[:, c * LANES:(c + 1) * LANES]
            vc = v[:, c * LANES:(c + 1) * LANES]
            outs = []
            for hh in range(2):
                head = 2 * c + hh
                if head >= HEADS_PER_GROUP:
                    outs.append(jnp.zeros((qb, LANES), F32))
                    continue
                qm = jnp.where(low if hh == 0 else jnp.logical_not(low), qc, jnp.zeros_like(qc))
                s = lax.dot_general(qm, kc, (((1,), (1,)), ((), ())), preferred_element_type=F32)
                s = jnp.where(mask, s, NEG_INF)
                m = jnp.max(s, axis=-1, keepdims=True)
                p = jnp.exp(s - m)
                den = jnp.sum(p, axis=-1, keepdims=True)
                o = jnp.dot(p.astype(BF16), vc, preferred_element_type=F32) / den
                lse_tile = jnp.where(lane == head, m + jnp.log(den), lse_tile)
                outs.append(o)
            o_chunks.append(jnp.where(low, outs[0], outs[1]))
        o_ref[0, 0, j * qb:(j + 1) * qb, :] = jnp.concatenate(o_chunks, axis=1).astype(BF16)
        lse_ref[0, 0, j * qb:(j + 1) * qb, :] = lse_tile


def _attention(q, k, v, tq=1024):
    b, d, seq, w = q.shape
    r = BAND_RADIUS
    nb = seq // r
    cur = pl.BlockSpec((1, 1, tq, w), lambda bi, ri, i: (bi, ri, i, 0))
    prev = pl.BlockSpec((1, 1, r, w), lambda bi, ri, i: (bi, ri, jnp.maximum(i * (tq // r) - 1, 0), 0))
    nxt = pl.BlockSpec((1, 1, r, w), lambda bi, ri, i: (bi, ri, jnp.minimum((i + 1) * (tq // r), nb - 1), 0))
    return pl.pallas_call(
        functools.partial(_attn_kernel, seq=seq, tq=tq),
        grid=(b, d, seq // tq),
        in_specs=[cur, prev, cur, nxt, prev, cur, nxt],
        out_specs=[pl.BlockSpec((1, 1, tq, w), lambda bi, ri, i: (bi, ri, i, 0)),
                   pl.BlockSpec((1, 1, tq, LANES), lambda bi, ri, i: (bi, ri, i, 0))],
        out_shape=[jax.ShapeDtypeStruct((b, d, seq, w), BF16),
                   jax.ShapeDtypeStruct((b, d, seq, LANES), F32)],
        scratch_shapes=[pltpu.VMEM((tq + 2 * r, w), BF16), pltpu.VMEM((tq + 2 * r, w), BF16)],
        compiler_params=_cparams(("arbitrary", "arbitrary", "arbitrary")),
        name=f"band_attn_d{d}",
    )(q, k, k, k, v, v, v)


def _attn_out_kernel(o0, o1, o2, l0, l1, l2, x_ref, wo_ref, exp_ref, gt_ref, gain_ref, sh_ref, sc_ref,
                     wr_ref, br_ref, x1_ref, xs_hbm, pos_ref, tab_out, cnt_out, oscr, lscr, *route_scratch,
                     tm, n_steps):
    n_slab = GROUP_WIDTH // LANES
    for g, (d, o_ref, l_ref) in enumerate(zip(DILATIONS, (o0, o1, o2), (l0, l1, l2))):
        rows = tm // d
        for r in range(d):
            dst = slice(None) if d == 1 else pl.ds(r, rows, stride=d)
            blk = o_ref[0, r].astype(F32)
            for c in range(n_slab):
                oscr[g * n_slab + c, dst, :] = blk[:, c * LANES:(c + 1) * LANES]
            lscr[g, dst, :] = l_ref[0, r]
    lse = [lscr[g] for g in range(N_ATTN_GROUPS)]
    mx = jnp.maximum(jnp.maximum(lse[0], lse[1]), lse[2])
    ex = [jnp.exp(l - mx) for l in lse]
    tot = ex[0] + ex[1] + ex[2]
    y = jnp.zeros((tm, D_MODEL), F32)
    for g in range(N_ATTN_GROUPS):
        alpha = ex[g] / tot
        a_hi, a_lo = _split_bf16(alpha)
        a_wide = (jnp.dot(a_hi, exp_ref[...], preferred_element_type=F32)
                  + jnp.dot(a_lo, exp_ref[...], preferred_element_type=F32))
        og = jnp.concatenate([oscr[g * n_slab + c] for c in range(n_slab)], axis=1)
        y = y + jnp.dot((og * a_wide).astype(BF16), wo_ref[g], preferred_element_type=F32)
    x1 = x_ref[0] + gt_ref[0, 0] * y
    x1_ref[0] = x1
    _route_outputs(x1, gain_ref, sh_ref, sc_ref, wr_ref, br_ref, xs_hbm, pos_ref, tab_out, cnt_out, route_scratch,
                   n_steps)


ZERO_TOKENS = MOE_TILE // 2


def _route_outputs(x1, gain_ref, sh_ref, sc_ref, wr_ref, br_ref, xs_hbm, pos_ref, tab_out, cnt_out, scratch,
                   n_steps):
    (carry_ref, cur_ref, tab_ref, rows_scr, pos_vmem, pos_smem, fin_vmem, fin_smem, zbuf, sems, aux_sem) = scratch
    step = pl.program_id(0) * pl.num_programs(1) + pl.program_id(1)
    tm = x1.shape[0]
    rows = tm * TOKEN_ROWS
    hf32, cls = _ffn_prep(x1, gain_ref[...], sh_ref[0, 0], sc_ref[0, 0], wr_ref, br_ref)
    pos = _place_tokens(cls, carry_ref, cur_ref, tab_ref, step == 0)
    pos_lines = _lines(pos)
    pos_ref[0, 0] = pos_lines
    pos_vmem[...] = pos_lines
    to_smem = pltpu.make_async_copy(pos_vmem, pos_smem, aux_sem)
    to_smem.start()

    def drain(slot):
        pltpu.make_async_copy(rows_scr.at[pl.ds(0, rows), :], xs_hbm.at[pl.ds(0, rows), :], sems.at[slot]).wait()

    slot = step % 2

    @pl.when(step >= 2)
    def _():
        drain(slot)

    base = pl.multiple_of(slot * rows, rows)
    for s in range(TOKEN_ROWS):
        rows_scr[pl.ds(base + s, tm, stride=TOKEN_ROWS), :] = hf32[:, s * LANES:(s + 1) * LANES]
    to_smem.wait()

    def issue(line, lane, priority):
        _token_copy(rows_scr, slot * tm + line * LANES + lane, xs_hbm, pos_smem[line, lane],
                    sems.at[slot]).start(priority=priority)

    _issue_burst(tm, issue, spread=False)
    tab_out[...] = tab_ref[...]
    cnt_out[...] = carry_ref[...]

    @pl.when(step == n_steps - 1)
    def _():
        drain(slot)
        if n_steps > 1:
            drain(1 - slot)
        _clear_unused_rows(carry_ref, cur_ref, xs_hbm, fin_vmem, fin_smem, zbuf, aux_sem)


def _clear_unused_rows(carry_ref, cur_ref, xs_hbm, fin_vmem, fin_smem, zbuf, sem):
    lane = lax.broadcasted_iota(jnp.int32, (CLASS_ROWS, LANES), 1)
    crow = lax.broadcasted_iota(jnp.int32, (CLASS_ROWS, LANES), 0)
    on_lane = lambda col: jnp.sum(jnp.where(lane == crow, col, 0.0), axis=0, keepdims=True)
    counts = carry_ref[...]
    used = jnp.sum(jnp.floor((counts + (MOE_TILE - 1)) * (1.0 / MOE_TILE)), axis=0, keepdims=True)
    r8 = lax.broadcasted_iota(jnp.int32, (SUBLANES, LANES), 0)
    fin = jnp.where(r8 == 0, on_lane(cur_ref[...]), jnp.where(r8 == 1, on_lane(counts), jnp.where(r8 == 2, used, 0.0)))
    fin_vmem[...] = fin.astype(jnp.int32)
    to_smem = pltpu.make_async_copy(fin_vmem, fin_smem, sem)
    to_smem.start()
    zbuf[...] = jnp.zeros_like(zbuf)
    to_smem.wait()

    def zero_copy(first_tok, n_tok):
        d0 = pl.multiple_of(first_tok * TOKEN_ROWS, TOKEN_ROWS)
        return pltpu.make_async_copy(zbuf.at[pl.ds(0, n_tok * TOKEN_ROWS), :],
                                     xs_hbm.at[pl.ds(d0, n_tok * TOKEN_ROWS), :], sem)

    for wait in (False, True):
        for c in range(N_CLASSES):
            count = fin_smem[1, c]
            fill = jnp.where(count > 0, (MOE_TILE - count % MOE_TILE) % MOE_TILE, 0)
            first = fin_smem[0, c] * MOE_TILE + (MOE_TILE - fill)
            piece = ZERO_TOKENS
            while piece >= 1:
                done = (fill // (2 * piece)) * (2 * piece)

                @pl.when((fill // piece) % 2 == 1)
                def _(first=first, done=done, piece=piece):
                    cp = zero_copy(first + done, piece)
                    cp.wait() if wait else cp.start()

                piece //= 2

    def clear_tile(j, carry):
        for half in range(MOE_TILE // ZERO_TOKENS):
            cp = zero_copy(j * MOE_TILE + half * ZERO_TOKENS, ZERO_TOKENS)
            cp.start()
            cp.wait()
        return carry

    lax.fori_loop(fin_smem[2, 0], xs_hbm.shape[0] // (MOE_TILE * TOKEN_ROWS), clear_tile, 0)


def _route_scratch(tm):
    assert tm <= MOE_TILE, "a class may open at most one tile per grid step"
    return [pltpu.VMEM((CLASS_ROWS, LANES), F32), pltpu.VMEM((CLASS_ROWS, LANES), F32),
            pltpu.VMEM((CLASS_ROWS, LANES), F32),
            pltpu.VMEM((2 * tm * TOKEN_ROWS, LANES), F32),
            pltpu.VMEM((SUBLANES, LANES), jnp.int32), pltpu.SMEM((SUBLANES, LANES), jnp.int32),
            pltpu.VMEM((SUBLANES, LANES), jnp.int32), pltpu.SMEM((SUBLANES, LANES), jnp.int32),
            pltpu.VMEM((ZERO_TOKENS * TOKEN_ROWS, LANES), F32),
            pltpu.SemaphoreType.DMA((2,)), pltpu.SemaphoreType.DMA]


def _route_out_specs(b, s, tm):
    n_tiles = b * s // MOE_TILE + N_CLASSES
    specs = [pl.BlockSpec((1, tm, D_MODEL), lambda bi, i: (bi, i, 0)),
             pl.BlockSpec(memory_space=pl.ANY),
             pl.BlockSpec((1, 1, SUBLANES, LANES), lambda bi, i: (bi, i, 0, 0)),
             pl.BlockSpec((CLASS_ROWS, LANES), lambda bi, i: (0, 0)),
             pl.BlockSpec((CLASS_ROWS, LANES), lambda bi, i: (0, 0))]
    shapes = [jax.ShapeDtypeStruct((b, s, D_MODEL), F32),
              jax.ShapeDtypeStruct((n_tiles * MOE_TILE * TOKEN_ROWS, LANES), F32),
              jax.ShapeDtypeStruct((b, s // tm, SUBLANES, LANES), jnp.int32),
              jax.ShapeDtypeStruct((CLASS_ROWS, LANES), F32),
              jax.ShapeDtypeStruct((CLASS_ROWS, LANES), F32)]
    return specs, shapes


def _mod_spec():
    return pl.BlockSpec((1, 1, 1, D_MODEL), lambda bi, i: (bi, 0, 0, 0))


def _attn_out(o_list, l_list, x, wo_groups, expand, gt, gain, shift, scale, wr_t, br, tm=512):
    b, s, _ = x.shape
    o_specs = [pl.BlockSpec((1, d, tm // d, GROUP_WIDTH), lambda bi, i: (bi, 0, i, 0)) for d in DILATIONS]
    l_specs = [pl.BlockSpec((1, d, tm // d, LANES), lambda bi, i: (bi, 0, i, 0)) for d in DILATIONS]
    out_specs, out_shape = _route_out_specs(b, s, tm)
    return pl.pallas_call(
        functools.partial(_attn_out_kernel, tm=tm, n_steps=b * s // tm),
        grid=(b, s // tm),
        in_specs=o_specs + l_specs + [
            pl.BlockSpec((1, tm, D_MODEL), lambda bi, i: (bi, i, 0)),
            _const_spec((N_ATTN_GROUPS, GROUP_WIDTH, D_MODEL), lambda bi, i: (0, 0, 0)),
            _const_spec((LANES, GROUP_WIDTH), lambda bi, i: (0, 0)),
            _mod_spec(),
            _const_spec((1, D_MODEL), lambda bi, i: (0, 0)),
            _mod_spec(), _mod_spec(),
            _const_spec((ROUTER_ROWS, D_MODEL), lambda bi, i: (0, 0)),
            _const_spec((ROUTER_ROWS, 1), lambda bi, i: (0, 0))],
        out_specs=out_specs,
        out_shape=out_shape,
        scratch_shapes=[pltpu.VMEM((N_ATTN_GROUPS * GROUP_WIDTH // LANES, tm, LANES), F32),
                        pltpu.VMEM((N_ATTN_GROUPS, tm, LANES), F32)] + _route_scratch(tm),
        compiler_params=_cparams(("arbitrary", "arbitrary")),
        name="attn_merge_proj",
    )(*o_list, *l_list, x, wo_groups, expand, gt, gain, shift, scale, wr_t, br)


DMA_BURST_UNROLL = 16
BACKGROUND_PRIORITY = 1


def _issue_burst(n, issue, spread=True):
    for line in range(n // LANES):
        def body(i, carry, line=line):
            for k in range(DMA_BURST_UNROLL):
                issue(line, i * DMA_BURST_UNROLL + k, k % 2 if spread else BACKGROUND_PRIORITY)
            return carry

        lax.fori_loop(0, LANES // DMA_BURST_UNROLL, body, 0)


def _lines(row):
    n = row.shape[1]
    line = lax.broadcasted_iota(jnp.int32, (SUBLANES, LANES), 0)
    out = jnp.zeros((SUBLANES, LANES), jnp.int32)
    for j in range(n // LANES):
        out = jnp.where(line == j, row[:, j * LANES:(j + 1) * LANES], out)
    return out


class _RowGather:
    def __init__(self, pos_ref, pos_next_ref, ys_hbm, ybuf, sems, step, n_steps, tm, spread):
        self.pos_ref, self.pos_next_ref, self.ys_hbm, self.ybuf, self.sems = pos_ref, pos_next_ref, ys_hbm, ybuf, sems
        self.step, self.n_steps, self.tm, self.spread = step, n_steps, tm, spread

    def _request(self, p_ref, slot):
        def issue(line, lane, priority):
            _token_copy(self.ys_hbm, p_ref[0, line, lane], self.ybuf, slot * self.tm + line * LANES + lane,
                        self.sems.at[slot]).start(priority=priority)

        _issue_burst(self.tm, issue, self.spread)

    def request_next(self):
        @pl.when(self.step + 1 < self.n_steps)
        def _():
            self._request(self.pos_next_ref, (self.step + 1) % 2)

    def current(self):
        rows = self.tm * TOKEN_ROWS

        @pl.when(self.step == 0)
        def _():
            self._request(self.pos_ref, 0)

        slot = self.step % 2
        base = pl.multiple_of(slot * rows, rows)
        pltpu.make_async_copy(self.ys_hbm.at[pl.ds(0, rows), :], self.ybuf.at[pl.ds(base, rows), :],
                              self.sems.at[slot]).wait()
        return jnp.concatenate([self.ybuf[pl.ds(base + s, self.tm, stride=TOKEN_ROWS), :]
                                for s in range(D_MODEL // LANES)], axis=1)


def _gather_specs(n_steps, tm, index_of):
    return [pl.BlockSpec((1, SUBLANES, LANES), lambda *g: (index_of(*g), 0, 0), memory_space=pltpu.SMEM),
            pl.BlockSpec((1, SUBLANES, LANES), lambda *g: (jnp.minimum(index_of(*g) + 1, n_steps - 1), 0, 0),
                         memory_space=pltpu.SMEM),
            pl.BlockSpec(memory_space=pl.ANY)]


def _pos_lines(pos, tm):
    lines = pos.reshape(-1, tm // LANES, LANES)
    return jnp.pad(lines, ((0, 0), (0, SUBLANES - tm // LANES), (0, 0)))


def _gather_scratch(tm):
    return [pltpu.VMEM((2 * tm * TOKEN_ROWS, LANES), F32), pltpu.SemaphoreType.DMA((2,))]


def _gmlp_kernel(pos_ref, pos_next_ref, ys_hbm, gtf_ref, x_ref, gain_ref, sh_ref, sc_ref, win_ref, vg_ref, ws_ref,
                 bs_ref, wo_ref, gt_ref, fgain_ref, fsh_ref, fsc_ref, wr_ref, br_ref, x1_ref, xs_hbm, pos_out, tab_out,
                 cnt_out, gated_scr, ybuf, sems, *route_scratch, tm, n_steps):
    step = pl.program_id(0) * pl.num_programs(1) + pl.program_id(1)
    gather = _RowGather(pos_ref, pos_next_ref, ys_hbm, ybuf, sems, step, n_steps, tm, spread=False)
    gather.request_next()
    x = x_ref[0] + gtf_ref[0, 0] * gather.current()
    hn = _norm_modulate(x, gain_ref[...], sh_ref[0, 0], sc_ref[0, 0]).astype(BF16)
    v = _gelu_x2(jnp.dot(hn, win_ref[:, GMLP_HALF:], preferred_element_type=F32))
    u = _gelu_x2(jnp.dot(hn, win_ref[:, :GMLP_HALF], preferred_element_type=F32))
    mu = jnp.mean(v, axis=-1, keepdims=True)
    vc = v - mu
    vn = (vc * lax.rsqrt(jnp.mean(vc * vc, axis=-1, keepdims=True) + 4.0 * EPS) * vg_ref[...]).astype(BF16)
    gw = GMLP_HALF // GMLP_GROUPS
    for c in range(tm // CHUNK):
        rs = slice(c * CHUNK, (c + 1) * CHUNK)
        for g in range(GMLP_GROUPS):
            cs = slice(g * gw, (g + 1) * gw)
            vs = jnp.dot(ws_ref[g], vn[rs, cs], preferred_element_type=F32) + bs_ref[:, g:g + 1]
            gated_scr[rs, cs] = (u[rs, cs] * vs).astype(BF16)
    y = jnp.dot(gated_scr[...], wo_ref[...], preferred_element_type=F32)
    x1 = x + gt_ref[0, 0] * y
    x1_ref[0] = x1
    _route_outputs(x1, fgain_ref, fsh_ref, fsc_ref, wr_ref, br_ref, xs_hbm, pos_out, tab_out, cnt_out,
                   route_scratch, n_steps)


def _gmlp(ys, pos, gt_prev, x, gain, shift, scale, w_in, v_gain, w_s, b_s_t, w_o, gt, fgain, fshift, fscale,
          wr_t, br, tm=512):
    b, s, _ = x.shape
    tok = lambda w: pl.BlockSpec((1, tm, w), lambda bi, i: (bi, i, 0))
    c2 = lambda shape: _const_spec(shape, lambda bi, i: (0,) * len(shape))
    out_specs, out_shape = _route_out_specs(b, s, tm)
    n_steps = b * s // tm
    return pl.pallas_call(
        functools.partial(_gmlp_kernel, tm=tm, n_steps=n_steps),
        grid=(b, s // tm),
        in_specs=_gather_specs(n_steps, tm, lambda bi, i: bi * (s // tm) + i) + [
                  _mod_spec(),
                  tok(D_MODEL), c2((1, D_MODEL)), _mod_spec(), _mod_spec(),
                  c2((D_MODEL, 2 * GMLP_HALF)), c2((1, GMLP_HALF)),
                  c2((GMLP_GROUPS, CHUNK, CHUNK)), c2((CHUNK, GMLP_GROUPS)),
                  c2((GMLP_HALF, D_MODEL)), _mod_spec(),
                  c2((1, D_MODEL)), _mod_spec(), _mod_spec(),
                  c2((ROUTER_ROWS, D_MODEL)), c2((ROUTER_ROWS, 1))],
        out_specs=out_specs,
        out_shape=out_shape,
        scratch_shapes=[pltpu.VMEM((tm, GMLP_HALF), BF16)] + _gather_scratch(tm) + _route_scratch(tm),
        compiler_params=_cparams(("arbitrary", "arbitrary")),
        name="gmlp",
    )(_pos_lines(pos, tm), _pos_lines(pos, tm), ys, gt_prev,
      x, gain, shift, scale, w_in, v_gain, w_s, b_s_t, w_o, gt, fgain, fshift, fscale, wr_t, br)


def _token_copy(src, src_tok, dst, dst_tok, sem):
    s0 = pl.multiple_of(src_tok * TOKEN_ROWS, TOKEN_ROWS)
    d0 = pl.multiple_of(dst_tok * TOKEN_ROWS, TOKEN_ROWS)
    return pltpu.make_async_copy(src.at[pl.ds(s0, TOKEN_ROWS), :], dst.at[pl.ds(d0, TOKEN_ROWS), :], sem)


def _slabs(ref, n_tok, n_slab):
    return jnp.concatenate([ref[pl.ds(s, n_tok, stride=TOKEN_ROWS), :] for s in range(n_slab)], axis=1)


def _expert_kernel(ea_ref, eb_ref, blk_ref, nused_ref, x_ref, wr_ref, br_ref, wga, wgb, wua, wub, wda, wdb, y_ref,
                   lt_scr):
    j = pl.program_id(0)

    @pl.when(j < nused_ref[0])
    def _():
        h = _slabs(x_ref, MOE_TILE, TOKEN_ROWS).astype(BF16)
        lt = _router_logits(h, wr_ref, br_ref)
        g_val, _ = _group_choice(lt)
        lt_scr[...] = lt
        el_a = lt_scr[pl.ds(N_EXPERT_GROUPS + ea_ref[j], 1), :]
        el_b = lt_scr[pl.ds(N_EXPERT_GROUPS + eb_ref[j], 1), :]
        top = jnp.maximum(el_a, el_b)
        p_a = jnp.exp(el_a - top)
        p_b = jnp.exp(el_b - top)
        row = lax.broadcasted_iota(jnp.int32, (LANES, MOE_TILE), 0)
        gates_t = jnp.where(row == 0, g_val * (p_a / (p_a + p_b)), jnp.where(row == 1, g_val * (p_b / (p_a + p_b)), 0.0))
        gates = gates_t.T
        y = jnp.zeros((MOE_TILE, D_MODEL), F32)
        for gate, wg, wu, wd in ((gates[:, 0:1], wga, wua, wda), (gates[:, 1:2], wgb, wub, wdb)):
            hg = jnp.dot(h, wg[0, 0], preferred_element_type=F32)
            hu = jnp.dot(h, wu[0, 0], preferred_element_type=F32)
            act = (_silu(hg) * hu * gate).astype(BF16)
            y = y + jnp.dot(act, wd[0, 0], preferred_element_type=F32)
        for s in range(TOKEN_ROWS):
            y_ref[pl.ds(s, MOE_TILE, stride=TOKEN_ROWS), :] = y[:, s * LANES:(s + 1) * LANES]

    @pl.when(j >= nused_ref[0])
    def _():
        y_ref[...] = jnp.zeros_like(y_ref)


def _experts(xs, tile_ea, tile_eb, tile_blk, n_used, wr_t, br, wg, wu, wd, layer):
    n_tiles = xs.shape[0] // (MOE_TILE * TOKEN_ROWS)
    up = lambda sel: pl.BlockSpec((1, 1, D_MODEL, D_EXPERT),
                                  lambda j, ea, eb, blk, nu: (layer, (ea, eb)[sel][j], 0, 0))
    down = lambda sel: pl.BlockSpec((1, 1, D_EXPERT, D_MODEL),
                                    lambda j, ea, eb, blk, nu: (layer, (ea, eb)[sel][j], 0, 0))
    shape = (MOE_TILE * TOKEN_ROWS, LANES)
    return pl.pallas_call(
        _expert_kernel,
        grid_spec=pltpu.PrefetchScalarGridSpec(
            num_scalar_prefetch=4,
            grid=(n_tiles,),
            in_specs=[pl.BlockSpec(shape, lambda j, ea, eb, blk, nu: (blk[j], 0)),
                      pl.BlockSpec((ROUTER_ROWS, D_MODEL), lambda j, ea, eb, blk, nu: (0, 0)),
                      pl.BlockSpec((ROUTER_ROWS, 1), lambda j, ea, eb, blk, nu: (0, 0)),
                      up(0), up(1), up(0), up(1), down(0), down(1)],
            out_specs=pl.BlockSpec(shape, lambda j, ea, eb, blk, nu: (j, 0)),
            scratch_shapes=[pltpu.VMEM((ROUTER_ROWS, MOE_TILE), F32)]),
        out_shape=jax.ShapeDtypeStruct(xs.shape, F32),
        compiler_params=_cparams(("arbitrary",)),
        name="moe_experts",
    )(tile_ea, tile_eb, tile_blk, n_used, xs, wr_t, br, wg, wg, wu, wu, wd, wd)


def _final_kernel(pos_ref, pos_next_ref, ys_hbm, x_ref, gt_ref, fin_ref, o_ref, ybuf, sems, *, tm):
    gather = _RowGather(pos_ref, pos_next_ref, ys_hbm, ybuf, sems, pl.program_id(0), pl.num_programs(0), tm,
                        spread=True)
    gather.request_next()
    y_moe = gather.current()
    x2 = x_ref[...] + gt_ref[0, 0] * y_moe
    ms = jnp.mean(x2 * x2, axis=-1, keepdims=True)
    o_ref[...] = x2 * lax.rsqrt(ms + EPS) * fin_ref[...]


def _final(ys, pos, x, gt, fin_gain, seq, tm=512):
    t = x.shape[0]
    tiles_per_batch = seq // tm
    n_steps = t // tm
    return pl.pallas_call(
        functools.partial(_final_kernel, tm=tm),
        grid=(n_steps,),
        in_specs=_gather_specs(n_steps, tm, lambda i: i) + [
                  pl.BlockSpec((tm, D_MODEL), lambda i: (i, 0)),
                  pl.BlockSpec((1, 1, 1, D_MODEL), lambda i: (i // tiles_per_batch, 0, 0, 0)),
                  pl.BlockSpec((1, D_MODEL), lambda i: (0, 0))],
        out_specs=pl.BlockSpec((tm, D_MODEL), lambda i: (i, 0)),
        out_shape=jax.ShapeDtypeStruct((t, D_MODEL), F32),
        scratch_shapes=_gather_scratch(tm),
        compiler_params=_cparams(("arbitrary",)),
        name="moe_combine_final_norm",
    )(_pos_lines(pos, tm), _pos_lines(pos, tm), ys, x, gt, fin_gain)


_PAIR_A = np.array([0, 0, 0, 1, 1, 2], np.int32)
_PAIR_B = np.array([1, 2, 3, 2, 3, 3], np.int32)


def _moe_sorted(xs, tab, counts, wr_t, br, wg, wu, wd, layer):
    n_tiles = xs.shape[0] // (MOE_TILE * TOKEN_ROWS)
    cnt = counts[:N_CLASSES, 0].astype(jnp.int32)
    n_used = jnp.sum((cnt + MOE_TILE - 1) // MOE_TILE)
    ids = jnp.arange(n_tiles, dtype=jnp.int32)
    tile_cls = jnp.sum(tab, axis=0)[:n_tiles].astype(jnp.int32)
    last_cls = jnp.sum(jnp.where(ids == n_used - 1, tile_cls, 0))
    tile_cls = jnp.where(ids < n_used, tile_cls, last_cls)
    j = jnp.minimum(ids, n_used - 1)
    pair = tile_cls % PAIRS_PER_GROUP
    pair_a = jnp.sum(jnp.where(pair[:, None] == jnp.arange(PAIRS_PER_GROUP)[None, :], _PAIR_A[None, :], 0), axis=1)
    pair_b = jnp.sum(jnp.where(pair[:, None] == jnp.arange(PAIRS_PER_GROUP)[None, :], _PAIR_B[None, :], 0), axis=1)
    base = (tile_cls // PAIRS_PER_GROUP) * EXPERTS_PER_GROUP
    return _experts(xs, base + pair_a, base + pair_b, j, n_used.reshape(1), wr_t, br, wg, wu, wd, layer)


def _rope_tables(seq):
    inv_freq = ROPE_THETA ** (-jnp.arange(0, ROT_DIM, 2, dtype=jnp.float32) / ROT_DIM)
    ang = jnp.arange(seq).astype(jnp.float32)[:, None] * inv_freq[None, :]
    cos, sin = jnp.cos(ang), jnp.sin(ang)
    ones = jnp.ones((seq, HEAD_DIM - ROT_DIM), jnp.float32)
    cos_h = jnp.concatenate([cos, cos, ones], axis=1)
    sgn_h = jnp.concatenate([-sin, sin, 0.0 * ones], axis=1)
    reps = LANES // HEAD_DIM
    return jnp.tile(cos_h, (1, reps)).astype(F32), jnp.tile(sgn_h, (1, reps)).astype(F32)


def _pad_heads(w, axis):
    pad = [(0, 0)] * w.ndim
    pad[axis] = (0, GROUP_WIDTH - HEADS_PER_GROUP * HEAD_DIM)
    return jnp.pad(w, pad)


def _router_weights(w_group, b_group, w_expert, b_expert):
    we = jnp.transpose(w_expert, (1, 0, 2)).reshape(D_MODEL, N_EXPERTS)
    w = jnp.concatenate([w_group, we], axis=1)
    w = jnp.pad(w, ((0, 0), (0, ROUTER_ROWS - w.shape[1])))
    bias = jnp.pad(jnp.concatenate([b_group, b_expert.reshape(-1)]), (0, ROUTER_ROWS - 20))
    return w.T.astype(BF16), bias.reshape(ROUTER_ROWS, 1)


def kernel(x, c, norm_mix, norm_ffn, w_ada, b_ada, a_w_qkv, a_w_o, b_w_in, b_v_gain, b_w_s, b_b_s, b_w_o,
           r_w_group, r_b_group, r_w_expert, r_b_expert, e_w_gate, e_w_up, e_w_down, final_norm):
    b, s, _ = x.shape
    mod = _ada_mod(c, w_ada, b_ada)
    sh_m, sc_m, gt_m, sh_f, sc_f, gt_f = [mod[:, :, i:i + 1] for i in range(6)]
    row = lambda v: v.reshape(1, -1)
    gw = HEADS_PER_GROUP * HEAD_DIM
    e_w_gate, e_w_up, e_w_down = [w.astype(BF16) for w in (e_w_gate, e_w_up, e_w_down)]

    w = a_w_qkv[0].reshape(D_MODEL, 3, N_ATTN_GROUPS, gw)
    w = w * jnp.asarray([HEAD_DIM ** -0.5, 1.0, 1.0], F32)[None, :, None, None]
    w_groups = jnp.transpose(_pad_heads(w, 3), (2, 0, 1, 3)).reshape(
        N_ATTN_GROUPS, D_MODEL, 3 * GROUP_WIDTH).astype(BF16)
    wo_groups = _pad_heads(a_w_o[0].reshape(N_ATTN_GROUPS, gw, D_MODEL), 1).astype(BF16)
    expand = (jnp.arange(LANES)[:, None] == jnp.arange(GROUP_WIDTH)[None, :] // HEAD_DIM).astype(BF16)
    cos_t, sgn_t = _rope_tables(s)
    qkv = _qkv(x, row(norm_mix[0]), sh_m[0], sc_m[0], w_groups, cos_t, sgn_t)
    o_list, l_list = [], []
    for g in range(N_ATTN_GROUPS):
        o, l = _attention(*qkv[3 * g:3 * g + 3])
        o_list.append(o)
        l_list.append(l)
    wr_t, br = _router_weights(r_w_group[0], r_b_group[0], r_w_expert[0], r_b_expert[0])
    x1, xs, pos, tab, counts = _attn_out(o_list, l_list, x, wo_groups, expand, gt_m[0], row(norm_ffn[0]),
                                         sh_f[0], sc_f[0], wr_t, br)
    t = b * s
    token_order = lambda p: p[:, :, :t // (p.shape[0] * p.shape[1] * LANES), :].reshape(t)
    ys = _moe_sorted(xs, tab, counts, wr_t, br, e_w_gate, e_w_up, e_w_down, 0)

    wr_t, br = _router_weights(r_w_group[1], r_b_group[1], r_w_expert[1], r_b_expert[1])
    x3, xs, pos, tab, counts = _gmlp(ys, token_order(pos), gt_f[0], x1, row(norm_mix[1]), sh_m[1], sc_m[1],
                                     b_w_in[0].astype(BF16), row(b_v_gain[0]), (0.5 * b_w_s[0]).astype(BF16), 0.5 * b_b_s[0].T,
                                     b_w_o[0].astype(BF16), gt_m[1], row(norm_ffn[1]), sh_f[1], sc_f[1], wr_t, br)
    ys = _moe_sorted(xs, tab, counts, wr_t, br, e_w_gate, e_w_up, e_w_down, 1)
    out = _final(ys, token_order(pos), x3.reshape(t, D_MODEL), gt_f[1], row(final_norm), s)
    return out.reshape(b, s, D_MODEL)
```

```python
import functools

import jax
import jax.numpy as jnp
import numpy as np
from jax import lax
from jax.experimental import pallas as pl
from jax.experimental.pallas import tpu as pltpu

D_MODEL = 1024
DEPTH = 2
HEAD_DIM = 64
HEADS_PER_GROUP = 5
N_ATTN_GROUPS = 3
DILATIONS = (1, 4, 16)
BAND_RADIUS = 64
ROT_DIM = HEAD_DIM // 4
ROPE_THETA = 500000.0
NEG_INF = -1e30
CHUNK = 128
GMLP_HALF = 2 * D_MODEL
GMLP_GROUPS = 8
N_EXPERT_GROUPS = 4
EXPERTS_PER_GROUP = 4
N_EXPERTS = 16
D_EXPERT = 256
EPS = 1e-6

LANES = 128
GROUP_WIDTH = 384
ROUTER_ROWS = 32
PAIRS_PER_GROUP = 6
N_CLASSES = N_EXPERT_GROUPS * PAIRS_PER_GROUP
CLASS_ROWS = 32
SUBLANES = 8
TOKEN_ROWS = D_MODEL // LANES
MOE_TILE = 512

F32 = jnp.float32
BF16 = jnp.bfloat16
VMEM_LIMIT = 56 * 1024 * 1024


def _cparams(sem):
    return pltpu.CompilerParams(dimension_semantics=sem, vmem_limit_bytes=VMEM_LIMIT)


def _const_spec(shape, index_map):
    return pl.BlockSpec(shape, index_map, pipeline_mode=pl.Buffered(1))


def _silu(x):
    return x * (1.0 / (1.0 + jnp.exp(-x)))


def _gelu_x2(x):
    return x * (1.0 + lax.erf(x * (2.0 ** -0.5)))


def _norm_modulate(x, gain, shift, scale):
    ms = jnp.mean(x * x, axis=-1, keepdims=True)
    return (x * lax.rsqrt(ms + EPS)) * (gain * (1.0 + scale)) + shift


def _split_bf16(x):
    hi = x.astype(BF16)
    return hi, (x - hi.astype(F32)).astype(BF16)


def _ada_kernel(c_ref, w_ref, b_ref, o_ref):
    ca_hi, ca_lo = _split_bf16(_silu(c_ref[...]))
    w_hi, w_lo = _split_bf16(w_ref[0])
    dot = functools.partial(jnp.dot, preferred_element_type=F32)
    o_ref[0] = dot(ca_hi, w_hi) + dot(ca_lo, w_hi) + dot(ca_hi, w_lo) + b_ref[0]


def _ada_mod(c, w_ada, b_ada):
    b = c.shape[0]
    tn = 1536
    c8 = jnp.pad(c, ((0, 8 - b), (0, 0)))
    out = pl.pallas_call(
        _ada_kernel,
        grid=(DEPTH, 6 * D_MODEL // tn),
        in_specs=[pl.BlockSpec((8, D_MODEL), lambda l, j: (0, 0)),
                  pl.BlockSpec((1, D_MODEL, tn), lambda l, j: (l, 0, j)),
                  pl.BlockSpec((1, 1, tn), lambda l, j: (l, 0, j))],
        out_specs=pl.BlockSpec((1, 8, tn), lambda l, j: (l, 0, j)),
        out_shape=jax.ShapeDtypeStruct((DEPTH, 8, 6 * D_MODEL), F32),
        compiler_params=_cparams(("arbitrary", "arbitrary")),
        name="ada_mod",
    )(c8, w_ada, b_ada.reshape(DEPTH, 1, 6 * D_MODEL))
    return out[:, :b].reshape(DEPTH, b, 6, 1, D_MODEL)


def _router_logits(hf, wr_ref, br_ref):
    return lax.dot_general(wr_ref[...], hf, (((1,), (1,)), ((), ())), preferred_element_type=F32) + br_ref[...]


def _group_choice(lt):
    gl = [lt[i:i + 1, :] for i in range(N_EXPERT_GROUPS)]
    gmax = jnp.maximum(jnp.maximum(gl[0], gl[1]), jnp.maximum(gl[2], gl[3]))
    ge = [jnp.exp(g - gmax) for g in gl]
    gsum = ge[0] + ge[1] + ge[2] + ge[3]
    gp = [e / gsum for e in ge]
    g_val = jnp.maximum(jnp.maximum(gp[0], gp[1]), jnp.maximum(gp[2], gp[3]))
    g_idx = jnp.where(gp[0] == g_val, 0, jnp.where(gp[1] == g_val, 1, jnp.where(gp[2] == g_val, 2, 3)))
    return g_val, g_idx


def _ffn_prep(x1, gain, shift, scale, wr_ref, br_ref):
    hf32 = _norm_modulate(x1, gain, shift, scale)
    lt = _router_logits(hf32.astype(BF16), wr_ref, br_ref)
    _, g_idx = _group_choice(lt)
    el = []
    for j in range(EXPERTS_PER_GROUP):
        rows = [lt[4 + 4 * g + j:5 + 4 * g + j, :] for g in range(N_EXPERT_GROUPS)]
        el.append(jnp.where(g_idx == 0, rows[0], jnp.where(g_idx == 1, rows[1],
                  jnp.where(g_idx == 2, rows[2], rows[3]))))
    v1 = jnp.maximum(jnp.maximum(el[0], el[1]), jnp.maximum(el[2], el[3]))
    i1 = jnp.where(el[0] == v1, 0, jnp.where(el[1] == v1, 1, jnp.where(el[2] == v1, 2, 3)))
    el2 = [jnp.where(i1 == j, -jnp.inf, el[j]) for j in range(EXPERTS_PER_GROUP)]
    v2 = jnp.maximum(jnp.maximum(el2[0], el2[1]), jnp.maximum(el2[2], el2[3]))
    i2 = jnp.where((el2[0] == v2) & (i1 != 0), 0,
                   jnp.where((el2[1] == v2) & (i1 != 1), 1,
                             jnp.where((el2[2] == v2) & (i1 != 2), 2, 3)))
    a = jnp.minimum(i1, i2)
    bb = jnp.maximum(i1, i2)
    cls = g_idx * PAIRS_PER_GROUP + ((a * (7 - a)) >> 1) + (bb - a - 1)
    return hf32, cls


def _place_tokens(cls, carry_ref, cur_ref, tab_ref, first_step):
    tm = cls.shape[1]

    @pl.when(first_step)
    def _():
        carry_ref[...] = jnp.zeros_like(carry_ref)
        cur_ref[...] = jnp.zeros_like(cur_ref)
        tab_ref[...] = jnp.zeros_like(tab_ref)

    crow = lax.broadcasted_iota(jnp.int32, (CLASS_ROWS, tm), 0)
    onehot = (crow == cls).astype(F32)
    si = lax.broadcasted_iota(jnp.int32, (tm, tm), 0)
    ti = lax.broadcasted_iota(jnp.int32, (tm, tm), 1)
    before = (si < ti).astype(BF16)
    prefix = jnp.dot(onehot.astype(BF16), before, preferred_element_type=F32)
    seen = prefix + carry_ref[:, 0:1]
    rank = jnp.sum(onehot * seen, axis=0, keepdims=True)
    inv_tile = 1.0 / MOE_TILE
    rem = rank - MOE_TILE * jnp.floor(rank * inv_tile)
    opens = (rem == 0.0).astype(F32)
    opened_before = jnp.sum(jnp.floor((seen + (MOE_TILE - 1)) * inv_tile), axis=0, keepdims=True)
    sel = onehot * opens
    opened = jnp.sum(sel, axis=1, keepdims=True) > 0.0
    new_tile = jnp.sum(sel * opened_before, axis=1, keepdims=True)
    first_rank = jnp.sum(sel * rank, axis=1, keepdims=True)
    tile = jnp.sum(onehot * jnp.where(opened & (seen >= first_rank), new_tile, cur_ref[:, 0:1]),
                   axis=0, keepdims=True)
    carry_ref[...] = carry_ref[...] + jnp.sum(onehot, axis=1, keepdims=True)
    cur_ref[...] = jnp.where(opened, new_tile, cur_ref[...])
    lane = lax.broadcasted_iota(jnp.int32, (CLASS_ROWS, LANES), 1).astype(F32)
    cidx = lax.broadcasted_iota(jnp.int32, (CLASS_ROWS, LANES), 0).astype(F32)
    tab_ref[...] = tab_ref[...] + jnp.where(opened & (lane == new_tile), cidx, 0.0)
    return (tile * MOE_TILE + rem).astype(jnp.int32)


def _rope(t, cos, sgn, first_half):
    out = []
    for c in range(GROUP_WIDTH // LANES):
        xc = t[:, c * LANES:(c + 1) * LANES]
        other = jnp.where(first_half, pltpu.roll(xc, LANES - ROT_DIM // 2, 1),
                          pltpu.roll(xc, ROT_DIM // 2, 1))
        out.append(xc * cos + other * sgn)
    return jnp.concatenate(out, axis=1)


def _qkv_kernel(x_ref, gain_ref, sh_ref, sc_ref, w_ref, cos_ref, sgn_ref, *rest, tm):
    out_refs, hn_scr = rest[:9], rest[9]
    hn = _norm_modulate(x_ref[0], gain_ref[...], sh_ref[0, 0], sc_ref[0, 0])
    n_slab = D_MODEL // LANES
    lane = lax.broadcasted_iota(jnp.int32, (tm, LANES), 1) % HEAD_DIM
    first_half = lane < ROT_DIM // 2
    for g, d in enumerate(DILATIONS):
        rows = tm // d
        if d == 1:
            hp, cos, sgn = hn, cos_ref[...], sgn_ref[...]
        else:
            if g == 1:
                for s in range(n_slab):
                    hn_scr[s] = hn[:, s * LANES:(s + 1) * LANES]
            hp = jnp.concatenate(
                [jnp.concatenate([hn_scr[s, pl.ds(r, rows, stride=d), :] for s in range(n_slab)], axis=1)
                 for r in range(d)], axis=0)
            cos = jnp.concatenate([cos_ref[pl.ds(r, rows, stride=d), :] for r in range(d)], axis=0)
            sgn = jnp.concatenate([sgn_ref[pl.ds(r, rows, stride=d), :] for r in range(d)], axis=0)
        res = jnp.dot(hp.astype(BF16), w_ref[g], preferred_element_type=F32)
        q = _rope(res[:, :GROUP_WIDTH], cos, sgn, first_half)
        k = _rope(res[:, GROUP_WIDTH:2 * GROUP_WIDTH], cos, sgn, first_half)
        v = res[:, 2 * GROUP_WIDTH:]
        for t, o_ref in zip((q, k, v), out_refs[3 * g:3 * g + 3]):
            o_ref[0] = t.astype(BF16).reshape(d, rows, GROUP_WIDTH)


def _qkv(x, gain, shift, scale, w_groups, cos_t, sgn_t, tm=1024):
    b, s, _ = x.shape
    out_shape, out_specs = [], []
    for d in DILATIONS:
        for _ in range(3):
            out_shape.append(jax.ShapeDtypeStruct((b, d, s // d, GROUP_WIDTH), BF16))
            out_specs.append(pl.BlockSpec((1, d, tm // d, GROUP_WIDTH), lambda bi, i: (bi, 0, i, 0)))
    return pl.pallas_call(
        functools.partial(_qkv_kernel, tm=tm),
        grid=(b, s // tm),
        in_specs=[pl.BlockSpec((1, tm, D_MODEL), lambda bi, i: (bi, i, 0)),
                  _const_spec((1, D_MODEL), lambda bi, i: (0, 0)),
                  pl.BlockSpec((1, 1, 1, D_MODEL), lambda bi, i: (bi, 0, 0, 0)),
                  pl.BlockSpec((1, 1, 1, D_MODEL), lambda bi, i: (bi, 0, 0, 0)),
                  _const_spec((N_ATTN_GROUPS, D_MODEL, 3 * GROUP_WIDTH), lambda bi, i: (0, 0, 0)),
                  pl.BlockSpec((tm, LANES), lambda bi, i: (i, 0)),
                  pl.BlockSpec((tm, LANES), lambda bi, i: (i, 0))],
        out_specs=out_specs,
        out_shape=out_shape,
        scratch_shapes=[pltpu.VMEM((D_MODEL // LANES, tm, LANES), F32)],
        compiler_params=_cparams(("arbitrary", "arbitrary")),
        name="qkv_rope",
    )(x, gain, shift, scale, w_groups, cos_t, sgn_t)


def _attn_kernel(q_ref, kp_ref, kc_ref, kn_ref, vp_ref, vc_ref, vn_ref, o_ref, lse_ref,
                 kbuf, vbuf, *, seq, tq):
    i = pl.program_id(2)
    r = BAND_RADIUS
    kbuf[0:r] = kp_ref[0, 0]
    kbuf[r:r + tq] = kc_ref[0, 0]
    kbuf[r + tq:] = kn_ref[0, 0]
    vbuf[0:r] = vp_ref[0, 0]
    vbuf[r:r + tq] = vc_ref[0, 0]
    vbuf[r + tq:] = vn_ref[0, 0]
    qb = 2 * r
    kw = qb + 2 * r
    lane = lax.broadcasted_iota(jnp.int32, (qb, LANES), 1)
    low = lane < HEAD_DIM
    tidx = lax.broadcasted_iota(jnp.int32, (qb, kw), 0)
    kidx = lax.broadcasted_iota(jnp.int32, (qb, kw), 1)
    rel = kidx - tidx
    band = (rel >= 0) & (rel <= 2 * r)
    for j in range(tq // qb):
        kpos = i * tq + (j * qb - r) + kidx
        mask = band & (kpos >= 0) & (kpos < seq)
        q = q_ref[0, 0, j * qb:(j + 1) * qb, :]
        k = kbuf[j * qb:j * qb + kw, :]
        v = vbuf[j * qb:j * qb + kw, :]
        lse_tile = jnp.zeros((qb, LANES), F32)
        o_chunks = []
        for c in range(GROUP_WIDTH // LANES):
            qc = q[:, c * LANES:(c + 1) * LANES]
            kc = k[:, c * LANES:(c + 1) * LANES]
            vc = v[:, c * LANES:(c + 1) * LANES]
            outs = []
            for hh in range(2):
                head = 2 * c + hh
                if head >= HEADS_PER_GROUP:
                    outs.append(jnp.zeros((qb, LANES), F32))
                    continue
                qm = jnp.where(low if hh == 0 else jnp.logical_not(low), qc, jnp.zeros_like(qc))
                s = lax.dot_general(qm, kc, (((1,), (1,)), ((), ())), preferred_element_type=F32)
                s = jnp.where(mask, s, NEG_INF)
                m = jnp.max(s, axis=-1, keepdims=True)
                p = jnp.exp(s - m)
                den = jnp.sum(p, axis=-1, keepdims=True)
                o = jnp.dot(p.astype(BF16), vc, preferred_element_type=F32) / den
                lse_tile = jnp.where(lane == head, m + jnp.log(den), lse_tile)
                outs.append(o)
            o_chunks.append(jnp.where(low, outs[0], outs[1]))
        o_ref[0, 0, j * qb:(j + 1) * qb, :] = jnp.concatenate(o_chunks, axis=1).astype(BF16)
        lse_ref[0, 0, j * qb:(j + 1) * qb, :] = lse_tile


def _attention(q, k, v, tq=1024):
    b, d, seq, w = q.shape
    r = BAND_RADIUS
    nb = seq // r
    cur = pl.BlockSpec((1, 1, tq, w), lambda bi, ri, i: (bi, ri, i, 0))
    prev = pl.BlockSpec((1, 1, r, w), lambda bi, ri, i: (bi, ri, jnp.maximum(i * (tq // r) - 1, 0), 0))
    nxt = pl.BlockSpec((1, 1, r, w), lambda bi, ri, i: (bi, ri, jnp.minimum((i + 1) * (tq // r), nb - 1), 0))
    return pl.pallas_call(
        functools.partial(_attn_kernel, seq=seq, tq=tq),
        grid=(b, d, seq // tq),
        in_specs=[cur, prev, cur, nxt, prev, cur, nxt],
        out_specs=[pl.BlockSpec((1, 1, tq, w), lambda bi, ri, i: (bi, ri, i, 0)),
                   pl.BlockSpec((1, 1, tq, LANES), lambda bi, ri, i: (bi, ri, i, 0))],
        out_shape=[jax.ShapeDtypeStruct((b, d, seq, w), BF16),
                   jax.ShapeDtypeStruct((b, d, seq, LANES), F32)],
        scratch_shapes=[pltpu.VMEM((tq + 2 * r, w), BF16), pltpu.VMEM((tq + 2 * r, w), BF16)],
        compiler_params=_cparams(("arbitrary", "arbitrary", "arbitrary")),
        name=f"band_attn_d{d}",
    )(q, k, k, k, v, v, v)


def _attn_out_kernel(o0, o1, o2, l0, l1, l2, x_ref, wo_ref, exp_ref, gt_ref, gain_ref, sh_ref, sc_ref,
                     wr_ref, br_ref, x1_ref, xs_hbm, pos_ref, tab_out, cnt_out, oscr, lscr, *route_scratch,
                     tm, n_steps):
    n_slab = GROUP_WIDTH // LANES
    for g, (d, o_ref, l_ref) in enumerate(zip(DILATIONS, (o0, o1, o2), (l0, l1, l2))):
        rows = tm // d
        for r in range(d):
            dst = slice(None) if d == 1 else pl.ds(r, rows, stride=d)
            blk = o_ref[0, r].astype(F32)
            for c in range(n_slab):
                oscr[g * n_slab + c, dst, :] = blk[:, c * LANES:(c + 1) * LANES]
            lscr[g, dst, :] = l_ref[0, r]
    lse = [lscr[g] for g in range(N_ATTN_GROUPS)]
    mx = jnp.maximum(jnp.maximum(lse[0], lse[1]), lse[2])
    ex = [jnp.exp(l - mx) for l in lse]
    tot = ex[0] + ex[1] + ex[2]
    y = jnp.zeros((tm, D_MODEL), F32)
    for g in range(N_ATTN_GROUPS):
        alpha = ex[g] / tot
        a_hi, a_lo = _split_bf16(alpha)
        a_wide = (jnp.dot(a_hi, exp_ref[...], preferred_element_type=F32)
                  + jnp.dot(a_lo, exp_ref[...], preferred_element_type=F32))
        og = jnp.concatenate([oscr[g * n_slab + c] for c in range(n_slab)], axis=1)
        y = y + jnp.dot((og * a_wide).astype(BF16), wo_ref[g], preferred_element_type=F32)
    x1 = x_ref[0] + gt_ref[0, 0] * y
    x1_ref[0] = x1
    _route_outputs(x1, gain_ref, sh_ref, sc_ref, wr_ref, br_ref, xs_hbm, pos_ref, tab_out, cnt_out, route_scratch,
                   n_steps)


ZERO_TOKENS = MOE_TILE // 2


def _route_outputs(x1, gain_ref, sh_ref, sc_ref, wr_ref, br_ref, xs_hbm, pos_ref, tab_out, cnt_out, scratch,
                   n_steps):
    (carry_ref, cur_ref, tab_ref, rows_scr, pos_vmem, pos_smem, fin_vmem, fin_smem, zbuf, sems, aux_sem) = scratch
    step = pl.program_id(0) * pl.num_programs(1) + pl.program_id(1)
    tm = x1.shape[0]
    rows = tm * TOKEN_ROWS
    hf32, cls = _ffn_prep(x1, gain_ref[...], sh_ref[0, 0], sc_ref[0, 0], wr_ref, br_ref)
    pos = _place_tokens(cls, carry_ref, cur_ref, tab_ref, step == 0)
    pos_lines = _lines(pos)
    pos_ref[0, 0] = pos_lines
    pos_vmem[...] = pos_lines
    to_smem = pltpu.make_async_copy(pos_vmem, pos_smem, aux_sem)
    to_smem.start()

    def drain(slot):
        pltpu.make_async_copy(rows_scr.at[pl.ds(0, rows), :], xs_hbm.at[pl.ds(0, rows), :], sems.at[slot]).wait()

    slot = step % 2

    @pl.when(step >= 2)
    def _():
        drain(slot)

    base = pl.multiple_of(slot * rows, rows)
    for s in range(TOKEN_ROWS):
        rows_scr[pl.ds(base + s, tm, stride=TOKEN_ROWS), :] = hf32[:, s * LANES:(s + 1) * LANES]
    to_smem.wait()

    def issue(line, lane, priority):
        _token_copy(rows_scr, slot * tm + line * LANES + lane, xs_hbm, pos_smem[line, lane],
                    sems.at[slot]).start(priority=priority)

    _issue_burst(tm, issue, spread=False)
    tab_out[...] = tab_ref[...]
    cnt_out[...] = carry_ref[...]

    @pl.when(step == n_steps - 1)
    def _():
        drain(slot)
        if n_steps > 1:
            drain(1 - slot)
        _clear_unused_rows(carry_ref, cur_ref, xs_hbm, fin_vmem, fin_smem, zbuf, aux_sem)


def _clear_unused_rows(carry_ref, cur_ref, xs_hbm, fin_vmem, fin_smem, zbuf, sem):
    lane = lax.broadcasted_iota(jnp.int32, (CLASS_ROWS, LANES), 1)
    crow = lax.broadcasted_iota(jnp.int32, (CLASS_ROWS, LANES), 0)
    on_lane = lambda col: jnp.sum(jnp.where(lane == crow, col, 0.0), axis=0, keepdims=True)
    counts = carry_ref[...]
    used = jnp.sum(jnp.floor((counts + (MOE_TILE - 1)) * (1.0 / MOE_TILE)), axis=0, keepdims=True)
    r8 = lax.broadcasted_iota(jnp.int32, (SUBLANES, LANES), 0)
    fin = jnp.where(r8 == 0, on_lane(cur_ref[...]), jnp.where(r8 == 1, on_lane(counts), jnp.where(r8 == 2, used, 0.0)))
    fin_vmem[...] = fin.astype(jnp.int32)
    to_smem = pltpu.make_async_copy(fin_vmem, fin_smem, sem)
    to_smem.start()
    zbuf[...] = jnp.zeros_like(zbuf)
    to_smem.wait()

    def zero_copy(first_tok, n_tok):
        d0 = pl.multiple_of(first_tok * TOKEN_ROWS, TOKEN_ROWS)
        return pltpu.make_async_copy(zbuf.at[pl.ds(0, n_tok * TOKEN_ROWS), :],
                                     xs_hbm.at[pl.ds(d0, n_tok * TOKEN_ROWS), :], sem)

    for wait in (False, True):
        for c in range(N_CLASSES):
            count = fin_smem[1, c]
            fill = jnp.where(count > 0, (MOE_TILE - count % MOE_TILE) % MOE_TILE, 0)
            first = fin_smem[0, c] * MOE_TILE + (MOE_TILE - fill)
            piece = ZERO_TOKENS
            while piece >= 1:
                done = (fill // (2 * piece)) * (2 * piece)

                @pl.when((fill // piece) % 2 == 1)
                def _(first=first, done=done, piece=piece):
                    cp = zero_copy(first + done, piece)
                    cp.wait() if wait else cp.start()

                piece //= 2

    def clear_tile(j, carry):
        for half in range(MOE_TILE // ZERO_TOKENS):
            cp = zero_copy(j * MOE_TILE + half * ZERO_TOKENS, ZERO_TOKENS)
            cp.start()
            cp.wait()
        return carry

    lax.fori_loop(fin_smem[2, 0], xs_hbm.shape[0] // (MOE_TILE * TOKEN_ROWS), clear_tile, 0)


def _route_scratch(tm):
    assert tm <= MOE_TILE, "a class may open at most one tile per grid step"
    return [pltpu.VMEM((CLASS_ROWS, LANES), F32), pltpu.VMEM((CLASS_ROWS, LANES), F32),
            pltpu.VMEM((CLASS_ROWS, LANES), F32),
            pltpu.VMEM((2 * tm * TOKEN_ROWS, LANES), F32),
            pltpu.VMEM((SUBLANES, LANES), jnp.int32), pltpu.SMEM((SUBLANES, LANES), jnp.int32),
            pltpu.VMEM((SUBLANES, LANES), jnp.int32), pltpu.SMEM((SUBLANES, LANES), jnp.int32),
            pltpu.VMEM((ZERO_TOKENS * TOKEN_ROWS, LANES), F32),
            pltpu.SemaphoreType.DMA((2,)), pltpu.SemaphoreType.DMA]


def _route_out_specs(b, s, tm):
    n_tiles = b * s // MOE_TILE + N_CLASSES
    specs = [pl.BlockSpec((1, tm, D_MODEL), lambda bi, i: (bi, i, 0)),
             pl.BlockSpec(memory_space=pl.ANY),
             pl.BlockSpec((1, 1, SUBLANES, LANES), lambda bi, i: (bi, i, 0, 0)),
             pl.BlockSpec((CLASS_ROWS, LANES), lambda bi, i: (0, 0)),
             pl.BlockSpec((CLASS_ROWS, LANES), lambda bi, i: (0, 0))]
    shapes = [jax.ShapeDtypeStruct((b, s, D_MODEL), F32),
              jax.ShapeDtypeStruct((n_tiles * MOE_TILE * TOKEN_ROWS, LANES), F32),
              jax.ShapeDtypeStruct((b, s // tm, SUBLANES, LANES), jnp.int32),
              jax.ShapeDtypeStruct((CLASS_ROWS, LANES), F32),
              jax.ShapeDtypeStruct((CLASS_ROWS, LANES), F32)]
    return specs, shapes


def _mod_spec():
    return pl.BlockSpec((1, 1, 1, D_MODEL), lambda bi, i: (bi, 0, 0, 0))


def _attn_out(o_list, l_list, x, wo_groups, expand, gt, gain, shift, scale, wr_t, br, tm=512):
    b, s, _ = x.shape
    o_specs = [pl.BlockSpec((1, d, tm // d, GROUP_WIDTH), lambda bi, i: (bi, 0, i, 0)) for d in DILATIONS]
    l_specs = [pl.BlockSpec((1, d, tm // d, LANES), lambda bi, i: (bi, 0, i, 0)) for d in DILATIONS]
    out_specs, out_shape = _route_out_specs(b, s, tm)
    return pl.pallas_call(
        functools.partial(_attn_out_kernel, tm=tm, n_steps=b * s // tm),
        grid=(b, s // tm),
        in_specs=o_specs + l_specs + [
            pl.BlockSpec((1, tm, D_MODEL), lambda bi, i: (bi, i, 0)),
            _const_spec((N_ATTN_GROUPS, GROUP_WIDTH, D_MODEL), lambda bi, i: (0, 0, 0)),
            _const_spec((LANES, GROUP_WIDTH), lambda bi, i: (0, 0)),
            _mod_spec(),
            _const_spec((1, D_MODEL), lambda bi, i: (0, 0)),
            _mod_spec(), _mod_spec(),
            _const_spec((ROUTER_ROWS, D_MODEL), lambda bi, i: (0, 0)),
            _const_spec((ROUTER_ROWS, 1), lambda bi, i: (0, 0))],
        out_specs=out_specs,
        out_shape=out_shape,
        scratch_shapes=[pltpu.VMEM((N_ATTN_GROUPS * GROUP_WIDTH // LANES, tm, LANES), F32),
                        pltpu.VMEM((N_ATTN_GROUPS, tm, LANES), F32)] + _route_scratch(tm),
        compiler_params=_cparams(("arbitrary", "arbitrary")),
        name="attn_merge_proj",
    )(*o_list, *l_list, x, wo_groups, expand, gt, gain, shift, scale, wr_t, br)


DMA_BURST_UNROLL = 16
BACKGROUND_PRIORITY = 1


def _issue_burst(n, issue, spread=True):
    for line in range(n // LANES):
        def body(i, carry, line=line):
            for k in range(DMA_BURST_UNROLL):
                issue(line, i * DMA_BURST_UNROLL + k, k % 2 if spread else BACKGROUND_PRIORITY)
            return carry

        lax.fori_loop(0, LANES // DMA_BURST_UNROLL, body, 0)


def _lines(row):
    n = row.shape[1]
    line = lax.broadcasted_iota(jnp.int32, (SUBLANES, LANES), 0)
    out = jnp.zeros((SUBLANES, LANES), jnp.int32)
    for j in range(n // LANES):
        out = jnp.where(line == j, row[:, j * LANES:(j + 1) * LANES], out)
    return out


class _RowGather:
    def __init__(self, pos_ref, pos_next_ref, ys_hbm, ybuf, sems, step, n_steps, tm, spread):
        self.pos_ref, self.pos_next_ref, self.ys_hbm, self.ybuf, self.sems = pos_ref, pos_next_ref, ys_hbm, ybuf, sems
        self.step, self.n_steps, self.tm, self.spread = step, n_steps, tm, spread

    def _request(self, p_ref, slot):
        def issue(line, lane, priority):
            _token_copy(self.ys_hbm, p_ref[0, line, lane], self.ybuf, slot * self.tm + line * LANES + lane,
                        self.sems.at[slot]).start(priority=priority)

        _issue_burst(self.tm, issue, self.spread)

    def request_next(self):
        @pl.when(self.step + 1 < self.n_steps)
        def _():
            self._request(self.pos_next_ref, (self.step + 1) % 2)

    def current(self):
        rows = self.tm * TOKEN_ROWS

        @pl.when(self.step == 0)
        def _():
            self._request(self.pos_ref, 0)

        slot = self.step % 2
        base = pl.multiple_of(slot * rows, rows)
        pltpu.make_async_copy(self.ys_hbm.at[pl.ds(0, rows), :], self.ybuf.at[pl.ds(base, rows), :],
                              self.sems.at[slot]).wait()
        return jnp.concatenate([self.ybuf[pl.ds(base + s, self.tm, stride=TOKEN_ROWS), :]
                                for s in range(D_MODEL // LANES)], axis=1)


def _gather_specs(n_steps, tm, index_of):
    return [pl.BlockSpec((1, SUBLANES, LANES), lambda *g: (index_of(*g), 0, 0), memory_space=pltpu.SMEM),
            pl.BlockSpec((1, SUBLANES, LANES), lambda *g: (jnp.minimum(index_of(*g) + 1, n_steps - 1), 0, 0),
                         memory_space=pltpu.SMEM),
            pl.BlockSpec(memory_space=pl.ANY)]


def _pos_lines(pos, tm):
    lines = pos.reshape(-1, tm // LANES, LANES)
    return jnp.pad(lines, ((0, 0), (0, SUBLANES - tm // LANES), (0, 0)))


def _gather_scratch(tm):
    return [pltpu.VMEM((2 * tm * TOKEN_ROWS, LANES), F32), pltpu.SemaphoreType.DMA((2,))]


def _gmlp_kernel(pos_ref, pos_next_ref, ys_hbm, gtf_ref, x_ref, gain_ref, sh_ref, sc_ref, win_ref, vg_ref, ws_ref,
                 bs_ref, wo_ref, gt_ref, fgain_ref, fsh_ref, fsc_ref, wr_ref, br_ref, x1_ref, xs_hbm, pos_out, tab_out,
                 cnt_out, gated_scr, ybuf, sems, *route_scratch, tm, n_steps):
    step = pl.program_id(0) * pl.num_programs(1) + pl.program_id(1)
    gather = _RowGather(pos_ref, pos_next_ref, ys_hbm, ybuf, sems, step, n_steps, tm, spread=False)
    gather.request_next()
    x = x_ref[0] + gtf_ref[0, 0] * gather.current()
    hn = _norm_modulate(x, gain_ref[...], sh_ref[0, 0], sc_ref[0, 0]).astype(BF16)
    v = _gelu_x2(jnp.dot(hn, win_ref[:, GMLP_HALF:], preferred_element_type=F32))
    u = _gelu_x2(jnp.dot(hn, win_ref[:, :GMLP_HALF], preferred_element_type=F32))
    mu = jnp.mean(v, axis=-1, keepdims=True)
    vc = v - mu
    vn = (vc * lax.rsqrt(jnp.mean(vc * vc, axis=-1, keepdims=True) + 4.0 * EPS) * vg_ref[...]).astype(BF16)
    gw = GMLP_HALF // GMLP_GROUPS
    for c in range(tm // CHUNK):
        rs = slice(c * CHUNK, (c + 1) * CHUNK)
        for g in range(GMLP_GROUPS):
            cs = slice(g * gw, (g + 1) * gw)
            vs = jnp.dot(ws_ref[g], vn[rs, cs], preferred_element_type=F32) + bs_ref[:, g:g + 1]
            gated_scr[rs, cs] = (u[rs, cs] * vs).astype(BF16)
    y = jnp.dot(gated_scr[...], wo_ref[...], preferred_element_type=F32)
    x1 = x + gt_ref[0, 0] * y
    x1_ref[0] = x1
    _route_outputs(x1, fgain_ref, fsh_ref, fsc_ref, wr_ref, br_ref, xs_hbm, pos_out, tab_out, cnt_out,
                   route_scratch, n_steps)


def _gmlp(ys, pos, gt_prev, x, gain, shift, scale, w_in, v_gain, w_s, b_s_t, w_o, gt, fgain, fshift, fscale,
          wr_t, br, tm=512):
    b, s, _ = x.shape
    tok = lambda w: pl.BlockSpec((1, tm, w), lambda bi, i: (bi, i, 0))
    c2 = lambda shape: _const_spec(shape, lambda bi, i: (0,) * len(shape))
    out_specs, out_shape = _route_out_specs(b, s, tm)
    n_steps = b * s // tm
    return pl.pallas_call(
        functools.partial(_gmlp_kernel, tm=tm, n_steps=n_steps),
        grid=(b, s // tm),
        in_specs=_gather_specs(n_steps, tm, lambda bi, i: bi * (s // tm) + i) + [
                  _mod_spec(),
                  tok(D_MODEL), c2((1, D_MODEL)), _mod_spec(), _mod_spec(),
                  c2((D_MODEL, 2 * GMLP_HALF)), c2((1, GMLP_HALF)),
                  c2((GMLP_GROUPS, CHUNK, CHUNK)), c2((CHUNK, GMLP_GROUPS)),
                  c2((GMLP_HALF, D_MODEL)), _mod_spec(),
                  c2((1, D_MODEL)), _mod_spec(), _mod_spec(),
                  c2((ROUTER_ROWS, D_MODEL)), c2((ROUTER_ROWS, 1))],
        out_specs=out_specs,
        out_shape=out_shape,
        scratch_shapes=[pltpu.VMEM((tm, GMLP_HALF), BF16)] + _gather_scratch(tm) + _route_scratch(tm),
        compiler_params=_cparams(("arbitrary", "arbitrary")),
        name="gmlp",
    )(_pos_lines(pos, tm), _pos_lines(pos, tm), ys, gt_prev,
      x, gain, shift, scale, w_in, v_gain, w_s, b_s_t, w_o, gt, fgain, fshift, fscale, wr_t, br)


def _token_copy(src, src_tok, dst, dst_tok, sem):
    s0 = pl.multiple_of(src_tok * TOKEN_ROWS, TOKEN_ROWS)
    d0 = pl.multiple_of(dst_tok * TOKEN_ROWS, TOKEN_ROWS)
    return pltpu.make_async_copy(src.at[pl.ds(s0, TOKEN_ROWS), :], dst.at[pl.ds(d0, TOKEN_ROWS), :], sem)


def _slabs(ref, n_tok, n_slab):
    return jnp.concatenate([ref[pl.ds(s, n_tok, stride=TOKEN_ROWS), :] for s in range(n_slab)], axis=1)


def _expert_kernel(ea_ref, eb_ref, blk_ref, nused_ref, x_ref, wr_ref, br_ref, wga, wgb, wua, wub, wda, wdb, y_ref,
                   lt_scr):
    j = pl.program_id(0)

    @pl.when(j < nused_ref[0])
    def _():
        h = _slabs(x_ref, MOE_TILE, TOKEN_ROWS).astype(BF16)
        lt = _router_logits(h, wr_ref, br_ref)
        g_val, _ = _group_choice(lt)
        lt_scr[...] = lt
        el_a = lt_scr[pl.ds(N_EXPERT_GROUPS + ea_ref[j], 1), :]
        el_b = lt_scr[pl.ds(N_EXPERT_GROUPS + eb_ref[j], 1), :]
        top = jnp.maximum(el_a, el_b)
        p_a = jnp.exp(el_a - top)
        p_b = jnp.exp(el_b - top)
        row = lax.broadcasted_iota(jnp.int32, (LANES, MOE_TILE), 0)
        gates_t = jnp.where(row == 0, g_val * (p_a / (p_a + p_b)), jnp.where(row == 1, g_val * (p_b / (p_a + p_b)), 0.0))
        gates = gates_t.T
        y = jnp.zeros((MOE_TILE, D_MODEL), F32)
        for gate, wg, wu, wd in ((gates[:, 0:1], wga, wua, wda), (gates[:, 1:2], wgb, wub, wdb)):
            hg = jnp.dot(h, wg[0, 0], preferred_element_type=F32)
            hu = jnp.dot(h, wu[0, 0], preferred_element_type=F32)
            act = (_silu(hg) * hu * gate).astype(BF16)
            y = y + jnp.dot(act, wd[0, 0], preferred_element_type=F32)
        for s in range(TOKEN_ROWS):
            y_ref[pl.ds(s, MOE_TILE, stride=TOKEN_ROWS), :] = y[:, s * LANES:(s + 1) * LANES]

    @pl.when(j >= nused_ref[0])
    def _():
        y_ref[...] = jnp.zeros_like(y_ref)


def _experts(xs, tile_ea, tile_eb, tile_blk, n_used, wr_t, br, wg, wu, wd, layer):
    n_tiles = xs.shape[0] // (MOE_TILE * TOKEN_ROWS)
    up = lambda sel: pl.BlockSpec((1, 1, D_MODEL, D_EXPERT),
                                  lambda j, ea, eb, blk, nu: (layer, (ea, eb)[sel][j], 0, 0))
    down = lambda sel: pl.BlockSpec((1, 1, D_EXPERT, D_MODEL),
                                    lambda j, ea, eb, blk, nu: (layer, (ea, eb)[sel][j], 0, 0))
    shape = (MOE_TILE * TOKEN_ROWS, LANES)
    return pl.pallas_call(
        _expert_kernel,
        grid_spec=pltpu.PrefetchScalarGridSpec(
            num_scalar_prefetch=4,
            grid=(n_tiles,),
            in_specs=[pl.BlockSpec(shape, lambda j, ea, eb, blk, nu: (blk[j], 0)),
                      pl.BlockSpec((ROUTER_ROWS, D_MODEL), lambda j, ea, eb, blk, nu: (0, 0)),
                      pl.BlockSpec((ROUTER_ROWS, 1), lambda j, ea, eb, blk, nu: (0, 0)),
                      up(0), up(1), up(0), up(1), down(0), down(1)],
            out_specs=pl.BlockSpec(shape, lambda j, ea, eb, blk, nu: (j, 0)),
            scratch_shapes=[pltpu.VMEM((ROUTER_ROWS, MOE_TILE), F32)]),
        out_shape=jax.ShapeDtypeStruct(xs.shape, F32),
        compiler_params=_cparams(("arbitrary",)),
        name="moe_experts",
    )(tile_ea, tile_eb, tile_blk, n_used, xs, wr_t, br, wg, wg, wu, wu, wd, wd)


def _final_kernel(pos_ref, pos_next_ref, ys_hbm, x_ref, gt_ref, fin_ref, o_ref, ybuf, sems, *, tm):
    gather = _RowGather(pos_ref, pos_next_ref, ys_hbm, ybuf, sems, pl.program_id(0), pl.num_programs(0), tm,
                        spread=True)
    gather.request_next()
    y_moe = gather.current()
    x2 = x_ref[...] + gt_ref[0, 0] * y_moe
    ms = jnp.mean(x2 * x2, axis=-1, keepdims=True)
    o_ref[...] = x2 * lax.rsqrt(ms + EPS) * fin_ref[...]


def _final(ys, pos, x, gt, fin_gain, seq, tm=512):
    t = x.shape[0]
    tiles_per_batch = seq // tm
    n_steps = t // tm
    return pl.pallas_call(
        functools.partial(_final_kernel, tm=tm),
        grid=(n_steps,),
        in_specs=_gather_specs(n_steps, tm, lambda i: i) + [
                  pl.BlockSpec((tm, D_MODEL), lambda i: (i, 0)),
                  pl.BlockSpec((1, 1, 1, D_MODEL), lambda i: (i // tiles_per_batch, 0, 0, 0)),
                  pl.BlockSpec((1, D_MODEL), lambda i: (0, 0))],
        out_specs=pl.BlockSpec((tm, D_MODEL), lambda i: (i, 0)),
        out_shape=jax.ShapeDtypeStruct((t, D_MODEL), F32),
        scratch_shapes=_gather_scratch(tm),
        compiler_params=_cparams(("arbitrary",)),
        name="moe_combine_final_norm",
    )(_pos_lines(pos, tm), _pos_lines(pos, tm), ys, x, gt, fin_gain)


_PAIR_A = np.array([0, 0, 0, 1, 1, 2], np.int32)
_PAIR_B = np.array([1, 2, 3, 2, 3, 3], np.int32)


def _moe_sorted(xs, tab, counts, wr_t, br, wg, wu, wd, layer):
    n_tiles = xs.shape[0] // (MOE_TILE * TOKEN_ROWS)
    cnt = counts[:N_CLASSES, 0].astype(jnp.int32)
    n_used = jnp.sum((cnt + MOE_TILE - 1) // MOE_TILE)
    ids = jnp.arange(n_tiles, dtype=jnp.int32)
    tile_cls = jnp.sum(tab, axis=0)[:n_tiles].astype(jnp.int32)
    last_cls = jnp.sum(jnp.where(ids == n_used - 1, tile_cls, 0))
    tile_cls = jnp.where(ids < n_used, tile_cls, last_cls)
    j = jnp.minimum(ids, n_used - 1)
    pair = tile_cls % PAIRS_PER_GROUP
    pair_a = jnp.sum(jnp.where(pair[:, None] == jnp.arange(PAIRS_PER_GROUP)[None, :], _PAIR_A[None, :], 0), axis=1)
    pair_b = jnp.sum(jnp.where(pair[:, None] == jnp.arange(PAIRS_PER_GROUP)[None, :], _PAIR_B[None, :], 0), axis=1)
    base = (tile_cls // PAIRS_PER_GROUP) * EXPERTS_PER_GROUP
    return _experts(xs, base + pair_a, base + pair_b, j, n_used.reshape(1), wr_t, br, wg, wu, wd, layer)


def _rope_tables(seq):
    inv_freq = ROPE_THETA ** (-jnp.arange(0, ROT_DIM, 2, dtype=jnp.float32) / ROT_DIM)
    ang = jnp.arange(seq).astype(jnp.float32)[:, None] * inv_freq[None, :]
    cos, sin = jnp.cos(ang), jnp.sin(ang)
    ones = jnp.ones((seq, HEAD_DIM - ROT_DIM), jnp.float32)
    cos_h = jnp.concatenate([cos, cos, ones], axis=1)
    sgn_h = jnp.concatenate([-sin, sin, 0.0 * ones], axis=1)
    reps = LANES // HEAD_DIM
    return jnp.tile(cos_h, (1, reps)).astype(F32), jnp.tile(sgn_h, (1, reps)).astype(F32)


def _pad_heads(w, axis):
    pad = [(0, 0)] * w.ndim
    pad[axis] = (0, GROUP_WIDTH - HEADS_PER_GROUP * HEAD_DIM)
    return jnp.pad(w, pad)


def _router_weights(w_group, b_group, w_expert, b_expert):
    we = jnp.transpose(w_expert, (1, 0, 2)).reshape(D_MODEL, N_EXPERTS)
    w = jnp.concatenate([w_group, we], axis=1)
    w = jnp.pad(w, ((0, 0), (0, ROUTER_ROWS - w.shape[1])))
    bias = jnp.pad(jnp.concatenate([b_group, b_expert.reshape(-1)]), (0, ROUTER_ROWS - 20))
    return w.T.astype(BF16), bias.reshape(ROUTER_ROWS, 1)


def kernel(x, c, norm_mix, norm_ffn, w_ada, b_ada, a_w_qkv, a_w_o, b_w_in, b_v_gain, b_w_s, b_b_s, b_w_o,
           r_w_group, r_b_group, r_w_expert, r_b_expert, e_w_gate, e_w_up, e_w_down, final_norm):
    b, s, _ = x.shape
    mod = _ada_mod(c, w_ada, b_ada)
    sh_m, sc_m, gt_m, sh_f, sc_f, gt_f = [mod[:, :, i:i + 1] for i in range(6)]
    row = lambda v: v.reshape(1, -1)
    gw = HEADS_PER_GROUP * HEAD_DIM
    e_w_gate, e_w_up, e_w_down = [w.astype(BF16) for w in (e_w_gate, e_w_up, e_w_down)]

    w = a_w_qkv[0].reshape(D_MODEL, 3, N_ATTN_GROUPS, gw)
    w = w * jnp.asarray([HEAD_DIM ** -0.5, 1.0, 1.0], F32)[None, :, None, None]
    w_groups = jnp.transpose(_pad_heads(w, 3), (2, 0, 1, 3)).reshape(
        N_ATTN_GROUPS, D_MODEL, 3 * GROUP_WIDTH).astype(BF16)
    wo_groups = _pad_heads(a_w_o[0].reshape(N_ATTN_GROUPS, gw, D_MODEL), 1).astype(BF16)
    expand = (jnp.arange(LANES)[:, None] == jnp.arange(GROUP_WIDTH)[None, :] // HEAD_DIM).astype(BF16)
    cos_t, sgn_t = _rope_tables(s)
    qkv = _qkv(x, row(norm_mix[0]), sh_m[0], sc_m[0], w_groups, cos_t, sgn_t)
    o_list, l_list = [], []
    for g in range(N_ATTN_GROUPS):
        o, l = _attention(*qkv[3 * g:3 * g + 3])
        o_list.append(o)
        l_list.append(l)
    wr_t, br = _router_weights(r_w_group[0], r_b_group[0], r_w_expert[0], r_b_expert[0])
    x1, xs, pos, tab, counts = _attn_out(o_list, l_list, x, wo_groups, expand, gt_m[0], row(norm_ffn[0]),
                                         sh_f[0], sc_f[0], wr_t, br)
    t = b * s
    token_order = lambda p: p[:, :, :t // (p.shape[0] * p.shape[1] * LANES), :].reshape(t)
    ys = _moe_sorted(xs, tab, counts, wr_t, br, e_w_gate, e_w_up, e_w_down, 0)

    wr_t, br = _router_weights(r_w_group[1], r_b_group[1], r_w_expert[1], r_b_expert[1])
    x3, xs, pos, tab, counts = _gmlp(ys, token_order(pos), gt_f[0], x1, row(norm_mix[1]), sh_m[1], sc_m[1],
                                     b_w_in[0].astype(BF16), row(b_v_gain[0]), (0.5 * b_w_s[0]).astype(BF16), 0.5 * b_b_s[0].T,
                                     b_w_o[0].astype(BF16), gt_m[1], row(norm_ffn[1]), sh_f[1], sc_f[1], wr_t, br)
    ys = _moe_sorted(xs, tab, counts, wr_t, br, e_w_gate, e_w_up, e_w_down, 1)
    out = _final(ys, token_order(pos), x3.reshape(t, D_MODEL), gt_f[1], row(final_norm), s)
    return out.reshape(b, s, D_MODEL)
```

```python
import functools

import jax
import jax.numpy as jnp
import numpy as np
from jax import lax
from jax.experimental import pallas as pl
from jax.experimental.pallas import tpu as pltpu

D_MODEL = 1024
DEPTH = 2
HEAD_DIM = 64
HEADS_PER_GROUP = 5
N_ATTN_GROUPS = 3
DILATIONS = (1, 4, 16)
BAND_RADIUS = 64
ROT_DIM = HEAD_DIM // 4
ROPE_THETA = 500000.0
NEG_INF = -1e30
CHUNK = 128
GMLP_HALF = 2 * D_MODEL
GMLP_GROUPS = 8
N_EXPERT_GROUPS = 4
EXPERTS_PER_GROUP = 4
N_EXPERTS = 16
D_EXPERT = 256
EPS = 1e-6

LANES = 128
GROUP_WIDTH = 384
ROUTER_ROWS = 32
PAIRS_PER_GROUP = 6
N_CLASSES = N_EXPERT_GROUPS * PAIRS_PER_GROUP
CLASS_ROWS = 32
SUBLANES = 8
TOKEN_ROWS = D_MODEL // LANES
MOE_TILE = 512

F32 = jnp.float32
BF16 = jnp.bfloat16
VMEM_LIMIT = 56 * 1024 * 1024


def _cparams(sem):
    return pltpu.CompilerParams(dimension_semantics=sem, vmem_limit_bytes=VMEM_LIMIT)


def _const_spec(shape, index_map):
    return pl.BlockSpec(shape, index_map, pipeline_mode=pl.Buffered(1))


def _silu(x):
    return x * (1.0 / (1.0 + jnp.exp(-x)))


def _gelu_x2(x):
    return x * (1.0 + lax.erf(x * (2.0 ** -0.5)))


def _norm_modulate(x, gain, shift, scale):
    ms = jnp.mean(x * x, axis=-1, keepdims=True)
    return (x * lax.rsqrt(ms + EPS)) * (gain * (1.0 + scale)) + shift


def _split_bf16(x):
    hi = x.astype(BF16)
    return hi, (x - hi.astype(F32)).astype(BF16)


def _ada_kernel(c_ref, w_ref, b_ref, o_ref):
    ca_hi, ca_lo = _split_bf16(_silu(c_ref[...]))
    w_hi, w_lo = _split_bf16(w_ref[0])
    dot = functools.partial(jnp.dot, preferred_element_type=F32)
    o_ref[0] = dot(ca_hi, w_hi) + dot(ca_lo, w_hi) + dot(ca_hi, w_lo) + b_ref[0]


def _ada_mod(c, w_ada, b_ada):
    b = c.shape[0]
    tn = 1536
    c8 = jnp.pad(c, ((0, 8 - b), (0, 0)))
    out = pl.pallas_call(
        _ada_kernel,
        grid=(DEPTH, 6 * D_MODEL // tn),
        in_specs=[pl.BlockSpec((8, D_MODEL), lambda l, j: (0, 0)),
                  pl.BlockSpec((1, D_MODEL, tn), lambda l, j: (l, 0, j)),
                  pl.BlockSpec((1, 1, tn), lambda l, j: (l, 0, j))],
        out_specs=pl.BlockSpec((1, 8, tn), lambda l, j: (l, 0, j)),
        out_shape=jax.ShapeDtypeStruct((DEPTH, 8, 6 * D_MODEL), F32),
        compiler_params=_cparams(("arbitrary", "arbitrary")),
        name="ada_mod",
    )(c8, w_ada, b_ada.reshape(DEPTH, 1, 6 * D_MODEL))
    return out[:, :b].reshape(DEPTH, b, 6, 1, D_MODEL)


def _router_logits(hf, wr_ref, br_ref):
    return lax.dot_general(wr_ref[...], hf, (((1,), (1,)), ((), ())), preferred_element_type=F32) + br_ref[...]


def _group_choice(lt):
    gl = [lt[i:i + 1, :] for i in range(N_EXPERT_GROUPS)]
    gmax = jnp.maximum(jnp.maximum(gl[0], gl[1]), jnp.maximum(gl[2], gl[3]))
    ge = [jnp.exp(g - gmax) for g in gl]
    gsum = ge[0] + ge[1] + ge[2] + ge[3]
    gp = [e / gsum for e in ge]
    g_val = jnp.maximum(jnp.maximum(gp[0], gp[1]), jnp.maximum(gp[2], gp[3]))
    g_idx = jnp.where(gp[0] == g_val, 0, jnp.where(gp[1] == g_val, 1, jnp.where(gp[2] == g_val, 2, 3)))
    return g_val, g_idx


def _ffn_prep(x1, gain, shift, scale, wr_ref, br_ref):
    hf32 = _norm_modulate(x1, gain, shift, scale)
    lt = _router_logits(hf32.astype(BF16), wr_ref, br_ref)
    _, g_idx = _group_choice(lt)
    el = []
    for j in range(EXPERTS_PER_GROUP):
        rows = [lt[4 + 4 * g + j:5 + 4 * g + j, :] for g in range(N_EXPERT_GROUPS)]
        el.append(jnp.where(g_idx == 0, rows[0], jnp.where(g_idx == 1, rows[1],
                  jnp.where(g_idx == 2, rows[2], rows[3]))))
    v1 = jnp.maximum(jnp.maximum(el[0], el[1]), jnp.maximum(el[2], el[3]))
    i1 = jnp.where(el[0] == v1, 0, jnp.where(el[1] == v1, 1, jnp.where(el[2] == v1, 2, 3)))
    el2 = [jnp.where(i1 == j, -jnp.inf, el[j]) for j in range(EXPERTS_PER_GROUP)]
    v2 = jnp.maximum(jnp.maximum(el2[0], el2[1]), jnp.maximum(el2[2], el2[3]))
    i2 = jnp.where((el2[0] == v2) & (i1 != 0), 0,
                   jnp.where((el2[1] == v2) & (i1 != 1), 1,
                             jnp.where((el2[2] == v2) & (i1 != 2), 2, 3)))
    a = jnp.minimum(i1, i2)
    bb = jnp.maximum(i1, i2)
    cls = g_idx * PAIRS_PER_GROUP + ((a * (7 - a)) >> 1) + (bb - a - 1)
    return hf32, cls


def _place_tokens(cls, carry_ref, cur_ref, tab_ref, first_step):
    tm = cls.shape[1]

    @pl.when(first_step)
    def _():
        carry_ref[...] = jnp.zeros_like(carry_ref)
        cur_ref[...] = jnp.zeros_like(cur_ref)
        tab_ref[...] = jnp.zeros_like(tab_ref)

    crow = lax.broadcasted_iota(jnp.int32, (CLASS_ROWS, tm), 0)
    onehot = (crow == cls).astype(F32)
    si = lax.broadcasted_iota(jnp.int32, (tm, tm), 0)
    ti = lax.broadcasted_iota(jnp.int32, (tm, tm), 1)
    before = (si < ti).astype(BF16)
    prefix = jnp.dot(onehot.astype(BF16), before, preferred_element_type=F32)
    seen = prefix + carry_ref[:, 0:1]
    rank = jnp.sum(onehot * seen, axis=0, keepdims=True)
    inv_tile = 1.0 / MOE_TILE
    rem = rank - MOE_TILE * jnp.floor(rank * inv_tile)
    opens = (rem == 0.0).astype(F32)
    opened_before = jnp.sum(jnp.floor((seen + (MOE_TILE - 1)) * inv_tile), axis=0, keepdims=True)
    sel = onehot * opens
    opened = jnp.sum(sel, axis=1, keepdims=True) > 0.0
    new_tile = jnp.sum(sel * opened_before, axis=1, keepdims=True)
    first_rank = jnp.sum(sel * rank, axis=1, keepdims=True)
    tile = jnp.sum(onehot * jnp.where(opened & (seen >= first_rank), new_tile, cur_ref[:, 0:1]),
                   axis=0, keepdims=True)
    carry_ref[...] = carry_ref[...] + jnp.sum(onehot, axis=1, keepdims=True)
    cur_ref[...] = jnp.where(opened, new_tile, cur_ref[...])
    lane = lax.broadcasted_iota(jnp.int32, (CLASS_ROWS, LANES), 1).astype(F32)
    cidx = lax.broadcasted_iota(jnp.int32, (CLASS_ROWS, LANES), 0).astype(F32)
    tab_ref[...] = tab_ref[...] + jnp.where(opened & (lane == new_tile), cidx, 0.0)
    return (tile * MOE_TILE + rem).astype(jnp.int32)


def _rope(t, cos, sgn, first_half):
    out = []
    for c in range(GROUP_WIDTH // LANES):
        xc = t[:, c * LANES:(c + 1) * LANES]
        other = jnp.where(first_half, pltpu.roll(xc, LANES - ROT_DIM // 2, 1),
                          pltpu.roll(xc, ROT_DIM // 2, 1))
        out.append(xc * cos + other * sgn)
    return jnp.concatenate(out, axis=1)


def _qkv_kernel(x_ref, gain_ref, sh_ref, sc_ref, w_ref, cos_ref, sgn_ref, *rest, tm):
    out_refs, hn_scr = rest[:9], rest[9]
    hn = _norm_modulate(x_ref[0], gain_ref[...], sh_ref[0, 0], sc_ref[0, 0])
    n_slab = D_MODEL // LANES
    lane = lax.broadcasted_iota(jnp.int32, (tm, LANES), 1) % HEAD_DIM
    first_half = lane < ROT_DIM // 2
    for g, d in enumerate(DILATIONS):
        rows = tm // d
        if d == 1:
            hp, cos, sgn = hn, cos_ref[...], sgn_ref[...]
        else:
            if g == 1:
                for s in range(n_slab):
                    hn_scr[s] = hn[:, s * LANES:(s + 1) * LANES]
            hp = jnp.concatenate(
                [jnp.concatenate([hn_scr[s, pl.ds(r, rows, stride=d), :] for s in range(n_slab)], axis=1)
                 for r in range(d)], axis=0)
            cos = jnp.concatenate([cos_ref[pl.ds(r, rows, stride=d), :] for r in range(d)], axis=0)
            sgn = jnp.concatenate([sgn_ref[pl.ds(r, rows, stride=d), :] for r in range(d)], axis=0)
        res = jnp.dot(hp.astype(BF16), w_ref[g], preferred_element_type=F32)
        q = _rope(res[:, :GROUP_WIDTH], cos, sgn, first_half)
        k = _rope(res[:, GROUP_WIDTH:2 * GROUP_WIDTH], cos, sgn, first_half)
        v = res[:, 2 * GROUP_WIDTH:]
        for t, o_ref in zip((q, k, v), out_refs[3 * g:3 * g + 3]):
            o_ref[0] = t.astype(BF16).reshape(d, rows, GROUP_WIDTH)


def _qkv(x, gain, shift, scale, w_groups, cos_t, sgn_t, tm=1024):
    b, s, _ = x.shape
    out_shape, out_specs = [], []
    for d in DILATIONS:
        for _ in range(3):
            out_shape.append(jax.ShapeDtypeStruct((b, d, s // d, GROUP_WIDTH), BF16))
            out_specs.append(pl.BlockSpec((1, d, tm // d, GROUP_WIDTH), lambda bi, i: (bi, 0, i, 0)))
    return pl.pallas_call(
        functools.partial(_qkv_kernel, tm=tm),
        grid=(b, s // tm),
        in_specs=[pl.BlockSpec((1, tm, D_MODEL), lambda bi, i: (bi, i, 0)),
                  _const_spec((1, D_MODEL), lambda bi, i: (0, 0)),
                  pl.BlockSpec((1, 1, 1, D_MODEL), lambda bi, i: (bi, 0, 0, 0)),
                  pl.BlockSpec((1, 1, 1, D_MODEL), lambda bi, i: (bi, 0, 0, 0)),
                  _const_spec((N_ATTN_GROUPS, D_MODEL, 3 * GROUP_WIDTH), lambda bi, i: (0, 0, 0)),
                  pl.BlockSpec((tm, LANES), lambda bi, i: (i, 0)),
                  pl.BlockSpec((tm, LANES), lambda bi, i: (i, 0))],
        out_specs=out_specs,
        out_shape=out_shape,
        scratch_shapes=[pltpu.VMEM((D_MODEL // LANES, tm, LANES), F32)],
        compiler_params=_cparams(("arbitrary", "arbitrary")),
        name="qkv_rope",
    )(x, gain, shift, scale, w_groups, cos_t, sgn_t)


def _attn_kernel(q_ref, kp_ref, kc_ref, kn_ref, vp_ref, vc_ref, vn_ref, o_ref, lse_ref,
                 kbuf, vbuf, *, seq, tq):
    i = pl.program_id(2)
    r = BAND_RADIUS
    kbuf[0:r] = kp_ref[0, 0]
    kbuf[r:r + tq] = kc_ref[0, 0]
    kbuf[r + tq:] = kn_ref[0, 0]
    vbuf[0:r] = vp_ref[0, 0]
    vbuf[r:r + tq] = vc_ref[0, 0]
    vbuf[r + tq:] = vn_ref[0, 0]
    qb = 2 * r
    kw = qb + 2 * r
    lane = lax.broadcasted_iota(jnp.int32, (qb, LANES), 1)
    low = lane < HEAD_DIM
    tidx = lax.broadcasted_iota(jnp.int32, (qb, kw), 0)
    kidx = lax.broadcasted_iota(jnp.int32, (qb, kw), 1)
    rel = kidx - tidx
    band = (rel >= 0) & (rel <= 2 * r)
    for j in range(tq // qb):
        kpos = i * tq + (j * qb - r) + kidx
        mask = band & (kpos >= 0) & (kpos < seq)
        q = q_ref[0, 0, j * qb:(j + 1) * qb, :]
        k = kbuf[j * qb:j * qb + kw, :]
        v = vbuf[j * qb:j * qb + kw, :]
        lse_tile = jnp.zeros((qb, LANES), F32)
        o_chunks = []
        for c in range(GROUP_WIDTH // LANES):
            qc = q[:, c * LANES:(c + 1) * LANES]
            kc = k[:, c * LANES:(c + 1) * LANES]
            vc = v[:, c * LANES:(c + 1) * LANES]
            outs = []
            for hh in range(2):
                head = 2 * c + hh
                if head >= HEADS_PER_GROUP:
                    outs.append(jnp.zeros((qb, LANES), F32))
                    continue
                qm = jnp.where(low if hh == 0 else jnp.logical_not(low), qc, jnp.zeros_like(qc))
                s = lax.dot_general(qm, kc, (((1,), (1,)), ((), ())), preferred_element_type=F32)
                s = jnp.where(mask, s, NEG_INF)
                m = jnp.max(s, axis=-1, keepdims=True)
                p = jnp.exp(s - m)
                den = jnp.sum(p, axis=-1, keepdims=True)
                o = jnp.dot(p.astype(BF16), vc, preferred_element_type=F32) / den
                lse_tile = jnp.where(lane == head, m + jnp.log(den), lse_tile)
                outs.append(o)
            o_chunks.append(jnp.where(low, outs[0], outs[1]))
        o_ref[0, 0, j * qb:(j + 1) * qb, :] = jnp.concatenate(o_chunks, axis=1).astype(BF16)
        lse_ref[0, 0, j * qb:(j + 1) * qb, :] = lse_tile


def _attention(q, k, v, tq=2048):
    b, d, seq, w = q.shape
    tq = min(tq, seq)
    r = BAND_RADIUS
    nb = seq // r
    cur = pl.BlockSpec((1, 1, tq, w), lambda bi, ri, i: (bi, ri, i, 0))
    prev = pl.BlockSpec((1, 1, r, w), lambda bi, ri, i: (bi, ri, jnp.maximum(i * (tq // r) - 1, 0), 0))
    nxt = pl.BlockSpec((1, 1, r, w), lambda bi, ri, i: (bi, ri, jnp.minimum((i + 1) * (tq // r), nb - 1), 0))
    return pl.pallas_call(
        functools.partial(_attn_kernel, seq=seq, tq=tq),
        grid=(b, d, seq // tq),
        in_specs=[cur, prev, cur, nxt, prev, cur, nxt],
        out_specs=[pl.BlockSpec((1, 1, tq, w), lambda bi, ri, i: (bi, ri, i, 0)),
                   pl.BlockSpec((1, 1, tq, LANES), lambda bi, ri, i: (bi, ri, i, 0))],
        out_shape=[jax.ShapeDtypeStruct((b, d, seq, w), BF16),
                   jax.ShapeDtypeStruct((b, d, seq, LANES), F32)],
        scratch_shapes=[pltpu.VMEM((tq + 2 * r, w), BF16), pltpu.VMEM((tq + 2 * r, w), BF16)],
        compiler_params=_cparams(("arbitrary", "arbitrary", "arbitrary")),
        name=f"band_attn_d{d}",
    )(q, k, k, k, v, v, v)


def _attn_out_kernel(o0, o1, o2, l0, l1, l2, x_ref, wo_ref, exp_ref, gt_ref, gain_ref, sh_ref, sc_ref,
                     wr_ref, br_ref, x1_ref, xs_hbm, pos_ref, tab_out, cnt_out, oscr, lscr, *route_scratch,
                     tm, n_steps):
    n_slab = GROUP_WIDTH // LANES
    for g, (d, o_ref, l_ref) in enumerate(zip(DILATIONS, (o0, o1, o2), (l0, l1, l2))):
        rows = tm // d
        for r in range(d):
            dst = slice(None) if d == 1 else pl.ds(r, rows, stride=d)
            blk = o_ref[0, r].astype(F32)
            for c in range(n_slab):
                oscr[g * n_slab + c, dst, :] = blk[:, c * LANES:(c + 1) * LANES]
            lscr[g, dst, :] = l_ref[0, r]
    lse = [lscr[g] for g in range(N_ATTN_GROUPS)]
    mx = jnp.maximum(jnp.maximum(lse[0], lse[1]), lse[2])
    ex = [jnp.exp(l - mx) for l in lse]
    tot = ex[0] + ex[1] + ex[2]
    y = jnp.zeros((tm, D_MODEL), F32)
    for g in range(N_ATTN_GROUPS):
        alpha = ex[g] / tot
        a_hi, a_lo = _split_bf16(alpha)
        a_wide = (jnp.dot(a_hi, exp_ref[...], preferred_element_type=F32)
                  + jnp.dot(a_lo, exp_ref[...], preferred_element_type=F32))
        og = jnp.concatenate([oscr[g * n_slab + c] for c in range(n_slab)], axis=1)
        y = y + jnp.dot((og * a_wide).astype(BF16), wo_ref[g], preferred_element_type=F32)
    x1 = x_ref[0] + gt_ref[0, 0] * y
    x1_ref[0] = x1
    _route_outputs(x1, gain_ref, sh_ref, sc_ref, wr_ref, br_ref, xs_hbm, pos_ref, tab_out, cnt_out, route_scratch,
                   n_steps)


ZERO_TOKENS = MOE_TILE // 2


def _route_outputs(x1, gain_ref, sh_ref, sc_ref, wr_ref, br_ref, xs_hbm, pos_ref, tab_out, cnt_out, scratch,
                   n_steps):
    (carry_ref, cur_ref, tab_ref, rows_scr, pos_vmem, pos_smem, fin_vmem, fin_smem, zbuf, sems, aux_sem) = scratch
    step = pl.program_id(0) * pl.num_programs(1) + pl.program_id(1)
    tm = x1.shape[0]
    rows = tm * TOKEN_ROWS
    hf32, cls = _ffn_prep(x1, gain_ref[...], sh_ref[0, 0], sc_ref[0, 0], wr_ref, br_ref)
    pos = _place_tokens(cls, carry_ref, cur_ref, tab_ref, step == 0)
    pos_lines = _lines(pos)
    pos_ref[0, 0] = pos_lines
    pos_vmem[...] = pos_lines
    to_smem = pltpu.make_async_copy(pos_vmem, pos_smem, aux_sem)
    to_smem.start()

    def drain(slot):
        pltpu.make_async_copy(rows_scr.at[pl.ds(0, rows), :], xs_hbm.at[pl.ds(0, rows), :], sems.at[slot]).wait()

    slot = step % 2

    @pl.when(step >= 2)
    def _():
        drain(slot)

    base = pl.multiple_of(slot * rows, rows)
    for s in range(TOKEN_ROWS):
        rows_scr[pl.ds(base + s, tm, stride=TOKEN_ROWS), :] = hf32[:, s * LANES:(s + 1) * LANES]
    to_smem.wait()

    def issue(line, lane, priority):
        _token_copy(rows_scr, slot * tm + line * LANES + lane, xs_hbm, pos_smem[line, lane],
                    sems.at[slot]).start(priority=priority)

    _issue_burst(tm, issue, spread=False)
    tab_out[...] = tab_ref[...]
    cnt_out[...] = carry_ref[...]

    @pl.when(step == n_steps - 1)
    def _():
        drain(slot)
        if n_steps > 1:
            drain(1 - slot)
        _clear_unused_rows(carry_ref, cur_ref, xs_hbm, fin_vmem, fin_smem, zbuf, aux_sem)


def _clear_unused_rows(carry_ref, cur_ref, xs_hbm, fin_vmem, fin_smem, zbuf, sem):
    lane = lax.broadcasted_iota(jnp.int32, (CLASS_ROWS, LANES), 1)
    crow = lax.broadcasted_iota(jnp.int32, (CLASS_ROWS, LANES), 0)
    on_lane = lambda col: jnp.sum(jnp.where(lane == crow, col, 0.0), axis=0, keepdims=True)
    counts = carry_ref[...]
    used = jnp.sum(jnp.floor((counts + (MOE_TILE - 1)) * (1.0 / MOE_TILE)), axis=0, keepdims=True)
    r8 = lax.broadcasted_iota(jnp.int32, (SUBLANES, LANES), 0)
    fin = jnp.where(r8 == 0, on_lane(cur_ref[...]), jnp.where(r8 == 1, on_lane(counts), jnp.where(r8 == 2, used, 0.0)))
    fin_vmem[...] = fin.astype(jnp.int32)
    to_smem = pltpu.make_async_copy(fin_vmem, fin_smem, sem)
    to_smem.start()
    zbuf[...] = jnp.zeros_like(zbuf)
    to_smem.wait()

    def zero_copy(first_tok, n_tok):
        d0 = pl.multiple_of(first_tok * TOKEN_ROWS, TOKEN_ROWS)
        return pltpu.make_async_copy(zbuf.at[pl.ds(0, n_tok * TOKEN_ROWS), :],
                                     xs_hbm.at[pl.ds(d0, n_tok * TOKEN_ROWS), :], sem)

    for wait in (False, True):
        for c in range(N_CLASSES):
            count = fin_smem[1, c]
            fill = jnp.where(count > 0, (MOE_TILE - count % MOE_TILE) % MOE_TILE, 0)
            first = fin_smem[0, c] * MOE_TILE + (MOE_TILE - fill)
            piece = ZERO_TOKENS
            while piece >= 1:
                done = (fill // (2 * piece)) * (2 * piece)

                @pl.when((fill // piece) % 2 == 1)
                def _(first=first, done=done, piece=piece):
                    cp = zero_copy(first + done, piece)
                    cp.wait() if wait else cp.start()

                piece //= 2

    def clear_tile(j, carry):
        for half in range(MOE_TILE // ZERO_TOKENS):
            cp = zero_copy(j * MOE_TILE + half * ZERO_TOKENS, ZERO_TOKENS)
            cp.start()
            cp.wait()
        return carry

    lax.fori_loop(fin_smem[2, 0], xs_hbm.shape[0] // (MOE_TILE * TOKEN_ROWS), clear_tile, 0)


def _route_scratch(tm):
    assert tm <= MOE_TILE, "a class may open at most one tile per grid step"
    return [pltpu.VMEM((CLASS_ROWS, LANES), F32), pltpu.VMEM((CLASS_ROWS, LANES), F32),
            pltpu.VMEM((CLASS_ROWS, LANES), F32),
            pltpu.VMEM((2 * tm * TOKEN_ROWS, LANES), F32),
            pltpu.VMEM((SUBLANES, LANES), jnp.int32), pltpu.SMEM((SUBLANES, LANES), jnp.int32),
            pltpu.VMEM((SUBLANES, LANES), jnp.int32), pltpu.SMEM((SUBLANES, LANES), jnp.int32),
            pltpu.VMEM((ZERO_TOKENS * TOKEN_ROWS, LANES), F32),
            pltpu.SemaphoreType.DMA((2,)), pltpu.SemaphoreType.DMA]


def _route_out_specs(b, s, tm):
    n_tiles = b * s // MOE_TILE + N_CLASSES
    specs = [pl.BlockSpec((1, tm, D_MODEL), lambda bi, i: (bi, i, 0)),
             pl.BlockSpec(memory_space=pl.ANY),
             pl.BlockSpec((1, 1, SUBLANES, LANES), lambda bi, i: (bi, i, 0, 0)),
             pl.BlockSpec((CLASS_ROWS, LANES), lambda bi, i: (0, 0)),
             pl.BlockSpec((CLASS_ROWS, LANES), lambda bi, i: (0, 0))]
    shapes = [jax.ShapeDtypeStruct((b, s, D_MODEL), F32),
              jax.ShapeDtypeStruct((n_tiles * MOE_TILE * TOKEN_ROWS, LANES), F32),
              jax.ShapeDtypeStruct((b, s // tm, SUBLANES, LANES), jnp.int32),
              jax.ShapeDtypeStruct((CLASS_ROWS, LANES), F32),
              jax.ShapeDtypeStruct((CLASS_ROWS, LANES), F32)]
    return specs, shapes


def _mod_spec():
    return pl.BlockSpec((1, 1, 1, D_MODEL), lambda bi, i: (bi, 0, 0, 0))


def _attn_out(o_list, l_list, x, wo_groups, expand, gt, gain, shift, scale, wr_t, br, tm=512):
    b, s, _ = x.shape
    o_specs = [pl.BlockSpec((1, d, tm // d, GROUP_WIDTH), lambda bi, i: (bi, 0, i, 0)) for d in DILATIONS]
    l_specs = [pl.BlockSpec((1, d, tm // d, LANES), lambda bi, i: (bi, 0, i, 0)) for d in DILATIONS]
    out_specs, out_shape = _route_out_specs(b, s, tm)
    return pl.pallas_call(
        functools.partial(_attn_out_kernel, tm=tm, n_steps=b * s // tm),
        grid=(b, s // tm),
        in_specs=o_specs + l_specs + [
            pl.BlockSpec((1, tm, D_MODEL), lambda bi, i: (bi, i, 0)),
            _const_spec((N_ATTN_GROUPS, GROUP_WIDTH, D_MODEL), lambda bi, i: (0, 0, 0)),
            _const_spec((LANES, GROUP_WIDTH), lambda bi, i: (0, 0)),
            _mod_spec(),
            _const_spec((1, D_MODEL), lambda bi, i: (0, 0)),
            _mod_spec(), _mod_spec(),
            _const_spec((ROUTER_ROWS, D_MODEL), lambda bi, i: (0, 0)),
            _const_spec((ROUTER_ROWS, 1), lambda bi, i: (0, 0))],
        out_specs=out_specs,
        out_shape=out_shape,
        scratch_shapes=[pltpu.VMEM((N_ATTN_GROUPS * GROUP_WIDTH // LANES, tm, LANES), F32),
                        pltpu.VMEM((N_ATTN_GROUPS, tm, LANES), F32)] + _route_scratch(tm),
        compiler_params=_cparams(("arbitrary", "arbitrary")),
        name="attn_merge_proj",
    )(*o_list, *l_list, x, wo_groups, expand, gt, gain, shift, scale, wr_t, br)


DMA_BURST_UNROLL = 16
BACKGROUND_PRIORITY = 1


def _issue_burst(n, issue, spread=True):
    for line in range(n // LANES):
        def body(i, carry, line=line):
            for k in range(DMA_BURST_UNROLL):
                issue(line, i * DMA_BURST_UNROLL + k, k % 2 if spread else BACKGROUND_PRIORITY)
            return carry

        lax.fori_loop(0, LANES // DMA_BURST_UNROLL, body, 0)


def _lines(row):
    n = row.shape[1]
    line = lax.broadcasted_iota(jnp.int32, (SUBLANES, LANES), 0)
    out = jnp.zeros((SUBLANES, LANES), jnp.int32)
    for j in range(n // LANES):
        out = jnp.where(line == j, row[:, j * LANES:(j + 1) * LANES], out)
    return out


class _RowGather:
    def __init__(self, pos_ref, pos_next_ref, ys_hbm, ybuf, sems, step, n_steps, tm, spread):
        self.pos_ref, self.pos_next_ref, self.ys_hbm, self.ybuf, self.sems = pos_ref, pos_next_ref, ys_hbm, ybuf, sems
        self.step, self.n_steps, self.tm, self.spread = step, n_steps, tm, spread

    def _request(self, p_ref, slot):
        def issue(line, lane, priority):
            _token_copy(self.ys_hbm, p_ref[0, line, lane], self.ybuf, slot * self.tm + line * LANES + lane,
                        self.sems.at[slot]).start(priority=priority)

        _issue_burst(self.tm, issue, self.spread)

    def request_next(self):
        @pl.when(self.step + 1 < self.n_steps)
        def _():
            self._request(self.pos_next_ref, (self.step + 1) % 2)

    def current(self):
        rows = self.tm * TOKEN_ROWS

        @pl.when(self.step == 0)
        def _():
            self._request(self.pos_ref, 0)

        slot = self.step % 2
        base = pl.multiple_of(slot * rows, rows)
        pltpu.make_async_copy(self.ys_hbm.at[pl.ds(0, rows), :], self.ybuf.at[pl.ds(base, rows), :],
                              self.sems.at[slot]).wait()
        return jnp.concatenate([self.ybuf[pl.ds(base + s, self.tm, stride=TOKEN_ROWS), :]
                                for s in range(D_MODEL // LANES)], axis=1)


def _gather_specs(n_steps, tm, index_of):
    return [pl.BlockSpec((1, SUBLANES, LANES), lambda *g: (index_of(*g), 0, 0), memory_space=pltpu.SMEM),
            pl.BlockSpec((1, SUBLANES, LANES), lambda *g: (jnp.minimum(index_of(*g) + 1, n_steps - 1), 0, 0),
                         memory_space=pltpu.SMEM),
            pl.BlockSpec(memory_space=pl.ANY)]


def _pos_lines(pos, tm):
    lines = pos.reshape(-1, tm // LANES, LANES)
    return jnp.pad(lines, ((0, 0), (0, SUBLANES - tm // LANES), (0, 0)))


def _gather_scratch(tm):
    return [pltpu.VMEM((2 * tm * TOKEN_ROWS, LANES), F32), pltpu.SemaphoreType.DMA((2,))]


def _gmlp_kernel(pos_ref, pos_next_ref, ys_hbm, gtf_ref, x_ref, gain_ref, sh_ref, sc_ref, win_ref, vg_ref, ws_ref,
                 bs_ref, wo_ref, gt_ref, fgain_ref, fsh_ref, fsc_ref, wr_ref, br_ref, x1_ref, xs_hbm, pos_out, tab_out,
                 cnt_out, gated_scr, ybuf, sems, *route_scratch, tm, n_steps):
    step = pl.program_id(0) * pl.num_programs(1) + pl.program_id(1)
    gather = _RowGather(pos_ref, pos_next_ref, ys_hbm, ybuf, sems, step, n_steps, tm, spread=False)
    gather.request_next()
    x = x_ref[0] + gtf_ref[0, 0] * gather.current()
    hn = _norm_modulate(x, gain_ref[...], sh_ref[0, 0], sc_ref[0, 0]).astype(BF16)
    v = _gelu_x2(jnp.dot(hn, win_ref[:, GMLP_HALF:], preferred_element_type=F32))
    u = _gelu_x2(jnp.dot(hn, win_ref[:, :GMLP_HALF], preferred_element_type=F32))
    mu = jnp.mean(v, axis=-1, keepdims=True)
    vc = v - mu
    vn = (vc * lax.rsqrt(jnp.mean(vc * vc, axis=-1, keepdims=True) + 4.0 * EPS) * vg_ref[...]).astype(BF16)
    gw = GMLP_HALF // GMLP_GROUPS
    for c in range(tm // CHUNK):
        rs = slice(c * CHUNK, (c + 1) * CHUNK)
        for g in range(GMLP_GROUPS):
            cs = slice(g * gw, (g + 1) * gw)
            vs = jnp.dot(ws_ref[g], vn[rs, cs], preferred_element_type=F32) + bs_ref[:, g:g + 1]
            gated_scr[rs, cs] = (u[rs, cs] * vs).astype(BF16)
    y = jnp.dot(gated_scr[...], wo_ref[...], preferred_element_type=F32)
    x1 = x + gt_ref[0, 0] * y
    x1_ref[0] = x1
    _route_outputs(x1, fgain_ref, fsh_ref, fsc_ref, wr_ref, br_ref, xs_hbm, pos_out, tab_out, cnt_out,
                   route_scratch, n_steps)


def _gmlp(ys, pos, gt_prev, x, gain, shift, scale, w_in, v_gain, w_s, b_s_t, w_o, gt, fgain, fshift, fscale,
          wr_t, br, tm=512):
    b, s, _ = x.shape
    tok = lambda w: pl.BlockSpec((1, tm, w), lambda bi, i: (bi, i, 0))
    c2 = lambda shape: _const_spec(shape, lambda bi, i: (0,) * len(shape))
    out_specs, out_shape = _route_out_specs(b, s, tm)
    n_steps = b * s // tm
    return pl.pallas_call(
        functools.partial(_gmlp_kernel, tm=tm, n_steps=n_steps),
        grid=(b, s // tm),
        in_specs=_gather_specs(n_steps, tm, lambda bi, i: bi * (s // tm) + i) + [
                  _mod_spec(),
                  tok(D_MODEL), c2((1, D_MODEL)), _mod_spec(), _mod_spec(),
                  c2((D_MODEL, 2 * GMLP_HALF)), c2((1, GMLP_HALF)),
                  c2((GMLP_GROUPS, CHUNK, CHUNK)), c2((CHUNK, GMLP_GROUPS)),
                  c2((GMLP_HALF, D_MODEL)), _mod_spec(),
                  c2((1, D_MODEL)), _mod_spec(), _mod_spec(),
                  c2((ROUTER_ROWS, D_MODEL)), c2((ROUTER_ROWS, 1))],
        out_specs=out_specs,
        out_shape=out_shape,
        scratch_shapes=[pltpu.VMEM((tm, GMLP_HALF), BF16)] + _gather_scratch(tm) + _route_scratch(tm),
        compiler_params=_cparams(("arbitrary", "arbitrary")),
        name="gmlp",
    )(_pos_lines(pos, tm), _pos_lines(pos, tm), ys, gt_prev,
      x, gain, shift, scale, w_in, v_gain, w_s, b_s_t, w_o, gt, fgain, fshift, fscale, wr_t, br)


def _token_copy(src, src_tok, dst, dst_tok, sem):
    s0 = pl.multiple_of(src_tok * TOKEN_ROWS, TOKEN_ROWS)
    d0 = pl.multiple_of(dst_tok * TOKEN_ROWS, TOKEN_ROWS)
    return pltpu.make_async_copy(src.at[pl.ds(s0, TOKEN_ROWS), :], dst.at[pl.ds(d0, TOKEN_ROWS), :], sem)


def _slabs(ref, n_tok, n_slab):
    return jnp.concatenate([ref[pl.ds(s, n_tok, stride=TOKEN_ROWS), :] for s in range(n_slab)], axis=1)


def _expert_kernel(ea_ref, eb_ref, blk_ref, nused_ref, x_ref, wr_ref, br_ref, wga, wgb, wua, wub, wda, wdb, y_ref,
                   lt_scr):
    j = pl.program_id(0)

    @pl.when(j < nused_ref[0])
    def _():
        h = _slabs(x_ref, MOE_TILE, TOKEN_ROWS).astype(BF16)
        lt = _router_logits(h, wr_ref, br_ref)
        g_val, _ = _group_choice(lt)
        lt_scr[...] = lt
        el_a = lt_scr[pl.ds(N_EXPERT_GROUPS + ea_ref[j], 1), :]
        el_b = lt_scr[pl.ds(N_EXPERT_GROUPS + eb_ref[j], 1), :]
        top = jnp.maximum(el_a, el_b)
        p_a = jnp.exp(el_a - top)
        p_b = jnp.exp(el_b - top)
        row = lax.broadcasted_iota(jnp.int32, (LANES, MOE_TILE), 0)
        gates_t = jnp.where(row == 0, g_val * (p_a / (p_a + p_b)), jnp.where(row == 1, g_val * (p_b / (p_a + p_b)), 0.0))
        gates = gates_t.T
        y = jnp.zeros((MOE_TILE, D_MODEL), F32)
        for gate, wg, wu, wd in ((gates[:, 0:1], wga, wua, wda), (gates[:, 1:2], wgb, wub, wdb)):
            hg = jnp.dot(h, wg[0, 0], preferred_element_type=F32)
            hu = jnp.dot(h, wu[0, 0], preferred_element_type=F32)
            act = (_silu(hg) * hu * gate).astype(BF16)
            y = y + jnp.dot(act, wd[0, 0], preferred_element_type=F32)
        for s in range(TOKEN_ROWS):
            y_ref[pl.ds(s, MOE_TILE, stride=TOKEN_ROWS), :] = y[:, s * LANES:(s + 1) * LANES]

    @pl.when(j >= nused_ref[0])
    def _():
        y_ref[...] = jnp.zeros_like(y_ref)


def _experts(xs, tile_ea, tile_eb, tile_blk, n_used, wr_t, br, wg, wu, wd, layer):
    n_tiles = xs.shape[0] // (MOE_TILE * TOKEN_ROWS)
    up = lambda sel: pl.BlockSpec((1, 1, D_MODEL, D_EXPERT),
                                  lambda j, ea, eb, blk, nu: (layer, (ea, eb)[sel][j], 0, 0))
    down = lambda sel: pl.BlockSpec((1, 1, D_EXPERT, D_MODEL),
                                    lambda j, ea, eb, blk, nu: (layer, (ea, eb)[sel][j], 0, 0))
    shape = (MOE_TILE * TOKEN_ROWS, LANES)
    return pl.pallas_call(
        _expert_kernel,
        grid_spec=pltpu.PrefetchScalarGridSpec(
            num_scalar_prefetch=4,
            grid=(n_tiles,),
            in_specs=[pl.BlockSpec(shape, lambda j, ea, eb, blk, nu: (blk[j], 0)),
                      pl.BlockSpec((ROUTER_ROWS, D_MODEL), lambda j, ea, eb, blk, nu: (0, 0)),
                      pl.BlockSpec((ROUTER_ROWS, 1), lambda j, ea, eb, blk, nu: (0, 0)),
                      up(0), up(1), up(0), up(1), down(0), down(1)],
            out_specs=pl.BlockSpec(shape, lambda j, ea, eb, blk, nu: (j, 0)),
            scratch_shapes=[pltpu.VMEM((ROUTER_ROWS, MOE_TILE), F32)]),
        out_shape=jax.ShapeDtypeStruct(xs.shape, F32),
        compiler_params=_cparams(("arbitrary",)),
        name="moe_experts",
    )(tile_ea, tile_eb, tile_blk, n_used, xs, wr_t, br, wg, wg, wu, wu, wd, wd)


def _final_kernel(pos_ref, pos_next_ref, ys_hbm, x_ref, gt_ref, fin_ref, o_ref, ybuf, sems, *, tm):
    gather = _RowGather(pos_ref, pos_next_ref, ys_hbm, ybuf, sems, pl.program_id(0), pl.num_programs(0), tm,
                        spread=True)
    gather.request_next()
    y_moe = gather.current()
    x2 = x_ref[...] + gt_ref[0, 0] * y_moe
    ms = jnp.mean(x2 * x2, axis=-1, keepdims=True)
    o_ref[...] = x2 * lax.rsqrt(ms + EPS) * fin_ref[...]


def _final(ys, pos, x, gt, fin_gain, seq, tm=512):
    t = x.shape[0]
    tiles_per_batch = seq // tm
    n_steps = t // tm
    return pl.pallas_call(
        functools.partial(_final_kernel, tm=tm),
        grid=(n_steps,),
        in_specs=_gather_specs(n_steps, tm, lambda i: i) + [
                  pl.BlockSpec((tm, D_MODEL), lambda i: (i, 0)),
                  pl.BlockSpec((1, 1, 1, D_MODEL), lambda i: (i // tiles_per_batch, 0, 0, 0)),
                  pl.BlockSpec((1, D_MODEL), lambda i: (0, 0))],
        out_specs=pl.BlockSpec((tm, D_MODEL), lambda i: (i, 0)),
        out_shape=jax.ShapeDtypeStruct((t, D_MODEL), F32),
        scratch_shapes=_gather_scratch(tm),
        compiler_params=_cparams(("arbitrary",)),
        name="moe_combine_final_norm",
    )(_pos_lines(pos, tm), _pos_lines(pos, tm), ys, x, gt, fin_gain)


_PAIR_A = np.array([0, 0, 0, 1, 1, 2], np.int32)
_PAIR_B = np.array([1, 2, 3, 2, 3, 3], np.int32)


def _moe_sorted(xs, tab, counts, wr_t, br, wg, wu, wd, layer):
    n_tiles = xs.shape[0] // (MOE_TILE * TOKEN_ROWS)
    cnt = counts[:N_CLASSES, 0].astype(jnp.int32)
    n_used = jnp.sum((cnt + MOE_TILE - 1) // MOE_TILE)
    ids = jnp.arange(n_tiles, dtype=jnp.int32)
    tile_cls = jnp.sum(tab, axis=0)[:n_tiles].astype(jnp.int32)
    last_cls = jnp.sum(jnp.where(ids == n_used - 1, tile_cls, 0))
    tile_cls = jnp.where(ids < n_used, tile_cls, last_cls)
    j = jnp.minimum(ids, n_used - 1)
    pair = tile_cls % PAIRS_PER_GROUP
    pair_a = jnp.sum(jnp.where(pair[:, None] == jnp.arange(PAIRS_PER_GROUP)[None, :], _PAIR_A[None, :], 0), axis=1)
    pair_b = jnp.sum(jnp.where(pair[:, None] == jnp.arange(PAIRS_PER_GROUP)[None, :], _PAIR_B[None, :], 0), axis=1)
    base = (tile_cls // PAIRS_PER_GROUP) * EXPERTS_PER_GROUP
    return _experts(xs, base + pair_a, base + pair_b, j, n_used.reshape(1), wr_t, br, wg, wu, wd, layer)


def _rope_tables(seq):
    inv_freq = ROPE_THETA ** (-jnp.arange(0, ROT_DIM, 2, dtype=jnp.float32) / ROT_DIM)
    ang = jnp.arange(seq).astype(jnp.float32)[:, None] * inv_freq[None, :]
    cos, sin = jnp.cos(ang), jnp.sin(ang)
    ones = jnp.ones((seq, HEAD_DIM - ROT_DIM), jnp.float32)
    cos_h = jnp.concatenate([cos, cos, ones], axis=1)
    sgn_h = jnp.concatenate([-sin, sin, 0.0 * ones], axis=1)
    reps = LANES // HEAD_DIM
    return jnp.tile(cos_h, (1, reps)).astype(F32), jnp.tile(sgn_h, (1, reps)).astype(F32)


def _pad_heads(w, axis):
    pad = [(0, 0)] * w.ndim
    pad[axis] = (0, GROUP_WIDTH - HEADS_PER_GROUP * HEAD_DIM)
    return jnp.pad(w, pad)


def _router_weights(w_group, b_group, w_expert, b_expert):
    we = jnp.transpose(w_expert, (1, 0, 2)).reshape(D_MODEL, N_EXPERTS)
    w = jnp.concatenate([w_group, we], axis=1)
    w = jnp.pad(w, ((0, 0), (0, ROUTER_ROWS - w.shape[1])))
    bias = jnp.pad(jnp.concatenate([b_group, b_expert.reshape(-1)]), (0, ROUTER_ROWS - 20))
    return w.T.astype(BF16), bias.reshape(ROUTER_ROWS, 1)


def kernel(x, c, norm_mix, norm_ffn, w_ada, b_ada, a_w_qkv, a_w_o, b_w_in, b_v_gain, b_w_s, b_b_s, b_w_o,
           r_w_group, r_b_group, r_w_expert, r_b_expert, e_w_gate, e_w_up, e_w_down, final_norm):
    b, s, _ = x.shape
    mod = _ada_mod(c, w_ada, b_ada)
    sh_m, sc_m, gt_m, sh_f, sc_f, gt_f = [mod[:, :, i:i + 1] for i in range(6)]
    row = lambda v: v.reshape(1, -1)
    gw = HEADS_PER_GROUP * HEAD_DIM
    e_w_gate, e_w_up, e_w_down = [w.astype(BF16) for w in (e_w_gate, e_w_up, e_w_down)]

    w = a_w_qkv[0].reshape(D_MODEL, 3, N_ATTN_GROUPS, gw)
    w = w * jnp.asarray([HEAD_DIM ** -0.5, 1.0, 1.0], F32)[None, :, None, None]
    w_groups = jnp.transpose(_pad_heads(w, 3), (2, 0, 1, 3)).reshape(
        N_ATTN_GROUPS, D_MODEL, 3 * GROUP_WIDTH).astype(BF16)
    wo_groups = _pad_heads(a_w_o[0].reshape(N_ATTN_GROUPS, gw, D_MODEL), 1).astype(BF16)
    expand = (jnp.arange(LANES)[:, None] == jnp.arange(GROUP_WIDTH)[None, :] // HEAD_DIM).astype(BF16)
    cos_t, sgn_t = _rope_tables(s)
    qkv = _qkv(x, row(norm_mix[0]), sh_m[0], sc_m[0], w_groups, cos_t, sgn_t)
    o_list, l_list = [], []
    for g in range(N_ATTN_GROUPS):
        o, l = _attention(*qkv[3 * g:3 * g + 3])
        o_list.append(o)
        l_list.append(l)
    wr_t, br = _router_weights(r_w_group[0], r_b_group[0], r_w_expert[0], r_b_expert[0])
    x1, xs, pos, tab, counts = _attn_out(o_list, l_list, x, wo_groups, expand, gt_m[0], row(norm_ffn[0]),
                                         sh_f[0], sc_f[0], wr_t, br)
    t = b * s
    token_order = lambda p: p[:, :, :t // (p.shape[0] * p.shape[1] * LANES), :].reshape(t)
    ys = _moe_sorted(xs, tab, counts, wr_t, br, e_w_gate, e_w_up, e_w_down, 0)

    wr_t, br = _router_weights(r_w_group[1], r_b_group[1], r_w_expert[1], r_b_expert[1])
    x3, xs, pos, tab, counts = _gmlp(ys, token_order(pos), gt_f[0], x1, row(norm_mix[1]), sh_m[1], sc_m[1],
                                     b_w_in[0].astype(BF16), row(b_v_gain[0]), (0.5 * b_w_s[0]).astype(BF16), 0.5 * b_b_s[0].T,
                                     b_w_o[0].astype(BF16), gt_m[1], row(norm_ffn[1]), sh_f[1], sc_f[1], wr_t, br)
    ys = _moe_sorted(xs, tab, counts, wr_t, br, e_w_gate, e_w_up, e_w_down, 1)
    out = _final(ys, token_order(pos), x3.reshape(t, D_MODEL), gt_f[1], row(final_norm), s)
    return out.reshape(b, s, D_MODEL)
```

```python
import functools

import jax
import jax.numpy as jnp
import numpy as np
from jax import lax
from jax.experimental import pallas as pl
from jax.experimental.pallas import tpu as pltpu

D_MODEL = 1024
DEPTH = 2
HEAD_DIM = 64
HEADS_PER_GROUP = 5
N_ATTN_GROUPS = 3
DILATIONS = (1, 4, 16)
BAND_RADIUS = 64
ROT_DIM = HEAD_DIM // 4
ROPE_THETA = 500000.0
NEG_INF = -1e30
CHUNK = 128
GMLP_HALF = 2 * D_MODEL
GMLP_GROUPS = 8
N_EXPERT_GROUPS = 4
EXPERTS_PER_GROUP = 4
N_EXPERTS = 16
D_EXPERT = 256
EPS = 1e-6

LANES = 128
GROUP_WIDTH = 384
ROUTER_ROWS = 32
PAIRS_PER_GROUP = 6
N_CLASSES = N_EXPERT_GROUPS * PAIRS_PER_GROUP
CLASS_ROWS = 32
SUBLANES = 8
TOKEN_ROWS = D_MODEL // LANES
MOE_TILE = 512

F32 = jnp.float32
BF16 = jnp.bfloat16
VMEM_LIMIT = 56 * 1024 * 1024


def _cparams(sem):
    return pltpu.CompilerParams(dimension_semantics=sem, vmem_limit_bytes=VMEM_LIMIT)


def _const_spec(shape, index_map):
    return pl.BlockSpec(shape, index_map, pipeline_mode=pl.Buffered(1))


def _silu(x):
    return x * (1.0 / (1.0 + jnp.exp(-x)))


def _gelu_x2(x):
    return x * (1.0 + lax.erf(x * (2.0 ** -0.5)))


def _norm_modulate(x, gain, shift, scale):
    ms = jnp.mean(x * x, axis=-1, keepdims=True)
    return (x * lax.rsqrt(ms + EPS)) * (gain * (1.0 + scale)) + shift


def _split_bf16(x):
    hi = x.astype(BF16)
    return hi, (x - hi.astype(F32)).astype(BF16)


def _ada_kernel(c_ref, w_ref, b_ref, o_ref):
    ca_hi, ca_lo = _split_bf16(_silu(c_ref[...]))
    w_hi, w_lo = _split_bf16(w_ref[0])
    dot = functools.partial(jnp.dot, preferred_element_type=F32)
    o_ref[0] = dot(ca_hi, w_hi) + dot(ca_lo, w_hi) + dot(ca_hi, w_lo) + b_ref[0]


def _ada_mod(c, w_ada, b_ada):
    b = c.shape[0]
    tn = 1536
    c8 = jnp.pad(c, ((0, 8 - b), (0, 0)))
    out = pl.pallas_call(
        _ada_kernel,
        grid=(DEPTH, 6 * D_MODEL // tn),
        in_specs=[pl.BlockSpec((8, D_MODEL), lambda l, j: (0, 0)),
                  pl.BlockSpec((1, D_MODEL, tn), lambda l, j: (l, 0, j)),
                  pl.BlockSpec((1, 1, tn), lambda l, j: (l, 0, j))],
        out_specs=pl.BlockSpec((1, 8, tn), lambda l, j: (l, 0, j)),
        out_shape=jax.ShapeDtypeStruct((DEPTH, 8, 6 * D_MODEL), F32),
        compiler_params=_cparams(("arbitrary", "arbitrary")),
        name="ada_mod",
    )(c8, w_ada, b_ada.reshape(DEPTH, 1, 6 * D_MODEL))
    return out[:, :b].reshape(DEPTH, b, 6, 1, D_MODEL)


def _router_logits(hf, wr_ref, br_ref):
    return lax.dot_general(wr_ref[...], hf, (((1,), (1,)), ((), ())), preferred_element_type=F32) + br_ref[...]


def _group_choice(lt):
    gl = [lt[i:i + 1, :] for i in range(N_EXPERT_GROUPS)]
    gmax = jnp.maximum(jnp.maximum(gl[0], gl[1]), jnp.maximum(gl[2], gl[3]))
    ge = [jnp.exp(g - gmax) for g in gl]
    gsum = ge[0] + ge[1] + ge[2] + ge[3]
    gp = [e / gsum for e in ge]
    g_val = jnp.maximum(jnp.maximum(gp[0], gp[1]), jnp.maximum(gp[2], gp[3]))
    g_idx = jnp.where(gp[0] == g_val, 0, jnp.where(gp[1] == g_val, 1, jnp.where(gp[2] == g_val, 2, 3)))
    return g_val, g_idx


def _ffn_prep(x1, gain, shift, scale, wr_ref, br_ref):
    hf32 = _norm_modulate(x1, gain, shift, scale)
    lt = _router_logits(hf32.astype(BF16), wr_ref, br_ref)
    _, g_idx = _group_choice(lt)
    el = []
    for j in range(EXPERTS_PER_GROUP):
        rows = [lt[4 + 4 * g + j:5 + 4 * g + j, :] for g in range(N_EXPERT_GROUPS)]
        el.append(jnp.where(g_idx == 0, rows[0], jnp.where(g_idx == 1, rows[1],
                  jnp.where(g_idx == 2, rows[2], rows[3]))))
    v1 = jnp.maximum(jnp.maximum(el[0], el[1]), jnp.maximum(el[2], el[3]))
    i1 = jnp.where(el[0] == v1, 0, jnp.where(el[1] == v1, 1, jnp.where(el[2] == v1, 2, 3)))
    el2 = [jnp.where(i1 == j, -jnp.inf, el[j]) for j in range(EXPERTS_PER_GROUP)]
    v2 = jnp.maximum(jnp.maximum(el2[0], el2[1]), jnp.maximum(el2[2], el2[3]))
    i2 = jnp.where((el2[0] == v2) & (i1 != 0), 0,
                   jnp.where((el2[1] == v2) & (i1 != 1), 1,
                             jnp.where((el2[2] == v2) & (i1 != 2), 2, 3)))
    a = jnp.minimum(i1, i2)
    bb = jnp.maximum(i1, i2)
    cls = g_idx * PAIRS_PER_GROUP + ((a * (7 - a)) >> 1) + (bb - a - 1)
    return hf32, cls


def _place_tokens(cls, carry_ref, cur_ref, tab_ref, first_step):
    tm = cls.shape[1]

    @pl.when(first_step)
    def _():
        carry_ref[...] = jnp.zeros_like(carry_ref)
        cur_ref[...] = jnp.zeros_like(cur_ref)
        tab_ref[...] = jnp.zeros_like(tab_ref)

    crow = lax.broadcasted_iota(jnp.int32, (CLASS_ROWS, tm), 0)
    onehot = (crow == cls).astype(F32)
    si = lax.broadcasted_iota(jnp.int32, (tm, tm), 0)
    ti = lax.broadcasted_iota(jnp.int32, (tm, tm), 1)
    before = (si < ti).astype(BF16)
    prefix = jnp.dot(onehot.astype(BF16), before, preferred_element_type=F32)
    seen = prefix + carry_ref[:, 0:1]
    rank = jnp.sum(onehot * seen, axis=0, keepdims=True)
    inv_tile = 1.0 / MOE_TILE
    rem = rank - MOE_TILE * jnp.floor(rank * inv_tile)
    opens = (rem == 0.0).astype(F32)
    opened_before = jnp.sum(jnp.floor((seen + (MOE_TILE - 1)) * inv_tile), axis=0, keepdims=True)
    sel = onehot * opens
    opened = jnp.sum(sel, axis=1, keepdims=True) > 0.0
    new_tile = jnp.sum(sel * opened_before, axis=1, keepdims=True)
    first_rank = jnp.sum(sel * rank, axis=1, keepdims=True)
    tile = jnp.sum(onehot * jnp.where(opened & (seen >= first_rank), new_tile, cur_ref[:, 0:1]),
                   axis=0, keepdims=True)
    carry_ref[...] = carry_ref[...] + jnp.sum(onehot, axis=1, keepdims=True)
    cur_ref[...] = jnp.where(opened, new_tile, cur_ref[...])
    lane = lax.broadcasted_iota(jnp.int32, (CLASS_ROWS, LANES), 1).astype(F32)
    cidx = lax.broadcasted_iota(jnp.int32, (CLASS_ROWS, LANES), 0).astype(F32)
    tab_ref[...] = tab_ref[...] + jnp.where(opened & (lane == new_tile), cidx, 0.0)
    return (tile * MOE_TILE + rem).astype(jnp.int32)


def _rope(t, cos, sgn, first_half):
    out = []
    for c in range(GROUP_WIDTH // LANES):
        xc = t[:, c * LANES:(c + 1) * LANES]
        other = jnp.where(first_half, pltpu.roll(xc, LANES - ROT_DIM // 2, 1),
                          pltpu.roll(xc, ROT_DIM // 2, 1))
        out.append(xc * cos + other * sgn)
    return jnp.concatenate(out, axis=1)


def _qkv_kernel(x_ref, gain_ref, sh_ref, sc_ref, w_ref, cos_ref, sgn_ref, *rest, tm):
    out_refs, hn_scr = rest[:9], rest[9]
    hn = _norm_modulate(x_ref[0], gain_ref[...], sh_ref[0, 0], sc_ref[0, 0])
    n_slab = D_MODEL // LANES
    lane = lax.broadcasted_iota(jnp.int32, (tm, LANES), 1) % HEAD_DIM
    first_half = lane < ROT_DIM // 2
    for g, d in enumerate(DILATIONS):
        rows = tm // d
        if d == 1:
            hp, cos, sgn = hn, cos_ref[...], sgn_ref[...]
        else:
            if g == 1:
                for s in range(n_slab):
                    hn_scr[s] = hn[:, s * LANES:(s + 1) * LANES]
            hp = jnp.concatenate(
                [jnp.concatenate([hn_scr[s, pl.ds(r, rows, stride=d), :] for s in range(n_slab)], axis=1)
                 for r in range(d)], axis=0)
            cos = jnp.concatenate([cos_ref[pl.ds(r, rows, stride=d), :] for r in range(d)], axis=0)
            sgn = jnp.concatenate([sgn_ref[pl.ds(r, rows, stride=d), :] for r in range(d)], axis=0)
        res = jnp.dot(hp.astype(BF16), w_ref[g], preferred_element_type=F32)
        q = _rope(res[:, :GROUP_WIDTH], cos, sgn, first_half)
        k = _rope(res[:, GROUP_WIDTH:2 * GROUP_WIDTH], cos, sgn, first_half)
        v = res[:, 2 * GROUP_WIDTH:]
        for t, o_ref in zip((q, k, v), out_refs[3 * g:3 * g + 3]):
            o_ref[0] = t.astype(BF16).reshape(d, rows, GROUP_WIDTH)


def _qkv(x, gain, shift, scale, w_groups, cos_t, sgn_t, tm=1024):
    b, s, _ = x.shape
    out_shape, out_specs = [], []
    for d in DILATIONS:
        for _ in range(3):
            out_shape.append(jax.ShapeDtypeStruct((b, d, s // d, GROUP_WIDTH), BF16))
            out_specs.append(pl.BlockSpec((1, d, tm // d, GROUP_WIDTH), lambda bi, i: (bi, 0, i, 0)))
    return pl.pallas_call(
        functools.partial(_qkv_kernel, tm=tm),
        grid=(b, s // tm),
        in_specs=[pl.BlockSpec((1, tm, D_MODEL), lambda bi, i: (bi, i, 0)),
                  _const_spec((1, D_MODEL), lambda bi, i: (0, 0)),
                  pl.BlockSpec((1, 1, 1, D_MODEL), lambda bi, i: (bi, 0, 0, 0)),
                  pl.BlockSpec((1, 1, 1, D_MODEL), lambda bi, i: (bi, 0, 0, 0)),
                  _const_spec((N_ATTN_GROUPS, D_MODEL, 3 * GROUP_WIDTH), lambda bi, i: (0, 0, 0)),
                  pl.BlockSpec((tm, LANES), lambda bi, i: (i, 0)),
                  pl.BlockSpec((tm, LANES), lambda bi, i: (i, 0))],
        out_specs=out_specs,
        out_shape=out_shape,
        scratch_shapes=[pltpu.VMEM((D_MODEL // LANES, tm, LANES), F32)],
        compiler_params=_cparams(("arbitrary", "arbitrary")),
        name="qkv_rope",
    )(x, gain, shift, scale, w_groups, cos_t, sgn_t)


def _attn_kernel(q_ref, kp_ref, kc_ref, kn_ref, vp_ref, vc_ref, vn_ref, o_ref, lse_ref,
                 kbuf, vbuf, *, seq, tq):
    i = pl.program_id(2)
    r = BAND_RADIUS
    kbuf[0:r] = kp_ref[0, 0]
    kbuf[r:r + tq] = kc_ref[0, 0]
    kbuf[r + tq:] = kn_ref[0, 0]
    vbuf[0:r] = vp_ref[0, 0]
    vbuf[r:r + tq] = vc_ref[0, 0]
    vbuf[r + tq:] = vn_ref[0, 0]
    qb = 2 * r
    kw = qb + 2 * r
    lane = lax.broadcasted_iota(jnp.int32, (qb, LANES), 1)
    low = lane < HEAD_DIM
    tidx = lax.broadcasted_iota(jnp.int32, (qb, kw), 0)
    kidx = lax.broadcasted_iota(jnp.int32, (qb, kw), 1)
    rel = kidx - tidx
    band = (rel >= 0) & (rel <= 2 * r)
    for j in range(tq // qb):
        kpos = i * tq + (j * qb - r) + kidx
        mask = band & (kpos >= 0) & (kpos < seq)
        q = q_ref[0, 0, j * qb:(j + 1) * qb, :]
        k = kbuf[j * qb:j * qb + kw, :]
        v = vbuf[j * qb:j * qb + kw, :]
        lse_tile = jnp.zeros((qb, LANES), F32)
        o_chunks = []
        for c in range(GROUP_WIDTH // LANES):
            qc = q[:, c * LANES:(c + 1) * LANES]
            kc = k[:, c * LANES:(c + 1) * LANES]
            vc = v[:, c * LANES:(c + 1) * LANES]
            outs = []
            for hh in range(2):
                head = 2 * c + hh
                if head >= HEADS_PER_GROUP:
                    outs.append(jnp.zeros((qb, LANES), F32))
                    continue
                qm = jnp.where(low if hh == 0 else jnp.logical_not(low), qc, jnp.zeros_like(qc))
                s = lax.dot_general(qm, kc, (((1,), (1,)), ((), ())), preferred_element_type=F32)
                s = jnp.where(mask, s, NEG_INF)
                m = jnp.max(s, axis=-1, keepdims=True)
                p = jnp.exp(s - m)
                den = jnp.sum(p, axis=-1, keepdims=True)
                o = jnp.dot(p.astype(BF16), vc, preferred_element_type=F32) / den
                lse_tile = jnp.where(lane == head, m + jnp.log(den), lse_tile)
                outs.append(o)
            o_chunks.append(jnp.where(low, outs[0], outs[1]))
        o_ref[0, 0, j * qb:(j + 1) * qb, :] = jnp.concatenate(o_chunks, axis=1).astype(BF16)
        lse_ref[0, 0, j * qb:(j + 1) * qb, :] = lse_tile


def _attention(q, k, v, tq=2048):
    b, d, seq, w = q.shape
    tq = min(tq, seq)
    r = BAND_RADIUS
    nb = seq // r
    cur = pl.BlockSpec((1, 1, tq, w), lambda bi, ri, i: (bi, ri, i, 0))
    prev = pl.BlockSpec((1, 1, r, w), lambda bi, ri, i: (bi, ri, jnp.maximum(i * (tq // r) - 1, 0), 0))
    nxt = pl.BlockSpec((1, 1, r, w), lambda bi, ri, i: (bi, ri, jnp.minimum((i + 1) * (tq // r), nb - 1), 0))
    return pl.pallas_call(
        functools.partial(_attn_kernel, seq=seq, tq=tq),
        grid=(b, d, seq // tq),
        in_specs=[cur, prev, cur, nxt, prev, cur, nxt],
        out_specs=[pl.BlockSpec((1, 1, tq, w), lambda bi, ri, i: (bi, ri, i, 0)),
                   pl.BlockSpec((1, 1, tq, LANES), lambda bi, ri, i: (bi, ri, i, 0))],
        out_shape=[jax.ShapeDtypeStruct((b, d, seq, w), BF16),
                   jax.ShapeDtypeStruct((b, d, seq, LANES), F32)],
        scratch_shapes=[pltpu.VMEM((tq + 2 * r, w), BF16), pltpu.VMEM((tq + 2 * r, w), BF16)],
        compiler_params=_cparams(("arbitrary", "arbitrary", "arbitrary")),
        name=f"band_attn_d{d}",
    )(q, k, k, k, v, v, v)


def _attn_out_kernel(o0, o1, o2, l0, l1, l2, x_ref, wo_ref, exp_ref, gt_ref, gain_ref, sh_ref, sc_ref,
                     wr_ref, br_ref, x1_ref, xs_hbm, pos_ref, tab_out, cnt_out, oscr, lscr, *route_scratch,
                     tm, n_steps):
    n_slab = GROUP_WIDTH // LANES
    for g, (d, o_ref, l_ref) in enumerate(zip(DILATIONS, (o0, o1, o2), (l0, l1, l2))):
        rows = tm // d
        for r in range(d):
            dst = slice(None) if d == 1 else pl.ds(r, rows, stride=d)
            blk = o_ref[0, r].astype(F32)
            for c in range(n_slab):
                oscr[g * n_slab + c, dst, :] = blk[:, c * LANES:(c + 1) * LANES]
            lscr[g, dst, :] = l_ref[0, r]
    lse = [lscr[g] for g in range(N_ATTN_GROUPS)]
    mx = jnp.maximum(jnp.maximum(lse[0], lse[1]), lse[2])
    ex = [jnp.exp(l - mx) for l in lse]
    tot = ex[0] + ex[1] + ex[2]
    y = jnp.zeros((tm, D_MODEL), F32)
    for g in range(N_ATTN_GROUPS):
        alpha = ex[g] / tot
        a_hi, a_lo = _split_bf16(alpha)
        a_wide = (jnp.dot(a_hi, exp_ref[...], preferred_element_type=F32)
                  + jnp.dot(a_lo, exp_ref[...], preferred_element_type=F32))
        og = jnp.concatenate([oscr[g * n_slab + c] for c in range(n_slab)], axis=1)
        y = y + jnp.dot((og * a_wide).astype(BF16), wo_ref[g], preferred_element_type=F32)
    x1 = x_ref[0] + gt_ref[0, 0] * y
    x1_ref[0] = x1
    _route_outputs(x1, gain_ref, sh_ref, sc_ref, wr_ref, br_ref, xs_hbm, pos_ref, tab_out, cnt_out, route_scratch,
                   n_steps)


ZERO_TOKENS = MOE_TILE // 2


def _route_outputs(x1, gain_ref, sh_ref, sc_ref, wr_ref, br_ref, xs_hbm, pos_ref, tab_out, cnt_out, scratch,
                   n_steps):
    (carry_ref, cur_ref, tab_ref, rows_scr, pos_vmem, pos_smem, fin_vmem, fin_smem, zbuf, sems, aux_sem) = scratch
    step = pl.program_id(0) * pl.num_programs(1) + pl.program_id(1)
    tm = x1.shape[0]
    rows = tm * TOKEN_ROWS
    hf32, cls = _ffn_prep(x1, gain_ref[...], sh_ref[0, 0], sc_ref[0, 0], wr_ref, br_ref)
    pos = _place_tokens(cls, carry_ref, cur_ref, tab_ref, step == 0)
    pos_lines = _lines(pos)
    pos_ref[0, 0] = pos_lines
    pos_vmem[...] = pos_lines
    to_smem = pltpu.make_async_copy(pos_vmem, pos_smem, aux_sem)
    to_smem.start()

    def drain(slot):
        pltpu.make_async_copy(rows_scr.at[pl.ds(0, rows), :], xs_hbm.at[pl.ds(0, rows), :], sems.at[slot]).wait()

    slot = step % 2

    @pl.when(step >= 2)
    def _():
        drain(slot)

    base = pl.multiple_of(slot * rows, rows)
    for s in range(TOKEN_ROWS):
        rows_scr[pl.ds(base + s, tm, stride=TOKEN_ROWS), :] = hf32[:, s * LANES:(s + 1) * LANES]
    to_smem.wait()

    def issue(line, lane, priority):
        _token_copy(rows_scr, slot * tm + line * LANES + lane, xs_hbm, pos_smem[line, lane],
                    sems.at[slot]).start(priority=priority)

    _issue_burst(tm, issue, spread=False)
    tab_out[...] = tab_ref[...]
    cnt_out[...] = carry_ref[...]

    @pl.when(step == n_steps - 1)
    def _():
        drain(slot)
        if n_steps > 1:
            drain(1 - slot)
        _clear_unused_rows(carry_ref, cur_ref, xs_hbm, fin_vmem, fin_smem, zbuf, aux_sem)


def _clear_unused_rows(carry_ref, cur_ref, xs_hbm, fin_vmem, fin_smem, zbuf, sem):
    lane = lax.broadcasted_iota(jnp.int32, (CLASS_ROWS, LANES), 1)
    crow = lax.broadcasted_iota(jnp.int32, (CLASS_ROWS, LANES), 0)
    on_lane = lambda col: jnp.sum(jnp.where(lane == crow, col, 0.0), axis=0, keepdims=True)
    counts = carry_ref[...]
    used = jnp.sum(jnp.floor((counts + (MOE_TILE - 1)) * (1.0 / MOE_TILE)), axis=0, keepdims=True)
    r8 = lax.broadcasted_iota(jnp.int32, (SUBLANES, LANES), 0)
    fin = jnp.where(r8 == 0, on_lane(cur_ref[...]), jnp.where(r8 == 1, on_lane(counts), jnp.where(r8 == 2, used, 0.0)))
    fin_vmem[...] = fin.astype(jnp.int32)
    to_smem = pltpu.make_async_copy(fin_vmem, fin_smem, sem)
    to_smem.start()
    zbuf[...] = jnp.zeros_like(zbuf)
    to_smem.wait()

    def zero_copy(first_tok, n_tok):
        d0 = pl.multiple_of(first_tok * TOKEN_ROWS, TOKEN_ROWS)
        return pltpu.make_async_copy(zbuf.at[pl.ds(0, n_tok * TOKEN_ROWS), :],
                                     xs_hbm.at[pl.ds(d0, n_tok * TOKEN_ROWS), :], sem)

    for wait in (False, True):
        for c in range(N_CLASSES):
            count = fin_smem[1, c]
            fill = jnp.where(count > 0, (MOE_TILE - count % MOE_TILE) % MOE_TILE, 0)
            first = fin_smem[0, c] * MOE_TILE + (MOE_TILE - fill)
            piece = ZERO_TOKENS
            while piece >= 1:
                done = (fill // (2 * piece)) * (2 * piece)

                @pl.when((fill // piece) % 2 == 1)
                def _(first=first, done=done, piece=piece):
                    cp = zero_copy(first + done, piece)
                    cp.wait() if wait else cp.start()

                piece //= 2

    def clear_tile(j, carry):
        for half in range(MOE_TILE // ZERO_TOKENS):
            cp = zero_copy(j * MOE_TILE + half * ZERO_TOKENS, ZERO_TOKENS)
            cp.start()
            cp.wait()
        return carry

    lax.fori_loop(fin_smem[2, 0], xs_hbm.shape[0] // (MOE_TILE * TOKEN_ROWS), clear_tile, 0)


def _route_scratch(tm):
    assert tm <= MOE_TILE, "a class may open at most one tile per grid step"
    return [pltpu.VMEM((CLASS_ROWS, LANES), F32), pltpu.VMEM((CLASS_ROWS, LANES), F32),
            pltpu.VMEM((CLASS_ROWS, LANES), F32),
            pltpu.VMEM((2 * tm * TOKEN_ROWS, LANES), F32),
            pltpu.VMEM((SUBLANES, LANES), jnp.int32), pltpu.SMEM((SUBLANES, LANES), jnp.int32),
            pltpu.VMEM((SUBLANES, LANES), jnp.int32), pltpu.SMEM((SUBLANES, LANES), jnp.int32),
            pltpu.VMEM((ZERO_TOKENS * TOKEN_ROWS, LANES), F32),
            pltpu.SemaphoreType.DMA((2,)), pltpu.SemaphoreType.DMA]


def _route_out_specs(b, s, tm):
    n_tiles = b * s // MOE_TILE + N_CLASSES
    specs = [pl.BlockSpec((1, tm, D_MODEL), lambda bi, i: (bi, i, 0)),
             pl.BlockSpec(memory_space=pl.ANY),
             pl.BlockSpec((1, 1, SUBLANES, LANES), lambda bi, i: (bi, i, 0, 0)),
             pl.BlockSpec((CLASS_ROWS, LANES), lambda bi, i: (0, 0)),
             pl.BlockSpec((CLASS_ROWS, LANES), lambda bi, i: (0, 0))]
    shapes = [jax.ShapeDtypeStruct((b, s, D_MODEL), F32),
              jax.ShapeDtypeStruct((n_tiles * MOE_TILE * TOKEN_ROWS, LANES), F32),
              jax.ShapeDtypeStruct((b, s // tm, SUBLANES, LANES), jnp.int32),
              jax.ShapeDtypeStruct((CLASS_ROWS, LANES), F32),
              jax.ShapeDtypeStruct((CLASS_ROWS, LANES), F32)]
    return specs, shapes


def _mod_spec():
    return pl.BlockSpec((1, 1, 1, D_MODEL), lambda bi, i: (bi, 0, 0, 0))


def _attn_out(o_list, l_list, x, wo_groups, expand, gt, gain, shift, scale, wr_t, br, tm=512):
    b, s, _ = x.shape
    o_specs = [pl.BlockSpec((1, d, tm // d, GROUP_WIDTH), lambda bi, i: (bi, 0, i, 0)) for d in DILATIONS]
    l_specs = [pl.BlockSpec((1, d, tm // d, LANES), lambda bi, i: (bi, 0, i, 0)) for d in DILATIONS]
    out_specs, out_shape = _route_out_specs(b, s, tm)
    return pl.pallas_call(
        functools.partial(_attn_out_kernel, tm=tm, n_steps=b * s // tm),
        grid=(b, s // tm),
        in_specs=o_specs + l_specs + [
            pl.BlockSpec((1, tm, D_MODEL), lambda bi, i: (bi, i, 0)),
            _const_spec((N_ATTN_GROUPS, GROUP_WIDTH, D_MODEL), lambda bi, i: (0, 0, 0)),
            _const_spec((LANES, GROUP_WIDTH), lambda bi, i: (0, 0)),
            _mod_spec(),
            _const_spec((1, D_MODEL), lambda bi, i: (0, 0)),
            _mod_spec(), _mod_spec(),
            _const_spec((ROUTER_ROWS, D_MODEL), lambda bi, i: (0, 0)),
            _const_spec((ROUTER_ROWS, 1), lambda bi, i: (0, 0))],
        out_specs=out_specs,
        out_shape=out_shape,
        scratch_shapes=[pltpu.VMEM((N_ATTN_GROUPS * GROUP_WIDTH // LANES, tm, LANES), F32),
                        pltpu.VMEM((N_ATTN_GROUPS, tm, LANES), F32)] + _route_scratch(tm),
        compiler_params=_cparams(("arbitrary", "arbitrary")),
        name="attn_merge_proj",
    )(*o_list, *l_list, x, wo_groups, expand, gt, gain, shift, scale, wr_t, br)


DMA_BURST_UNROLL = 16
BACKGROUND_PRIORITY = 1


def _issue_burst(n, issue, spread=True):
    for line in range(n // LANES):
        def body(i, carry, line=line):
            for k in range(DMA_BURST_UNROLL):
                issue(line, i * DMA_BURST_UNROLL + k, k % 2 if spread else BACKGROUND_PRIORITY)
            return carry

        lax.fori_loop(0, LANES // DMA_BURST_UNROLL, body, 0)


def _lines(row):
    n = row.shape[1]
    line = lax.broadcasted_iota(jnp.int32, (SUBLANES, LANES), 0)
    out = jnp.zeros((SUBLANES, LANES), jnp.int32)
    for j in range(n // LANES):
        out = jnp.where(line == j, row[:, j * LANES:(j + 1) * LANES], out)
    return out


class _RowGather:
    def __init__(self, pos_ref, pos_next_ref, ys_hbm, ybuf, sems, step, n_steps, tm, spread):
        self.pos_ref, self.pos_next_ref, self.ys_hbm, self.ybuf, self.sems = pos_ref, pos_next_ref, ys_hbm, ybuf, sems
        self.step, self.n_steps, self.tm, self.spread = step, n_steps, tm, spread

    def _request(self, p_ref, slot):
        def issue(line, lane, priority):
            _token_copy(self.ys_hbm, p_ref[0, line, lane], self.ybuf, slot * self.tm + line * LANES + lane,
                        self.sems.at[slot]).start(priority=priority)

        _issue_burst(self.tm, issue, self.spread)

    def request_next(self):
        @pl.when(self.step + 1 < self.n_steps)
        def _():
            self._request(self.pos_next_ref, (self.step + 1) % 2)

    def current(self):
        rows = self.tm * TOKEN_ROWS

        @pl.when(self.step == 0)
        def _():
            self._request(self.pos_ref, 0)

        slot = self.step % 2
        base = pl.multiple_of(slot * rows, rows)
        pltpu.make_async_copy(self.ys_hbm.at[pl.ds(0, rows), :], self.ybuf.at[pl.ds(base, rows), :],
                              self.sems.at[slot]).wait()
        return jnp.concatenate([self.ybuf[pl.ds(base + s, self.tm, stride=TOKEN_ROWS), :]
                                for s in range(D_MODEL // LANES)], axis=1)


def _gather_specs(n_steps, tm, index_of):
    return [pl.BlockSpec((1, SUBLANES, LANES), lambda *g: (index_of(*g), 0, 0), memory_space=pltpu.SMEM),
            pl.BlockSpec((1, SUBLANES, LANES), lambda *g: (jnp.minimum(index_of(*g) + 1, n_steps - 1), 0, 0),
                         memory_space=pltpu.SMEM),
            pl.BlockSpec(memory_space=pl.ANY)]


def _pos_lines(pos, tm):
    lines = pos.reshape(-1, tm // LANES, LANES)
    return jnp.pad(lines, ((0, 0), (0, SUBLANES - tm // LANES), (0, 0)))


def _gather_scratch(tm):
    return [pltpu.VMEM((2 * tm * TOKEN_ROWS, LANES), F32), pltpu.SemaphoreType.DMA((2,))]


def _gmlp_kernel(pos_ref, pos_next_ref, ys_hbm, gtf_ref, x_ref, gain_ref, sh_ref, sc_ref, win_ref, vg_ref, ws_ref,
                 bs_ref, wo_ref, gt_ref, fgain_ref, fsh_ref, fsc_ref, wr_ref, br_ref, x1_ref, xs_hbm, pos_out, tab_out,
                 cnt_out, gated_scr, ybuf, sems, *route_scratch, tm, n_steps):
    step = pl.program_id(0) * pl.num_programs(1) + pl.program_id(1)
    gather = _RowGather(pos_ref, pos_next_ref, ys_hbm, ybuf, sems, step, n_steps, tm, spread=False)
    gather.request_next()
    x = x_ref[0] + gtf_ref[0, 0] * gather.current()
    hn = _norm_modulate(x, gain_ref[...], sh_ref[0, 0], sc_ref[0, 0]).astype(BF16)
    v = _gelu_x2(jnp.dot(hn, win_ref[:, GMLP_HALF:], preferred_element_type=F32))
    uw = 4 * LANES
    for k in range(GMLP_HALF // uw):
        gated_scr[:, k * uw:(k + 1) * uw] = _gelu_x2(
            jnp.dot(hn, win_ref[:, k * uw:(k + 1) * uw], preferred_element_type=F32)).astype(BF16)
    mu = jnp.mean(v, axis=-1, keepdims=True)
    vc = v - mu
    vn = (vc * lax.rsqrt(jnp.mean(vc * vc, axis=-1, keepdims=True) + 4.0 * EPS) * vg_ref[...]).astype(BF16)
    gw = GMLP_HALF // GMLP_GROUPS
    for c in range(tm // CHUNK):
        rs = slice(c * CHUNK, (c + 1) * CHUNK)
        for g in range(GMLP_GROUPS):
            cs = slice(g * gw, (g + 1) * gw)
            vs = jnp.dot(ws_ref[g], vn[rs, cs], preferred_element_type=F32) + bs_ref[:, g:g + 1]
            gated_scr[rs, cs] = (gated_scr[rs, cs].astype(F32) * vs).astype(BF16)
    y = jnp.dot(gated_scr[...], wo_ref[...], preferred_element_type=F32)
    x1 = x + gt_ref[0, 0] * y
    x1_ref[0] = x1
    _route_outputs(x1, fgain_ref, fsh_ref, fsc_ref, wr_ref, br_ref, xs_hbm, pos_out, tab_out, cnt_out,
                   route_scratch, n_steps)


def _gmlp(ys, pos, gt_prev, x, gain, shift, scale, w_in, v_gain, w_s, b_s_t, w_o, gt, fgain, fshift, fscale,
          wr_t, br, tm=512):
    b, s, _ = x.shape
    tok = lambda w: pl.BlockSpec((1, tm, w), lambda bi, i: (bi, i, 0))
    c2 = lambda shape: _const_spec(shape, lambda bi, i: (0,) * len(shape))
    out_specs, out_shape = _route_out_specs(b, s, tm)
    n_steps = b * s // tm
    return pl.pallas_call(
        functools.partial(_gmlp_kernel, tm=tm, n_steps=n_steps),
        grid=(b, s // tm),
        in_specs=_gather_specs(n_steps, tm, lambda bi, i: bi * (s // tm) + i) + [
                  _mod_spec(),
                  tok(D_MODEL), c2((1, D_MODEL)), _mod_spec(), _mod_spec(),
                  c2((D_MODEL, 2 * GMLP_HALF)), c2((1, GMLP_HALF)),
                  c2((GMLP_GROUPS, CHUNK, CHUNK)), c2((CHUNK, GMLP_GROUPS)),
                  c2((GMLP_HALF, D_MODEL)), _mod_spec(),
                  c2((1, D_MODEL)), _mod_spec(), _mod_spec(),
                  c2((ROUTER_ROWS, D_MODEL)), c2((ROUTER_ROWS, 1))],
        out_specs=out_specs,
        out_shape=out_shape,
        scratch_shapes=[pltpu.VMEM((tm, GMLP_HALF), BF16)] + _gather_scratch(tm) + _route_scratch(tm),
        compiler_params=_cparams(("arbitrary", "arbitrary")),
        name="gmlp",
    )(_pos_lines(pos, tm), _pos_lines(pos, tm), ys, gt_prev,
      x, gain, shift, scale, w_in, v_gain, w_s, b_s_t, w_o, gt, fgain, fshift, fscale, wr_t, br)


def _token_copy(src, src_tok, dst, dst_tok, sem):
    s0 = pl.multiple_of(src_tok * TOKEN_ROWS, TOKEN_ROWS)
    d0 = pl.multiple_of(dst_tok * TOKEN_ROWS, TOKEN_ROWS)
    return pltpu.make_async_copy(src.at[pl.ds(s0, TOKEN_ROWS), :], dst.at[pl.ds(d0, TOKEN_ROWS), :], sem)


def _slabs(ref, n_tok, n_slab):
    return jnp.concatenate([ref[pl.ds(s, n_tok, stride=TOKEN_ROWS), :] for s in range(n_slab)], axis=1)


def _expert_kernel(ea_ref, eb_ref, blk_ref, nused_ref, x_ref, wr_ref, br_ref, wga, wgb, wua, wub, wda, wdb, y_ref,
                   lt_scr):
    j = pl.program_id(0)

    @pl.when(j < nused_ref[0])
    def _():
        h = _slabs(x_ref, MOE_TILE, TOKEN_ROWS).astype(BF16)
        lt = _router_logits(h, wr_ref, br_ref)
        g_val, _ = _group_choice(lt)
        lt_scr[...] = lt
        el_a = lt_scr[pl.ds(N_EXPERT_GROUPS + ea_ref[j], 1), :]
        el_b = lt_scr[pl.ds(N_EXPERT_GROUPS + eb_ref[j], 1), :]
        top = jnp.maximum(el_a, el_b)
        p_a = jnp.exp(el_a - top)
        p_b = jnp.exp(el_b - top)
        row = lax.broadcasted_iota(jnp.int32, (LANES, MOE_TILE), 0)
        gates_t = jnp.where(row == 0, g_val * (p_a / (p_a + p_b)), jnp.where(row == 1, g_val * (p_b / (p_a + p_b)), 0.0))
        gates = gates_t.T
        y = jnp.zeros((MOE_TILE, D_MODEL), F32)
        for gate, wg, wu, wd in ((gates[:, 0:1], wga, wua, wda), (gates[:, 1:2], wgb, wub, wdb)):
            hg = jnp.dot(h, wg[0, 0], preferred_element_type=F32)
            hu = jnp.dot(h, wu[0, 0], preferred_element_type=F32)
            act = (_silu(hg) * hu * gate).astype(BF16)
            y = y + jnp.dot(act, wd[0, 0], preferred_element_type=F32)
        for s in range(TOKEN_ROWS):
            y_ref[pl.ds(s, MOE_TILE, stride=TOKEN_ROWS), :] = y[:, s * LANES:(s + 1) * LANES]

    @pl.when(j >= nused_ref[0])
    def _():
        y_ref[...] = jnp.zeros_like(y_ref)


def _experts(xs, tile_ea, tile_eb, tile_blk, n_used, wr_t, br, wg, wu, wd, layer):
    n_tiles = xs.shape[0] // (MOE_TILE * TOKEN_ROWS)
    up = lambda sel: pl.BlockSpec((1, 1, D_MODEL, D_EXPERT),
                                  lambda j, ea, eb, blk, nu: (layer, (ea, eb)[sel][j], 0, 0))
    down = lambda sel: pl.BlockSpec((1, 1, D_EXPERT, D_MODEL),
                                    lambda j, ea, eb, blk, nu: (layer, (ea, eb)[sel][j], 0, 0))
    shape = (MOE_TILE * TOKEN_ROWS, LANES)
    return pl.pallas_call(
        _expert_kernel,
        grid_spec=pltpu.PrefetchScalarGridSpec(
            num_scalar_prefetch=4,
            grid=(n_tiles,),
            in_specs=[pl.BlockSpec(shape, lambda j, ea, eb, blk, nu: (blk[j], 0)),
                      pl.BlockSpec((ROUTER_ROWS, D_MODEL), lambda j, ea, eb, blk, nu: (0, 0)),
                      pl.BlockSpec((ROUTER_ROWS, 1), lambda j, ea, eb, blk, nu: (0, 0)),
                      up(0), up(1), up(0), up(1), down(0), down(1)],
            out_specs=pl.BlockSpec(shape, lambda j, ea, eb, blk, nu: (j, 0)),
            scratch_shapes=[pltpu.VMEM((ROUTER_ROWS, MOE_TILE), F32)]),
        out_shape=jax.ShapeDtypeStruct(xs.shape, F32),
        compiler_params=_cparams(("arbitrary",)),
        name="moe_experts",
    )(tile_ea, tile_eb, tile_blk, n_used, xs, wr_t, br, wg, wg, wu, wu, wd, wd)


def _final_kernel(pos_ref, pos_next_ref, ys_hbm, x_ref, gt_ref, fin_ref, o_ref, ybuf, sems, *, tm):
    gather = _RowGather(pos_ref, pos_next_ref, ys_hbm, ybuf, sems, pl.program_id(0), pl.num_programs(0), tm,
                        spread=True)
    gather.request_next()
    y_moe = gather.current()
    x2 = x_ref[...] + gt_ref[0, 0] * y_moe
    ms = jnp.mean(x2 * x2, axis=-1, keepdims=True)
    o_ref[...] = x2 * lax.rsqrt(ms + EPS) * fin_ref[...]


def _final(ys, pos, x, gt, fin_gain, seq, tm=512):
    t = x.shape[0]
    tiles_per_batch = seq // tm
    n_steps = t // tm
    return pl.pallas_call(
        functools.partial(_final_kernel, tm=tm),
        grid=(n_steps,),
        in_specs=_gather_specs(n_steps, tm, lambda i: i) + [
                  pl.BlockSpec((tm, D_MODEL), lambda i: (i, 0)),
                  pl.BlockSpec((1, 1, 1, D_MODEL), lambda i: (i // tiles_per_batch, 0, 0, 0)),
                  pl.BlockSpec((1, D_MODEL), lambda i: (0, 0))],
        out_specs=pl.BlockSpec((tm, D_MODEL), lambda i: (i, 0)),
        out_shape=jax.ShapeDtypeStruct((t, D_MODEL), F32),
        scratch_shapes=_gather_scratch(tm),
        compiler_params=_cparams(("arbitrary",)),
        name="moe_combine_final_norm",
    )(_pos_lines(pos, tm), _pos_lines(pos, tm), ys, x, gt, fin_gain)


_PAIR_A = np.array([0, 0, 0, 1, 1, 2], np.int32)
_PAIR_B = np.array([1, 2, 3, 2, 3, 3], np.int32)


def _moe_sorted(xs, tab, counts, wr_t, br, wg, wu, wd, layer):
    n_tiles = xs.shape[0] // (MOE_TILE * TOKEN_ROWS)
    cnt = counts[:N_CLASSES, 0].astype(jnp.int32)
    n_used = jnp.sum((cnt + MOE_TILE - 1) // MOE_TILE)
    ids = jnp.arange(n_tiles, dtype=jnp.int32)
    tile_cls = jnp.sum(tab, axis=0)[:n_tiles].astype(jnp.int32)
    last_cls = jnp.sum(jnp.where(ids == n_used - 1, tile_cls, 0))
    tile_cls = jnp.where(ids < n_used, tile_cls, last_cls)
    j = jnp.minimum(ids, n_used - 1)
    pair = tile_cls % PAIRS_PER_GROUP
    pair_a = jnp.sum(jnp.where(pair[:, None] == jnp.arange(PAIRS_PER_GROUP)[None, :], _PAIR_A[None, :], 0), axis=1)
    pair_b = jnp.sum(jnp.where(pair[:, None] == jnp.arange(PAIRS_PER_GROUP)[None, :], _PAIR_B[None, :], 0), axis=1)
    base = (tile_cls // PAIRS_PER_GROUP) * EXPERTS_PER_GROUP
    return _experts(xs, base + pair_a, base + pair_b, j, n_used.reshape(1), wr_t, br, wg, wu, wd, layer)


def _rope_tables(seq):
    inv_freq = ROPE_THETA ** (-jnp.arange(0, ROT_DIM, 2, dtype=jnp.float32) / ROT_DIM)
    ang = jnp.arange(seq).astype(jnp.float32)[:, None] * inv_freq[None, :]
    cos, sin = jnp.cos(ang), jnp.sin(ang)
    ones = jnp.ones((seq, HEAD_DIM - ROT_DIM), jnp.float32)
    cos_h = jnp.concatenate([cos, cos, ones], axis=1)
    sgn_h = jnp.concatenate([-sin, sin, 0.0 * ones], axis=1)
    reps = LANES // HEAD_DIM
    return jnp.tile(cos_h, (1, reps)).astype(F32), jnp.tile(sgn_h, (1, reps)).astype(F32)


def _pad_heads(w, axis):
    pad = [(0, 0)] * w.ndim
    pad[axis] = (0, GROUP_WIDTH - HEADS_PER_GROUP * HEAD_DIM)
    return jnp.pad(w, pad)


def _router_weights(w_group, b_group, w_expert, b_expert):
    we = jnp.transpose(w_expert, (1, 0, 2)).reshape(D_MODEL, N_EXPERTS)
    w = jnp.concatenate([w_group, we], axis=1)
    w = jnp.pad(w, ((0, 0), (0, ROUTER_ROWS - w.shape[1])))
    bias = jnp.pad(jnp.concatenate([b_group, b_expert.reshape(-1)]), (0, ROUTER_ROWS - 20))
    return w.T.astype(BF16), bias.reshape(ROUTER_ROWS, 1)


def kernel(x, c, norm_mix, norm_ffn, w_ada, b_ada, a_w_qkv, a_w_o, b_w_in, b_v_gain, b_w_s, b_b_s, b_w_o,
           r_w_group, r_b_group, r_w_expert, r_b_expert, e_w_gate, e_w_up, e_w_down, final_norm):
    b, s, _ = x.shape
    mod = _ada_mod(c, w_ada, b_ada)
    sh_m, sc_m, gt_m, sh_f, sc_f, gt_f = [mod[:, :, i:i + 1] for i in range(6)]
    row = lambda v: v.reshape(1, -1)
    gw = HEADS_PER_GROUP * HEAD_DIM
    e_w_gate, e_w_up, e_w_down = [w.astype(BF16) for w in (e_w_gate, e_w_up, e_w_down)]

    w = a_w_qkv[0].reshape(D_MODEL, 3, N_ATTN_GROUPS, gw)
    w = w * jnp.asarray([HEAD_DIM ** -0.5, 1.0, 1.0], F32)[None, :, None, None]
    w_groups = jnp.transpose(_pad_heads(w, 3), (2, 0, 1, 3)).reshape(
        N_ATTN_GROUPS, D_MODEL, 3 * GROUP_WIDTH).astype(BF16)
    wo_groups = _pad_heads(a_w_o[0].reshape(N_ATTN_GROUPS, gw, D_MODEL), 1).astype(BF16)
    expand = (jnp.arange(LANES)[:, None] == jnp.arange(GROUP_WIDTH)[None, :] // HEAD_DIM).astype(BF16)
    cos_t, sgn_t = _rope_tables(s)
    qkv = _qkv(x, row(norm_mix[0]), sh_m[0], sc_m[0], w_groups, cos_t, sgn_t)
    o_list, l_list = [], []
    for g in range(N_ATTN_GROUPS):
        o, l = _attention(*qkv[3 * g:3 * g + 3])
        o_list.append(o)
        l_list.append(l)
    wr_t, br = _router_weights(r_w_group[0], r_b_group[0], r_w_expert[0], r_b_expert[0])
    x1, xs, pos, tab, counts = _attn_out(o_list, l_list, x, wo_groups, expand, gt_m[0], row(norm_ffn[0]),
                                         sh_f[0], sc_f[0], wr_t, br)
    t = b * s
    token_order = lambda p: p[:, :, :t // (p.shape[0] * p.shape[1] * LANES), :].reshape(t)
    ys = _moe_sorted(xs, tab, counts, wr_t, br, e_w_gate, e_w_up, e_w_down, 0)

    wr_t, br = _router_weights(r_w_group[1], r_b_group[1], r_w_expert[1], r_b_expert[1])
    x3, xs, pos, tab, counts = _gmlp(ys, token_order(pos), gt_f[0], x1, row(norm_mix[1]), sh_m[1], sc_m[1],
                                     b_w_in[0].astype(BF16), row(b_v_gain[0]), (0.5 * b_w_s[0]).astype(BF16), 0.5 * b_b_s[0].T,
                                     b_w_o[0].astype(BF16), gt_m[1], row(norm_ffn[1]), sh_f[1], sc_f[1], wr_t, br)
    ys = _moe_sorted(xs, tab, counts, wr_t, br, e_w_gate, e_w_up, e_w_down, 1)
    out = _final(ys, token_order(pos), x3.reshape(t, D_MODEL), gt_f[1], row(final_norm), s)
    return out.reshape(b, s, D_MODEL)
```
